```python
import math
import jax
import jax.numpy as jnp
from jax import lax
import numpy as np

D_MODEL = 1024
BATCH = 32
SEQ = 2048
DEPTH = 1
DEC_BATCH = 8
DEC_SEQ = 16
PAST_LEN = 4096

CHUNK = 64
Q_BLOCK = 128
DIFF_HEADS = 8
DIFF_HEAD_DIM = 64
DIFF_V_DIM = 2 * DIFF_HEAD_DIM
MLA_HEADS = 16
MLA_NOPE_DIM = 64
MLA_ROPE_DIM = 32
MLA_V_DIM = 64
MLA_Q_LORA = 256
MLA_KV_LORA = 256
D_FF = 4 * D_MODEL
NUM_BUCKETS = 32
MAX_DISTANCE = 128
ROPE_THETA = 10000.0
EPS = 1e-6
NEG_INF = -1e30

DIFF_QK_W = DIFF_HEADS * 2 * DIFF_HEAD_DIM
DIFF_V_W = DIFF_HEADS * DIFF_V_DIM
IN_COLS = 2 * DIFF_QK_W + DIFF_V_W + MLA_Q_LORA + MLA_KV_LORA + MLA_ROPE_DIM + 2 * D_MODEL

kernel_name = 'hybrid_diffattn_mla_stream_step'


def _in_splits():
    sizes = (DIFF_QK_W, DIFF_QK_W, DIFF_V_W, MLA_Q_LORA, MLA_KV_LORA, MLA_ROPE_DIM, D_MODEL, D_MODEL)
    out, acc = [], 0
    for s in sizes[:-1]:
        acc += s
        out.append(acc)
    return tuple(out)


def _rmsnorm(x, g):
    xf = x.astype(jnp.float32)
    y = xf * lax.rsqrt(jnp.mean(xf * xf, axis=-1, keepdims=True) + EPS) * g.astype(jnp.float32)
    return y.astype(x.dtype)


def _rope(x, pos):
    r = x.shape[-1]
    half = r // 2
    inv = jnp.power(ROPE_THETA, -jnp.arange(half, dtype=jnp.float32) * 2.0 / r)
    ang = pos.astype(jnp.float32)[:, None] * inv[None, :]
    ang = ang.reshape((ang.shape[0],) + (1,) * (x.ndim - 3) + (half,))
    cos, sin = jnp.cos(ang), jnp.sin(ang)
    xf = x.astype(jnp.float32)
    x1, x2 = xf[..., :half], xf[..., half:]
    return jnp.concatenate([x1 * cos - x2 * sin, x2 * cos + x1 * sin], axis=-1).astype(x.dtype)


def _t5_bias(q_pos, k_pos, table):
    rel = k_pos[None, :] - q_pos[:, None]
    half = NUM_BUCKETS // 2
    max_exact = half // 2
    n = jnp.abs(rel)
    nf = jnp.maximum(n, max_exact).astype(jnp.float32)
    large = max_exact + (jnp.log(nf / max_exact) / math.log(MAX_DISTANCE / max_exact)
                         * (half - max_exact)).astype(jnp.int32)
    large = jnp.minimum(large, half - 1)
    bucket = jnp.where(rel > 0, half, 0) + jnp.where(n < max_exact, n, large)
    return jnp.transpose(table[bucket].astype(jnp.float32), (2, 0, 1))


def _chunk_mask(q_pos, k_pos):
    return (k_pos // CHUNK)[None, :] <= (q_pos // CHUNK)[:, None]


def _sweep_queries(fn, qs, q_pos):
    t = q_pos.shape[0]
    if t <= Q_BLOCK:
        return fn(*qs, q_pos)
    nb = t // Q_BLOCK

    def to_blocks(a):
        return jnp.moveaxis(a.reshape((a.shape[0], nb, Q_BLOCK) + a.shape[2:]), 1, 0)

    out = lax.map(lambda args: fn(*args[0], args[1]),
                  (tuple(to_blocks(a) for a in qs), q_pos.reshape(nb, Q_BLOCK)))
    out = jnp.moveaxis(out, 0, 1)
    return out.reshape((out.shape[0], t) + out.shape[3:])


def _diff_attention(q, k_all, v_all, q_pos, k_pos, rel_bias, lam, subln, lambda_init):
    b, kl = k_all.shape[0], k_all.shape[1]
    k = k_all.reshape(b, kl, DIFF_HEADS, 2, DIFF_HEAD_DIM)
    scale = DIFF_HEAD_DIM ** -0.5

    def block(qb, qp):
        s = jnp.einsum('bqhnd,bkhnd->bnhqk', qb, k).astype(jnp.float32) * scale
        s = s + _t5_bias(qp, k_pos, rel_bias)[None, None]
        s = jnp.where(_chunk_mask(qp, k_pos)[None, None, None], s, NEG_INF)
        p = jax.nn.softmax(s, axis=-1)
        a = (p[:, 0] - lam * p[:, 1]).astype(v_all.dtype)
        return jnp.einsum('bhqk,bkhe->bqhe', a, v_all)

    o = _sweep_queries(block, (q,), q_pos)
    return _rmsnorm(o, subln) * (1.0 - lambda_init)


def _mla_attention(q_lat, q_rope, ckv_all, kr_all, q_pos, k_pos):
    scale = (MLA_NOPE_DIM + MLA_ROPE_DIM) ** -0.5

    def block(ql, qr, qp):
        s = (jnp.einsum('bqhc,bkc->bhqk', ql, ckv_all)
             + jnp.einsum('bqhr,bkr->bhqk', qr, kr_all)).astype(jnp.float32) * scale
        s = jnp.where(_chunk_mask(qp, k_pos)[None, None], s, NEG_INF)
        p = jax.nn.softmax(s, axis=-1).astype(ckv_all.dtype)
        return jnp.einsum('bhqk,bkc->bqhc', p, ckv_all)

    return _sweep_queries(block, (q_lat, q_rope), q_pos)


def _layer(x, q_pos, k_pos, past, layer_idx, rel_bias, norm_mix, w_in, lam_q1, lam_k1, lam_q2, lam_k2,
           diff_subln, mla_q_norm, mla_w_uq, mla_kv_norm, mla_w_uk, mla_w_uv, w_o_diff, w_o_mla, w_out,
           norm_mlp, w_up, w_down):
    b, t, _ = x.shape
    h = _rmsnorm(x, norm_mix)
    z = jnp.einsum('btd,dc->btc', h, w_in)
    q_d, k_d, v_d, c_q, c_kv, k_r, g_d, g_m = jnp.split(z, _in_splits(), axis=-1)

    k_new = k_d.reshape(b, t, DIFF_HEADS, 2 * DIFF_HEAD_DIM)
    v_new = v_d.reshape(b, t, DIFF_HEADS, DIFF_V_DIM)
    ckv_new = _rmsnorm(c_kv, mla_kv_norm)
    kr_new = _rope(k_r, q_pos)
    if past is None:
        k_all, v_all, ckv_all, kr_all = k_new, v_new, ckv_new, kr_new
    else:
        k_all = jnp.concatenate([past[0], k_new], axis=1)
        v_all = jnp.concatenate([past[1], v_new], axis=1)
        ckv_all = jnp.concatenate([past[2], ckv_new], axis=1)
        kr_all = jnp.concatenate([past[3], kr_new], axis=1)

    lambda_init = 0.8 - 0.6 * math.exp(-0.3 * layer_idx)
    lam = (jnp.exp(jnp.sum(lam_q1.astype(jnp.float32) * lam_k1.astype(jnp.float32)))
           - jnp.exp(jnp.sum(lam_q2.astype(jnp.float32) * lam_k2.astype(jnp.float32))) + lambda_init)
    q = q_d.reshape(b, t, DIFF_HEADS, 2, DIFF_HEAD_DIM)
    diff_out = _diff_attention(q, k_all, v_all, q_pos, k_pos, rel_bias, lam, diff_subln, lambda_init)

    cq = _rmsnorm(c_q, mla_q_norm)
    qm = jnp.einsum('btc,chd->bthd', cq, mla_w_uq)
    q_nope, q_rope = qm[..., :MLA_NOPE_DIM], _rope(qm[..., MLA_NOPE_DIM:], q_pos)
    q_lat = jnp.einsum('bthd,chd->bthc', q_nope, mla_w_uk)
    o_lat = _mla_attention(q_lat, q_rope, ckv_all, kr_all, q_pos, k_pos)
    mla_out = jnp.einsum('bthc,chd->bthd', o_lat, mla_w_uv)

    o_d = jnp.einsum('bthe,hed->btd', diff_out, w_o_diff)
    o_m = jnp.einsum('bthe,hed->btd', mla_out, w_o_mla)
    merged = jax.nn.sigmoid(g_d) * o_d + jax.nn.sigmoid(g_m) * o_m
    x = x + jnp.einsum('btd,de->bte', merged, w_out)

    u = jax.nn.relu(jnp.einsum('btd,df->btf', _rmsnorm(x, norm_mlp), w_up))
    x = x + jnp.einsum('btf,fd->btd', u * u, w_down)
    return x, (k_new, v_new, ckv_new, kr_new)


def setup_inputs(seed: int = 0) -> dict:
    key = jax.random.key(seed)
    ks = jax.random.split(key, 32)

    def nrm(k, shape, scale):
        return jax.random.normal(k, shape, dtype=jnp.float32) * scale

    def gain(k, shape):
        return 1.0 + nrm(k, shape, 0.05)

    L = DEPTH
    return {
        'x_prompt': nrm(ks[0], (BATCH, SEQ, D_MODEL), 1.0),
        'x_sample': nrm(ks[1], (DEC_BATCH, DEC_SEQ, D_MODEL), 1.0),
        'cache_diff_k': nrm(ks[2], (L, DEC_BATCH, PAST_LEN, DIFF_HEADS, 2 * DIFF_HEAD_DIM), 1.0),
        'cache_diff_v': nrm(ks[3], (L, DEC_BATCH, PAST_LEN, DIFF_HEADS, DIFF_V_DIM), 1.0),
        'cache_mla_ckv': nrm(ks[4], (L, DEC_BATCH, PAST_LEN, MLA_KV_LORA), 1.0),
        'cache_mla_krope': nrm(ks[5], (L, DEC_BATCH, PAST_LEN, MLA_ROPE_DIM), 1.0),
        'rel_bias': nrm(ks[6], (NUM_BUCKETS, DIFF_HEADS), 0.5),
        'norm_mix': gain(ks[7], (L, D_MODEL)),
        'w_in': nrm(ks[8], (L, D_MODEL, IN_COLS), D_MODEL ** -0.5),
        'lam_q1': nrm(ks[9], (L, DIFF_HEAD_DIM), 0.1),
        'lam_k1': nrm(ks[10], (L, DIFF_HEAD_DIM), 0.1),
        'lam_q2': nrm(ks[11], (L, DIFF_HEAD_DIM), 0.1),
        'lam_k2': nrm(ks[12], (L, DIFF_HEAD_DIM), 0.1),
        'diff_subln': gain(ks[13], (L, DIFF_V_DIM)),
        'mla_q_norm': gain(ks[14], (L, MLA_Q_LORA)),
        'mla_w_uq': nrm(ks[15], (L, MLA_Q_LORA, MLA_HEADS, MLA_NOPE_DIM + MLA_ROPE_DIM), MLA_Q_LORA ** -0.5),
        'mla_kv_norm': gain(ks[16], (L, MLA_KV_LORA)),
        'mla_w_uk': nrm(ks[17], (L, MLA_KV_LORA, MLA_HEADS, MLA_NOPE_DIM), MLA_KV_LORA ** -0.5),
        'mla_w_uv': nrm(ks[18], (L, MLA_KV_LORA, MLA_HEADS, MLA_V_DIM), MLA_KV_LORA ** -0.5),
        'w_o_diff': nrm(ks[19], (L, DIFF_HEADS, DIFF_V_DIM, D_MODEL), DIFF_V_W ** -0.5),
        'w_o_mla': nrm(ks[20], (L, MLA_HEADS, MLA_V_DIM, D_MODEL), (MLA_HEADS * MLA_V_DIM) ** -0.5),
        'w_out': nrm(ks[21], (L, D_MODEL, D_MODEL), D_MODEL ** -0.5),
        'norm_mlp': gain(ks[22], (L, D_MODEL)),
        'w_up': nrm(ks[23], (L, D_MODEL, D_FF), D_MODEL ** -0.5),
        'w_down': nrm(ks[24], (L, D_FF, D_MODEL), D_FF ** -0.5),
        'norm_final': gain(ks[25], (D_MODEL,)),
    }


def reference(x_prompt, x_sample, cache_diff_k, cache_diff_v, cache_mla_ckv, cache_mla_krope, rel_bias,
              norm_mix, w_in, lam_q1, lam_k1, lam_q2, lam_k2, diff_subln, mla_q_norm, mla_w_uq, mla_kv_norm,
              mla_w_uk, mla_w_uv, w_o_diff, w_o_mla, w_out, norm_mlp, w_up, w_down, norm_final):
    past_len = cache_diff_k.shape[2]
    pos_p = jnp.arange(x_prompt.shape[1], dtype=jnp.int32)
    pos_s = past_len + jnp.arange(x_sample.shape[1], dtype=jnp.int32)
    kpos_s = jnp.arange(past_len + x_sample.shape[1], dtype=jnp.int32)

    xp, xs = x_prompt, x_sample
    rows_p, rows_s = [], []
    for l in range(DEPTH):
        lw = (norm_mix[l], w_in[l], lam_q1[l], lam_k1[l], lam_q2[l], lam_k2[l], diff_subln[l], mla_q_norm[l],
              mla_w_uq[l], mla_kv_norm[l], mla_w_uk[l], mla_w_uv[l], w_o_diff[l], w_o_mla[l], w_out[l],
              norm_mlp[l], w_up[l], w_down[l])
        xp, rp = _layer(xp, pos_p, pos_p, None, l, rel_bias, *lw)
        past = (cache_diff_k[l], cache_diff_v[l], cache_mla_ckv[l], cache_mla_krope[l])
        xs, rs = _layer(xs, pos_s, kpos_s, past, l, rel_bias, *lw)
        rows_p.append(rp)
        rows_s.append(rs)

    y_prompt = _rmsnorm(xp, norm_final)
    y_sample = _rmsnorm(xs, norm_final)
    new_diff_k_prompt = jnp.stack([r[0] for r in rows_p])
    new_diff_v_prompt = jnp.stack([r[1] for r in rows_p])
    new_mla_ckv_prompt = jnp.stack([r[2] for r in rows_p])
    new_mla_krope_prompt = jnp.stack([r[3] for r in rows_p])
    new_diff_k_sample = jnp.stack([r[0] for r in rows_s])
    new_diff_v_sample = jnp.stack([r[1] for r in rows_s])
    new_mla_ckv_sample = jnp.stack([r[2] for r in rows_s])
    new_mla_krope_sample = jnp.stack([r[3] for r in rows_s])
    return (y_prompt, y_sample, new_diff_k_prompt, new_diff_v_prompt, new_mla_ckv_prompt, new_mla_krope_prompt,
            new_diff_k_sample, new_diff_v_sample, new_mla_ckv_sample, new_mla_krope_sample)
```

```python
import functools
import math

import numpy as np
import jax
import jax.numpy as jnp
from jax import lax
from jax.experimental import pallas as pl
from jax.experimental.pallas import tpu as pltpu

D_MODEL = 1024
CHUNK = 64
DIFF_HEADS = 8
DIFF_HEAD_DIM = 64
DIFF_V_DIM = 128
MLA_HEADS = 16
MLA_NOPE_DIM = 64
MLA_ROPE_DIM = 32
MLA_V_DIM = 64
MLA_Q_LORA = 256
MLA_KV_LORA = 256
D_FF = 4 * D_MODEL
NUM_BUCKETS = 32
MAX_DISTANCE = 128
ROPE_THETA = 10000.0
EPS = 1e-6
NEG_INF = -1e30
LAMBDA_INIT = 0.8 - 0.6 * math.exp(-0.3 * 0)

BF = jnp.bfloat16
F32 = jnp.float32

LANES = 128
TQ = 256
TK = 256
DEC_PAD = LANES
CACHE_BLK = 512
VMEM_LIMIT = 56 * 1024 * 1024


def _nn(a, b):
    return jnp.dot(a, b, preferred_element_type=F32)


def _nt(a, b):
    return lax.dot_general(a, b, (((1,), (1,)), ((), ())), preferred_element_type=F32)


def _tn(a, b):
    return lax.dot_general(a, b, (((0,), (0,)), ((), ())), preferred_element_type=F32)


def _rms(x, g):
    return x * lax.rsqrt(jnp.mean(x * x, axis=-1, keepdims=True) + EPS) * g


def _const_spec(shape):
    zeros = (0,) * len(shape)
    return pl.BlockSpec(shape, lambda *_: zeros, pipeline_mode=pl.Buffered(1))


def _params(n_axes):
    return pltpu.CompilerParams(dimension_semantics=("arbitrary",) * n_axes,
                                vmem_limit_bytes=VMEM_LIMIT)


def _bias_kernel(tab_ref, bkt_ref, msk_ref, o_ref, *, shift_bucket):
    h = pl.program_id(0)
    bkt = bkt_ref[...]
    acc = jnp.zeros(bkt.shape, F32)
    for b in range(NUM_BUCKETS):
        acc = jnp.where(bkt == b, tab_ref[b, h], acc)
    if shift_bucket is not None:
        acc = acc - tab_ref[shift_bucket, h]
    o_ref[...] = jnp.where(msk_ref[...] != 0, acc, NEG_INF)


def _bias_tiles(rel_bias, bucket, mask, shift_bucket):
    n, r, c = bucket.shape
    return pl.pallas_call(
        functools.partial(_bias_kernel, shift_bucket=shift_bucket),
        grid=(DIFF_HEADS,),
        in_specs=[pl.BlockSpec(memory_space=pltpu.SMEM),
                  pl.BlockSpec((n, r, c), lambda h: (0, 0, 0)),
                  pl.BlockSpec((n, r, c), lambda h: (0, 0, 0))],
        out_specs=pl.BlockSpec((None, n, r, c), lambda h: (h, 0, 0, 0)),
        out_shape=jax.ShapeDtypeStruct((DIFF_HEADS, n, r, c), F32),
        compiler_params=_params(1),
        name="bias_tiles",
    )(rel_bias, bucket, mask)


def _t5_bucket(rel):
    half = NUM_BUCKETS // 2
    max_exact = half // 2
    n = jnp.abs(rel)
    nf = jnp.maximum(n, max_exact).astype(F32)
    large = max_exact + (jnp.log(nf / max_exact) / math.log(MAX_DISTANCE / max_exact)
                         * (half - max_exact)).astype(jnp.int32)
    large = jnp.minimum(large, half - 1)
    return jnp.where(rel > 0, half, 0) + jnp.where(n < max_exact, n, large)


def _far_bucket():
    half = NUM_BUCKETS // 2
    max_exact = half // 2
    assert max_exact + math.log(TK / max_exact) / math.log(MAX_DISTANCE / max_exact) * (half - max_exact) >= half
    return half - 1


def _proj_kernel(x_ref, gmix_ref, wqT_ref, wk_ref, wv_ref, wvT_ref, wlat_ref, gq_ref, gkv_ref,
                 wuqaT_ref, wuqbT_ref, wuk_ref, wuvT_ref, cosp_ref, sinp_ref, ct_ref, st_ref,
                 qTd_ref, kf_ref, kb_ref, vf_ref, vTd_ref, qTm_ref, ckv_ref, kn_ref, vTm_ref,
                 r_ref, kr_ref):
    hb = _rms(x_ref[...], gmix_ref[...]).astype(BF)
    qTd_ref[...] = _nt(wqT_ref[...], hb).astype(BF)
    k = _nn(hb, wk_ref[...])
    kf_ref[...] = k
    kb_ref[...] = k.astype(BF)
    vf_ref[...] = _nn(hb, wv_ref[...])
    vTd_ref[...] = _nt(wvT_ref[...], hb).astype(BF)

    lat = _nn(hb, wlat_ref[...])
    cq = _rms(lat[:, 0:MLA_Q_LORA], gq_ref[...]).astype(BF)
    qa = _nt(wuqaT_ref[...], cq)
    qb = _nt(wuqbT_ref[...], cq)
    ct = ct_ref[...]
    st = st_ref[...]
    pair = 2 * LANES
    for p in range(MLA_HEADS // 2):
        sl = slice(p * pair, (p + 1) * pair)
        qTm_ref[sl, :] = (qa[sl, :] * ct + qb[sl, :] * st).astype(BF)

    ckv = _rms(lat[:, MLA_Q_LORA:MLA_Q_LORA + MLA_KV_LORA], gkv_ref[...])
    ckv_ref[...] = ckv
    cb = ckv.astype(BF)
    kn_ref[...] = _nn(cb, wuk_ref[...]).astype(BF)
    vTm_ref[...] = _nt(wuvT_ref[...], cb).astype(BF)

    o = MLA_Q_LORA + MLA_KV_LORA
    r = lat[:, o:o + LANES] * cosp_ref[...] + lat[:, o + LANES:o + 2 * LANES] * sinp_ref[...]
    r_ref[...] = r.astype(BF)
    kr_ref[...] = r[:, 0:MLA_ROPE_DIM]


def _project(x, w, tabs, tm):
    nb, t, _ = x.shape
    nt = t // tm
    tok = lambda width: pl.BlockSpec((None, tm, width), lambda b, i: (b, i, 0))
    feat = lambda rows: pl.BlockSpec((None, rows, tm), lambda b, i: (b, 0, i))
    sds = jax.ShapeDtypeStruct
    weights = [w["gmix"], w["wqT"], w["wk"], w["wv"], w["wvT"], w["wlat"], w["gq"], w["gkv"],
               w["wuqaT"], w["wuqbT"], w["wuk"], w["wuvT"]]
    in_specs = ([tok(D_MODEL)] + [_const_spec(a.shape) for a in weights]
                + [pl.BlockSpec((tm, LANES), lambda b, i: (i, 0)),
                   pl.BlockSpec((tm, LANES), lambda b, i: (i, 0)),
                   pl.BlockSpec((2 * LANES, tm), lambda b, i: (0, i)),
                   pl.BlockSpec((2 * LANES, tm), lambda b, i: (0, i))])
    out_shape = [sds((nb, D_MODEL, t), BF), sds((nb, t, D_MODEL), F32), sds((nb, t, D_MODEL), BF),
                 sds((nb, t, D_MODEL), F32), sds((nb, D_MODEL, t), BF),
                 sds((nb, MLA_HEADS * LANES, t), BF), sds((nb, t, MLA_KV_LORA), F32),
                 sds((nb, t, D_MODEL), BF), sds((nb, D_MODEL, t), BF),
                 sds((nb, t, LANES), BF), sds((nb, t, MLA_ROPE_DIM), F32)]
    out_specs = [feat(D_MODEL), tok(D_MODEL), tok(D_MODEL), tok(D_MODEL), feat(D_MODEL),
                 feat(MLA_HEADS * LANES), tok(MLA_KV_LORA), tok(D_MODEL), feat(D_MODEL),
                 tok(LANES), tok(MLA_ROPE_DIM)]
    outs = pl.pallas_call(
        _proj_kernel, grid=(nb, nt), in_specs=in_specs, out_specs=out_specs, out_shape=out_shape,
        compiler_params=_params(2), name="project",
    )(x, *weights, tabs["cosp"], tabs["sinp"], tabs["ct"], tabs["st"])
    names = ["qTd", "kf", "kb", "vf", "vTd", "qTm", "ckv", "kn", "vTm", "r", "kr"]
    return dict(zip(names, outs))


def _flash_step(state, s, pv):
    bm = jnp.max(s, axis=0, keepdims=True)
    if state is None:
        p = jnp.exp(s - bm)
        return bm, jnp.sum(p, axis=0, keepdims=True), pv(p.astype(BF))
    m, l, acc = state
    m_new = jnp.maximum(m, bm)
    alpha = jnp.exp(m - m_new)
    p = jnp.exp(s - m_new)
    return m_new, alpha * l + jnp.sum(p, axis=0, keepdims=True), alpha * acc + pv(p.astype(BF))


def _normalized(state):
    _, l, acc = state
    return acc * (1.0 / l)


def _split_maps(qT):
    rows = lax.broadcasted_iota(jnp.int32, qT.shape, 0)
    qf = qT.astype(F32)
    zero = jnp.zeros_like(qf)
    return (jnp.where(rows < DIFF_HEAD_DIM, qf, zero).astype(BF),
            jnp.where(rows >= DIFF_HEAD_DIM, qf, zero).astype(BF))


def _lambda(lam_ref):
    lp = lam_ref[...]
    a = jnp.sum(lp[0:1, :] * lp[1:2, :], axis=1, keepdims=True)
    b = jnp.sum(lp[2:3, :] * lp[3:4, :], axis=1, keepdims=True)
    return jnp.exp(a) - jnp.exp(b) + LAMBDA_INIT


def _diff_finish(st1, st2, lam, g):
    o = _normalized(st1) - lam * _normalized(st2)
    ms = jnp.mean(o * o, axis=0, keepdims=True)
    return (o * lax.rsqrt(ms + EPS) * g * (1.0 - LAMBDA_INIT)).astype(BF)


def _mla_keys(kp, r):
    lane = lax.broadcasted_iota(jnp.int32, kp.shape, 1)
    kpf = kp.astype(F32)
    rf = r.astype(F32)
    return (jnp.where(lane < MLA_NOPE_DIM, kpf, rf).astype(BF),
            jnp.where(lane >= MLA_NOPE_DIM, kpf, rf).astype(BF))


def _diff_prompt_kernel(qT_ref, k_ref, vT_ref, bias_ref, lam_ref, g_ref, o_ref, *, t):
    lam = _lambda(lam_ref)
    g = g_ref[...]
    for qb in range(t // TQ):
        qs = slice(qb * TQ, (qb + 1) * TQ)
        q1, q2 = _split_maps(qT_ref[:, qs])
        st1 = st2 = None
        for kb in range(qb + 1):
            ks = slice(kb * TK, (kb + 1) * TK)
            kblk = k_ref[ks, :]
            s1 = _nn(kblk, q1)
            s2 = _nn(kblk, q2)
            if kb >= qb - 1:
                b = bias_ref[qb - kb]
                s1 = s1 + b
                s2 = s2 + b
            pv = lambda p, ks=ks: _nn(vT_ref[:, ks], p)
            st1 = _flash_step(st1, s1, pv)
            st2 = _flash_step(st2, s2, pv)
        o_ref[:, qs] = _diff_finish(st1, st2, lam, g)


def _diff_prompt(p, bias, lamp, g):
    nb, _, t = p["qTd"].shape
    head_feat = pl.BlockSpec((None, DIFF_V_DIM, t), lambda b, h: (b, h, 0))
    return pl.pallas_call(
        functools.partial(_diff_prompt_kernel, t=t),
        grid=(nb, DIFF_HEADS),
        in_specs=[head_feat,
                  pl.BlockSpec((None, t, DIFF_V_DIM), lambda b, h: (b, 0, h)),
                  head_feat,
                  pl.BlockSpec((None, 2, TK, TQ), lambda b, h: (h, 0, 0, 0)),
                  _const_spec(lamp.shape), _const_spec(g.shape)],
        out_specs=head_feat,
        out_shape=jax.ShapeDtypeStruct((nb, D_MODEL, t), BF),
        compiler_params=_params(2), name="diff_prompt",
    )(p["qTd"], p["kb"], p["vTd"], bias, lamp, g)


def _mla_prompt_kernel(qT_ref, kp_ref, r_ref, vT_ref, mask_ref, o_ref, ke_ref, ko_ref, *, t):
    ke, ko = _mla_keys(kp_ref[...], r_ref[...])
    ke_ref[...] = ke
    ko_ref[...] = ko
    for qb in range(t // TQ):
        qs = slice(qb * TQ, (qb + 1) * TQ)
        qe = qT_ref[0:LANES, qs]
        qo = qT_ref[LANES:2 * LANES, qs]
        ste = sto = None
        for kb in range(qb + 1):
            ks = slice(kb * TK, (kb + 1) * TK)
            se = _nn(ke_ref[ks, :], qe)
            so = _nn(ko_ref[ks, :], qo)
            if kb == qb:
                se = se + mask_ref[...]
                so = so + mask_ref[...]
            ste = _flash_step(ste, se, lambda p, ks=ks: _nn(vT_ref[0:MLA_V_DIM, ks], p))
            sto = _flash_step(sto, so, lambda p, ks=ks: _nn(vT_ref[MLA_V_DIM:2 * MLA_V_DIM, ks], p))
        o_ref[0:MLA_V_DIM, qs] = _normalized(ste).astype(BF)
        o_ref[MLA_V_DIM:2 * MLA_V_DIM, qs] = _normalized(sto).astype(BF)


def _mla_prompt(p, mask):
    nb, _, t = p["qTm"].shape
    return pl.pallas_call(
        functools.partial(_mla_prompt_kernel, t=t),
        grid=(nb, MLA_HEADS // 2),
        in_specs=[pl.BlockSpec((None, 2 * LANES, t), lambda b, h: (b, h, 0)),
                  pl.BlockSpec((None, t, LANES), lambda b, h: (b, 0, h)),
                  pl.BlockSpec((None, t, LANES), lambda b, h: (b, 0, 0)),
                  pl.BlockSpec((None, 2 * MLA_V_DIM, t), lambda b, h: (b, h, 0)),
                  _const_spec(mask.shape)],
        out_specs=pl.BlockSpec((None, 2 * MLA_V_DIM, t), lambda b, h: (b, h, 0)),
        out_shape=jax.ShapeDtypeStruct((nb, D_MODEL, t), BF),
        scratch_shapes=[pltpu.VMEM((t, LANES), BF), pltpu.VMEM((t, LANES), BF)],
        compiler_params=_params(2), name="mla_prompt",
    )(p["qTm"], p["kn"], p["r"], p["vTm"], mask)


def _diff_decode_kernel(qT_ref, ck_ref, cv_ref, kn_ref, vn_ref, bc_ref, bn_ref, lam_ref, g_ref, o_ref,
                        *, past, new):
    lam = _lambda(lam_ref)
    q1, q2 = _split_maps(qT_ref[...])
    st1 = st2 = None
    for c in range(past // CACHE_BLK):
        cs = slice(c * CACHE_BLK, (c + 1) * CACHE_BLK)
        kblk = ck_ref[cs, :].astype(BF)
        b = bc_ref[cs, :]
        pv = lambda p, cs=cs: _tn(cv_ref[cs, :].astype(BF), p)
        st1 = _flash_step(st1, _nn(kblk, q1) + b, pv)
        st2 = _flash_step(st2, _nn(kblk, q2) + b, pv)
    kblk = kn_ref[0:new, :]
    b = bn_ref[...]
    pv = lambda p: _tn(vn_ref[0:new, :].astype(BF), p)
    st1 = _flash_step(st1, _nn(kblk, q1) + b, pv)
    st2 = _flash_step(st2, _nn(kblk, q2) + b, pv)
    o_ref[...] = _diff_finish(st1, st2, lam, g_ref[...])


def _diff_decode(s, cache_k, cache_v, bias_c, bias_n, lamp, g, new):
    nb, past, _ = cache_k.shape
    head_feat = pl.BlockSpec((None, DIFF_V_DIM, DEC_PAD), lambda b, h: (b, h, 0))
    head_tok = lambda rows: pl.BlockSpec((None, rows, DIFF_V_DIM), lambda b, h: (b, 0, h))
    return pl.pallas_call(
        functools.partial(_diff_decode_kernel, past=past, new=new),
        grid=(nb, DIFF_HEADS),
        in_specs=[head_feat, head_tok(past), head_tok(past), head_tok(DEC_PAD), head_tok(DEC_PAD),
                  pl.BlockSpec((None, None, past, DEC_PAD), lambda b, h: (h, 0, 0, 0)),
                  pl.BlockSpec((None, None, new, DEC_PAD), lambda b, h: (h, 0, 0, 0)),
                  _const_spec(lamp.shape), _const_spec(g.shape)],
        out_specs=head_feat,
        out_shape=jax.ShapeDtypeStruct((nb, D_MODEL, DEC_PAD), BF),
        compiler_params=_params(2), name="diff_decode",
    )(s["qTd"], cache_k, cache_v, s["kb"], s["vf"], bias_c, bias_n, lamp, g)


def _cache_prep_kernel(ckv_ref, kr_ref, wuk_ref, wuvT_ref, place_ref, kn_ref, vT_ref, r_ref):
    cb = ckv_ref[...].astype(BF)
    kn_ref[...] = _nn(cb, wuk_ref[...]).astype(BF)
    vT_ref[...] = _nt(wuvT_ref[...], cb).astype(BF)
    r_ref[...] = _nn(kr_ref[...].astype(BF), place_ref[...]).astype(BF)


def _cache_prep(cache_ckv, cache_kr, w, place):
    nb, past, _ = cache_ckv.shape
    tok = lambda width: pl.BlockSpec((None, CACHE_BLK, width), lambda b, i: (b, i, 0))
    sds = jax.ShapeDtypeStruct
    return pl.pallas_call(
        _cache_prep_kernel, grid=(nb, past // CACHE_BLK),
        in_specs=[tok(MLA_KV_LORA), tok(MLA_ROPE_DIM), _const_spec(w["wuk"].shape),
                  _const_spec(w["wuvT"].shape), _const_spec(place.shape)],
        out_specs=[tok(D_MODEL), pl.BlockSpec((None, D_MODEL, CACHE_BLK), lambda b, i: (b, 0, i)),
                   tok(LANES)],
        out_shape=[sds((nb, past, D_MODEL), BF), sds((nb, D_MODEL, past), BF), sds((nb, past, LANES), BF)],
        compiler_params=_params(2), name="cache_prep",
    )(cache_ckv, cache_kr, w["wuk"], w["wuvT"], place)


def _mla_decode_kernel(qT_ref, ckn_ref, cr_ref, cvT_ref, kn_ref, r_ref, vT_ref, o_ref, *, past, new):
    qe = qT_ref[0:LANES, :]
    qo = qT_ref[LANES:2 * LANES, :]
    ste = sto = None
    for c in range(past // CACHE_BLK):
        cs = slice(c * CACHE_BLK, (c + 1) * CACHE_BLK)
        ke, ko = _mla_keys(ckn_ref[cs, :], cr_ref[cs, :])
        ste = _flash_step(ste, _nn(ke, qe), lambda p, cs=cs: _nn(cvT_ref[0:MLA_V_DIM, cs], p))
        sto = _flash_step(sto, _nn(ko, qo), lambda p, cs=cs: _nn(cvT_ref[MLA_V_DIM:2 * MLA_V_DIM, cs], p))
    ke, ko = _mla_keys(kn_ref[0:new, :], r_ref[0:new, :])
    ste = _flash_step(ste, _nn(ke, qe), lambda p: _nn(vT_ref[0:MLA_V_DIM, 0:new], p))
    sto = _flash_step(sto, _nn(ko, qo), lambda p: _nn(vT_ref[MLA_V_DIM:2 * MLA_V_DIM, 0:new], p))
    o_ref[0:MLA_V_DIM, :] = _normalized(ste).astype(BF)
    o_ref[MLA_V_DIM:2 * MLA_V_DIM, :] = _normalized(sto).astype(BF)


def _mla_decode(s, ckn, cr, cvT, new):
    nb, past, _ = ckn.shape
    pair_feat = lambda rows, cols: pl.BlockSpec((None, rows, cols), lambda b, h: (b, h, 0))
    return pl.pallas_call(
        functools.partial(_mla_decode_kernel, past=past, new=new),
        grid=(nb, MLA_HEADS // 2),
        in_specs=[pair_feat(2 * LANES, DEC_PAD),
                  pl.BlockSpec((None, past, LANES), lambda b, h: (b, 0, h)),
                  pl.BlockSpec((None, past, LANES), lambda b, h: (b, 0, 0)),
                  pair_feat(2 * MLA_V_DIM, past),
                  pl.BlockSpec((None, DEC_PAD, LANES), lambda b, h: (b, 0, h)),
                  pl.BlockSpec((None, DEC_PAD, LANES), lambda b, h: (b, 0, 0)),
                  pair_feat(2 * MLA_V_DIM, DEC_PAD)],
        out_specs=pair_feat(2 * MLA_V_DIM, DEC_PAD),
        out_shape=jax.ShapeDtypeStruct((nb, D_MODEL, DEC_PAD), BF),
        compiler_params=_params(2), name="mla_decode",
    )(s["qTm"], ckn, cr, cvT, s["kn"], s["r"], s["vTm"])


def _post_kernel(x_ref, oTd_ref, oTm_ref, gmix_ref, wgT_ref, wodT_ref, womT_ref, woutT_ref,
                 gmlp_ref, wup_ref, wdown_ref, gfin_ref, y_ref):
    x = x_ref[...]
    hb = _rms(x, gmix_ref[...]).astype(BF)
    gT = jax.nn.sigmoid(_nt(wgT_ref[...], hb))
    mergedT = (gT[0:D_MODEL, :] * _nn(wodT_ref[...], oTd_ref[...])
               + gT[D_MODEL:2 * D_MODEL, :] * _nn(womT_ref[...], oTm_ref[...]))
    x1 = x + _nn(woutT_ref[...], mergedT.astype(BF)).T
    u = jnp.maximum(_nn(_rms(x1, gmlp_ref[...]).astype(BF), wup_ref[...]), 0.0)
    x2 = x1 + _nn((u * u).astype(BF), wdown_ref[...])
    y_ref[...] = _rms(x2, gfin_ref[...])


def _post(x, oTd, oTm, w, tm):
    nb, t, _ = x.shape
    tok = pl.BlockSpec((None, tm, D_MODEL), lambda b, i: (b, i, 0))
    feat = pl.BlockSpec((None, D_MODEL, tm), lambda b, i: (b, 0, i))
    weights = [w["gmix"], w["wgT"], w["wodT"], w["womT"], w["woutT"], w["gmlp"], w["wup"],
               w["wdown"], w["gfin"]]
    return pl.pallas_call(
        _post_kernel, grid=(nb, t // tm),
        in_specs=[tok, feat, feat] + [_const_spec(a.shape) for a in weights],
        out_specs=tok, out_shape=jax.ShapeDtypeStruct((nb, t, D_MODEL), F32),
        compiler_params=_params(2), name="post",
    )(x, oTd, oTm, *weights)


def _prep_weights(norm_mix, w_in, mla_q_norm, mla_w_uq, mla_kv_norm, mla_w_uk, mla_w_uv, w_o_diff,
                  w_o_mla, w_out, norm_mlp, w_up, w_down, norm_final):
    d = D_MODEL
    o_cq = 3 * d
    o_ckv = o_cq + MLA_Q_LORA
    o_kr = o_ckv + MLA_KV_LORA
    o_g = o_kr + MLA_ROPE_DIM
    half = MLA_ROPE_DIM // 2
    rot = lambda a: jnp.concatenate([-a[..., half:], a[..., :half]], axis=-1)
    zpad = lambda a, n: jnp.zeros(a.shape[:-1] + (n,), a.dtype)

    wkr = w_in[:, o_kr:o_g]
    place = lambda a: jnp.concatenate([a, zpad(a, 32), a, zpad(a, 32)], axis=-1)
    wlat = jnp.concatenate([w_in[:, o_cq:o_kr], place(wkr), place(rot(wkr))], axis=-1)

    nope = mla_w_uq[:, :, :MLA_NOPE_DIM]
    rope = mla_w_uq[:, :, MLA_NOPE_DIM:]

    def arrange(n, r):
        even = jnp.concatenate([n[:, 0::2], r[:, 0::2], zpad(r[:, 0::2], 32)], axis=-1)
        odd = jnp.concatenate([r[:, 1::2], zpad(r[:, 1::2], 32), n[:, 1::2]], axis=-1)
        both = jnp.stack([even, odd], axis=2)
        return both.reshape(MLA_Q_LORA, MLA_HEADS * LANES)

    row = lambda a: a.reshape(1, -1).astype(F32)
    return {
        "gmix": row(norm_mix), "gq": row(mla_q_norm), "gkv": row(mla_kv_norm),
        "gmlp": row(norm_mlp), "gfin": row(norm_final),
        "wqT": (w_in[:, 0:d] * DIFF_HEAD_DIM ** -0.5).T.astype(BF),
        "wk": w_in[:, d:2 * d].astype(BF),
        "wv": w_in[:, 2 * d:3 * d].astype(BF),
        "wvT": w_in[:, 2 * d:3 * d].T.astype(BF),
        "wlat": wlat.astype(BF),
        "wgT": w_in[:, o_g:].T.astype(BF),
        "wuqaT": arrange(nope, rope).T.astype(BF),
        "wuqbT": arrange(jnp.zeros_like(nope), rot(rope)).T.astype(BF),
        "wuk": mla_w_uk.reshape(MLA_KV_LORA, d).astype(BF),
        "wuvT": mla_w_uv.reshape(MLA_KV_LORA, d).T.astype(BF),
        "wodT": w_o_diff.reshape(d, d).T.astype(BF),
        "womT": w_o_mla.reshape(d, d).T.astype(BF),
        "woutT": w_out.T.astype(BF),
        "wup": w_up.astype(BF),
        "wdown": w_down.astype(BF),
    }


def _rope_tables(pos):
    half = MLA_ROPE_DIM // 2
    inv = jnp.power(ROPE_THETA, -jnp.arange(half, dtype=F32) * 2.0 / MLA_ROPE_DIM)
    ang = pos.astype(F32)[:, None] * inv[None, :]
    cos2 = jnp.tile(jnp.cos(ang), (1, 2))
    sin2 = jnp.tile(jnp.sin(ang), (1, 2))
    t = pos.shape[0]
    z32 = jnp.zeros((t, 32), F32)
    place = lambda a: jnp.concatenate([a, z32, a, z32], axis=1)
    scale = (MLA_NOPE_DIM + MLA_ROPE_DIM) ** -0.5
    ones = jnp.ones((t, MLA_NOPE_DIM), F32)
    z64 = jnp.zeros((t, MLA_NOPE_DIM), F32)
    ct = jnp.concatenate([ones, cos2, z32, cos2, z32, ones], axis=1) * scale
    st = jnp.concatenate([z64, sin2, z32, sin2, z32, z64], axis=1) * scale
    return {"cosp": place(cos2), "sinp": place(sin2), "ct": ct.T, "st": st.T}


def _score_tiles(rel_bias, k_pos_tiles, q_pos, shift_bucket):
    k_pos = jnp.stack(k_pos_tiles)
    rel = k_pos[:, :, None] - q_pos[None, None, :]
    mask = (k_pos // CHUNK)[:, :, None] <= (q_pos // CHUNK)[None, None, :]
    return _bias_tiles(rel_bias, _t5_bucket(rel).astype(jnp.int32), mask.astype(jnp.int32), shift_bucket)


def kernel(x_prompt, x_sample, cache_diff_k, cache_diff_v, cache_mla_ckv, cache_mla_krope, rel_bias,
           norm_mix, w_in, lam_q1, lam_k1, lam_q2, lam_k2, diff_subln, mla_q_norm, mla_w_uq, mla_kv_norm,
           mla_w_uk, mla_w_uv, w_o_diff, w_o_mla, w_out, norm_mlp, w_up, w_down, norm_final):
    assert norm_mix.shape[0] == 1, "single-layer model"
    nb, t, _ = x_prompt.shape
    ns, new, _ = x_sample.shape
    past = cache_diff_k.shape[2]
    assert t % TQ == 0 and past % CACHE_BLK == 0 and new <= DEC_PAD and new % 16 == 0
    assert past % CHUNK == 0 and new <= CHUNK

    w = _prep_weights(norm_mix[0], w_in[0], mla_q_norm[0], mla_w_uq[0], mla_kv_norm[0], mla_w_uk[0],
                      mla_w_uv[0], w_o_diff[0], w_o_mla[0], w_out[0], norm_mlp[0], w_up[0], w_down[0],
                      norm_final)
    lamp = jnp.stack([lam_q1[0], lam_k1[0], lam_q2[0], lam_k2[0]]).astype(F32)
    g_sub = diff_subln[0].reshape(DIFF_V_DIM, 1).astype(F32)

    pos_p = jnp.arange(t, dtype=jnp.int32)
    p = _project(x_prompt, w, _rope_tables(pos_p), tm=256)
    blk = jnp.arange(TQ, dtype=jnp.int32)
    tiles_p = _score_tiles(rel_bias, [TQ + blk, blk], TQ + blk, _far_bucket())
    mask_p = jnp.where((blk // CHUNK)[:, None] <= (blk // CHUNK)[None, :], 0.0, NEG_INF).astype(F32)
    oTd = _diff_prompt(p, tiles_p, lamp, g_sub)
    oTm = _mla_prompt(p, mask_p)
    y_prompt = _post(x_prompt, oTd, oTm, w, tm=256)

    xs = jnp.pad(x_sample, ((0, 0), (0, DEC_PAD - new), (0, 0)))
    pos_s = past + jnp.arange(DEC_PAD, dtype=jnp.int32)
    s = _project(xs, w, _rope_tables(pos_s), tm=DEC_PAD)
    bias_c = _score_tiles(rel_bias, [jnp.arange(past, dtype=jnp.int32)], pos_s, None)
    bias_n = _score_tiles(rel_bias, [pos_s[:new]], pos_s, None)
    ck = cache_diff_k[0].reshape(ns, past, D_MODEL)
    cv = cache_diff_v[0].reshape(ns, past, D_MODEL)
    oTd_s = _diff_decode(s, ck, cv, bias_c, bias_n, lamp, g_sub, new)
    eye = jnp.eye(MLA_ROPE_DIM, dtype=BF)
    z = jnp.zeros_like(eye)
    place = jnp.concatenate([eye, z, eye, z], axis=1)
    ckn, cvT, cr = _cache_prep(cache_mla_ckv[0], cache_mla_krope[0], w, place)
    oTm_s = _mla_decode(s, ckn, cr, cvT, new)
    y_sample = _post(xs, oTd_s, oTm_s, w, tm=DEC_PAD)[:, :new]

    heads = lambda a, n: a.reshape((1,) + a.shape[:2] + (DIFF_HEADS, n))
    return (y_prompt, y_sample,
            heads(p["kf"], 2 * DIFF_HEAD_DIM), heads(p["vf"], DIFF_V_DIM), p["ckv"][None], p["kr"][None],
            heads(s["kf"][:, :new], 2 * DIFF_HEAD_DIM), heads(s["vf"][:, :new], DIFF_V_DIM),
            s["ckv"][:, :new][None], s["kr"][:, :new][None])
```

```python
import functools
import math

import numpy as np
import jax
import jax.numpy as jnp
from jax import lax
from jax.experimental import pallas as pl
from jax.experimental.pallas import tpu as pltpu

D_MODEL = 1024
CHUNK = 64
DIFF_HEADS = 8
DIFF_HEAD_DIM = 64
DIFF_V_DIM = 128
MLA_HEADS = 16
MLA_NOPE_DIM = 64
MLA_ROPE_DIM = 32
MLA_V_DIM = 64
MLA_Q_LORA = 256
MLA_KV_LORA = 256
D_FF = 4 * D_MODEL
NUM_BUCKETS = 32
MAX_DISTANCE = 128
ROPE_THETA = 10000.0
EPS = 1e-6
NEG_INF = -1e30
LAMBDA_INIT = 0.8 - 0.6 * math.exp(-0.3 * 0)

BF = jnp.bfloat16
F32 = jnp.float32

LANES = 128
TQ = 256
TK = 256
DEC_PAD = LANES
CACHE_BLK = 512
VMEM_LIMIT = 56 * 1024 * 1024
LOG2E = math.log2(math.e)
ONES_ROWS = 16


def _nn(a, b):
    return jnp.dot(a, b, preferred_element_type=F32)


def _nt(a, b):
    return lax.dot_general(a, b, (((1,), (1,)), ((), ())), preferred_element_type=F32)


def _tn(a, b):
    return lax.dot_general(a, b, (((0,), (0,)), ((), ())), preferred_element_type=F32)


def _rms(x, g):
    return x * lax.rsqrt(jnp.mean(x * x, axis=-1, keepdims=True) + EPS) * g


def _const_spec(shape):
    zeros = (0,) * len(shape)
    return pl.BlockSpec(shape, lambda *_: zeros, pipeline_mode=pl.Buffered(1))


def _params(n_axes):
    return pltpu.CompilerParams(dimension_semantics=("arbitrary",) * n_axes,
                                vmem_limit_bytes=VMEM_LIMIT)


def _bias_kernel(tab_ref, bkt_ref, msk_ref, o_ref, *, shift_bucket):
    h = pl.program_id(0)
    bkt = bkt_ref[...]
    acc = jnp.zeros(bkt.shape, F32)
    for b in range(NUM_BUCKETS):
        acc = jnp.where(bkt == b, tab_ref[b, h], acc)
    if shift_bucket is not None:
        acc = acc - tab_ref[shift_bucket, h]
    o_ref[...] = jnp.where(msk_ref[...] != 0, acc * LOG2E, NEG_INF)


def _bias_tiles(rel_bias, bucket, mask, shift_bucket):
    n, r, c = bucket.shape
    return pl.pallas_call(
        functools.partial(_bias_kernel, shift_bucket=shift_bucket),
        grid=(DIFF_HEADS,),
        in_specs=[pl.BlockSpec(memory_space=pltpu.SMEM),
                  pl.BlockSpec((n, r, c), lambda h: (0, 0, 0)),
                  pl.BlockSpec((n, r, c), lambda h: (0, 0, 0))],
        out_specs=pl.BlockSpec((None, n, r, c), lambda h: (h, 0, 0, 0)),
        out_shape=jax.ShapeDtypeStruct((DIFF_HEADS, n, r, c), F32),
        compiler_params=_params(1),
        name="bias_tiles",
    )(rel_bias, bucket, mask)


def _t5_bucket(rel):
    half = NUM_BUCKETS // 2
    max_exact = half // 2
    n = jnp.abs(rel)
    nf = jnp.maximum(n, max_exact).astype(F32)
    large = max_exact + (jnp.log(nf / max_exact) / math.log(MAX_DISTANCE / max_exact)
                         * (half - max_exact)).astype(jnp.int32)
    large = jnp.minimum(large, half - 1)
    return jnp.where(rel > 0, half, 0) + jnp.where(n < max_exact, n, large)


def _far_bucket():
    half = NUM_BUCKETS // 2
    max_exact = half // 2
    assert max_exact + math.log(TK / max_exact) / math.log(MAX_DISTANCE / max_exact) * (half - max_exact) >= half
    return half - 1


def _proj_kernel(x_ref, gmix_ref, wqT_ref, wk_ref, wv_ref, wvT_ref, wlat_ref, gq_ref, gkv_ref,
                 wuqaT_ref, wuqbT_ref, wuk_ref, wuvT_ref, cosp_ref, sinp_ref, ct_ref, st_ref,
                 qTd_ref, kf_ref, kb_ref, vf_ref, vTd_ref, qTm_ref, ckv_ref, kn_ref, vTm_ref,
                 r_ref, kr_ref):
    hb = _rms(x_ref[...], gmix_ref[...]).astype(BF)
    qTd_ref[...] = (_nt(wqT_ref[...], hb) * (DIFF_HEAD_DIM ** -0.5 * LOG2E)).astype(BF)
    k = _nn(hb, wk_ref[...])
    kf_ref[...] = k
    kb_ref[...] = k.astype(BF)
    vf_ref[...] = _nn(hb, wv_ref[...])
    vTd_ref[...] = _nt(wvT_ref[...], hb).astype(BF)

    lat = _nn(hb, wlat_ref[...])
    cq = _rms(lat[:, 0:MLA_Q_LORA], gq_ref[...]).astype(BF)
    qa = _nt(wuqaT_ref[...], cq)
    qb = _nt(wuqbT_ref[...], cq)
    ct = ct_ref[...]
    st = st_ref[...]
    pair = 2 * LANES
    for p in range(MLA_HEADS // 2):
        sl = slice(p * pair, (p + 1) * pair)
        qTm_ref[sl, :] = (qa[sl, :] * ct + qb[sl, :] * st).astype(BF)

    ckv = _rms(lat[:, MLA_Q_LORA:MLA_Q_LORA + MLA_KV_LORA], gkv_ref[...])
    ckv_ref[...] = ckv
    cb = ckv.astype(BF)
    kn_ref[...] = _nn(cb, wuk_ref[...]).astype(BF)
    vTm_ref[...] = _nt(wuvT_ref[...], cb).astype(BF)

    o = MLA_Q_LORA + MLA_KV_LORA
    r = lat[:, o:o + LANES] * cosp_ref[...] + lat[:, o + LANES:o + 2 * LANES] * sinp_ref[...]
    r_ref[...] = r.astype(BF)
    kr_ref[...] = r[:, 0:MLA_ROPE_DIM]


def _project(x, w, tabs, tm):
    nb, t, _ = x.shape
    nt = t // tm
    tok = lambda width: pl.BlockSpec((None, tm, width), lambda b, i: (b, i, 0))
    feat = lambda rows: pl.BlockSpec((None, rows, tm), lambda b, i: (b, 0, i))
    sds = jax.ShapeDtypeStruct
    weights = [w["gmix"], w["wqT"], w["wk"], w["wv"], w["wvT"], w["wlat"], w["gq"], w["gkv"],
               w["wuqaT"], w["wuqbT"], w["wuk"], w["wuvT"]]
    in_specs = ([tok(D_MODEL)] + [_const_spec(a.shape) for a in weights]
                + [pl.BlockSpec((tm, LANES), lambda b, i: (i, 0)),
                   pl.BlockSpec((tm, LANES), lambda b, i: (i, 0)),
                   pl.BlockSpec((2 * LANES, tm), lambda b, i: (0, i)),
                   pl.BlockSpec((2 * LANES, tm), lambda b, i: (0, i))])
    out_shape = [sds((nb, D_MODEL, t), BF), sds((nb, t, D_MODEL), F32), sds((nb, t, D_MODEL), BF),
                 sds((nb, t, D_MODEL), F32), sds((nb, D_MODEL, t), BF),
                 sds((nb, MLA_HEADS * LANES, t), BF), sds((nb, t, MLA_KV_LORA), F32),
                 sds((nb, t, D_MODEL), BF), sds((nb, D_MODEL, t), BF),
                 sds((nb, t, LANES), BF), sds((nb, t, MLA_ROPE_DIM), F32)]
    out_specs = [feat(D_MODEL), tok(D_MODEL), tok(D_MODEL), tok(D_MODEL), feat(D_MODEL),
                 feat(MLA_HEADS * LANES), tok(MLA_KV_LORA), tok(D_MODEL), feat(D_MODEL),
                 tok(LANES), tok(MLA_ROPE_DIM)]
    outs = pl.pallas_call(
        _proj_kernel, grid=(nb, nt), in_specs=in_specs, out_specs=out_specs, out_shape=out_shape,
        compiler_params=_params(2), name="project",
    )(x, *weights, tabs["cosp"], tabs["sinp"], tabs["ct"], tabs["st"])
    names = ["qTd", "kf", "kb", "vf", "vTd", "qTm", "ckv", "kn", "vTm", "r", "kr"]
    return dict(zip(names, outs))


def _flash_step(state, s, pv):
    bm = jnp.max(s, axis=0, keepdims=True)
    if state is None:
        p = jnp.exp2(s - bm)
        return bm, jnp.sum(p, axis=0, keepdims=True), pv(p.astype(BF))
    m, l, acc = state
    m_new = jnp.maximum(m, bm)
    alpha = jnp.exp2(m - m_new)
    p = jnp.exp2(s - m_new)
    return m_new, alpha * l + jnp.sum(p, axis=0, keepdims=True), alpha * acc + pv(p.astype(BF))


def _normalized(state):
    _, l, acc = state
    return acc * (1.0 / l)


def _split_maps(qT):
    rows = lax.broadcasted_iota(jnp.int32, qT.shape, 0)
    qf = qT.astype(F32)
    zero = jnp.zeros_like(qf)
    return (jnp.where(rows < DIFF_HEAD_DIM, qf, zero).astype(BF),
            jnp.where(rows >= DIFF_HEAD_DIM, qf, zero).astype(BF))


def _lambda(lam_ref):
    lp = lam_ref[...]
    a = jnp.sum(lp[0:1, :] * lp[1:2, :], axis=1, keepdims=True)
    b = jnp.sum(lp[2:3, :] * lp[3:4, :], axis=1, keepdims=True)
    return jnp.exp(a) - jnp.exp(b) + LAMBDA_INIT


def _diff_finish(o1, o2, lam, g):
    o = o1 - lam * o2
    ms = jnp.mean(o * o, axis=0, keepdims=True)
    return (o * lax.rsqrt(ms + EPS) * g * (1.0 - LAMBDA_INIT)).astype(BF)


def _mla_keys(kp, r):
    lane = lax.broadcasted_iota(jnp.int32, kp.shape, 1)
    kpf = kp.astype(F32)
    rf = r.astype(F32)
    return (jnp.where(lane < MLA_NOPE_DIM, kpf, rf).astype(BF),
            jnp.where(lane >= MLA_NOPE_DIM, kpf, rf).astype(BF))


def _qslice(qb):
    return slice(qb * TQ, (qb + 1) * TQ)


def _kslice(kb):
    return slice(kb * TK, (kb + 1) * TK)


def _with_ones(va_ref, vT):
    dv, t = vT.shape
    va_ref[0:dv, :] = vT
    va_ref[dv:dv + ONES_ROWS, :] = jnp.ones((ONES_ROWS, t), BF)


def _weighted_mean(acc):
    dv = acc.shape[0] - ONES_ROWS
    return acc[0:dv, :] * (1.0 / acc[dv:dv + 1, :])


def _causal_two_pass(nq, prep, scores, va_refs, s_ref, p_ref, emit):
    n_maps = len(va_refs)

    def score_pass(qb):
        ops = prep(qb)
        m8 = [None] * n_maps
        for kb in range(qb + 1):
            for i, s in enumerate(scores(ops, qb, kb)):
                s_ref[qb % 2, i, _kslice(kb), :] = s
                c = jnp.max(s.reshape(TK // 8, 8, TQ), axis=0)
                m8[i] = c if m8[i] is None else jnp.maximum(m8[i], c)
        return [jnp.max(m, axis=0, keepdims=True) for m in m8]

    def softmax_pass(qb, ms):
        buf = qb % 2
        klen = (qb + 1) * TK
        for kb in range(qb + 1):
            for i in range(n_maps):
                p_ref[buf, i, _kslice(kb), :] = jnp.exp2(s_ref[buf, i, _kslice(kb), :] - ms[i]).astype(BF)
        emit(qb, [_nn(va_refs[i][:, 0:klen], p_ref[buf, i, 0:klen, :]) for i in range(n_maps)])

    ms = score_pass(0)
    for qb in range(nq):
        ms_next = score_pass(qb + 1) if qb + 1 < nq else None
        softmax_pass(qb, ms)
        ms = ms_next


def _diff_prompt_kernel(qT_ref, k_ref, vT_ref, bias_ref, lam_ref, g_ref, o_ref, va_ref, s_ref, p_ref, *, t):
    lam = _lambda(lam_ref)
    g = g_ref[...]
    _with_ones(va_ref, vT_ref[...])

    def prep(qb):
        return _split_maps(qT_ref[:, _qslice(qb)])

    def scores(ops, qb, kb):
        kblk = k_ref[_kslice(kb), :]
        out = [_nn(kblk, q) for q in ops]
        if kb >= qb - 1:
            out = [s + bias_ref[qb - kb] for s in out]
        return out

    def emit(qb, accs):
        o_ref[:, _qslice(qb)] = _diff_finish(_weighted_mean(accs[0]), _weighted_mean(accs[1]), lam, g)

    _causal_two_pass(t // TQ, prep, scores, [va_ref, va_ref], s_ref, p_ref, emit)


def _score_scratch(n_maps, t):
    return [pltpu.VMEM((2, n_maps, t, TQ), F32), pltpu.VMEM((2, n_maps, t, TQ), BF)]


def _diff_prompt(p, bias, lamp, g):
    nb, _, t = p["qTd"].shape
    head_feat = pl.BlockSpec((None, DIFF_V_DIM, t), lambda b, h: (b, h, 0))
    return pl.pallas_call(
        functools.partial(_diff_prompt_kernel, t=t),
        grid=(nb, DIFF_HEADS),
        in_specs=[head_feat,
                  pl.BlockSpec((None, t, DIFF_V_DIM), lambda b, h: (b, 0, h)),
                  head_feat,
                  pl.BlockSpec((None, 2, TK, TQ), lambda b, h: (h, 0, 0, 0)),
                  _const_spec(lamp.shape), _const_spec(g.shape)],
        out_specs=head_feat,
        out_shape=jax.ShapeDtypeStruct((nb, D_MODEL, t), BF),
        scratch_shapes=[pltpu.VMEM((DIFF_V_DIM + ONES_ROWS, t), BF)] + _score_scratch(2, t),
        compiler_params=_params(2), name="diff_prompt",
    )(p["qTd"], p["kb"], p["vTd"], bias, lamp, g)


def _mla_prompt_kernel(qT_ref, kp_ref, r_ref, vT_ref, mask_ref, o_ref, ke_ref, ko_ref, vae_ref, vao_ref,
                       s_ref, p_ref, *, t):
    ke, ko = _mla_keys(kp_ref[...], r_ref[...])
    ke_ref[...] = ke
    ko_ref[...] = ko
    _with_ones(vae_ref, vT_ref[0:MLA_V_DIM, :])
    _with_ones(vao_ref, vT_ref[MLA_V_DIM:2 * MLA_V_DIM, :])

    def prep(qb):
        return qT_ref[0:LANES, _qslice(qb)], qT_ref[LANES:2 * LANES, _qslice(qb)]

    def scores(ops, qb, kb):
        out = [_nn(ke_ref[_kslice(kb), :], ops[0]), _nn(ko_ref[_kslice(kb), :], ops[1])]
        if kb == qb:
            out = [s + mask_ref[...] for s in out]
        return out

    def emit(qb, accs):
        o_ref[0:MLA_V_DIM, _qslice(qb)] = _weighted_mean(accs[0]).astype(BF)
        o_ref[MLA_V_DIM:2 * MLA_V_DIM, _qslice(qb)] = _weighted_mean(accs[1]).astype(BF)

    _causal_two_pass(t // TQ, prep, scores, [vae_ref, vao_ref], s_ref, p_ref, emit)


def _mla_prompt(p, mask):
    nb, _, t = p["qTm"].shape
    return pl.pallas_call(
        functools.partial(_mla_prompt_kernel, t=t),
        grid=(nb, MLA_HEADS // 2),
        in_specs=[pl.BlockSpec((None, 2 * LANES, t), lambda b, h: (b, h, 0)),
                  pl.BlockSpec((None, t, LANES), lambda b, h: (b, 0, h)),
                  pl.BlockSpec((None, t, LANES), lambda b, h: (b, 0, 0)),
                  pl.BlockSpec((None, 2 * MLA_V_DIM, t), lambda b, h: (b, h, 0)),
                  _const_spec(mask.shape)],
        out_specs=pl.BlockSpec((None, 2 * MLA_V_DIM, t), lambda b, h: (b, h, 0)),
        out_shape=jax.ShapeDtypeStruct((nb, D_MODEL, t), BF),
        scratch_shapes=[pltpu.VMEM((t, LANES), BF), pltpu.VMEM((t, LANES), BF),
                        pltpu.VMEM((MLA_V_DIM + ONES_ROWS, t), BF),
                        pltpu.VMEM((MLA_V_DIM + ONES_ROWS, t), BF)] + _score_scratch(2, t),
        compiler_params=_params(2), name="mla_prompt",
    )(p["qTm"], p["kn"], p["r"], p["vTm"], mask)


def _diff_decode_kernel(qT_ref, ck_ref, cv_ref, kn_ref, vn_ref, bc_ref, bn_ref, lam_ref, g_ref, o_ref,
                        *, past, new):
    lam = _lambda(lam_ref)
    q1, q2 = _split_maps(qT_ref[...])
    st1 = st2 = None
    for c in range(past // CACHE_BLK):
        cs = slice(c * CACHE_BLK, (c + 1) * CACHE_BLK)
        kblk = ck_ref[cs, :].astype(BF)
        b = bc_ref[cs, :]
        pv = lambda p, cs=cs: _tn(cv_ref[cs, :].astype(BF), p)
        st1 = _flash_step(st1, _nn(kblk, q1) + b, pv)
        st2 = _flash_step(st2, _nn(kblk, q2) + b, pv)
    kblk = kn_ref[0:new, :]
    b = bn_ref[...]
    pv = lambda p: _tn(vn_ref[0:new, :].astype(BF), p)
    st1 = _flash_step(st1, _nn(kblk, q1) + b, pv)
    st2 = _flash_step(st2, _nn(kblk, q2) + b, pv)
    o_ref[...] = _diff_finish(_normalized(st1), _normalized(st2), lam, g_ref[...])


def _diff_decode(s, cache_k, cache_v, bias_c, bias_n, lamp, g, new):
    nb, past, _ = cache_k.shape
    head_feat = pl.BlockSpec((None, DIFF_V_DIM, DEC_PAD), lambda b, h: (b, h, 0))
    head_tok = lambda rows: pl.BlockSpec((None, rows, DIFF_V_DIM), lambda b, h: (b, 0, h))
    return pl.pallas_call(
        functools.partial(_diff_decode_kernel, past=past, new=new),
        grid=(nb, DIFF_HEADS),
        in_specs=[head_feat, head_tok(past), head_tok(past), head_tok(DEC_PAD), head_tok(DEC_PAD),
                  pl.BlockSpec((None, None, past, DEC_PAD), lambda b, h: (h, 0, 0, 0)),
                  pl.BlockSpec((None, None, new, DEC_PAD), lambda b, h: (h, 0, 0, 0)),
                  _const_spec(lamp.shape), _const_spec(g.shape)],
        out_specs=head_feat,
        out_shape=jax.ShapeDtypeStruct((nb, D_MODEL, DEC_PAD), BF),
        compiler_params=_params(2), name="diff_decode",
    )(s["qTd"], cache_k, cache_v, s["kb"], s["vf"], bias_c, bias_n, lamp, g)


def _cache_prep_kernel(ckv_ref, kr_ref, wuk_ref, wuvT_ref, place_ref, kn_ref, vT_ref, r_ref):
    cb = ckv_ref[...].astype(BF)
    kn_ref[...] = _nn(cb, wuk_ref[...]).astype(BF)
    vT_ref[...] = _nt(wuvT_ref[...], cb).astype(BF)
    r_ref[...] = _nn(kr_ref[...].astype(BF), place_ref[...]).astype(BF)


def _cache_prep(cache_ckv, cache_kr, w, place):
    nb, past, _ = cache_ckv.shape
    tok = lambda width: pl.BlockSpec((None, CACHE_BLK, width), lambda b, i: (b, i, 0))
    sds = jax.ShapeDtypeStruct
    return pl.pallas_call(
        _cache_prep_kernel, grid=(nb, past // CACHE_BLK),
        in_specs=[tok(MLA_KV_LORA), tok(MLA_ROPE_DIM), _const_spec(w["wuk"].shape),
                  _const_spec(w["wuvT"].shape), _const_spec(place.shape)],
        out_specs=[tok(D_MODEL), pl.BlockSpec((None, D_MODEL, CACHE_BLK), lambda b, i: (b, 0, i)),
                   tok(LANES)],
        out_shape=[sds((nb, past, D_MODEL), BF), sds((nb, D_MODEL, past), BF), sds((nb, past, LANES), BF)],
        compiler_params=_params(2), name="cache_prep",
    )(cache_ckv, cache_kr, w["wuk"], w["wuvT"], place)


def _mla_decode_kernel(qT_ref, ckn_ref, cr_ref, cvT_ref, kn_ref, r_ref, vT_ref, o_ref, *, past, new):
    qe = qT_ref[0:LANES, :]
    qo = qT_ref[LANES:2 * LANES, :]
    ste = sto = None
    for c in range(past // CACHE_BLK):
        cs = slice(c * CACHE_BLK, (c + 1) * CACHE_BLK)
        ke, ko = _mla_keys(ckn_ref[cs, :], cr_ref[cs, :])
        ste = _flash_step(ste, _nn(ke, qe), lambda p, cs=cs: _nn(cvT_ref[0:MLA_V_DIM, cs], p))
        sto = _flash_step(sto, _nn(ko, qo), lambda p, cs=cs: _nn(cvT_ref[MLA_V_DIM:2 * MLA_V_DIM, cs], p))
    ke, ko = _mla_keys(kn_ref[0:new, :], r_ref[0:new, :])
    ste = _flash_step(ste, _nn(ke, qe), lambda p: _nn(vT_ref[0:MLA_V_DIM, 0:new], p))
    sto = _flash_step(sto, _nn(ko, qo), lambda p: _nn(vT_ref[MLA_V_DIM:2 * MLA_V_DIM, 0:new], p))
    o_ref[0:MLA_V_DIM, :] = _normalized(ste).astype(BF)
    o_ref[MLA_V_DIM:2 * MLA_V_DIM, :] = _normalized(sto).astype(BF)


def _mla_decode(s, ckn, cr, cvT, new):
    nb, past, _ = ckn.shape
    pair_feat = lambda rows, cols: pl.BlockSpec((None, rows, cols), lambda b, h: (b, h, 0))
    return pl.pallas_call(
        functools.partial(_mla_decode_kernel, past=past, new=new),
        grid=(nb, MLA_HEADS // 2),
        in_specs=[pair_feat(2 * LANES, DEC_PAD),
                  pl.BlockSpec((None, past, LANES), lambda b, h: (b, 0, h)),
                  pl.BlockSpec((None, past, LANES), lambda b, h: (b, 0, 0)),
                  pair_feat(2 * MLA_V_DIM, past),
                  pl.BlockSpec((None, DEC_PAD, LANES), lambda b, h: (b, 0, h)),
                  pl.BlockSpec((None, DEC_PAD, LANES), lambda b, h: (b, 0, 0)),
                  pair_feat(2 * MLA_V_DIM, DEC_PAD)],
        out_specs=pair_feat(2 * MLA_V_DIM, DEC_PAD),
        out_shape=jax.ShapeDtypeStruct((nb, D_MODEL, DEC_PAD), BF),
        compiler_params=_params(2), name="mla_decode",
    )(s["qTm"], ckn, cr, cvT, s["kn"], s["r"], s["vTm"])


def _post_kernel(x_ref, oTd_ref, oTm_ref, gmix_ref, wgT_ref, wodT_ref, womT_ref, woutT_ref,
                 gmlp_ref, wup_ref, wdown_ref, gfin_ref, y_ref):
    x = x_ref[...]
    hb = _rms(x, gmix_ref[...]).astype(BF)
    gT = jax.nn.sigmoid(_nt(wgT_ref[...], hb))
    mergedT = (gT[0:D_MODEL, :] * _nn(wodT_ref[...], oTd_ref[...])
               + gT[D_MODEL:2 * D_MODEL, :] * _nn(womT_ref[...], oTm_ref[...]))
    x1 = x + _nn(woutT_ref[...], mergedT.astype(BF)).T
    u = jnp.maximum(_nn(_rms(x1, gmlp_ref[...]).astype(BF), wup_ref[...]), 0.0)
    x2 = x1 + _nn((u * u).astype(BF), wdown_ref[...])
    y_ref[...] = _rms(x2, gfin_ref[...])


def _post(x, oTd, oTm, w, tm):
    nb, t, _ = x.shape
    tok = pl.BlockSpec((None, tm, D_MODEL), lambda b, i: (b, i, 0))
    feat = pl.BlockSpec((None, D_MODEL, tm), lambda b, i: (b, 0, i))
    weights = [w["gmix"], w["wgT"], w["wodT"], w["womT"], w["woutT"], w["gmlp"], w["wup"],
               w["wdown"], w["gfin"]]
    return pl.pallas_call(
        _post_kernel, grid=(nb, t // tm),
        in_specs=[tok, feat, feat] + [_const_spec(a.shape) for a in weights],
        out_specs=tok, out_shape=jax.ShapeDtypeStruct((nb, t, D_MODEL), F32),
        compiler_params=_params(2), name="post",
    )(x, oTd, oTm, *weights)


def _prep_weights(norm_mix, w_in, mla_q_norm, mla_w_uq, mla_kv_norm, mla_w_uk, mla_w_uv, w_o_diff,
                  w_o_mla, w_out, norm_mlp, w_up, w_down, norm_final):
    d = D_MODEL
    o_cq = 3 * d
    o_ckv = o_cq + MLA_Q_LORA
    o_kr = o_ckv + MLA_KV_LORA
    o_g = o_kr + MLA_ROPE_DIM
    half = MLA_ROPE_DIM // 2
    rot = lambda a: jnp.concatenate([-a[..., half:], a[..., :half]], axis=-1)
    zpad = lambda a, n: jnp.zeros(a.shape[:-1] + (n,), a.dtype)

    wkr = w_in[:, o_kr:o_g]
    place = lambda a: jnp.concatenate([a, zpad(a, 32), a, zpad(a, 32)], axis=-1)
    wlat = jnp.concatenate([w_in[:, o_cq:o_kr], place(wkr), place(rot(wkr))], axis=-1)

    nope = mla_w_uq[:, :, :MLA_NOPE_DIM]
    rope = mla_w_uq[:, :, MLA_NOPE_DIM:]

    def arrange(n, r):
        even = jnp.concatenate([n[:, 0::2], r[:, 0::2], zpad(r[:, 0::2], 32)], axis=-1)
        odd = jnp.concatenate([r[:, 1::2], zpad(r[:, 1::2], 32), n[:, 1::2]], axis=-1)
        both = jnp.stack([even, odd], axis=2)
        return both.reshape(MLA_Q_LORA, MLA_HEADS * LANES)

    row = lambda a: a.reshape(1, -1).astype(F32)
    return {
        "gmix": row(norm_mix), "gq": row(mla_q_norm), "gkv": row(mla_kv_norm),
        "gmlp": row(norm_mlp), "gfin": row(norm_final),
        "wqT": w_in[:, 0:d].T.astype(BF),
        "wk": w_in[:, d:2 * d].astype(BF),
        "wv": w_in[:, 2 * d:3 * d].astype(BF),
        "wvT": w_in[:, 2 * d:3 * d].T.astype(BF),
        "wlat": wlat.astype(BF),
        "wgT": w_in[:, o_g:].T.astype(BF),
        "wuqaT": arrange(nope, rope).T.astype(BF),
        "wuqbT": arrange(jnp.zeros_like(nope), rot(rope)).T.astype(BF),
        "wuk": mla_w_uk.reshape(MLA_KV_LORA, d).astype(BF),
        "wuvT": mla_w_uv.reshape(MLA_KV_LORA, d).T.astype(BF),
        "wodT": w_o_diff.reshape(d, d).T.astype(BF),
        "womT": w_o_mla.reshape(d, d).T.astype(BF),
        "woutT": w_out.T.astype(BF),
        "wup": w_up.astype(BF),
        "wdown": w_down.astype(BF),
    }


def _rope_tables(pos):
    half = MLA_ROPE_DIM // 2
    inv = jnp.power(ROPE_THETA, -jnp.arange(half, dtype=F32) * 2.0 / MLA_ROPE_DIM)
    ang = pos.astype(F32)[:, None] * inv[None, :]
    cos2 = jnp.tile(jnp.cos(ang), (1, 2))
    sin2 = jnp.tile(jnp.sin(ang), (1, 2))
    t = pos.shape[0]
    z32 = jnp.zeros((t, 32), F32)
    place = lambda a: jnp.concatenate([a, z32, a, z32], axis=1)
    scale = (MLA_NOPE_DIM + MLA_ROPE_DIM) ** -0.5 * LOG2E
    ones = jnp.ones((t, MLA_NOPE_DIM), F32)
    z64 = jnp.zeros((t, MLA_NOPE_DIM), F32)
    ct = jnp.concatenate([ones, cos2, z32, cos2, z32, ones], axis=1) * scale
    st = jnp.concatenate([z64, sin2, z32, sin2, z32, z64], axis=1) * scale
    return {"cosp": place(cos2), "sinp": place(sin2), "ct": ct.T, "st": st.T}


def _score_tiles(rel_bias, k_pos_tiles, q_pos, shift_bucket):
    k_pos = jnp.stack(k_pos_tiles)
    rel = k_pos[:, :, None] - q_pos[None, None, :]
    mask = (k_pos // CHUNK)[:, :, None] <= (q_pos // CHUNK)[None, None, :]
    return _bias_tiles(rel_bias, _t5_bucket(rel).astype(jnp.int32), mask.astype(jnp.int32), shift_bucket)


def kernel(x_prompt, x_sample, cache_diff_k, cache_diff_v, cache_mla_ckv, cache_mla_krope, rel_bias,
           norm_mix, w_in, lam_q1, lam_k1, lam_q2, lam_k2, diff_subln, mla_q_norm, mla_w_uq, mla_kv_norm,
           mla_w_uk, mla_w_uv, w_o_diff, w_o_mla, w_out, norm_mlp, w_up, w_down, norm_final):
    assert norm_mix.shape[0] == 1, "single-layer model"
    nb, t, _ = x_prompt.shape
    ns, new, _ = x_sample.shape
    past = cache_diff_k.shape[2]
    assert t % TQ == 0 and past % CACHE_BLK == 0 and new <= DEC_PAD and new % 16 == 0
    assert past % CHUNK == 0 and new <= CHUNK

    w = _prep_weights(norm_mix[0], w_in[0], mla_q_norm[0], mla_w_uq[0], mla_kv_norm[0], mla_w_uk[0],
                      mla_w_uv[0], w_o_diff[0], w_o_mla[0], w_out[0], norm_mlp[0], w_up[0], w_down[0],
                      norm_final)
    lamp = jnp.stack([lam_q1[0], lam_k1[0], lam_q2[0], lam_k2[0]]).astype(F32)
    g_sub = diff_subln[0].reshape(DIFF_V_DIM, 1).astype(F32)

    pos_p = jnp.arange(t, dtype=jnp.int32)
    p = _project(x_prompt, w, _rope_tables(pos_p), tm=256)
    blk = jnp.arange(TQ, dtype=jnp.int32)
    tiles_p = _score_tiles(rel_bias, [TQ + blk, blk], TQ + blk, _far_bucket())
    mask_p = jnp.where((blk // CHUNK)[:, None] <= (blk // CHUNK)[None, :], 0.0, NEG_INF).astype(F32)
    oTd = _diff_prompt(p, tiles_p, lamp, g_sub)
    oTm = _mla_prompt(p, mask_p)
    y_prompt = _post(x_prompt, oTd, oTm, w, tm=256)

    xs = jnp.pad(x_sample, ((0, 0), (0, DEC_PAD - new), (0, 0)))
    pos_s = past + jnp.arange(DEC_PAD, dtype=jnp.int32)
    s = _project(xs, w, _rope_tables(pos_s), tm=DEC_PAD)
    bias_c = _score_tiles(rel_bias, [jnp.arange(past, dtype=jnp.int32)], pos_s, None)
    bias_n = _score_tiles(rel_bias, [pos_s[:new]], pos_s, None)
    ck = cache_diff_k[0].reshape(ns, past, D_MODEL)
    cv = cache_diff_v[0].reshape(ns, past, D_MODEL)
    oTd_s = _diff_decode(s, ck, cv, bias_c, bias_n, lamp, g_sub, new)
    eye = jnp.eye(MLA_ROPE_DIM, dtype=BF)
    z = jnp.zeros_like(eye)
    place = jnp.concatenate([eye, z, eye, z], axis=1)
    ckn, cvT, cr = _cache_prep(cache_mla_ckv[0], cache_mla_krope[0], w, place)
    oTm_s = _mla_decode(s, ckn, cr, cvT, new)
    y_sample = _post(xs, oTd_s, oTm_s, w, tm=DEC_PAD)[:, :new]

    heads = lambda a, n: a.reshape((1,) + a.shape[:2] + (DIFF_HEADS, n))
    return (y_prompt, y_sample,
            heads(p["kf"], 2 * DIFF_HEAD_DIM), heads(p["vf"], DIFF_V_DIM), p["ckv"][None], p["kr"][None],
            heads(s["kf"][:, :new], 2 * DIFF_HEAD_DIM), heads(s["vf"][:, :new], DIFF_V_DIM),
            s["ckv"][:, :new][None], s["kr"][:, :new][None])
```

```python
import functools
import math

import numpy as np
import jax
import jax.numpy as jnp
from jax import lax
from jax.experimental import pallas as pl
from jax.experimental.pallas import tpu as pltpu

D_MODEL = 1024
CHUNK = 64
DIFF_HEADS = 8
DIFF_HEAD_DIM = 64
DIFF_V_DIM = 128
MLA_HEADS = 16
MLA_NOPE_DIM = 64
MLA_ROPE_DIM = 32
MLA_V_DIM = 64
MLA_Q_LORA = 256
MLA_KV_LORA = 256
D_FF = 4 * D_MODEL
NUM_BUCKETS = 32
MAX_DISTANCE = 128
ROPE_THETA = 10000.0
EPS = 1e-6
NEG_INF = -1e30
LAMBDA_INIT = 0.8 - 0.6 * math.exp(-0.3 * 0)

BF = jnp.bfloat16
F32 = jnp.float32

LANES = 128
TQ = 256
TK = 256
DEC_PAD = LANES
CACHE_BLK = 512
VMEM_LIMIT = 56 * 1024 * 1024
LOG2E = math.log2(math.e)
ONES_ROWS = 16


def _nn(a, b):
    return jnp.dot(a, b, preferred_element_type=F32)


def _nt(a, b):
    return lax.dot_general(a, b, (((1,), (1,)), ((), ())), preferred_element_type=F32)


def _tn(a, b):
    return lax.dot_general(a, b, (((0,), (0,)), ((), ())), preferred_element_type=F32)


def _rms(x, g):
    return x * lax.rsqrt(jnp.mean(x * x, axis=-1, keepdims=True) + EPS) * g


def _const_spec(shape):
    zeros = (0,) * len(shape)
    return pl.BlockSpec(shape, lambda *_: zeros, pipeline_mode=pl.Buffered(1))


def _params(n_axes):
    return pltpu.CompilerParams(dimension_semantics=("arbitrary",) * n_axes,
                                vmem_limit_bytes=VMEM_LIMIT)


def _bias_kernel(tab_ref, bkt_ref, msk_ref, o_ref, *, shift_bucket):
    h = pl.program_id(0)
    bkt = bkt_ref[...]
    acc = jnp.zeros(bkt.shape, F32)
    for b in range(NUM_BUCKETS):
        acc = jnp.where(bkt == b, tab_ref[b, h], acc)
    if shift_bucket is not None:
        acc = acc - tab_ref[shift_bucket, h]
    o_ref[...] = jnp.where(msk_ref[...] != 0, acc * LOG2E, NEG_INF)


def _bias_tiles(rel_bias, bucket, mask, shift_bucket):
    n, r, c = bucket.shape
    return pl.pallas_call(
        functools.partial(_bias_kernel, shift_bucket=shift_bucket),
        grid=(DIFF_HEADS,),
        in_specs=[pl.BlockSpec(memory_space=pltpu.SMEM),
                  pl.BlockSpec((n, r, c), lambda h: (0, 0, 0)),
                  pl.BlockSpec((n, r, c), lambda h: (0, 0, 0))],
        out_specs=pl.BlockSpec((None, n, r, c), lambda h: (h, 0, 0, 0)),
        out_shape=jax.ShapeDtypeStruct((DIFF_HEADS, n, r, c), F32),
        compiler_params=_params(1),
        name="bias_tiles",
    )(rel_bias, bucket, mask)


def _t5_bucket(rel):
    half = NUM_BUCKETS // 2
    max_exact = half // 2
    n = jnp.abs(rel)
    nf = jnp.maximum(n, max_exact).astype(F32)
    large = max_exact + (jnp.log(nf / max_exact) / math.log(MAX_DISTANCE / max_exact)
                         * (half - max_exact)).astype(jnp.int32)
    large = jnp.minimum(large, half - 1)
    return jnp.where(rel > 0, half, 0) + jnp.where(n < max_exact, n, large)


def _far_bucket():
    half = NUM_BUCKETS // 2
    max_exact = half // 2
    assert max_exact + math.log(TK / max_exact) / math.log(MAX_DISTANCE / max_exact) * (half - max_exact) >= half
    return half - 1


def _proj_kernel(x_ref, gmix_ref, wqT_ref, wk_ref, wv_ref, wlat_ref, gq_ref, gkv_ref,
                 wuqaT_ref, wuqbT_ref, wuk_ref, wuvT_ref, cosp_ref, sinp_ref, ct_ref, st_ref,
                 qTd_ref, kf_ref, kb_ref, vf_ref, vb_ref, qTm_ref, ckv_ref, kn_ref, vTm_ref,
                 r_ref, kr_ref):
    hb = _rms(x_ref[...], gmix_ref[...]).astype(BF)
    qTd_ref[...] = (_nt(wqT_ref[...], hb) * (DIFF_HEAD_DIM ** -0.5 * LOG2E)).astype(BF)
    k = _nn(hb, wk_ref[...])
    kf_ref[...] = k
    kb_ref[...] = k.astype(BF)
    v = _nn(hb, wv_ref[...])
    vf_ref[...] = v
    vb_ref[...] = v.astype(BF)

    lat = _nn(hb, wlat_ref[...])
    cq = _rms(lat[:, 0:MLA_Q_LORA], gq_ref[...]).astype(BF)
    qa = _nt(wuqaT_ref[...], cq)
    qb = _nt(wuqbT_ref[...], cq)
    ct = ct_ref[...]
    st = st_ref[...]
    pair = 2 * LANES
    for p in range(MLA_HEADS // 2):
        sl = slice(p * pair, (p + 1) * pair)
        qTm_ref[sl, :] = (qa[sl, :] * ct + qb[sl, :] * st).astype(BF)

    ckv = _rms(lat[:, MLA_Q_LORA:MLA_Q_LORA + MLA_KV_LORA], gkv_ref[...])
    ckv_ref[...] = ckv
    cb = ckv.astype(BF)
    kn_ref[...] = _nn(cb, wuk_ref[...]).astype(BF)
    vTm_ref[...] = _nt(wuvT_ref[...], cb).astype(BF)

    o = MLA_Q_LORA + MLA_KV_LORA
    r = lat[:, o:o + LANES] * cosp_ref[...] + lat[:, o + LANES:o + 2 * LANES] * sinp_ref[...]
    r_ref[...] = r.astype(BF)
    kr_ref[...] = r[:, 0:MLA_ROPE_DIM]


def _project(x, w, tabs, tm):
    nb, t, _ = x.shape
    nt = t // tm
    tok = lambda width: pl.BlockSpec((None, tm, width), lambda b, i: (b, i, 0))
    feat = lambda rows: pl.BlockSpec((None, rows, tm), lambda b, i: (b, 0, i))
    sds = jax.ShapeDtypeStruct
    weights = [w["gmix"], w["wqT"], w["wk"], w["wv"], w["wlat"], w["gq"], w["gkv"],
               w["wuqaT"], w["wuqbT"], w["wuk"], w["wuvT"]]
    in_specs = ([tok(D_MODEL)] + [_const_spec(a.shape) for a in weights]
                + [pl.BlockSpec((tm, LANES), lambda b, i: (i, 0)),
                   pl.BlockSpec((tm, LANES), lambda b, i: (i, 0)),
                   pl.BlockSpec((2 * LANES, tm), lambda b, i: (0, i)),
                   pl.BlockSpec((2 * LANES, tm), lambda b, i: (0, i))])
    out_shape = [sds((nb, D_MODEL, t), BF), sds((nb, t, D_MODEL), F32), sds((nb, t, D_MODEL), BF),
                 sds((nb, t, D_MODEL), F32), sds((nb, t, D_MODEL), BF),
                 sds((nb, MLA_HEADS * LANES, t), BF), sds((nb, t, MLA_KV_LORA), F32),
                 sds((nb, t, D_MODEL), BF), sds((nb, D_MODEL, t), BF),
                 sds((nb, t, LANES), BF), sds((nb, t, MLA_ROPE_DIM), F32)]
    out_specs = [feat(D_MODEL), tok(D_MODEL), tok(D_MODEL), tok(D_MODEL), tok(D_MODEL),
                 feat(MLA_HEADS * LANES), tok(MLA_KV_LORA), tok(D_MODEL), feat(D_MODEL),
                 tok(LANES), tok(MLA_ROPE_DIM)]
    outs = pl.pallas_call(
        _proj_kernel, grid=(nb, nt), in_specs=in_specs, out_specs=out_specs, out_shape=out_shape,
        compiler_params=_params(2), name="project",
    )(x, *weights, tabs["cosp"], tabs["sinp"], tabs["ct"], tabs["st"])
    names = ["qTd", "kf", "kb", "vf", "vb", "qTm", "ckv", "kn", "vTm", "r", "kr"]
    return dict(zip(names, outs))


def _flash_step(state, s, pv):
    bm = jnp.max(s, axis=0, keepdims=True)
    if state is None:
        p = jnp.exp2(s - bm)
        return bm, jnp.sum(p, axis=0, keepdims=True), pv(p.astype(BF))
    m, l, acc = state
    m_new = jnp.maximum(m, bm)
    alpha = jnp.exp2(m - m_new)
    p = jnp.exp2(s - m_new)
    return m_new, alpha * l + jnp.sum(p, axis=0, keepdims=True), alpha * acc + pv(p.astype(BF))


def _normalized(state):
    _, l, acc = state
    return acc * (1.0 / l)


def _split_maps(qT):
    rows = lax.broadcasted_iota(jnp.int32, qT.shape, 0)
    qf = qT.astype(F32)
    zero = jnp.zeros_like(qf)
    return (jnp.where(rows < DIFF_HEAD_DIM, qf, zero).astype(BF),
            jnp.where(rows >= DIFF_HEAD_DIM, qf, zero).astype(BF))


def _lambda(lam_ref):
    lp = lam_ref[...]
    a = jnp.sum(lp[0:1, :] * lp[1:2, :], axis=1, keepdims=True)
    b = jnp.sum(lp[2:3, :] * lp[3:4, :], axis=1, keepdims=True)
    return jnp.exp(a) - jnp.exp(b) + LAMBDA_INIT


def _diff_finish(o1, o2, lam, g):
    o = o1 - lam * o2
    ms = jnp.mean(o * o, axis=0, keepdims=True)
    return (o * lax.rsqrt(ms + EPS) * g * (1.0 - LAMBDA_INIT)).astype(BF)


def _mla_keys(kp, r):
    lane = lax.broadcasted_iota(jnp.int32, kp.shape, 1)
    kpf = kp.astype(F32)
    rf = r.astype(F32)
    return (jnp.where(lane < MLA_NOPE_DIM, kpf, rf).astype(BF),
            jnp.where(lane >= MLA_NOPE_DIM, kpf, rf).astype(BF))


def _qslice(qb):
    return slice(qb * TQ, (qb + 1) * TQ)


def _kslice(kb):
    return slice(kb * TK, (kb + 1) * TK)


def _with_ones(va_ref, vT):
    dv, t = vT.shape
    va_ref[0:dv, :] = vT
    va_ref[dv:dv + ONES_ROWS, :] = jnp.ones((ONES_ROWS, t), BF)


def _weighted_mean(acc):
    dv = acc.shape[0] - ONES_ROWS
    return acc[0:dv, :] * (1.0 / acc[dv:dv + 1, :])


def _causal_two_pass(nq, prep, scores, va_refs, s_ref, p_ref, emit):
    n_maps = len(va_refs)

    def score_pass(qb):
        ops = prep(qb)
        m8 = [None] * n_maps
        for kb in range(qb + 1):
            for i, s in enumerate(scores(ops, qb, kb)):
                s_ref[qb % 2, i, _kslice(kb), :] = s
                c = jnp.max(s.reshape(TK // 8, 8, TQ), axis=0)
                m8[i] = c if m8[i] is None else jnp.maximum(m8[i], c)
        return [jnp.max(m, axis=0, keepdims=True) for m in m8]

    def softmax_pass(qb, ms):
        buf = qb % 2
        klen = (qb + 1) * TK
        for kb in range(qb + 1):
            for i in range(n_maps):
                p_ref[buf, i, _kslice(kb), :] = jnp.exp2(s_ref[buf, i, _kslice(kb), :] - ms[i]).astype(BF)
        emit(qb, [_nn(va_refs[i][:, 0:klen], p_ref[buf, i, 0:klen, :]) for i in range(n_maps)])

    ms = score_pass(0)
    for qb in range(nq):
        ms_next = score_pass(qb + 1) if qb + 1 < nq else None
        softmax_pass(qb, ms)
        ms = ms_next


def _diff_prompt_kernel(qT_ref, k_ref, v_ref, bias_ref, lam_ref, g_ref, o_ref, va_ref, s_ref, p_ref, *, t):
    lam = _lambda(lam_ref)
    g = g_ref[...]
    _with_ones(va_ref, v_ref[...].astype(F32).T.astype(BF))

    def prep(qb):
        return _split_maps(qT_ref[:, _qslice(qb)])

    def scores(ops, qb, kb):
        kblk = k_ref[_kslice(kb), :]
        out = [_nn(kblk, q) for q in ops]
        if kb >= qb - 1:
            out = [s + bias_ref[qb - kb] for s in out]
        return out

    def emit(qb, accs):
        o_ref[:, _qslice(qb)] = _diff_finish(_weighted_mean(accs[0]), _weighted_mean(accs[1]), lam, g)

    _causal_two_pass(t // TQ, prep, scores, [va_ref, va_ref], s_ref, p_ref, emit)


def _score_scratch(n_maps, t):
    return [pltpu.VMEM((2, n_maps, t, TQ), F32), pltpu.VMEM((2, n_maps, t, TQ), BF)]


def _diff_prompt(p, bias, lamp, g):
    nb, _, t = p["qTd"].shape
    head_feat = pl.BlockSpec((None, DIFF_V_DIM, t), lambda b, h: (b, h, 0))
    head_tok = pl.BlockSpec((None, t, DIFF_V_DIM), lambda b, h: (b, 0, h))
    return pl.pallas_call(
        functools.partial(_diff_prompt_kernel, t=t),
        grid=(nb, DIFF_HEADS),
        in_specs=[head_feat, head_tok, head_tok,
                  pl.BlockSpec((None, 2, TK, TQ), lambda b, h: (h, 0, 0, 0)),
                  _const_spec(lamp.shape), _const_spec(g.shape)],
        out_specs=head_feat,
        out_shape=jax.ShapeDtypeStruct((nb, D_MODEL, t), BF),
        scratch_shapes=[pltpu.VMEM((DIFF_V_DIM + ONES_ROWS, t), BF)] + _score_scratch(2, t),
        compiler_params=_params(2), name="diff_prompt",
    )(p["qTd"], p["kb"], p["vb"], bias, lamp, g)


def _mla_prompt_kernel(qT_ref, kp_ref, r_ref, vT_ref, mask_ref, o_ref, ke_ref, ko_ref, vae_ref, vao_ref,
                       s_ref, p_ref, *, t):
    ke, ko = _mla_keys(kp_ref[...], r_ref[...])
    ke_ref[...] = ke
    ko_ref[...] = ko
    _with_ones(vae_ref, vT_ref[0:MLA_V_DIM, :])
    _with_ones(vao_ref, vT_ref[MLA_V_DIM:2 * MLA_V_DIM, :])

    def prep(qb):
        return qT_ref[0:LANES, _qslice(qb)], qT_ref[LANES:2 * LANES, _qslice(qb)]

    def scores(ops, qb, kb):
        out = [_nn(ke_ref[_kslice(kb), :], ops[0]), _nn(ko_ref[_kslice(kb), :], ops[1])]
        if kb == qb:
            out = [s + mask_ref[...] for s in out]
        return out

    def emit(qb, accs):
        o_ref[0:MLA_V_DIM, _qslice(qb)] = _weighted_mean(accs[0]).astype(BF)
        o_ref[MLA_V_DIM:2 * MLA_V_DIM, _qslice(qb)] = _weighted_mean(accs[1]).astype(BF)

    _causal_two_pass(t // TQ, prep, scores, [vae_ref, vao_ref], s_ref, p_ref, emit)


def _mla_prompt(p, mask):
    nb, _, t = p["qTm"].shape
    return pl.pallas_call(
        functools.partial(_mla_prompt_kernel, t=t),
        grid=(nb, MLA_HEADS // 2),
        in_specs=[pl.BlockSpec((None, 2 * LANES, t), lambda b, h: (b, h, 0)),
                  pl.BlockSpec((None, t, LANES), lambda b, h: (b, 0, h)),
                  pl.BlockSpec((None, t, LANES), lambda b, h: (b, 0, 0)),
                  pl.BlockSpec((None, 2 * MLA_V_DIM, t), lambda b, h: (b, h, 0)),
                  _const_spec(mask.shape)],
        out_specs=pl.BlockSpec((None, 2 * MLA_V_DIM, t), lambda b, h: (b, h, 0)),
        out_shape=jax.ShapeDtypeStruct((nb, D_MODEL, t), BF),
        scratch_shapes=[pltpu.VMEM((t, LANES), BF), pltpu.VMEM((t, LANES), BF),
                        pltpu.VMEM((MLA_V_DIM + ONES_ROWS, t), BF),
                        pltpu.VMEM((MLA_V_DIM + ONES_ROWS, t), BF)] + _score_scratch(2, t),
        compiler_params=_params(2), name="mla_prompt",
    )(p["qTm"], p["kn"], p["r"], p["vTm"], mask)


def _diff_decode_kernel(qT_ref, ck_ref, cv_ref, kn_ref, vn_ref, bc_ref, bn_ref, lam_ref, g_ref, o_ref,
                        m_ref, l_ref, acc_ref, *, new):
    c = pl.program_id(1)
    last = pl.num_programs(1) - 1

    @pl.when(c == 0)
    def _():
        m_ref[...] = jnp.full(m_ref.shape, NEG_INF, F32)
        l_ref[...] = jnp.zeros(l_ref.shape, F32)
        acc_ref[...] = jnp.zeros(acc_ref.shape, F32)

    def update(h, keys, vals, bias):
        hs = slice(h * DIFF_V_DIM, (h + 1) * DIFF_V_DIM)
        for i, q in enumerate(_split_maps(qT_ref[hs, :])):
            s = _nn(keys, q)
            if bias is not None:
                s = s + bias
            j = 2 * h + i
            m_ref[j], l_ref[j], acc_ref[j] = _flash_step((m_ref[j], l_ref[j], acc_ref[j]), s,
                                                         lambda p: _tn(vals, p))

    def cache_chunk(with_bias):
        for h in range(DIFF_HEADS):
            rows = pl.ds(h, CACHE_BLK, stride=DIFF_HEADS)
            update(h, ck_ref[rows, :].astype(BF), cv_ref[rows, :].astype(BF),
                   bc_ref[h] if with_bias else None)

    @pl.when(c < last)
    def _():
        cache_chunk(False)

    @pl.when(c == last)
    def _():
        cache_chunk(True)
        lam = _lambda(lam_ref)
        for h in range(DIFF_HEADS):
            hs = slice(h * DIFF_V_DIM, (h + 1) * DIFF_V_DIM)
            update(h, kn_ref[0:new, hs], vn_ref[0:new, hs].astype(BF), bn_ref[h])
            o1, o2 = (acc_ref[2 * h + i] * (1.0 / l_ref[2 * h + i]) for i in range(2))
            o_ref[hs, :] = _diff_finish(o1, o2, lam, g_ref[...])


def _diff_decode(s, cache_k, cache_v, bias_c, bias_n, lamp, g, new):
    nb, past = cache_k.shape[:2]
    cache_k, cache_v = (a.reshape(nb, past * DIFF_HEADS, DIFF_V_DIM) for a in (cache_k, cache_v))
    stream = lambda rows: pl.BlockSpec((None, rows, D_MODEL), lambda b, c: (b, 0, 0))
    chunk = pl.BlockSpec((None, CACHE_BLK * DIFF_HEADS, DIFF_V_DIM), lambda b, c: (b, c, 0))
    tile = lambda rows: pl.BlockSpec((DIFF_HEADS, None, rows, DEC_PAD), lambda b, c: (0, 0, 0, 0))
    return pl.pallas_call(
        functools.partial(_diff_decode_kernel, new=new),
        grid=(nb, past // CACHE_BLK),
        in_specs=[pl.BlockSpec((None, D_MODEL, DEC_PAD), lambda b, c: (b, 0, 0)), chunk, chunk,
                  stream(DEC_PAD), stream(DEC_PAD), tile(CACHE_BLK), tile(new),
                  _const_spec(lamp.shape), _const_spec(g.shape)],
        out_specs=pl.BlockSpec((None, D_MODEL, DEC_PAD), lambda b, c: (b, 0, 0)),
        out_shape=jax.ShapeDtypeStruct((nb, D_MODEL, DEC_PAD), BF),
        scratch_shapes=[pltpu.VMEM((2 * DIFF_HEADS, 1, DEC_PAD), F32),
                        pltpu.VMEM((2 * DIFF_HEADS, 1, DEC_PAD), F32),
                        pltpu.VMEM((2 * DIFF_HEADS, DIFF_V_DIM, DEC_PAD), F32)],
        compiler_params=_params(2), name="diff_decode",
    )(s["qTd"], cache_k, cache_v, s["kb"], s["vf"], bias_c, bias_n, lamp, g)


def _cache_prep_kernel(ckv_ref, kr_ref, wuk_ref, wuvT_ref, place_ref, kn_ref, vT_ref, r_ref):
    cb = ckv_ref[...].astype(BF)
    kn_ref[...] = _nn(cb, wuk_ref[...]).astype(BF)
    vT_ref[...] = _nt(wuvT_ref[...], cb).astype(BF)
    r_ref[...] = _nn(kr_ref[...].astype(BF), place_ref[...]).astype(BF)


def _cache_prep(cache_ckv, cache_kr, w, place):
    nb, past, _ = cache_ckv.shape
    tok = lambda width: pl.BlockSpec((None, CACHE_BLK, width), lambda b, i: (b, i, 0))
    sds = jax.ShapeDtypeStruct
    return pl.pallas_call(
        _cache_prep_kernel, grid=(nb, past // CACHE_BLK),
        in_specs=[tok(MLA_KV_LORA), tok(MLA_ROPE_DIM), _const_spec(w["wuk"].shape),
                  _const_spec(w["wuvT"].shape), _const_spec(place.shape)],
        out_specs=[tok(D_MODEL), pl.BlockSpec((None, D_MODEL, CACHE_BLK), lambda b, i: (b, 0, i)),
                   tok(LANES)],
        out_shape=[sds((nb, past, D_MODEL), BF), sds((nb, D_MODEL, past), BF), sds((nb, past, LANES), BF)],
        compiler_params=_params(2), name="cache_prep",
    )(cache_ckv, cache_kr, w["wuk"], w["wuvT"], place)


def _mla_decode_kernel(qT_ref, ckn_ref, cr_ref, cvT_ref, kn_ref, r_ref, vT_ref, o_ref, *, past, new):
    qe = qT_ref[0:LANES, :]
    qo = qT_ref[LANES:2 * LANES, :]
    ste = sto = None
    for c in range(past // CACHE_BLK):
        cs = slice(c * CACHE_BLK, (c + 1) * CACHE_BLK)
        ke, ko = _mla_keys(ckn_ref[cs, :], cr_ref[cs, :])
        ste = _flash_step(ste, _nn(ke, qe), lambda p, cs=cs: _nn(cvT_ref[0:MLA_V_DIM, cs], p))
        sto = _flash_step(sto, _nn(ko, qo), lambda p, cs=cs: _nn(cvT_ref[MLA_V_DIM:2 * MLA_V_DIM, cs], p))
    ke, ko = _mla_keys(kn_ref[0:new, :], r_ref[0:new, :])
    ste = _flash_step(ste, _nn(ke, qe), lambda p: _nn(vT_ref[0:MLA_V_DIM, 0:new], p))
    sto = _flash_step(sto, _nn(ko, qo), lambda p: _nn(vT_ref[MLA_V_DIM:2 * MLA_V_DIM, 0:new], p))
    o_ref[0:MLA_V_DIM, :] = _normalized(ste).astype(BF)
    o_ref[MLA_V_DIM:2 * MLA_V_DIM, :] = _normalized(sto).astype(BF)


def _mla_decode(s, ckn, cr, cvT, new):
    nb, past, _ = ckn.shape
    pair_feat = lambda rows, cols: pl.BlockSpec((None, rows, cols), lambda b, h: (b, h, 0))
    return pl.pallas_call(
        functools.partial(_mla_decode_kernel, past=past, new=new),
        grid=(nb, MLA_HEADS // 2),
        in_specs=[pair_feat(2 * LANES, DEC_PAD),
                  pl.BlockSpec((None, past, LANES), lambda b, h: (b, 0, h)),
                  pl.BlockSpec((None, past, LANES), lambda b, h: (b, 0, 0)),
                  pair_feat(2 * MLA_V_DIM, past),
                  pl.BlockSpec((None, DEC_PAD, LANES), lambda b, h: (b, 0, h)),
                  pl.BlockSpec((None, DEC_PAD, LANES), lambda b, h: (b, 0, 0)),
                  pair_feat(2 * MLA_V_DIM, DEC_PAD)],
        out_specs=pair_feat(2 * MLA_V_DIM, DEC_PAD),
        out_shape=jax.ShapeDtypeStruct((nb, D_MODEL, DEC_PAD), BF),
        compiler_params=_params(2), name="mla_decode",
    )(s["qTm"], ckn, cr, cvT, s["kn"], s["r"], s["vTm"])


def _post_kernel(x_ref, oTd_ref, oTm_ref, gmix_ref, wgT_ref, wodT_ref, womT_ref, woutT_ref,
                 gmlp_ref, wup_ref, wdown_ref, gfin_ref, y_ref, *, n_groups):
    rows = x_ref.shape[0] // n_groups
    groups = [slice(i * rows, (i + 1) * rows) for i in range(n_groups)]
    d = D_MODEL

    def norm_in(v, r):
        v["x"] = x_ref[r, :]
        v["hb"] = _rms(v["x"], gmix_ref[...]).astype(BF)

    def branches(v, r):
        v["gT"] = _nt(wgT_ref[...], v.pop("hb"))
        v["od"] = _nn(wodT_ref[...], oTd_ref[:, r])
        v["om"] = _nn(womT_ref[...], oTm_ref[:, r])

    def merge(v, r):
        g = jax.nn.sigmoid(v.pop("gT"))
        v["mT"] = (g[0:d, :] * v.pop("od") + g[d:2 * d, :] * v.pop("om")).astype(BF)

    def out_proj(v, r):
        v["aT"] = _nn(woutT_ref[...], v.pop("mT"))

    def residual(v, r):
        v["x1"] = v.pop("x") + v.pop("aT").T
        v["h2"] = _rms(v["x1"], gmlp_ref[...]).astype(BF)

    def up(v, r):
        v["u"] = _nn(v.pop("h2"), wup_ref[...])

    def act(v, r):
        u = jnp.maximum(v.pop("u"), 0.0)
        v["uu"] = (u * u).astype(BF)

    def down(v, r):
        v["x2"] = v.pop("x1") + _nn(v.pop("uu"), wdown_ref[...])

    def norm_out(v, r):
        y_ref[r, :] = _rms(v.pop("x2"), gfin_ref[...])

    values = [{} for _ in groups]
    for stage in (norm_in, branches, merge, out_proj, residual, up, act, down, norm_out):
        for v, r in zip(values, groups):
            stage(v, r)


def _post(x, oTd, oTm, w, tm, n_groups):
    nb, t, _ = x.shape
    tok = pl.BlockSpec((None, tm, D_MODEL), lambda b, i: (b, i, 0))
    feat = pl.BlockSpec((None, D_MODEL, tm), lambda b, i: (b, 0, i))
    weights = [w["gmix"], w["wgT"], w["wodT"], w["womT"], w["woutT"], w["gmlp"], w["wup"],
               w["wdown"], w["gfin"]]
    return pl.pallas_call(
        functools.partial(_post_kernel, n_groups=n_groups), grid=(nb, t // tm),
        in_specs=[tok, feat, feat] + [_const_spec(a.shape) for a in weights],
        out_specs=tok, out_shape=jax.ShapeDtypeStruct((nb, t, D_MODEL), F32),
        compiler_params=_params(2), name="post",
    )(x, oTd, oTm, *weights)


def _prep_weights(norm_mix, w_in, mla_q_norm, mla_w_uq, mla_kv_norm, mla_w_uk, mla_w_uv, w_o_diff,
                  w_o_mla, w_out, norm_mlp, w_up, w_down, norm_final):
    d = D_MODEL
    o_cq = 3 * d
    o_ckv = o_cq + MLA_Q_LORA
    o_kr = o_ckv + MLA_KV_LORA
    o_g = o_kr + MLA_ROPE_DIM
    half = MLA_ROPE_DIM // 2
    rot = lambda a: jnp.concatenate([-a[..., half:], a[..., :half]], axis=-1)
    zpad = lambda a, n: jnp.zeros(a.shape[:-1] + (n,), a.dtype)

    wkr = w_in[:, o_kr:o_g]
    place = lambda a: jnp.concatenate([a, zpad(a, 32), a, zpad(a, 32)], axis=-1)
    wlat = jnp.concatenate([w_in[:, o_cq:o_kr], place(wkr), place(rot(wkr))], axis=-1)

    nope = mla_w_uq[:, :, :MLA_NOPE_DIM]
    rope = mla_w_uq[:, :, MLA_NOPE_DIM:]

    def arrange(n, r):
        even = jnp.concatenate([n[:, 0::2], r[:, 0::2], zpad(r[:, 0::2], 32)], axis=-1)
        odd = jnp.concatenate([r[:, 1::2], zpad(r[:, 1::2], 32), n[:, 1::2]], axis=-1)
        both = jnp.stack([even, odd], axis=2)
        return both.reshape(MLA_Q_LORA, MLA_HEADS * LANES)

    row = lambda a: a.reshape(1, -1).astype(F32)
    return {
        "gmix": row(norm_mix), "gq": row(mla_q_norm), "gkv": row(mla_kv_norm),
        "gmlp": row(norm_mlp), "gfin": row(norm_final),
        "wqT": w_in[:, 0:d].T.astype(BF),
        "wk": w_in[:, d:2 * d].astype(BF),
        "wv": w_in[:, 2 * d:3 * d].astype(BF),
        "wlat": wlat.astype(BF),
        "wgT": w_in[:, o_g:].T.astype(BF),
        "wuqaT": arrange(nope, rope).T.astype(BF),
        "wuqbT": arrange(jnp.zeros_like(nope), rot(rope)).T.astype(BF),
        "wuk": mla_w_uk.reshape(MLA_KV_LORA, d).astype(BF),
        "wuvT": mla_w_uv.reshape(MLA_KV_LORA, d).T.astype(BF),
        "wodT": w_o_diff.reshape(d, d).T.astype(BF),
        "womT": w_o_mla.reshape(d, d).T.astype(BF),
        "woutT": w_out.T.astype(BF),
        "wup": w_up.astype(BF),
        "wdown": w_down.astype(BF),
    }


def _rope_tables(pos):
    half = MLA_ROPE_DIM // 2
    inv = jnp.power(ROPE_THETA, -jnp.arange(half, dtype=F32) * 2.0 / MLA_ROPE_DIM)
    ang = pos.astype(F32)[:, None] * inv[None, :]
    cos2 = jnp.tile(jnp.cos(ang), (1, 2))
    sin2 = jnp.tile(jnp.sin(ang), (1, 2))
    t = pos.shape[0]
    z32 = jnp.zeros((t, 32), F32)
    place = lambda a: jnp.concatenate([a, z32, a, z32], axis=1)
    scale = (MLA_NOPE_DIM + MLA_ROPE_DIM) ** -0.5 * LOG2E
    ones = jnp.ones((t, MLA_NOPE_DIM), F32)
    z64 = jnp.zeros((t, MLA_NOPE_DIM), F32)
    ct = jnp.concatenate([ones, cos2, z32, cos2, z32, ones], axis=1) * scale
    st = jnp.concatenate([z64, sin2, z32, sin2, z32, z64], axis=1) * scale
    return {"cosp": place(cos2), "sinp": place(sin2), "ct": ct.T, "st": st.T}


def _score_tiles(rel_bias, k_pos_tiles, q_pos, shift_bucket):
    k_pos = jnp.stack(k_pos_tiles)
    rel = k_pos[:, :, None] - q_pos[None, None, :]
    mask = (k_pos // CHUNK)[:, :, None] <= (q_pos // CHUNK)[None, None, :]
    return _bias_tiles(rel_bias, _t5_bucket(rel).astype(jnp.int32), mask.astype(jnp.int32), shift_bucket)


def kernel(x_prompt, x_sample, cache_diff_k, cache_diff_v, cache_mla_ckv, cache_mla_krope, rel_bias,
           norm_mix, w_in, lam_q1, lam_k1, lam_q2, lam_k2, diff_subln, mla_q_norm, mla_w_uq, mla_kv_norm,
           mla_w_uk, mla_w_uv, w_o_diff, w_o_mla, w_out, norm_mlp, w_up, w_down, norm_final):
    assert norm_mix.shape[0] == 1, "single-layer model"
    nb, t, _ = x_prompt.shape
    ns, new, _ = x_sample.shape
    past = cache_diff_k.shape[2]
    assert t % TQ == 0 and past % CACHE_BLK == 0 and CACHE_BLK >= TK and new <= DEC_PAD and new % 16 == 0
    assert past % CHUNK == 0 and new <= CHUNK

    w = _prep_weights(norm_mix[0], w_in[0], mla_q_norm[0], mla_w_uq[0], mla_kv_norm[0], mla_w_uk[0],
                      mla_w_uv[0], w_o_diff[0], w_o_mla[0], w_out[0], norm_mlp[0], w_up[0], w_down[0],
                      norm_final)
    lamp = jnp.stack([lam_q1[0], lam_k1[0], lam_q2[0], lam_k2[0]]).astype(F32)
    g_sub = diff_subln[0].reshape(DIFF_V_DIM, 1).astype(F32)

    pos_p = jnp.arange(t, dtype=jnp.int32)
    p = _project(x_prompt, w, _rope_tables(pos_p), tm=256)
    blk = jnp.arange(TQ, dtype=jnp.int32)
    tiles_p = _score_tiles(rel_bias, [TQ + blk, blk], TQ + blk, _far_bucket())
    mask_p = jnp.where((blk // CHUNK)[:, None] <= (blk // CHUNK)[None, :], 0.0, NEG_INF).astype(F32)
    oTd = _diff_prompt(p, tiles_p, lamp, g_sub)
    oTm = _mla_prompt(p, mask_p)
    y_prompt = _post(x_prompt, oTd, oTm, w, tm=512, n_groups=2)

    xs = jnp.pad(x_sample, ((0, 0), (0, DEC_PAD - new), (0, 0)))
    pos_s = past + jnp.arange(DEC_PAD, dtype=jnp.int32)
    s = _project(xs, w, _rope_tables(pos_s), tm=DEC_PAD)
    bias_c = _score_tiles(rel_bias, [jnp.arange(past - CACHE_BLK, past, dtype=jnp.int32)], pos_s, _far_bucket())
    bias_n = _score_tiles(rel_bias, [pos_s[:new]], pos_s, _far_bucket())
    oTd_s = _diff_decode(s, cache_diff_k[0], cache_diff_v[0], bias_c, bias_n, lamp, g_sub, new)
    eye = jnp.eye(MLA_ROPE_DIM, dtype=BF)
    z = jnp.zeros_like(eye)
    place = jnp.concatenate([eye, z, eye, z], axis=1)
    ckn, cvT, cr = _cache_prep(cache_mla_ckv[0], cache_mla_krope[0], w, place)
    oTm_s = _mla_decode(s, ckn, cr, cvT, new)
    y_sample = _post(xs, oTd_s, oTm_s, w, tm=DEC_PAD, n_groups=1)[:, :new]

    heads = lambda a, n: a.reshape((1,) + a.shape[:2] + (DIFF_HEADS, n))
    return (y_prompt, y_sample,
            heads(p["kf"], 2 * DIFF_HEAD_DIM), heads(p["vf"], DIFF_V_DIM), p["ckv"][None], p["kr"][None],
            heads(s["kf"][:, :new], 2 * DIFF_HEAD_DIM), heads(s["vf"][:, :new], DIFF_V_DIM),
            s["ckv"][:, :new][None], s["kr"][:, :new][None])
```

```python
import functools
import math

import numpy as np
import jax
import jax.numpy as jnp
from jax import lax
from jax.experimental import pallas as pl
from jax.experimental.pallas import tpu as pltpu

D_MODEL = 1024
CHUNK = 64
DIFF_HEADS = 8
DIFF_HEAD_DIM = 64
DIFF_V_DIM = 128
MLA_HEADS = 16
MLA_NOPE_DIM = 64
MLA_ROPE_DIM = 32
MLA_V_DIM = 64
MLA_Q_LORA = 256
MLA_KV_LORA = 256
D_FF = 4 * D_MODEL
NUM_BUCKETS = 32
MAX_DISTANCE = 128
ROPE_THETA = 10000.0
EPS = 1e-6
NEG_INF = -1e30
LAMBDA_INIT = 0.8 - 0.6 * math.exp(-0.3 * 0)

BF = jnp.bfloat16
F32 = jnp.float32

LANES = 128
TQ = 256
TK = 256
DEC_PAD = LANES
CACHE_BLK = 512
VMEM_LIMIT = 56 * 1024 * 1024
LOG2E = math.log2(math.e)
ONES_ROWS = 16


def _nn(a, b):
    return jnp.dot(a, b, preferred_element_type=F32)


def _nt(a, b):
    return lax.dot_general(a, b, (((1,), (1,)), ((), ())), preferred_element_type=F32)


def _tn(a, b):
    return lax.dot_general(a, b, (((0,), (0,)), ((), ())), preferred_element_type=F32)


def _rms(x, g):
    return x * lax.rsqrt(jnp.mean(x * x, axis=-1, keepdims=True) + EPS) * g


def _const_spec(shape):
    zeros = (0,) * len(shape)
    return pl.BlockSpec(shape, lambda *_: zeros, pipeline_mode=pl.Buffered(1))


def _params(n_axes):
    return pltpu.CompilerParams(dimension_semantics=("arbitrary",) * n_axes,
                                vmem_limit_bytes=VMEM_LIMIT)


def _bias_kernel(tab_ref, bkt_ref, msk_ref, o_ref, *, shift_bucket):
    h = pl.program_id(0)
    bkt = bkt_ref[...]
    acc = jnp.zeros(bkt.shape, F32)
    for b in range(NUM_BUCKETS):
        acc = jnp.where(bkt == b, tab_ref[b, h], acc)
    if shift_bucket is not None:
        acc = acc - tab_ref[shift_bucket, h]
    o_ref[...] = jnp.where(msk_ref[...] != 0, acc * LOG2E, NEG_INF)


def _bias_tiles(rel_bias, bucket, mask, shift_bucket):
    n, r, c = bucket.shape
    return pl.pallas_call(
        functools.partial(_bias_kernel, shift_bucket=shift_bucket),
        grid=(DIFF_HEADS,),
        in_specs=[pl.BlockSpec(memory_space=pltpu.SMEM),
                  pl.BlockSpec((n, r, c), lambda h: (0, 0, 0)),
                  pl.BlockSpec((n, r, c), lambda h: (0, 0, 0))],
        out_specs=pl.BlockSpec((None, n, r, c), lambda h: (h, 0, 0, 0)),
        out_shape=jax.ShapeDtypeStruct((DIFF_HEADS, n, r, c), F32),
        compiler_params=_params(1),
        name="bias_tiles",
    )(rel_bias, bucket, mask)


def _t5_bucket(rel):
    half = NUM_BUCKETS // 2
    max_exact = half // 2
    n = jnp.abs(rel)
    nf = jnp.maximum(n, max_exact).astype(F32)
    large = max_exact + (jnp.log(nf / max_exact) / math.log(MAX_DISTANCE / max_exact)
                         * (half - max_exact)).astype(jnp.int32)
    large = jnp.minimum(large, half - 1)
    return jnp.where(rel > 0, half, 0) + jnp.where(n < max_exact, n, large)


def _far_bucket():
    half = NUM_BUCKETS // 2
    max_exact = half // 2
    assert max_exact + math.log(TK / max_exact) / math.log(MAX_DISTANCE / max_exact) * (half - max_exact) >= half
    return half - 1


def _proj_kernel(x_ref, gmix_ref, wqT_ref, wk_ref, wv_ref, wlat_ref, gq_ref, gkv_ref,
                 wuqaT_ref, wuqbT_ref, wuk_ref, wuvT_ref, cosp_ref, sinp_ref, ct_ref, st_ref,
                 qTd_ref, kf_ref, kb_ref, vf_ref, vb_ref, qTm_ref, ckv_ref, kn_ref, vTm_ref,
                 r_ref, kr_ref):
    hb = _rms(x_ref[...], gmix_ref[...]).astype(BF)
    qTd_ref[...] = (_nt(wqT_ref[...], hb) * (DIFF_HEAD_DIM ** -0.5 * LOG2E)).astype(BF)
    k = _nn(hb, wk_ref[...])
    kf_ref[...] = k
    kb_ref[...] = k.astype(BF)
    v = _nn(hb, wv_ref[...])
    vf_ref[...] = v
    vb_ref[...] = v.astype(BF)

    lat = _nn(hb, wlat_ref[...])
    cq = _rms(lat[:, 0:MLA_Q_LORA], gq_ref[...]).astype(BF)
    qa = _nt(wuqaT_ref[...], cq)
    qb = _nt(wuqbT_ref[...], cq)
    ct = ct_ref[...]
    st = st_ref[...]
    pair = 2 * LANES
    for p in range(MLA_HEADS // 2):
        sl = slice(p * pair, (p + 1) * pair)
        qTm_ref[sl, :] = (qa[sl, :] * ct + qb[sl, :] * st).astype(BF)

    ckv = _rms(lat[:, MLA_Q_LORA:MLA_Q_LORA + MLA_KV_LORA], gkv_ref[...])
    ckv_ref[...] = ckv
    cb = ckv.astype(BF)
    kn_ref[...] = _nn(cb, wuk_ref[...]).astype(BF)
    vTm_ref[...] = _nt(wuvT_ref[...], cb).astype(BF)

    o = MLA_Q_LORA + MLA_KV_LORA
    r = lat[:, o:o + LANES] * cosp_ref[...] + lat[:, o + LANES:o + 2 * LANES] * sinp_ref[...]
    r_ref[...] = r.astype(BF)
    kr_ref[...] = r[:, 0:MLA_ROPE_DIM]


def _project(x, w, tabs, tm):
    nb, t, _ = x.shape
    nt = t // tm
    tok = lambda width: pl.BlockSpec((None, tm, width), lambda b, i: (b, i, 0))
    feat = lambda rows: pl.BlockSpec((None, rows, tm), lambda b, i: (b, 0, i))
    sds = jax.ShapeDtypeStruct
    weights = [w["gmix"], w["wqT"], w["wk"], w["wv"], w["wlat"], w["gq"], w["gkv"],
               w["wuqaT"], w["wuqbT"], w["wuk"], w["wuvT"]]
    in_specs = ([tok(D_MODEL)] + [_const_spec(a.shape) for a in weights]
                + [pl.BlockSpec((tm, LANES), lambda b, i: (i, 0)),
                   pl.BlockSpec((tm, LANES), lambda b, i: (i, 0)),
                   pl.BlockSpec((2 * LANES, tm), lambda b, i: (0, i)),
                   pl.BlockSpec((2 * LANES, tm), lambda b, i: (0, i))])
    out_shape = [sds((nb, D_MODEL, t), BF), sds((nb, t, D_MODEL), F32), sds((nb, t, D_MODEL), BF),
                 sds((nb, t, D_MODEL), F32), sds((nb, t, D_MODEL), BF),
                 sds((nb, MLA_HEADS * LANES, t), BF), sds((nb, t, MLA_KV_LORA), F32),
                 sds((nb, t, D_MODEL), BF), sds((nb, D_MODEL, t), BF),
                 sds((nb, t, LANES), BF), sds((nb, t, MLA_ROPE_DIM), F32)]
    out_specs = [feat(D_MODEL), tok(D_MODEL), tok(D_MODEL), tok(D_MODEL), tok(D_MODEL),
                 feat(MLA_HEADS * LANES), tok(MLA_KV_LORA), tok(D_MODEL), feat(D_MODEL),
                 tok(LANES), tok(MLA_ROPE_DIM)]
    outs = pl.pallas_call(
        _proj_kernel, grid=(nb, nt), in_specs=in_specs, out_specs=out_specs, out_shape=out_shape,
        compiler_params=_params(2), name="project",
    )(x, *weights, tabs["cosp"], tabs["sinp"], tabs["ct"], tabs["st"])
    names = ["qTd", "kf", "kb", "vf", "vb", "qTm", "ckv", "kn", "vTm", "r", "kr"]
    return dict(zip(names, outs))


def _flash_step(state, s, pv):
    bm = jnp.max(s, axis=0, keepdims=True)
    if state is None:
        p = jnp.exp2(s - bm)
        return bm, jnp.sum(p, axis=0, keepdims=True), pv(p.astype(BF))
    m, l, acc = state
    m_new = jnp.maximum(m, bm)
    alpha = jnp.exp2(m - m_new)
    p = jnp.exp2(s - m_new)
    return m_new, alpha * l + jnp.sum(p, axis=0, keepdims=True), alpha * acc + pv(p.astype(BF))


def _normalized(state):
    _, l, acc = state
    return acc * (1.0 / l)


def _split_maps(qT):
    rows = lax.broadcasted_iota(jnp.int32, qT.shape, 0)
    qf = qT.astype(F32)
    zero = jnp.zeros_like(qf)
    return (jnp.where(rows < DIFF_HEAD_DIM, qf, zero).astype(BF),
            jnp.where(rows >= DIFF_HEAD_DIM, qf, zero).astype(BF))


def _lambda(lam_ref):
    lp = lam_ref[...]
    a = jnp.sum(lp[0:1, :] * lp[1:2, :], axis=1, keepdims=True)
    b = jnp.sum(lp[2:3, :] * lp[3:4, :], axis=1, keepdims=True)
    return jnp.exp(a) - jnp.exp(b) + LAMBDA_INIT


def _diff_finish(o1, o2, lam, g):
    o = o1 - lam * o2
    ms = jnp.mean(o * o, axis=0, keepdims=True)
    return (o * lax.rsqrt(ms + EPS) * g * (1.0 - LAMBDA_INIT)).astype(BF)


def _mla_keys(kp, r):
    lane = lax.broadcasted_iota(jnp.int32, kp.shape, 1)
    kpf = kp.astype(F32)
    rf = r.astype(F32)
    return (jnp.where(lane < MLA_NOPE_DIM, kpf, rf).astype(BF),
            jnp.where(lane >= MLA_NOPE_DIM, kpf, rf).astype(BF))


def _qslice(qb):
    return slice(qb * TQ, (qb + 1) * TQ)


def _kslice(kb):
    return slice(kb * TK, (kb + 1) * TK)


def _with_ones(va_ref, vT):
    dv, t = vT.shape
    va_ref[0:dv, :] = vT
    va_ref[dv:dv + ONES_ROWS, :] = jnp.ones((ONES_ROWS, t), BF)


def _weighted_mean(acc):
    dv = acc.shape[0] - ONES_ROWS
    return acc[0:dv, :] * (1.0 / acc[dv:dv + 1, :])


def _causal_two_pass(nq, prep, scores, va_refs, s_ref, p_ref, emit):
    n_maps = len(va_refs)

    def score_pass(qb):
        ops = prep(qb)
        m8 = [None] * n_maps
        for kb in range(qb + 1):
            for i, s in enumerate(scores(ops, qb, kb)):
                s_ref[qb % 2, i, _kslice(kb), :] = s
                c = jnp.max(s.reshape(TK // 8, 8, TQ), axis=0)
                m8[i] = c if m8[i] is None else jnp.maximum(m8[i], c)
        return [jnp.max(m, axis=0, keepdims=True) for m in m8]

    def softmax_pass(qb, ms):
        buf = qb % 2
        klen = (qb + 1) * TK
        for kb in range(qb + 1):
            for i in range(n_maps):
                p_ref[buf, i, _kslice(kb), :] = jnp.exp2(s_ref[buf, i, _kslice(kb), :] - ms[i]).astype(BF)
        emit(qb, [_nn(va_refs[i][:, 0:klen], p_ref[buf, i, 0:klen, :]) for i in range(n_maps)])

    ms = score_pass(0)
    for qb in range(nq):
        ms_next = score_pass(qb + 1) if qb + 1 < nq else None
        softmax_pass(qb, ms)
        ms = ms_next


def _diff_prompt_kernel(qT_ref, k_ref, v_ref, bias_ref, lam_ref, g_ref, o_ref, va_ref, s_ref, p_ref, *, t):
    lam = _lambda(lam_ref)
    g = g_ref[...]
    _with_ones(va_ref, v_ref[...].astype(F32).T.astype(BF))

    def prep(qb):
        return _split_maps(qT_ref[:, _qslice(qb)])

    def scores(ops, qb, kb):
        kblk = k_ref[_kslice(kb), :]
        out = [_nn(kblk, q) for q in ops]
        if kb >= qb - 1:
            out = [s + bias_ref[qb - kb] for s in out]
        return out

    def emit(qb, accs):
        o_ref[:, _qslice(qb)] = _diff_finish(_weighted_mean(accs[0]), _weighted_mean(accs[1]), lam, g)

    _causal_two_pass(t // TQ, prep, scores, [va_ref, va_ref], s_ref, p_ref, emit)


def _score_scratch(n_maps, t):
    return [pltpu.VMEM((2, n_maps, t, TQ), F32), pltpu.VMEM((2, n_maps, t, TQ), BF)]


def _diff_prompt(p, bias, lamp, g):
    nb, _, t = p["qTd"].shape
    head_feat = pl.BlockSpec((None, DIFF_V_DIM, t), lambda b, h: (b, h, 0))
    head_tok = pl.BlockSpec((None, t, DIFF_V_DIM), lambda b, h: (b, 0, h))
    return pl.pallas_call(
        functools.partial(_diff_prompt_kernel, t=t),
        grid=(nb, DIFF_HEADS),
        in_specs=[head_feat, head_tok, head_tok,
                  pl.BlockSpec((None, 2, TK, TQ), lambda b, h: (h, 0, 0, 0)),
                  _const_spec(lamp.shape), _const_spec(g.shape)],
        out_specs=head_feat,
        out_shape=jax.ShapeDtypeStruct((nb, D_MODEL, t), BF),
        scratch_shapes=[pltpu.VMEM((DIFF_V_DIM + ONES_ROWS, t), BF)] + _score_scratch(2, t),
        compiler_params=_params(2), name="diff_prompt",
    )(p["qTd"], p["kb"], p["vb"], bias, lamp, g)


def _mla_prompt_kernel(qT_ref, kp_ref, r_ref, vT_ref, mask_ref, o_ref, ke_ref, ko_ref, vae_ref, vao_ref,
                       s_ref, p_ref, *, t):
    ke, ko = _mla_keys(kp_ref[...], r_ref[...])
    ke_ref[...] = ke
    ko_ref[...] = ko
    _with_ones(vae_ref, vT_ref[0:MLA_V_DIM, :])
    _with_ones(vao_ref, vT_ref[MLA_V_DIM:2 * MLA_V_DIM, :])

    def prep(qb):
        return qT_ref[0:LANES, _qslice(qb)], qT_ref[LANES:2 * LANES, _qslice(qb)]

    def scores(ops, qb, kb):
        out = [_nn(ke_ref[_kslice(kb), :], ops[0]), _nn(ko_ref[_kslice(kb), :], ops[1])]
        if kb == qb:
            out = [s + mask_ref[...] for s in out]
        return out

    def emit(qb, accs):
        o_ref[0:MLA_V_DIM, _qslice(qb)] = _weighted_mean(accs[0]).astype(BF)
        o_ref[MLA_V_DIM:2 * MLA_V_DIM, _qslice(qb)] = _weighted_mean(accs[1]).astype(BF)

    _causal_two_pass(t // TQ, prep, scores, [vae_ref, vao_ref], s_ref, p_ref, emit)


def _mla_prompt(p, mask):
    nb, _, t = p["qTm"].shape
    return pl.pallas_call(
        functools.partial(_mla_prompt_kernel, t=t),
        grid=(nb, MLA_HEADS // 2),
        in_specs=[pl.BlockSpec((None, 2 * LANES, t), lambda b, h: (b, h, 0)),
                  pl.BlockSpec((None, t, LANES), lambda b, h: (b, 0, h)),
                  pl.BlockSpec((None, t, LANES), lambda b, h: (b, 0, 0)),
                  pl.BlockSpec((None, 2 * MLA_V_DIM, t), lambda b, h: (b, h, 0)),
                  _const_spec(mask.shape)],
        out_specs=pl.BlockSpec((None, 2 * MLA_V_DIM, t), lambda b, h: (b, h, 0)),
        out_shape=jax.ShapeDtypeStruct((nb, D_MODEL, t), BF),
        scratch_shapes=[pltpu.VMEM((t, LANES), BF), pltpu.VMEM((t, LANES), BF),
                        pltpu.VMEM((MLA_V_DIM + ONES_ROWS, t), BF),
                        pltpu.VMEM((MLA_V_DIM + ONES_ROWS, t), BF)] + _score_scratch(2, t),
        compiler_params=_params(2), name="mla_prompt",
    )(p["qTm"], p["kn"], p["r"], p["vTm"], mask)


def _diff_decode_kernel(qT_ref, ck_ref, cv_ref, kn_ref, vn_ref, bc_ref, bn_ref, lam_ref, g_ref, o_ref,
                        m_ref, l_ref, acc_ref, *, new):
    c = pl.program_id(1)
    last = pl.num_programs(1) - 1

    @pl.when(c == 0)
    def _():
        m_ref[...] = jnp.full(m_ref.shape, NEG_INF, F32)
        l_ref[...] = jnp.zeros(l_ref.shape, F32)
        acc_ref[...] = jnp.zeros(acc_ref.shape, F32)

    def update(h, keys, vals, bias):
        hs = slice(h * DIFF_V_DIM, (h + 1) * DIFF_V_DIM)
        for i, q in enumerate(_split_maps(qT_ref[hs, :])):
            s = _nn(keys, q)
            if bias is not None:
                s = s + bias
            j = 2 * h + i
            m_ref[j], l_ref[j], acc_ref[j] = _flash_step((m_ref[j], l_ref[j], acc_ref[j]), s,
                                                         lambda p: _tn(vals, p))

    def cache_chunk(with_bias):
        for h in range(DIFF_HEADS):
            rows = pl.ds(h, CACHE_BLK, stride=DIFF_HEADS)
            update(h, ck_ref[rows, :].astype(BF), cv_ref[rows, :].astype(BF),
                   bc_ref[h] if with_bias else None)

    @pl.when(c < last)
    def _():
        cache_chunk(False)

    @pl.when(c == last)
    def _():
        cache_chunk(True)
        lam = _lambda(lam_ref)
        for h in range(DIFF_HEADS):
            hs = slice(h * DIFF_V_DIM, (h + 1) * DIFF_V_DIM)
            update(h, kn_ref[0:new, hs], vn_ref[0:new, hs].astype(BF), bn_ref[h])
            o1, o2 = (acc_ref[2 * h + i] * (1.0 / l_ref[2 * h + i]) for i in range(2))
            o_ref[hs, :] = _diff_finish(o1, o2, lam, g_ref[...])


def _diff_decode(s, cache_k, cache_v, bias_c, bias_n, lamp, g, new):
    nb, past = cache_k.shape[:2]
    cache_k, cache_v = (a.reshape(nb, past * DIFF_HEADS, DIFF_V_DIM) for a in (cache_k, cache_v))
    stream = lambda rows: pl.BlockSpec((None, rows, D_MODEL), lambda b, c: (b, 0, 0))
    chunk = pl.BlockSpec((None, CACHE_BLK * DIFF_HEADS, DIFF_V_DIM), lambda b, c: (b, c, 0))
    tile = lambda rows: pl.BlockSpec((DIFF_HEADS, None, rows, DEC_PAD), lambda b, c: (0, 0, 0, 0))
    return pl.pallas_call(
        functools.partial(_diff_decode_kernel, new=new),
        grid=(nb, past // CACHE_BLK),
        in_specs=[pl.BlockSpec((None, D_MODEL, DEC_PAD), lambda b, c: (b, 0, 0)), chunk, chunk,
                  stream(DEC_PAD), stream(DEC_PAD), tile(CACHE_BLK), tile(new),
                  _const_spec(lamp.shape), _const_spec(g.shape)],
        out_specs=pl.BlockSpec((None, D_MODEL, DEC_PAD), lambda b, c: (b, 0, 0)),
        out_shape=jax.ShapeDtypeStruct((nb, D_MODEL, DEC_PAD), BF),
        scratch_shapes=[pltpu.VMEM((2 * DIFF_HEADS, 1, DEC_PAD), F32),
                        pltpu.VMEM((2 * DIFF_HEADS, 1, DEC_PAD), F32),
                        pltpu.VMEM((2 * DIFF_HEADS, DIFF_V_DIM, DEC_PAD), F32)],
        compiler_params=_params(2), name="diff_decode",
    )(s["qTd"], cache_k, cache_v, s["kb"], s["vf"], bias_c, bias_n, lamp, g)


def _mla_decode_kernel(qT_ref, ckv_ref, kr_ref, ckvn_ref, krn_ref, wuk_ref, wuvT_ref, o_ref, s_ref,
                       *, past, new):
    nq = MLA_HEADS * new
    assert nq == 2 * LANES and new & (new - 1) == 0
    log2 = lambda n: n.bit_length() - 1

    def groups(shape, rows_per_group):
        r = lax.broadcasted_iota(jnp.int32, shape, 0)
        c = lax.broadcasted_iota(jnp.int32, shape, 1)
        return lax.shift_right_logical(r, log2(rows_per_group)) == lax.shift_right_logical(c, log2(new))

    def frame_match(shape, lane_axis):
        a = lax.broadcasted_iota(jnp.int32, shape, lane_axis)
        b = lax.broadcasted_iota(jnp.int32, shape, 1 - lane_axis)
        return jnp.where((a & (new - 1)) == b, 1.0, 0.0).astype(BF)

    nope, rope = [], []
    for h in range(MLA_HEADS):
        base = h * LANES
        o_n, o_r = (0, MLA_NOPE_DIM) if h % 2 == 0 else (MLA_NOPE_DIM, 0)
        nope.append(qT_ref[base + o_n:base + o_n + MLA_NOPE_DIM, :])
        rope.append(qT_ref[base + o_r:base + o_r + MLA_ROPE_DIM, :])
    spread = frame_match((DEC_PAD, nq), 1)
    qn = _nn(jnp.concatenate(nope, axis=0), spread)
    qn_bd = jnp.where(groups(qn.shape, MLA_NOPE_DIM), qn, 0.0).astype(BF)
    qr = _nn(jnp.concatenate(rope, axis=0), spread)
    qr_bd = jnp.where(groups(qr.shape, MLA_ROPE_DIM), qr, 0.0)
    qrT = jnp.sum(qr_bd.reshape(MLA_HEADS, MLA_ROPE_DIM, nq), axis=0).astype(BF)
    qlatT = _nn(wuk_ref[...], qn_bd).astype(BF)

    def keys(lo, n):
        if lo < past:
            return ckv_ref[lo:lo + n, :].astype(BF), kr_ref[lo:lo + n, :].astype(BF)
        return ckvn_ref[0:new, :].astype(BF), krn_ref[0:new, :].astype(BF)

    spans = [(c0, CACHE_BLK) for c0 in range(0, past, CACHE_BLK)] + [(past, new)]
    m = None
    for lo, n in spans:
        ck, kr = keys(lo, n)
        s = _nn(ck, qlatT) + _nn(kr, qrT)
        s_ref[lo:lo + n, :] = s
        c = jnp.max(s.reshape(n // 8, 8, nq), axis=0)
        m = c if m is None else jnp.maximum(m, c)
    m = jnp.max(m, axis=0, keepdims=True)
    l = acc = None
    for lo, n in spans:
        p = jnp.exp2(s_ref[lo:lo + n, :] - m)
        ps = jnp.sum(p.reshape(n // 8, 8, nq), axis=0)
        pv = _tn(keys(lo, n)[0], p.astype(BF))
        l, acc = (ps, pv) if l is None else (l + ps, acc + pv)
    o_latT = (acc * (1.0 / jnp.sum(l, axis=0, keepdims=True))).astype(BF)
    y = _nn(wuvT_ref[...], o_latT)
    y_bd = jnp.where(groups(y.shape, MLA_V_DIM), y, 0.0).astype(BF)
    o_ref[...] = _nn(y_bd, frame_match((nq, DEC_PAD), 0)).astype(BF)


def _mla_decode(s, cache_ckv, cache_kr, w, new):
    nb, past, _ = cache_ckv.shape
    per_stream = lambda rows, cols: pl.BlockSpec((None, rows, cols), lambda b: (b, 0, 0))
    return pl.pallas_call(
        functools.partial(_mla_decode_kernel, past=past, new=new),
        grid=(nb,),
        in_specs=[per_stream(MLA_HEADS * LANES, DEC_PAD), per_stream(past, MLA_KV_LORA),
                  per_stream(past, MLA_ROPE_DIM), per_stream(DEC_PAD, MLA_KV_LORA),
                  per_stream(DEC_PAD, MLA_ROPE_DIM), _const_spec(w["wuk"].shape),
                  _const_spec(w["wuvT"].shape)],
        out_specs=per_stream(D_MODEL, DEC_PAD),
        out_shape=jax.ShapeDtypeStruct((nb, D_MODEL, DEC_PAD), BF),
        scratch_shapes=[pltpu.VMEM((past + new, MLA_HEADS * new), F32)],
        compiler_params=_params(1), name="mla_decode",
    )(s["qTm"], cache_ckv, cache_kr, s["ckv"], s["kr"], w["wuk"], w["wuvT"])


def _post_kernel(x_ref, oTd_ref, oTm_ref, gmix_ref, wgT_ref, wodT_ref, womT_ref, woutT_ref,
                 gmlp_ref, wup_ref, wdown_ref, gfin_ref, y_ref, *, n_groups):
    rows = x_ref.shape[0] // n_groups
    groups = [slice(i * rows, (i + 1) * rows) for i in range(n_groups)]
    d = D_MODEL

    def norm_in(v, r):
        v["x"] = x_ref[r, :]
        v["hb"] = _rms(v["x"], gmix_ref[...]).astype(BF)

    def branches(v, r):
        v["gT"] = _nt(wgT_ref[...], v.pop("hb"))
        v["od"] = _nn(wodT_ref[...], oTd_ref[:, r])
        v["om"] = _nn(womT_ref[...], oTm_ref[:, r])

    def merge(v, r):
        g = jax.nn.sigmoid(v.pop("gT"))
        v["mT"] = (g[0:d, :] * v.pop("od") + g[d:2 * d, :] * v.pop("om")).astype(BF)

    def out_proj(v, r):
        v["aT"] = _nn(woutT_ref[...], v.pop("mT"))

    def residual(v, r):
        v["x1"] = v.pop("x") + v.pop("aT").T
        v["h2"] = _rms(v["x1"], gmlp_ref[...]).astype(BF)

    def up(v, r):
        v["u"] = _nn(v.pop("h2"), wup_ref[...])

    def act(v, r):
        u = jnp.maximum(v.pop("u"), 0.0)
        v["uu"] = (u * u).astype(BF)

    def down(v, r):
        v["x2"] = v.pop("x1") + _nn(v.pop("uu"), wdown_ref[...])

    def norm_out(v, r):
        y_ref[r, :] = _rms(v.pop("x2"), gfin_ref[...])

    values = [{} for _ in groups]
    for stage in (norm_in, branches, merge, out_proj, residual, up, act, down, norm_out):
        for v, r in zip(values, groups):
            stage(v, r)


def _post(x, oTd, oTm, w, tm, n_groups):
    nb, t, _ = x.shape
    tok = pl.BlockSpec((None, tm, D_MODEL), lambda b, i: (b, i, 0))
    feat = pl.BlockSpec((None, D_MODEL, tm), lambda b, i: (b, 0, i))
    weights = [w["gmix"], w["wgT"], w["wodT"], w["womT"], w["woutT"], w["gmlp"], w["wup"],
               w["wdown"], w["gfin"]]
    return pl.pallas_call(
        functools.partial(_post_kernel, n_groups=n_groups), grid=(nb, t // tm),
        in_specs=[tok, feat, feat] + [_const_spec(a.shape) for a in weights],
        out_specs=tok, out_shape=jax.ShapeDtypeStruct((nb, t, D_MODEL), F32),
        compiler_params=_params(2), name="post",
    )(x, oTd, oTm, *weights)


def _prep_weights(norm_mix, w_in, mla_q_norm, mla_w_uq, mla_kv_norm, mla_w_uk, mla_w_uv, w_o_diff,
                  w_o_mla, w_out, norm_mlp, w_up, w_down, norm_final):
    d = D_MODEL
    o_cq = 3 * d
    o_ckv = o_cq + MLA_Q_LORA
    o_kr = o_ckv + MLA_KV_LORA
    o_g = o_kr + MLA_ROPE_DIM
    half = MLA_ROPE_DIM // 2
    rot = lambda a: jnp.concatenate([-a[..., half:], a[..., :half]], axis=-1)
    zpad = lambda a, n: jnp.zeros(a.shape[:-1] + (n,), a.dtype)

    wkr = w_in[:, o_kr:o_g]
    place = lambda a: jnp.concatenate([a, zpad(a, 32), a, zpad(a, 32)], axis=-1)
    wlat = jnp.concatenate([w_in[:, o_cq:o_kr], place(wkr), place(rot(wkr))], axis=-1)

    nope = mla_w_uq[:, :, :MLA_NOPE_DIM]
    rope = mla_w_uq[:, :, MLA_NOPE_DIM:]

    def arrange(n, r):
        even = jnp.concatenate([n[:, 0::2], r[:, 0::2], zpad(r[:, 0::2], 32)], axis=-1)
        odd = jnp.concatenate([r[:, 1::2], zpad(r[:, 1::2], 32), n[:, 1::2]], axis=-1)
        both = jnp.stack([even, odd], axis=2)
        return both.reshape(MLA_Q_LORA, MLA_HEADS * LANES)

    row = lambda a: a.reshape(1, -1).astype(F32)
    return {
        "gmix": row(norm_mix), "gq": row(mla_q_norm), "gkv": row(mla_kv_norm),
        "gmlp": row(norm_mlp), "gfin": row(norm_final),
        "wqT": w_in[:, 0:d].T.astype(BF),
        "wk": w_in[:, d:2 * d].astype(BF),
        "wv": w_in[:, 2 * d:3 * d].astype(BF),
        "wlat": wlat.astype(BF),
        "wgT": w_in[:, o_g:].T.astype(BF),
        "wuqaT": arrange(nope, rope).T.astype(BF),
        "wuqbT": arrange(jnp.zeros_like(nope), rot(rope)).T.astype(BF),
        "wuk": mla_w_uk.reshape(MLA_KV_LORA, d).astype(BF),
        "wuvT": mla_w_uv.reshape(MLA_KV_LORA, d).T.astype(BF),
        "wodT": w_o_diff.reshape(d, d).T.astype(BF),
        "womT": w_o_mla.reshape(d, d).T.astype(BF),
        "woutT": w_out.T.astype(BF),
        "wup": w_up.astype(BF),
        "wdown": w_down.astype(BF),
    }


def _rope_tables(pos):
    half = MLA_ROPE_DIM // 2
    inv = jnp.power(ROPE_THETA, -jnp.arange(half, dtype=F32) * 2.0 / MLA_ROPE_DIM)
    ang = pos.astype(F32)[:, None] * inv[None, :]
    cos2 = jnp.tile(jnp.cos(ang), (1, 2))
    sin2 = jnp.tile(jnp.sin(ang), (1, 2))
    t = pos.shape[0]
    z32 = jnp.zeros((t, 32), F32)
    place = lambda a: jnp.concatenate([a, z32, a, z32], axis=1)
    scale = (MLA_NOPE_DIM + MLA_ROPE_DIM) ** -0.5 * LOG2E
    ones = jnp.ones((t, MLA_NOPE_DIM), F32)
    z64 = jnp.zeros((t, MLA_NOPE_DIM), F32)
    ct = jnp.concatenate([ones, cos2, z32, cos2, z32, ones], axis=1) * scale
    st = jnp.concatenate([z64, sin2, z32, sin2, z32, z64], axis=1) * scale
    return {"cosp": place(cos2), "sinp": place(sin2), "ct": ct.T, "st": st.T}


def _score_tiles(rel_bias, k_pos_tiles, q_pos, shift_bucket):
    k_pos = jnp.stack(k_pos_tiles)
    rel = k_pos[:, :, None] - q_pos[None, None, :]
    mask = (k_pos // CHUNK)[:, :, None] <= (q_pos // CHUNK)[None, None, :]
    return _bias_tiles(rel_bias, _t5_bucket(rel).astype(jnp.int32), mask.astype(jnp.int32), shift_bucket)


def kernel(x_prompt, x_sample, cache_diff_k, cache_diff_v, cache_mla_ckv, cache_mla_krope, rel_bias,
           norm_mix, w_in, lam_q1, lam_k1, lam_q2, lam_k2, diff_subln, mla_q_norm, mla_w_uq, mla_kv_norm,
           mla_w_uk, mla_w_uv, w_o_diff, w_o_mla, w_out, norm_mlp, w_up, w_down, norm_final):
    assert norm_mix.shape[0] == 1, "single-layer model"
    nb, t, _ = x_prompt.shape
    ns, new, _ = x_sample.shape
    past = cache_diff_k.shape[2]
    assert t % TQ == 0 and past % CACHE_BLK == 0 and CACHE_BLK >= TK and new <= DEC_PAD and new % 16 == 0
    assert past % CHUNK == 0 and new <= CHUNK

    w = _prep_weights(norm_mix[0], w_in[0], mla_q_norm[0], mla_w_uq[0], mla_kv_norm[0], mla_w_uk[0],
                      mla_w_uv[0], w_o_diff[0], w_o_mla[0], w_out[0], norm_mlp[0], w_up[0], w_down[0],
                      norm_final)
    lamp = jnp.stack([lam_q1[0], lam_k1[0], lam_q2[0], lam_k2[0]]).astype(F32)
    g_sub = diff_subln[0].reshape(DIFF_V_DIM, 1).astype(F32)

    pos_p = jnp.arange(t, dtype=jnp.int32)
    p = _project(x_prompt, w, _rope_tables(pos_p), tm=256)
    blk = jnp.arange(TQ, dtype=jnp.int32)
    tiles_p = _score_tiles(rel_bias, [TQ + blk, blk], TQ + blk, _far_bucket())
    mask_p = jnp.where((blk // CHUNK)[:, None] <= (blk // CHUNK)[None, :], 0.0, NEG_INF).astype(F32)
    oTd = _diff_prompt(p, tiles_p, lamp, g_sub)
    oTm = _mla_prompt(p, mask_p)
    y_prompt = _post(x_prompt, oTd, oTm, w, tm=512, n_groups=2)

    xs = jnp.pad(x_sample, ((0, 0), (0, DEC_PAD - new), (0, 0)))
    pos_s = past + jnp.arange(DEC_PAD, dtype=jnp.int32)
    s = _project(xs, w, _rope_tables(pos_s), tm=DEC_PAD)
    bias_c = _score_tiles(rel_bias, [jnp.arange(past - CACHE_BLK, past, dtype=jnp.int32)], pos_s, _far_bucket())
    bias_n = _score_tiles(rel_bias, [pos_s[:new]], pos_s, _far_bucket())
    oTd_s = _diff_decode(s, cache_diff_k[0], cache_diff_v[0], bias_c, bias_n, lamp, g_sub, new)
    oTm_s = _mla_decode(s, cache_mla_ckv[0], cache_mla_krope[0], w, new)
    y_sample = _post(xs, oTd_s, oTm_s, w, tm=DEC_PAD, n_groups=1)[:, :new]

    heads = lambda a, n: a.reshape((1,) + a.shape[:2] + (DIFF_HEADS, n))
    return (y_prompt, y_sample,
            heads(p["kf"], 2 * DIFF_HEAD_DIM), heads(p["vf"], DIFF_V_DIM), p["ckv"][None], p["kr"][None],
            heads(s["kf"][:, :new], 2 * DIFF_HEAD_DIM), heads(s["vf"][:, :new], DIFF_V_DIM),
            s["ckv"][:, :new][None], s["kr"][:, :new][None])
```

```python
import functools
import math

import numpy as np
import jax
import jax.numpy as jnp
from jax import lax
from jax.experimental import pallas as pl
from jax.experimental.pallas import tpu as pltpu

D_MODEL = 1024
CHUNK = 64
DIFF_HEADS = 8
DIFF_HEAD_DIM = 64
DIFF_V_DIM = 128
MLA_HEADS = 16
MLA_NOPE_DIM = 64
MLA_ROPE_DIM = 32
MLA_V_DIM = 64
MLA_Q_LORA = 256
MLA_KV_LORA = 256
D_FF = 4 * D_MODEL
NUM_BUCKETS = 32
MAX_DISTANCE = 128
ROPE_THETA = 10000.0
EPS = 1e-6
NEG_INF = -1e30
LAMBDA_INIT = 0.8 - 0.6 * math.exp(-0.3 * 0)

BF = jnp.bfloat16
F32 = jnp.float32

LANES = 128
TQ = 256
TK = 256
DEC_PAD = LANES
CACHE_BLK = 512
VMEM_LIMIT = 56 * 1024 * 1024
LOG2E = math.log2(math.e)
ONES_ROWS = 16


def _nn(a, b):
    return jnp.dot(a, b, preferred_element_type=F32)


def _nt(a, b):
    return lax.dot_general(a, b, (((1,), (1,)), ((), ())), preferred_element_type=F32)


def _tn(a, b):
    return lax.dot_general(a, b, (((0,), (0,)), ((), ())), preferred_element_type=F32)


def _rms(x, g):
    return x * lax.rsqrt(jnp.mean(x * x, axis=-1, keepdims=True) + EPS) * g


def _const_spec(shape):
    zeros = (0,) * len(shape)
    return pl.BlockSpec(shape, lambda *_: zeros, pipeline_mode=pl.Buffered(1))


def _params(n_axes):
    return pltpu.CompilerParams(dimension_semantics=("arbitrary",) * n_axes,
                                vmem_limit_bytes=VMEM_LIMIT)


def _bias_kernel(tab_ref, bkt_ref, msk_ref, o_ref, *, shift_bucket):
    h = pl.program_id(0)
    bkt = bkt_ref[...]
    acc = jnp.zeros(bkt.shape, F32)
    for b in range(NUM_BUCKETS):
        acc = jnp.where(bkt == b, tab_ref[b, h], acc)
    if shift_bucket is not None:
        acc = acc - tab_ref[shift_bucket, h]
    o_ref[...] = jnp.where(msk_ref[...] != 0, acc * LOG2E, NEG_INF)


def _bias_tiles(rel_bias, bucket, mask, shift_bucket):
    n, r, c = bucket.shape
    return pl.pallas_call(
        functools.partial(_bias_kernel, shift_bucket=shift_bucket),
        grid=(DIFF_HEADS,),
        in_specs=[pl.BlockSpec(memory_space=pltpu.SMEM),
                  pl.BlockSpec((n, r, c), lambda h: (0, 0, 0)),
                  pl.BlockSpec((n, r, c), lambda h: (0, 0, 0))],
        out_specs=pl.BlockSpec((None, n, r, c), lambda h: (h, 0, 0, 0)),
        out_shape=jax.ShapeDtypeStruct((DIFF_HEADS, n, r, c), F32),
        compiler_params=_params(1),
        name="bias_tiles",
    )(rel_bias, bucket, mask)


def _t5_bucket(rel):
    half = NUM_BUCKETS // 2
    max_exact = half // 2
    n = jnp.abs(rel)
    nf = jnp.maximum(n, max_exact).astype(F32)
    large = max_exact + (jnp.log(nf / max_exact) / math.log(MAX_DISTANCE / max_exact)
                         * (half - max_exact)).astype(jnp.int32)
    large = jnp.minimum(large, half - 1)
    return jnp.where(rel > 0, half, 0) + jnp.where(n < max_exact, n, large)


def _far_bucket():
    half = NUM_BUCKETS // 2
    max_exact = half // 2
    assert max_exact + math.log(TK / max_exact) / math.log(MAX_DISTANCE / max_exact) * (half - max_exact) >= half
    return half - 1


def _proj_kernel(x_ref, gmix_ref, wqT_ref, wk_ref, wv_ref, wlat_ref, gq_ref, gkv_ref,
                 wuqaT_ref, wuqbT_ref, wuk_ref, wuvT_ref, cosp_ref, sinp_ref, ct_ref, st_ref,
                 qTd_ref, kf_ref, kb_ref, vf_ref, vb_ref, qTm_ref, ckv_ref, kn_ref, vTm_ref,
                 r_ref, kr_ref):
    hb = _rms(x_ref[...], gmix_ref[...]).astype(BF)
    qTd_ref[...] = (_nt(wqT_ref[...], hb) * (DIFF_HEAD_DIM ** -0.5 * LOG2E)).astype(BF)
    k = _nn(hb, wk_ref[...])
    kf_ref[...] = k
    kb_ref[...] = k.astype(BF)
    v = _nn(hb, wv_ref[...])
    vf_ref[...] = v
    vb_ref[...] = v.astype(BF)

    lat = _nn(hb, wlat_ref[...])
    cq = _rms(lat[:, 0:MLA_Q_LORA], gq_ref[...]).astype(BF)
    qa = _nt(wuqaT_ref[...], cq)
    qb = _nt(wuqbT_ref[...], cq)
    ct = ct_ref[...]
    st = st_ref[...]
    pair = 2 * LANES
    for p in range(MLA_HEADS // 2):
        sl = slice(p * pair, (p + 1) * pair)
        qTm_ref[sl, :] = (qa[sl, :] * ct + qb[sl, :] * st).astype(BF)

    ckv = _rms(lat[:, MLA_Q_LORA:MLA_Q_LORA + MLA_KV_LORA], gkv_ref[...])
    ckv_ref[...] = ckv
    cb = ckv.astype(BF)
    kn_ref[...] = _nn(cb, wuk_ref[...]).astype(BF)
    vTm_ref[...] = _nt(wuvT_ref[...], cb).astype(BF)

    o = MLA_Q_LORA + MLA_KV_LORA
    r = lat[:, o:o + LANES] * cosp_ref[...] + lat[:, o + LANES:o + 2 * LANES] * sinp_ref[...]
    r_ref[...] = r.astype(BF)
    kr_ref[...] = r[:, 0:MLA_ROPE_DIM]


def _project(x, w, tabs, tm):
    nb, t, _ = x.shape
    nt = t // tm
    tok = lambda width: pl.BlockSpec((None, tm, width), lambda b, i: (b, i, 0))
    feat = lambda rows: pl.BlockSpec((None, rows, tm), lambda b, i: (b, 0, i))
    sds = jax.ShapeDtypeStruct
    weights = [w["gmix"], w["wqT"], w["wk"], w["wv"], w["wlat"], w["gq"], w["gkv"],
               w["wuqaT"], w["wuqbT"], w["wuk"], w["wuvT"]]
    in_specs = ([tok(D_MODEL)] + [_const_spec(a.shape) for a in weights]
                + [pl.BlockSpec((tm, LANES), lambda b, i: (i, 0)),
                   pl.BlockSpec((tm, LANES), lambda b, i: (i, 0)),
                   pl.BlockSpec((2 * LANES, tm), lambda b, i: (0, i)),
                   pl.BlockSpec((2 * LANES, tm), lambda b, i: (0, i))])
    out_shape = [sds((nb, D_MODEL, t), BF), sds((nb, t, D_MODEL), F32), sds((nb, t, D_MODEL), BF),
                 sds((nb, t, D_MODEL), F32), sds((nb, t, D_MODEL), BF),
                 sds((nb, MLA_HEADS * LANES, t), BF), sds((nb, t, MLA_KV_LORA), F32),
                 sds((nb, t, D_MODEL), BF), sds((nb, D_MODEL, t), BF),
                 sds((nb, t, LANES), BF), sds((nb, t, MLA_ROPE_DIM), F32)]
    out_specs = [feat(D_MODEL), tok(D_MODEL), tok(D_MODEL), tok(D_MODEL), tok(D_MODEL),
                 feat(MLA_HEADS * LANES), tok(MLA_KV_LORA), tok(D_MODEL), feat(D_MODEL),
                 tok(LANES), tok(MLA_ROPE_DIM)]
    outs = pl.pallas_call(
        _proj_kernel, grid=(nb, nt), in_specs=in_specs, out_specs=out_specs, out_shape=out_shape,
        compiler_params=_params(2), name="project",
    )(x, *weights, tabs["cosp"], tabs["sinp"], tabs["ct"], tabs["st"])
    names = ["qTd", "kf", "kb", "vf", "vb", "qTm", "ckv", "kn", "vTm", "r", "kr"]
    return dict(zip(names, outs))


def _flash_step(state, s, pv):
    bm = jnp.max(s, axis=0, keepdims=True)
    if state is None:
        p = jnp.exp2(s - bm)
        return bm, jnp.sum(p, axis=0, keepdims=True), pv(p.astype(BF))
    m, l, acc = state
    m_new = jnp.maximum(m, bm)
    alpha = jnp.exp2(m - m_new)
    p = jnp.exp2(s - m_new)
    return m_new, alpha * l + jnp.sum(p, axis=0, keepdims=True), alpha * acc + pv(p.astype(BF))


def _normalized(state):
    _, l, acc = state
    return acc * (1.0 / l)


def _split_maps(qT):
    rows = lax.broadcasted_iota(jnp.int32, qT.shape, 0)
    qf = qT.astype(F32)
    zero = jnp.zeros_like(qf)
    return (jnp.where(rows < DIFF_HEAD_DIM, qf, zero).astype(BF),
            jnp.where(rows >= DIFF_HEAD_DIM, qf, zero).astype(BF))


def _lambda(lam_ref):
    lp = lam_ref[...]
    a = jnp.sum(lp[0:1, :] * lp[1:2, :], axis=1, keepdims=True)
    b = jnp.sum(lp[2:3, :] * lp[3:4, :], axis=1, keepdims=True)
    return jnp.exp(a) - jnp.exp(b) + LAMBDA_INIT


def _diff_finish(o1, o2, lam, g):
    o = o1 - lam * o2
    ms = jnp.mean(o * o, axis=0, keepdims=True)
    return (o * lax.rsqrt(ms + EPS) * g * (1.0 - LAMBDA_INIT)).astype(BF)


def _mla_keys(kp, r):
    lane = lax.broadcasted_iota(jnp.int32, kp.shape, 1)
    kpf = kp.astype(F32)
    rf = r.astype(F32)
    return (jnp.where(lane < MLA_NOPE_DIM, kpf, rf).astype(BF),
            jnp.where(lane >= MLA_NOPE_DIM, kpf, rf).astype(BF))


def _qslice(qb):
    return slice(qb * TQ, (qb + 1) * TQ)


def _kslice(kb):
    return slice(kb * TK, (kb + 1) * TK)


def _with_ones(va_ref, vT):
    dv, t = vT.shape
    va_ref[0:dv, :] = vT
    va_ref[dv:dv + ONES_ROWS, :] = jnp.ones((ONES_ROWS, t), BF)


def _weighted_mean(acc):
    dv = acc.shape[0] - ONES_ROWS
    return acc[0:dv, :] * (1.0 / acc[dv:dv + 1, :])


def _causal_two_pass(nq, prep, scores, va_refs, s_ref, p_ref, emit):
    n_maps = len(va_refs)

    def score_pass(qb):
        ops = prep(qb)
        m8 = [None] * n_maps
        for kb in range(qb + 1):
            for i, s in enumerate(scores(ops, qb, kb)):
                s_ref[qb % 2, i, _kslice(kb), :] = s
                c = jnp.max(s.reshape(TK // 8, 8, TQ), axis=0)
                m8[i] = c if m8[i] is None else jnp.maximum(m8[i], c)
        return [jnp.max(m, axis=0, keepdims=True) for m in m8]

    def exp_pass(qb, ms):
        for kb in range(qb + 1):
            for i in range(n_maps):
                p_ref[qb % 2, i, _kslice(kb), :] = jnp.exp2(s_ref[qb % 2, i, _kslice(kb), :] - ms[i]).astype(BF)

    def value_pass(qb):
        klen = (qb + 1) * TK
        emit(qb, [_nn(va_refs[i][:, 0:klen], p_ref[qb % 2, i, 0:klen, :]) for i in range(n_maps)])

    ms = score_pass(0)
    for qb in range(nq + 1):
        ms_next = score_pass(qb + 1) if qb + 1 < nq else None
        if qb < nq:
            exp_pass(qb, ms)
        if qb >= 1:
            value_pass(qb - 1)
        ms = ms_next


def _diff_prompt_kernel(qT_ref, k_ref, v_ref, bias_ref, lam_ref, g_ref, o_ref, va_ref, s_ref, p_ref, *, t):
    lam = _lambda(lam_ref)
    g = g_ref[...]
    _with_ones(va_ref, v_ref[...].astype(F32).T.astype(BF))

    def prep(qb):
        return _split_maps(qT_ref[:, _qslice(qb)])

    def scores(ops, qb, kb):
        kblk = k_ref[_kslice(kb), :]
        out = [_nn(kblk, q) for q in ops]
        if kb >= qb - 1:
            out = [s + bias_ref[qb - kb] for s in out]
        return out

    def emit(qb, accs):
        o_ref[:, _qslice(qb)] = _diff_finish(_weighted_mean(accs[0]), _weighted_mean(accs[1]), lam, g)

    _causal_two_pass(t // TQ, prep, scores, [va_ref, va_ref], s_ref, p_ref, emit)


def _score_scratch(n_maps, t):
    return [pltpu.VMEM((2, n_maps, t, TQ), F32), pltpu.VMEM((2, n_maps, t, TQ), BF)]


def _diff_prompt(p, bias, lamp, g):
    nb, _, t = p["qTd"].shape
    head_feat = pl.BlockSpec((None, DIFF_V_DIM, t), lambda b, h: (b, h, 0))
    head_tok = pl.BlockSpec((None, t, DIFF_V_DIM), lambda b, h: (b, 0, h))
    return pl.pallas_call(
        functools.partial(_diff_prompt_kernel, t=t),
        grid=(nb, DIFF_HEADS),
        in_specs=[head_feat, head_tok, head_tok,
                  pl.BlockSpec((None, 2, TK, TQ), lambda b, h: (h, 0, 0, 0)),
                  _const_spec(lamp.shape), _const_spec(g.shape)],
        out_specs=head_feat,
        out_shape=jax.ShapeDtypeStruct((nb, D_MODEL, t), BF),
        scratch_shapes=[pltpu.VMEM((DIFF_V_DIM + ONES_ROWS, t), BF)] + _score_scratch(2, t),
        compiler_params=_params(2), name="diff_prompt",
    )(p["qTd"], p["kb"], p["vb"], bias, lamp, g)


def _mla_prompt_kernel(qT_ref, kp_ref, r_ref, vT_ref, mask_ref, o_ref, ke_ref, ko_ref, vae_ref, vao_ref,
                       s_ref, p_ref, *, t):
    ke, ko = _mla_keys(kp_ref[...], r_ref[...])
    ke_ref[...] = ke
    ko_ref[...] = ko
    _with_ones(vae_ref, vT_ref[0:MLA_V_DIM, :])
    _with_ones(vao_ref, vT_ref[MLA_V_DIM:2 * MLA_V_DIM, :])

    def prep(qb):
        return qT_ref[0:LANES, _qslice(qb)], qT_ref[LANES:2 * LANES, _qslice(qb)]

    def scores(ops, qb, kb):
        out = [_nn(ke_ref[_kslice(kb), :], ops[0]), _nn(ko_ref[_kslice(kb), :], ops[1])]
        if kb == qb:
            out = [s + mask_ref[...] for s in out]
        return out

    def emit(qb, accs):
        o_ref[0:MLA_V_DIM, _qslice(qb)] = _weighted_mean(accs[0]).astype(BF)
        o_ref[MLA_V_DIM:2 * MLA_V_DIM, _qslice(qb)] = _weighted_mean(accs[1]).astype(BF)

    _causal_two_pass(t // TQ, prep, scores, [vae_ref, vao_ref], s_ref, p_ref, emit)


def _mla_prompt(p, mask):
    nb, _, t = p["qTm"].shape
    return pl.pallas_call(
        functools.partial(_mla_prompt_kernel, t=t),
        grid=(nb, MLA_HEADS // 2),
        in_specs=[pl.BlockSpec((None, 2 * LANES, t), lambda b, h: (b, h, 0)),
                  pl.BlockSpec((None, t, LANES), lambda b, h: (b, 0, h)),
                  pl.BlockSpec((None, t, LANES), lambda b, h: (b, 0, 0)),
                  pl.BlockSpec((None, 2 * MLA_V_DIM, t), lambda b, h: (b, h, 0)),
                  _const_spec(mask.shape)],
        out_specs=pl.BlockSpec((None, 2 * MLA_V_DIM, t), lambda b, h: (b, h, 0)),
        out_shape=jax.ShapeDtypeStruct((nb, D_MODEL, t), BF),
        scratch_shapes=[pltpu.VMEM((t, LANES), BF), pltpu.VMEM((t, LANES), BF),
                        pltpu.VMEM((MLA_V_DIM + ONES_ROWS, t), BF),
                        pltpu.VMEM((MLA_V_DIM + ONES_ROWS, t), BF)] + _score_scratch(2, t),
        compiler_params=_params(2), name="mla_prompt",
    )(p["qTm"], p["kn"], p["r"], p["vTm"], mask)


def _diff_decode_kernel(qT_ref, ck_ref, cv_ref, kn_ref, vn_ref, bc_ref, bn_ref, lam_ref, g_ref, o_ref,
                        m_ref, l_ref, acc_ref, *, new):
    c = pl.program_id(1)
    last = pl.num_programs(1) - 1

    @pl.when(c == 0)
    def _():
        m_ref[...] = jnp.full(m_ref.shape, NEG_INF, F32)
        l_ref[...] = jnp.zeros(l_ref.shape, F32)
        acc_ref[...] = jnp.zeros(acc_ref.shape, F32)

    def update(h, keys, vals, bias):
        hs = slice(h * DIFF_V_DIM, (h + 1) * DIFF_V_DIM)
        for i, q in enumerate(_split_maps(qT_ref[hs, :])):
            s = _nn(keys, q)
            if bias is not None:
                s = s + bias
            j = 2 * h + i
            m_ref[j], l_ref[j], acc_ref[j] = _flash_step((m_ref[j], l_ref[j], acc_ref[j]), s,
                                                         lambda p: _tn(vals, p))

    def cache_chunk(with_bias):
        for h in range(DIFF_HEADS):
            rows = pl.ds(h, CACHE_BLK, stride=DIFF_HEADS)
            update(h, ck_ref[rows, :].astype(BF), cv_ref[rows, :].astype(BF),
                   bc_ref[h] if with_bias else None)

    @pl.when(c < last)
    def _():
        cache_chunk(False)

    @pl.when(c == last)
    def _():
        cache_chunk(True)
        lam = _lambda(lam_ref)
        for h in range(DIFF_HEADS):
            hs = slice(h * DIFF_V_DIM, (h + 1) * DIFF_V_DIM)
            update(h, kn_ref[0:new, hs], vn_ref[0:new, hs].astype(BF), bn_ref[h])
            o1, o2 = (acc_ref[2 * h + i] * (1.0 / l_ref[2 * h + i]) for i in range(2))
            o_ref[hs, :] = _diff_finish(o1, o2, lam, g_ref[...])


def _diff_decode(s, cache_k, cache_v, bias_c, bias_n, lamp, g, new):
    nb, past = cache_k.shape[:2]
    cache_k, cache_v = (a.reshape(nb, past * DIFF_HEADS, DIFF_V_DIM) for a in (cache_k, cache_v))
    stream = lambda rows: pl.BlockSpec((None, rows, D_MODEL), lambda b, c: (b, 0, 0))
    chunk = pl.BlockSpec((None, CACHE_BLK * DIFF_HEADS, DIFF_V_DIM), lambda b, c: (b, c, 0))
    tile = lambda rows: pl.BlockSpec((DIFF_HEADS, None, rows, DEC_PAD), lambda b, c: (0, 0, 0, 0))
    return pl.pallas_call(
        functools.partial(_diff_decode_kernel, new=new),
        grid=(nb, past // CACHE_BLK),
        in_specs=[pl.BlockSpec((None, D_MODEL, DEC_PAD), lambda b, c: (b, 0, 0)), chunk, chunk,
                  stream(DEC_PAD), stream(DEC_PAD), tile(CACHE_BLK), tile(new),
                  _const_spec(lamp.shape), _const_spec(g.shape)],
        out_specs=pl.BlockSpec((None, D_MODEL, DEC_PAD), lambda b, c: (b, 0, 0)),
        out_shape=jax.ShapeDtypeStruct((nb, D_MODEL, DEC_PAD), BF),
        scratch_shapes=[pltpu.VMEM((2 * DIFF_HEADS, 1, DEC_PAD), F32),
                        pltpu.VMEM((2 * DIFF_HEADS, 1, DEC_PAD), F32),
                        pltpu.VMEM((2 * DIFF_HEADS, DIFF_V_DIM, DEC_PAD), F32)],
        compiler_params=_params(2), name="diff_decode",
    )(s["qTd"], cache_k, cache_v, s["kb"], s["vf"], bias_c, bias_n, lamp, g)


def _mla_decode_kernel(qT_ref, ckv_ref, kr_ref, ckvn_ref, krn_ref, wuk_ref, wuvT_ref, o_ref, s_ref,
                       *, past, new):
    nq = MLA_HEADS * new
    assert nq == 2 * LANES and new & (new - 1) == 0
    log2 = lambda n: n.bit_length() - 1

    def groups(shape, rows_per_group):
        r = lax.broadcasted_iota(jnp.int32, shape, 0)
        c = lax.broadcasted_iota(jnp.int32, shape, 1)
        return lax.shift_right_logical(r, log2(rows_per_group)) == lax.shift_right_logical(c, log2(new))

    def frame_match(shape, lane_axis):
        a = lax.broadcasted_iota(jnp.int32, shape, lane_axis)
        b = lax.broadcasted_iota(jnp.int32, shape, 1 - lane_axis)
        return jnp.where((a & (new - 1)) == b, 1.0, 0.0).astype(BF)

    nope, rope = [], []
    for h in range(MLA_HEADS):
        base = h * LANES
        o_n, o_r = (0, MLA_NOPE_DIM) if h % 2 == 0 else (MLA_NOPE_DIM, 0)
        nope.append(qT_ref[base + o_n:base + o_n + MLA_NOPE_DIM, :])
        rope.append(qT_ref[base + o_r:base + o_r + MLA_ROPE_DIM, :])
    spread = frame_match((DEC_PAD, nq), 1)
    qn = _nn(jnp.concatenate(nope, axis=0), spread)
    qn_bd = jnp.where(groups(qn.shape, MLA_NOPE_DIM), qn, 0.0).astype(BF)
    qr = _nn(jnp.concatenate(rope, axis=0), spread)
    qr_bd = jnp.where(groups(qr.shape, MLA_ROPE_DIM), qr, 0.0)
    qrT = jnp.sum(qr_bd.reshape(MLA_HEADS, MLA_ROPE_DIM, nq), axis=0).astype(BF)
    qlatT = _nn(wuk_ref[...], qn_bd).astype(BF)

    def keys(lo, n):
        if lo < past:
            return ckv_ref[lo:lo + n, :].astype(BF), kr_ref[lo:lo + n, :].astype(BF)
        return ckvn_ref[0:new, :].astype(BF), krn_ref[0:new, :].astype(BF)

    spans = [(c0, CACHE_BLK) for c0 in range(0, past, CACHE_BLK)] + [(past, new)]
    m = None
    for lo, n in spans:
        ck, kr = keys(lo, n)
        s = _nn(ck, qlatT) + _nn(kr, qrT)
        s_ref[lo:lo + n, :] = s
        c = jnp.max(s.reshape(n // 8, 8, nq), axis=0)
        m = c if m is None else jnp.maximum(m, c)
    m = jnp.max(m, axis=0, keepdims=True)
    l = acc = None
    for lo, n in spans:
        p = jnp.exp2(s_ref[lo:lo + n, :] - m)
        ps = jnp.sum(p.reshape(n // 8, 8, nq), axis=0)
        pv = _tn(keys(lo, n)[0], p.astype(BF))
        l, acc = (ps, pv) if l is None else (l + ps, acc + pv)
    o_latT = (acc * (1.0 / jnp.sum(l, axis=0, keepdims=True))).astype(BF)
    y = _nn(wuvT_ref[...], o_latT)
    y_bd = jnp.where(groups(y.shape, MLA_V_DIM), y, 0.0).astype(BF)
    o_ref[...] = _nn(y_bd, frame_match((nq, DEC_PAD), 0)).astype(BF)


def _mla_decode(s, cache_ckv, cache_kr, w, new):
    nb, past, _ = cache_ckv.shape
    per_stream = lambda rows, cols: pl.BlockSpec((None, rows, cols), lambda b: (b, 0, 0))
    return pl.pallas_call(
        functools.partial(_mla_decode_kernel, past=past, new=new),
        grid=(nb,),
        in_specs=[per_stream(MLA_HEADS * LANES, DEC_PAD), per_stream(past, MLA_KV_LORA),
                  per_stream(past, MLA_ROPE_DIM), per_stream(DEC_PAD, MLA_KV_LORA),
                  per_stream(DEC_PAD, MLA_ROPE_DIM), _const_spec(w["wuk"].shape),
                  _const_spec(w["wuvT"].shape)],
        out_specs=per_stream(D_MODEL, DEC_PAD),
        out_shape=jax.ShapeDtypeStruct((nb, D_MODEL, DEC_PAD), BF),
        scratch_shapes=[pltpu.VMEM((past + new, MLA_HEADS * new), F32)],
        compiler_params=_params(1), name="mla_decode",
    )(s["qTm"], cache_ckv, cache_kr, s["ckv"], s["kr"], w["wuk"], w["wuvT"])


def _post_kernel(x_ref, oTd_ref, oTm_ref, gmix_ref, wgT_ref, wodT_ref, womT_ref, woutT_ref,
                 gmlp_ref, wup_ref, wdown_ref, gfin_ref, y_ref, *, n_groups):
    rows = x_ref.shape[0] // n_groups
    groups = [slice(i * rows, (i + 1) * rows) for i in range(n_groups)]
    d = D_MODEL

    def norm_in(v, r):
        v["x"] = x_ref[r, :]
        v["hb"] = _rms(v["x"], gmix_ref[...]).astype(BF)

    def branches(v, r):
        v["gT"] = _nt(wgT_ref[...], v.pop("hb"))
        v["od"] = _nn(wodT_ref[...], oTd_ref[:, r])
        v["om"] = _nn(womT_ref[...], oTm_ref[:, r])

    def merge(v, r):
        g = jax.nn.sigmoid(v.pop("gT"))
        v["mT"] = (g[0:d, :] * v.pop("od") + g[d:2 * d, :] * v.pop("om")).astype(BF)

    def out_proj(v, r):
        v["aT"] = _nn(woutT_ref[...], v.pop("mT"))

    def residual(v, r):
        v["x1"] = v.pop("x") + v.pop("aT").T
        v["h2"] = _rms(v["x1"], gmlp_ref[...]).astype(BF)

    def up(v, r):
        v["u"] = _nn(v.pop("h2"), wup_ref[...])

    def act(v, r):
        u = jnp.maximum(v.pop("u"), 0.0)
        v["uu"] = (u * u).astype(BF)

    def down(v, r):
        v["x2"] = v.pop("x1") + _nn(v.pop("uu"), wdown_ref[...])

    def norm_out(v, r):
        y_ref[r, :] = _rms(v.pop("x2"), gfin_ref[...])

    values = [{} for _ in groups]
    for stage in (norm_in, branches, merge, out_proj, residual, up, act, down, norm_out):
        for v, r in zip(values, groups):
            stage(v, r)


def _post(x, oTd, oTm, w, tm, n_groups):
    nb, t, _ = x.shape
    tok = pl.BlockSpec((None, tm, D_MODEL), lambda b, i: (b, i, 0))
    feat = pl.BlockSpec((None, D_MODEL, tm), lambda b, i: (b, 0, i))
    weights = [w["gmix"], w["wgT"], w["wodT"], w["womT"], w["woutT"], w["gmlp"], w["wup"],
               w["wdown"], w["gfin"]]
    return pl.pallas_call(
        functools.partial(_post_kernel, n_groups=n_groups), grid=(nb, t // tm),
        in_specs=[tok, feat, feat] + [_const_spec(a.shape) for a in weights],
        out_specs=tok, out_shape=jax.ShapeDtypeStruct((nb, t, D_MODEL), F32),
        compiler_params=_params(2), name="post",
    )(x, oTd, oTm, *weights)


def _prep_weights(norm_mix, w_in, mla_q_norm, mla_w_uq, mla_kv_norm, mla_w_uk, mla_w_uv, w_o_diff,
                  w_o_mla, w_out, norm_mlp, w_up, w_down, norm_final):
    d = D_MODEL
    o_cq = 3 * d
    o_ckv = o_cq + MLA_Q_LORA
    o_kr = o_ckv + MLA_KV_LORA
    o_g = o_kr + MLA_ROPE_DIM
    half = MLA_ROPE_DIM // 2
    rot = lambda a: jnp.concatenate([-a[..., half:], a[..., :half]], axis=-1)
    zpad = lambda a, n: jnp.zeros(a.shape[:-1] + (n,), a.dtype)

    wkr = w_in[:, o_kr:o_g]
    place = lambda a: jnp.concatenate([a, zpad(a, 32), a, zpad(a, 32)], axis=-1)
    wlat = jnp.concatenate([w_in[:, o_cq:o_kr], place(wkr), place(rot(wkr))], axis=-1)

    nope = mla_w_uq[:, :, :MLA_NOPE_DIM]
    rope = mla_w_uq[:, :, MLA_NOPE_DIM:]

    def arrange(n, r):
        even = jnp.concatenate([n[:, 0::2], r[:, 0::2], zpad(r[:, 0::2], 32)], axis=-1)
        odd = jnp.concatenate([r[:, 1::2], zpad(r[:, 1::2], 32), n[:, 1::2]], axis=-1)
        both = jnp.stack([even, odd], axis=2)
        return both.reshape(MLA_Q_LORA, MLA_HEADS * LANES)

    row = lambda a: a.reshape(1, -1).astype(F32)
    return {
        "gmix": row(norm_mix), "gq": row(mla_q_norm), "gkv": row(mla_kv_norm),
        "gmlp": row(norm_mlp), "gfin": row(norm_final),
        "wqT": w_in[:, 0:d].T.astype(BF),
        "wk": w_in[:, d:2 * d].astype(BF),
        "wv": w_in[:, 2 * d:3 * d].astype(BF),
        "wlat": wlat.astype(BF),
        "wgT": w_in[:, o_g:].T.astype(BF),
        "wuqaT": arrange(nope, rope).T.astype(BF),
        "wuqbT": arrange(jnp.zeros_like(nope), rot(rope)).T.astype(BF),
        "wuk": mla_w_uk.reshape(MLA_KV_LORA, d).astype(BF),
        "wuvT": mla_w_uv.reshape(MLA_KV_LORA, d).T.astype(BF),
        "wodT": w_o_diff.reshape(d, d).T.astype(BF),
        "womT": w_o_mla.reshape(d, d).T.astype(BF),
        "woutT": w_out.T.astype(BF),
        "wup": w_up.astype(BF),
        "wdown": w_down.astype(BF),
    }


def _rope_tables(pos):
    half = MLA_ROPE_DIM // 2
    inv = jnp.power(ROPE_THETA, -jnp.arange(half, dtype=F32) * 2.0 / MLA_ROPE_DIM)
    ang = pos.astype(F32)[:, None] * inv[None, :]
    cos2 = jnp.tile(jnp.cos(ang), (1, 2))
    sin2 = jnp.tile(jnp.sin(ang), (1, 2))
    t = pos.shape[0]
    z32 = jnp.zeros((t, 32), F32)
    place = lambda a: jnp.concatenate([a, z32, a, z32], axis=1)
    scale = (MLA_NOPE_DIM + MLA_ROPE_DIM) ** -0.5 * LOG2E
    ones = jnp.ones((t, MLA_NOPE_DIM), F32)
    z64 = jnp.zeros((t, MLA_NOPE_DIM), F32)
    ct = jnp.concatenate([ones, cos2, z32, cos2, z32, ones], axis=1) * scale
    st = jnp.concatenate([z64, sin2, z32, sin2, z32, z64], axis=1) * scale
    return {"cosp": place(cos2), "sinp": place(sin2), "ct": ct.T, "st": st.T}


def _score_tiles(rel_bias, k_pos_tiles, q_pos, shift_bucket):
    k_pos = jnp.stack(k_pos_tiles)
    rel = k_pos[:, :, None] - q_pos[None, None, :]
    mask = (k_pos // CHUNK)[:, :, None] <= (q_pos // CHUNK)[None, None, :]
    return _bias_tiles(rel_bias, _t5_bucket(rel).astype(jnp.int32), mask.astype(jnp.int32), shift_bucket)


def kernel(x_prompt, x_sample, cache_diff_k, cache_diff_v, cache_mla_ckv, cache_mla_krope, rel_bias,
           norm_mix, w_in, lam_q1, lam_k1, lam_q2, lam_k2, diff_subln, mla_q_norm, mla_w_uq, mla_kv_norm,
           mla_w_uk, mla_w_uv, w_o_diff, w_o_mla, w_out, norm_mlp, w_up, w_down, norm_final):
    assert norm_mix.shape[0] == 1, "single-layer model"
    nb, t, _ = x_prompt.shape
    ns, new, _ = x_sample.shape
    past = cache_diff_k.shape[2]
    assert t % TQ == 0 and past % CACHE_BLK == 0 and CACHE_BLK >= TK and new <= DEC_PAD and new % 16 == 0
    assert past % CHUNK == 0 and new <= CHUNK

    w = _prep_weights(norm_mix[0], w_in[0], mla_q_norm[0], mla_w_uq[0], mla_kv_norm[0], mla_w_uk[0],
                      mla_w_uv[0], w_o_diff[0], w_o_mla[0], w_out[0], norm_mlp[0], w_up[0], w_down[0],
                      norm_final)
    lamp = jnp.stack([lam_q1[0], lam_k1[0], lam_q2[0], lam_k2[0]]).astype(F32)
    g_sub = diff_subln[0].reshape(DIFF_V_DIM, 1).astype(F32)

    pos_p = jnp.arange(t, dtype=jnp.int32)
    p = _project(x_prompt, w, _rope_tables(pos_p), tm=256)
    blk = jnp.arange(TQ, dtype=jnp.int32)
    tiles_p = _score_tiles(rel_bias, [TQ + blk, blk], TQ + blk, _far_bucket())
    mask_p = jnp.where((blk // CHUNK)[:, None] <= (blk // CHUNK)[None, :], 0.0, NEG_INF).astype(F32)
    oTd = _diff_prompt(p, tiles_p, lamp, g_sub)
    oTm = _mla_prompt(p, mask_p)
    y_prompt = _post(x_prompt, oTd, oTm, w, tm=512, n_groups=2)

    xs = jnp.pad(x_sample, ((0, 0), (0, DEC_PAD - new), (0, 0)))
    pos_s = past + jnp.arange(DEC_PAD, dtype=jnp.int32)
    s = _project(xs, w, _rope_tables(pos_s), tm=DEC_PAD)
    bias_c = _score_tiles(rel_bias, [jnp.arange(past - CACHE_BLK, past, dtype=jnp.int32)], pos_s, _far_bucket())
    bias_n = _score_tiles(rel_bias, [pos_s[:new]], pos_s, _far_bucket())
    oTd_s = _diff_decode(s, cache_diff_k[0], cache_diff_v[0], bias_c, bias_n, lamp, g_sub, new)
    oTm_s = _mla_decode(s, cache_mla_ckv[0], cache_mla_krope[0], w, new)
    y_sample = _post(xs, oTd_s, oTm_s, w, tm=DEC_PAD, n_groups=1)[:, :new]

    heads = lambda a, n: a.reshape((1,) + a.shape[:2] + (DIFF_HEADS, n))
    return (y_prompt, y_sample,
            heads(p["kf"], 2 * DIFF_HEAD_DIM), heads(p["vf"], DIFF_V_DIM), p["ckv"][None], p["kr"][None],
            heads(s["kf"][:, :new], 2 * DIFF_HEAD_DIM), heads(s["vf"][:, :new], DIFF_V_DIM),
            s["ckv"][:, :new][None], s["kr"][:, :new][None])
```

```python
import functools
import math

import numpy as np
import jax
import jax.numpy as jnp
from jax import lax
from jax.experimental import pallas as pl
from jax.experimental.pallas import tpu as pltpu

D_MODEL = 1024
CHUNK = 64
DIFF_HEADS = 8
DIFF_HEAD_DIM = 64
DIFF_V_DIM = 128
MLA_HEADS = 16
MLA_NOPE_DIM = 64
MLA_ROPE_DIM = 32
MLA_V_DIM = 64
MLA_Q_LORA = 256
MLA_KV_LORA = 256
D_FF = 4 * D_MODEL
NUM_BUCKETS = 32
MAX_DISTANCE = 128
ROPE_THETA = 10000.0
EPS = 1e-6
NEG_INF = -1e30
LAMBDA_INIT = 0.8 - 0.6 * math.exp(-0.3 * 0)

BF = jnp.bfloat16
F32 = jnp.float32

LANES = 128
TQ = 256
TK = 256
DEC_PAD = LANES
CACHE_BLK = 512
VMEM_LIMIT = 56 * 1024 * 1024
LOG2E = math.log2(math.e)
ONES_ROWS = 16


def _nn(a, b):
    return jnp.dot(a, b, preferred_element_type=F32)


def _nt(a, b):
    return lax.dot_general(a, b, (((1,), (1,)), ((), ())), preferred_element_type=F32)


def _tn(a, b):
    return lax.dot_general(a, b, (((0,), (0,)), ((), ())), preferred_element_type=F32)


def _rms(x, g):
    return x * lax.rsqrt(jnp.mean(x * x, axis=-1, keepdims=True) + EPS) * g


def _const_spec(shape):
    zeros = (0,) * len(shape)
    return pl.BlockSpec(shape, lambda *_: zeros, pipeline_mode=pl.Buffered(1))


def _params(n_axes):
    return pltpu.CompilerParams(dimension_semantics=("arbitrary",) * n_axes,
                                vmem_limit_bytes=VMEM_LIMIT)


def _bias_kernel(tab_ref, bkt_ref, msk_ref, o_ref, *, shift_bucket):
    h = pl.program_id(0)
    bkt = bkt_ref[...]
    acc = jnp.zeros(bkt.shape, F32)
    for b in range(NUM_BUCKETS):
        acc = jnp.where(bkt == b, tab_ref[b, h], acc)
    if shift_bucket is not None:
        acc = acc - tab_ref[shift_bucket, h]
    o_ref[...] = jnp.where(msk_ref[...] != 0, acc * LOG2E, NEG_INF)


def _bias_tiles(rel_bias, bucket, mask, shift_bucket):
    n, r, c = bucket.shape
    return pl.pallas_call(
        functools.partial(_bias_kernel, shift_bucket=shift_bucket),
        grid=(DIFF_HEADS,),
        in_specs=[pl.BlockSpec(memory_space=pltpu.SMEM),
                  pl.BlockSpec((n, r, c), lambda h: (0, 0, 0)),
                  pl.BlockSpec((n, r, c), lambda h: (0, 0, 0))],
        out_specs=pl.BlockSpec((None, n, r, c), lambda h: (h, 0, 0, 0)),
        out_shape=jax.ShapeDtypeStruct((DIFF_HEADS, n, r, c), F32),
        compiler_params=_params(1),
        name="bias_tiles",
    )(rel_bias, bucket, mask)


def _bias_lanes_kernel(tab_ref, bkt_ref, msk_ref, o_ref, *, shift_bucket):
    bkt = bkt_ref[...]
    acc = jnp.zeros(bkt.shape, F32)
    for b in range(NUM_BUCKETS):
        acc = jnp.where(bkt == b, tab_ref[b:b + 1, :], acc)
    if shift_bucket is not None:
        acc = acc - tab_ref[shift_bucket:shift_bucket + 1, :]
    o_ref[...] = jnp.where(msk_ref[...] != 0, acc * LOG2E, NEG_INF)


def _bias_lanes(tab, bucket, mask, shift_bucket):
    return pl.pallas_call(
        functools.partial(_bias_lanes_kernel, shift_bucket=shift_bucket),
        out_shape=jax.ShapeDtypeStruct(bucket.shape, F32), name="bias_lanes",
    )(tab, bucket, mask)


def _t5_bucket(rel):
    half = NUM_BUCKETS // 2
    max_exact = half // 2
    n = jnp.abs(rel)
    nf = jnp.maximum(n, max_exact).astype(F32)
    large = max_exact + (jnp.log(nf / max_exact) / math.log(MAX_DISTANCE / max_exact)
                         * (half - max_exact)).astype(jnp.int32)
    large = jnp.minimum(large, half - 1)
    return jnp.where(rel > 0, half, 0) + jnp.where(n < max_exact, n, large)


def _far_bucket():
    half = NUM_BUCKETS // 2
    max_exact = half // 2
    assert max_exact + math.log(TK / max_exact) / math.log(MAX_DISTANCE / max_exact) * (half - max_exact) >= half
    return half - 1


def _proj_kernel(x_ref, gmix_ref, wqT_ref, wk_ref, wv_ref, wlat_ref, gq_ref, gkv_ref,
                 wuqaT_ref, wuqbT_ref, wuk_ref, wuvT_ref, cosp_ref, sinp_ref, ct_ref, st_ref,
                 qTd_ref, kf_ref, kb_ref, vf_ref, vb_ref, qTm_ref, ckv_ref, kn_ref, vTm_ref,
                 r_ref, kr_ref):
    hb = _rms(x_ref[...], gmix_ref[...]).astype(BF)
    qTd_ref[...] = (_nt(wqT_ref[...], hb) * (DIFF_HEAD_DIM ** -0.5 * LOG2E)).astype(BF)
    k = _nn(hb, wk_ref[...])
    kf_ref[...] = k
    kb_ref[...] = k.astype(BF)
    v = _nn(hb, wv_ref[...])
    vf_ref[...] = v
    vb_ref[...] = v.astype(BF)

    lat = _nn(hb, wlat_ref[...])
    cq = _rms(lat[:, 0:MLA_Q_LORA], gq_ref[...]).astype(BF)
    qa = _nt(wuqaT_ref[...], cq)
    qb = _nt(wuqbT_ref[...], cq)
    ct = ct_ref[...]
    st = st_ref[...]
    pair = 2 * LANES
    for p in range(MLA_HEADS // 2):
        sl = slice(p * pair, (p + 1) * pair)
        qTm_ref[sl, :] = (qa[sl, :] * ct + qb[sl, :] * st).astype(BF)

    ckv = _rms(lat[:, MLA_Q_LORA:MLA_Q_LORA + MLA_KV_LORA], gkv_ref[...])
    ckv_ref[...] = ckv
    cb = ckv.astype(BF)
    kn_ref[...] = _nn(cb, wuk_ref[...]).astype(BF)
    vTm_ref[...] = _nt(wuvT_ref[...], cb).astype(BF)

    o = MLA_Q_LORA + MLA_KV_LORA
    r = lat[:, o:o + LANES] * cosp_ref[...] + lat[:, o + LANES:o + 2 * LANES] * sinp_ref[...]
    r_ref[...] = r.astype(BF)
    kr_ref[...] = r[:, 0:MLA_ROPE_DIM]


def _project(x, w, tabs, tm):
    nb, t, _ = x.shape
    nt = t // tm
    tok = lambda width: pl.BlockSpec((None, tm, width), lambda b, i: (b, i, 0))
    feat = lambda rows: pl.BlockSpec((None, rows, tm), lambda b, i: (b, 0, i))
    sds = jax.ShapeDtypeStruct
    weights = [w["gmix"], w["wqT"], w["wk"], w["wv"], w["wlat"], w["gq"], w["gkv"],
               w["wuqaT"], w["wuqbT"], w["wuk"], w["wuvT"]]
    in_specs = ([tok(D_MODEL)] + [_const_spec(a.shape) for a in weights]
                + [pl.BlockSpec((tm, LANES), lambda b, i: (i, 0)),
                   pl.BlockSpec((tm, LANES), lambda b, i: (i, 0)),
                   pl.BlockSpec((2 * LANES, tm), lambda b, i: (0, i)),
                   pl.BlockSpec((2 * LANES, tm), lambda b, i: (0, i))])
    out_shape = [sds((nb, D_MODEL, t), BF), sds((nb, t, D_MODEL), F32), sds((nb, t, D_MODEL), BF),
                 sds((nb, t, D_MODEL), F32), sds((nb, t, D_MODEL), BF),
                 sds((nb, MLA_HEADS * LANES, t), BF), sds((nb, t, MLA_KV_LORA), F32),
                 sds((nb, t, D_MODEL), BF), sds((nb, D_MODEL, t), BF),
                 sds((nb, t, LANES), BF), sds((nb, t, MLA_ROPE_DIM), F32)]
    out_specs = [feat(D_MODEL), tok(D_MODEL), tok(D_MODEL), tok(D_MODEL), tok(D_MODEL),
                 feat(MLA_HEADS * LANES), tok(MLA_KV_LORA), tok(D_MODEL), feat(D_MODEL),
                 tok(LANES), tok(MLA_ROPE_DIM)]
    outs = pl.pallas_call(
        _proj_kernel, grid=(nb, nt), in_specs=in_specs, out_specs=out_specs, out_shape=out_shape,
        compiler_params=_params(2), name="project",
    )(x, *weights, tabs["cosp"], tabs["sinp"], tabs["ct"], tabs["st"])
    names = ["qTd", "kf", "kb", "vf", "vb", "qTm", "ckv", "kn", "vTm", "r", "kr"]
    return dict(zip(names, outs))


def _flash_step(state, s, pv):
    bm = jnp.max(s, axis=0, keepdims=True)
    if state is None:
        p = jnp.exp2(s - bm)
        return bm, jnp.sum(p, axis=0, keepdims=True), pv(p.astype(BF))
    m, l, acc = state
    m_new = jnp.maximum(m, bm)
    alpha = jnp.exp2(m - m_new)
    p = jnp.exp2(s - m_new)
    return m_new, alpha * l + jnp.sum(p, axis=0, keepdims=True), alpha * acc + pv(p.astype(BF))


def _normalized(state):
    _, l, acc = state
    return acc * (1.0 / l)


def _split_maps(qT):
    rows = lax.broadcasted_iota(jnp.int32, qT.shape, 0)
    qf = qT.astype(F32)
    zero = jnp.zeros_like(qf)
    return (jnp.where(rows < DIFF_HEAD_DIM, qf, zero).astype(BF),
            jnp.where(rows >= DIFF_HEAD_DIM, qf, zero).astype(BF))


def _lambda(lam_ref):
    lp = lam_ref[...]
    a = jnp.sum(lp[0:1, :] * lp[1:2, :], axis=1, keepdims=True)
    b = jnp.sum(lp[2:3, :] * lp[3:4, :], axis=1, keepdims=True)
    return jnp.exp(a) - jnp.exp(b) + LAMBDA_INIT


def _diff_finish(o1, o2, lam, g):
    o = o1 - lam * o2
    ms = jnp.mean(o * o, axis=0, keepdims=True)
    return (o * lax.rsqrt(ms + EPS) * g * (1.0 - LAMBDA_INIT)).astype(BF)


def _mla_keys(kp, r):
    lane = lax.broadcasted_iota(jnp.int32, kp.shape, 1)
    kpf = kp.astype(F32)
    rf = r.astype(F32)
    return (jnp.where(lane < MLA_NOPE_DIM, kpf, rf).astype(BF),
            jnp.where(lane >= MLA_NOPE_DIM, kpf, rf).astype(BF))


def _qslice(qb):
    return slice(qb * TQ, (qb + 1) * TQ)


def _kslice(kb):
    return slice(kb * TK, (kb + 1) * TK)


def _with_ones(va_ref, vT):
    dv, t = vT.shape
    va_ref[0:dv, :] = vT
    va_ref[dv:dv + ONES_ROWS, :] = jnp.ones((ONES_ROWS, t), BF)


def _weighted_mean(acc):
    dv = acc.shape[0] - ONES_ROWS
    return acc[0:dv, :] * (1.0 / acc[dv:dv + 1, :])


def _causal_two_pass(nq, prep, scores, va_refs, s_ref, p_ref, emit):
    n_maps = len(va_refs)

    def score_pass(qb):
        ops = prep(qb)
        m8 = [None] * n_maps
        for kb in range(qb + 1):
            for i, s in enumerate(scores(ops, qb, kb)):
                s_ref[qb % 2, i, _kslice(kb), :] = s
                c = jnp.max(s.reshape(TK // 8, 8, TQ), axis=0)
                m8[i] = c if m8[i] is None else jnp.maximum(m8[i], c)
        return [jnp.max(m, axis=0, keepdims=True) for m in m8]

    def exp_pass(qb, ms):
        for kb in range(qb + 1):
            for i in range(n_maps):
                p_ref[qb % 2, i, _kslice(kb), :] = jnp.exp2(s_ref[qb % 2, i, _kslice(kb), :] - ms[i]).astype(BF)

    def value_pass(qb):
        klen = (qb + 1) * TK
        emit(qb, [_nn(va_refs[i][:, 0:klen], p_ref[qb % 2, i, 0:klen, :]) for i in range(n_maps)])

    ms = score_pass(0)
    for qb in range(nq + 1):
        ms_next = score_pass(qb + 1) if qb + 1 < nq else None
        if qb < nq:
            exp_pass(qb, ms)
        if qb >= 1:
            value_pass(qb - 1)
        ms = ms_next


def _diff_prompt_kernel(qT_ref, k_ref, v_ref, bias_ref, lam_ref, g_ref, o_ref, va_ref, s_ref, p_ref, *, t):
    lam = _lambda(lam_ref)
    g = g_ref[...]
    _with_ones(va_ref, v_ref[...].astype(F32).T.astype(BF))

    def prep(qb):
        return _split_maps(qT_ref[:, _qslice(qb)])

    def scores(ops, qb, kb):
        kblk = k_ref[_kslice(kb), :]
        out = [_nn(kblk, q) for q in ops]
        if kb >= qb - 1:
            out = [s + bias_ref[qb - kb] for s in out]
        return out

    def emit(qb, accs):
        o_ref[:, _qslice(qb)] = _diff_finish(_weighted_mean(accs[0]), _weighted_mean(accs[1]), lam, g)

    _causal_two_pass(t // TQ, prep, scores, [va_ref, va_ref], s_ref, p_ref, emit)


def _score_scratch(n_maps, t):
    return [pltpu.VMEM((2, n_maps, t, TQ), F32), pltpu.VMEM((2, n_maps, t, TQ), BF)]


def _diff_prompt(p, bias, lamp, g):
    nb, _, t = p["qTd"].shape
    head_feat = pl.BlockSpec((None, DIFF_V_DIM, t), lambda b, h: (b, h, 0))
    head_tok = pl.BlockSpec((None, t, DIFF_V_DIM), lambda b, h: (b, 0, h))
    return pl.pallas_call(
        functools.partial(_diff_prompt_kernel, t=t),
        grid=(nb, DIFF_HEADS),
        in_specs=[head_feat, head_tok, head_tok,
                  pl.BlockSpec((None, 2, TK, TQ), lambda b, h: (h, 0, 0, 0)),
                  _const_spec(lamp.shape), _const_spec(g.shape)],
        out_specs=head_feat,
        out_shape=jax.ShapeDtypeStruct((nb, D_MODEL, t), BF),
        scratch_shapes=[pltpu.VMEM((DIFF_V_DIM + ONES_ROWS, t), BF)] + _score_scratch(2, t),
        compiler_params=_params(2), name="diff_prompt",
    )(p["qTd"], p["kb"], p["vb"], bias, lamp, g)


def _mla_prompt_kernel(qT_ref, kp_ref, r_ref, vT_ref, mask_ref, o_ref, ke_ref, ko_ref, vae_ref, vao_ref,
                       s_ref, p_ref, *, t):
    ke, ko = _mla_keys(kp_ref[...], r_ref[...])
    ke_ref[...] = ke
    ko_ref[...] = ko
    _with_ones(vae_ref, vT_ref[0:MLA_V_DIM, :])
    _with_ones(vao_ref, vT_ref[MLA_V_DIM:2 * MLA_V_DIM, :])

    def prep(qb):
        return qT_ref[0:LANES, _qslice(qb)], qT_ref[LANES:2 * LANES, _qslice(qb)]

    def scores(ops, qb, kb):
        out = [_nn(ke_ref[_kslice(kb), :], ops[0]), _nn(ko_ref[_kslice(kb), :], ops[1])]
        if kb == qb:
            out = [s + mask_ref[...] for s in out]
        return out

    def emit(qb, accs):
        o_ref[0:MLA_V_DIM, _qslice(qb)] = _weighted_mean(accs[0]).astype(BF)
        o_ref[MLA_V_DIM:2 * MLA_V_DIM, _qslice(qb)] = _weighted_mean(accs[1]).astype(BF)

    _causal_two_pass(t // TQ, prep, scores, [vae_ref, vao_ref], s_ref, p_ref, emit)


def _mla_prompt(p, mask):
    nb, _, t = p["qTm"].shape
    return pl.pallas_call(
        functools.partial(_mla_prompt_kernel, t=t),
        grid=(nb, MLA_HEADS // 2),
        in_specs=[pl.BlockSpec((None, 2 * LANES, t), lambda b, h: (b, h, 0)),
                  pl.BlockSpec((None, t, LANES), lambda b, h: (b, 0, h)),
                  pl.BlockSpec((None, t, LANES), lambda b, h: (b, 0, 0)),
                  pl.BlockSpec((None, 2 * MLA_V_DIM, t), lambda b, h: (b, h, 0)),
                  _const_spec(mask.shape)],
        out_specs=pl.BlockSpec((None, 2 * MLA_V_DIM, t), lambda b, h: (b, h, 0)),
        out_shape=jax.ShapeDtypeStruct((nb, D_MODEL, t), BF),
        scratch_shapes=[pltpu.VMEM((t, LANES), BF), pltpu.VMEM((t, LANES), BF),
                        pltpu.VMEM((MLA_V_DIM + ONES_ROWS, t), BF),
                        pltpu.VMEM((MLA_V_DIM + ONES_ROWS, t), BF)] + _score_scratch(2, t),
        compiler_params=_params(2), name="mla_prompt",
    )(p["qTm"], p["kn"], p["r"], p["vTm"], mask)


def _diff_decode_kernel(qT_ref, ck_ref, cv_ref, kn_ref, vn_ref, bc_ref, bn_ref, lam_ref, g_ref, o_ref,
                        qbd_ref, m_ref, l_ref, acc_ref, *, new):
    nq = 2 * DIFF_HEADS * new
    assert nq == 2 * LANES and new & (new - 1) == 0
    c = pl.program_id(1)
    last = pl.num_programs(1) - 1
    log2 = lambda n: n.bit_length() - 1

    def lane_head(shape):
        lane = lax.broadcasted_iota(jnp.int32, shape, 1)
        return lax.shift_right_logical(lane, log2(new)) & (DIFF_HEADS - 1)

    @pl.when(c == 0)
    def _():
        m_ref[...] = jnp.full(m_ref.shape, NEG_INF, F32)
        l_ref[...] = jnp.zeros(l_ref.shape, F32)
        acc_ref[...] = jnp.zeros(acc_ref.shape, F32)
        shape = (2 * DIFF_HEAD_DIM, nq)
        row = lax.broadcasted_iota(jnp.int32, shape, 0)
        lane = lax.broadcasted_iota(jnp.int32, shape, 1)
        frame = lax.broadcasted_iota(jnp.int32, (DEC_PAD, nq), 0)
        spread = jnp.where((lax.broadcasted_iota(jnp.int32, (DEC_PAD, nq), 1) & (new - 1)) == frame,
                           1.0, 0.0).astype(BF)
        same_map = lax.shift_right_logical(row, log2(DIFF_HEAD_DIM)) == lax.shift_right_logical(lane, log2(LANES))
        for h in range(DIFF_HEADS):
            rep = _nn(qT_ref[h * DIFF_V_DIM:(h + 1) * DIFF_V_DIM, :], spread)
            qbd_ref[h] = jnp.where(same_map & (lane_head(shape) == h), rep, 0.0).astype(BF)

    def update(keys, vals, bias):
        s = None
        for h in range(DIFF_HEADS):
            sh = _nn(keys(h), qbd_ref[h])
            s = sh if s is None else s + sh
        if bias is not None:
            s = s + bias
        m_new = jnp.maximum(m_ref[...], jnp.max(s, axis=0, keepdims=True))
        alpha = jnp.exp2(m_ref[...] - m_new)
        p = jnp.exp2(s - m_new)
        l_ref[...] = alpha * l_ref[...] + jnp.sum(p, axis=0, keepdims=True)
        m_ref[...] = m_new
        pb = p.astype(BF)
        head = lane_head((DIFF_V_DIM, nq))
        pv = jnp.zeros((DIFF_V_DIM, nq), F32)
        for h in range(DIFF_HEADS):
            pv = jnp.where(head == h, _tn(vals(h), pb), pv)
        acc_ref[...] = alpha * acc_ref[...] + pv

    def cache_rows(ref):
        return lambda h: ref[pl.ds(h, CACHE_BLK, stride=DIFF_HEADS), :].astype(BF)

    @pl.when(c < last)
    def _():
        update(cache_rows(ck_ref), cache_rows(cv_ref), None)

    @pl.when(c == last)
    def _():
        update(cache_rows(ck_ref), cache_rows(cv_ref), bc_ref[...])
        hs = lambda h: slice(h * DIFF_V_DIM, (h + 1) * DIFF_V_DIM)
        update(lambda h: kn_ref[0:new, hs(h)], lambda h: vn_ref[0:new, hs(h)].astype(BF), bn_ref[...])
        o = acc_ref[...] * (1.0 / l_ref[...])
        y = _diff_finish(o[:, 0:LANES], o[:, LANES:nq], _lambda(lam_ref), g_ref[...])
        r = lax.broadcasted_iota(jnp.int32, (LANES, DEC_PAD), 0)
        col = lax.broadcasted_iota(jnp.int32, (LANES, DEC_PAD), 1)
        for h in range(DIFF_HEADS):
            fold = jnp.where(r == col + h * new, 1.0, 0.0).astype(BF)
            o_ref[hs(h), :] = _nn(y, jnp.where(col < new, fold, jnp.zeros_like(fold))).astype(BF)


def _diff_decode(s, cache_k, cache_v, bias_c, bias_n, lamp, g, new):
    nb, past = cache_k.shape[:2]
    nq = 2 * DIFF_HEADS * new
    cache_k, cache_v = (a.reshape(nb, past * DIFF_HEADS, DIFF_V_DIM) for a in (cache_k, cache_v))
    stream = lambda rows: pl.BlockSpec((None, rows, D_MODEL), lambda b, c: (b, 0, 0))
    chunk = pl.BlockSpec((None, CACHE_BLK * DIFF_HEADS, DIFF_V_DIM), lambda b, c: (b, c, 0))
    return pl.pallas_call(
        functools.partial(_diff_decode_kernel, new=new),
        grid=(nb, past // CACHE_BLK),
        in_specs=[pl.BlockSpec((None, D_MODEL, DEC_PAD), lambda b, c: (b, 0, 0)), chunk, chunk,
                  stream(DEC_PAD), stream(DEC_PAD), _const_spec(bias_c.shape), _const_spec(bias_n.shape),
                  _const_spec(lamp.shape), _const_spec(g.shape)],
        out_specs=pl.BlockSpec((None, D_MODEL, DEC_PAD), lambda b, c: (b, 0, 0)),
        out_shape=jax.ShapeDtypeStruct((nb, D_MODEL, DEC_PAD), BF),
        scratch_shapes=[pltpu.VMEM((DIFF_HEADS, 2 * DIFF_HEAD_DIM, nq), BF),
                        pltpu.VMEM((1, nq), F32), pltpu.VMEM((1, nq), F32),
                        pltpu.VMEM((DIFF_V_DIM, nq), F32)],
        compiler_params=_params(2), name="diff_decode",
    )(s["qTd"], cache_k, cache_v, s["kb"], s["vf"], bias_c, bias_n, lamp, g)


def _mla_decode_kernel(qT_ref, ckv_ref, kr_ref, ckvn_ref, krn_ref, wuk_ref, wuvT_ref, o_ref, s_ref,
                       *, past, new):
    nq = MLA_HEADS * new
    assert nq == 2 * LANES and new & (new - 1) == 0
    log2 = lambda n: n.bit_length() - 1

    def groups(shape, rows_per_group):
        r = lax.broadcasted_iota(jnp.int32, shape, 0)
        c = lax.broadcasted_iota(jnp.int32, shape, 1)
        return lax.shift_right_logical(r, log2(rows_per_group)) == lax.shift_right_logical(c, log2(new))

    def frame_match(shape, lane_axis):
        a = lax.broadcasted_iota(jnp.int32, shape, lane_axis)
        b = lax.broadcasted_iota(jnp.int32, shape, 1 - lane_axis)
        return jnp.where((a & (new - 1)) == b, 1.0, 0.0).astype(BF)

    nope, rope = [], []
    for h in range(MLA_HEADS):
        base = h * LANES
        o_n, o_r = (0, MLA_NOPE_DIM) if h % 2 == 0 else (MLA_NOPE_DIM, 0)
        nope.append(qT_ref[base + o_n:base + o_n + MLA_NOPE_DIM, :])
        rope.append(qT_ref[base + o_r:base + o_r + MLA_ROPE_DIM, :])
    spread = frame_match((DEC_PAD, nq), 1)
    qn = _nn(jnp.concatenate(nope, axis=0), spread)
    qn_bd = jnp.where(groups(qn.shape, MLA_NOPE_DIM), qn, 0.0).astype(BF)
    qr = _nn(jnp.concatenate(rope, axis=0), spread)
    qr_bd = jnp.where(groups(qr.shape, MLA_ROPE_DIM), qr, 0.0)
    qrT = jnp.sum(qr_bd.reshape(MLA_HEADS, MLA_ROPE_DIM, nq), axis=0).astype(BF)
    qlatT = _nn(wuk_ref[...], qn_bd).astype(BF)

    def keys(lo, n):
        if lo < past:
            return ckv_ref[lo:lo + n, :].astype(BF), kr_ref[lo:lo + n, :].astype(BF)
        return ckvn_ref[0:new, :].astype(BF), krn_ref[0:new, :].astype(BF)

    spans = [(c0, CACHE_BLK) for c0 in range(0, past, CACHE_BLK)] + [(past, new)]
    m = None
    for lo, n in spans:
        ck, kr = keys(lo, n)
        s = _nn(ck, qlatT) + _nn(kr, qrT)
        s_ref[lo:lo + n, :] = s
        c = jnp.max(s.reshape(n // 8, 8, nq), axis=0)
        m = c if m is None else jnp.maximum(m, c)
    m = jnp.max(m, axis=0, keepdims=True)
    l = acc = None
    for lo, n in spans:
        p = jnp.exp2(s_ref[lo:lo + n, :] - m)
        ps = jnp.sum(p.reshape(n // 8, 8, nq), axis=0)
        pv = _tn(keys(lo, n)[0], p.astype(BF))
        l, acc = (ps, pv) if l is None else (l + ps, acc + pv)
    o_latT = (acc * (1.0 / jnp.sum(l, axis=0, keepdims=True))).astype(BF)
    y = _nn(wuvT_ref[...], o_latT)
    y_bd = jnp.where(groups(y.shape, MLA_V_DIM), y, 0.0).astype(BF)
    o_ref[...] = _nn(y_bd, frame_match((nq, DEC_PAD), 0)).astype(BF)


def _mla_decode(s, cache_ckv, cache_kr, w, new):
    nb, past, _ = cache_ckv.shape
    per_stream = lambda rows, cols: pl.BlockSpec((None, rows, cols), lambda b: (b, 0, 0))
    return pl.pallas_call(
        functools.partial(_mla_decode_kernel, past=past, new=new),
        grid=(nb,),
        in_specs=[per_stream(MLA_HEADS * LANES, DEC_PAD), per_stream(past, MLA_KV_LORA),
                  per_stream(past, MLA_ROPE_DIM), per_stream(DEC_PAD, MLA_KV_LORA),
                  per_stream(DEC_PAD, MLA_ROPE_DIM), _const_spec(w["wuk"].shape),
                  _const_spec(w["wuvT"].shape)],
        out_specs=per_stream(D_MODEL, DEC_PAD),
        out_shape=jax.ShapeDtypeStruct((nb, D_MODEL, DEC_PAD), BF),
        scratch_shapes=[pltpu.VMEM((past + new, MLA_HEADS * new), F32)],
        compiler_params=_params(1), name="mla_decode",
    )(s["qTm"], cache_ckv, cache_kr, s["ckv"], s["kr"], w["wuk"], w["wuvT"])


def _post_kernel(x_ref, oTd_ref, oTm_ref, gmix_ref, wgT_ref, wodT_ref, womT_ref, woutT_ref,
                 gmlp_ref, wup_ref, wdown_ref, gfin_ref, y_ref, *, n_groups):
    rows = x_ref.shape[0] // n_groups
    groups = [slice(i * rows, (i + 1) * rows) for i in range(n_groups)]
    d = D_MODEL

    def norm_in(v, r):
        v["x"] = x_ref[r, :]
        v["hb"] = _rms(v["x"], gmix_ref[...]).astype(BF)

    def branches(v, r):
        v["gT"] = _nt(wgT_ref[...], v.pop("hb"))
        v["od"] = _nn(wodT_ref[...], oTd_ref[:, r])
        v["om"] = _nn(womT_ref[...], oTm_ref[:, r])

    def merge(v, r):
        g = jax.nn.sigmoid(v.pop("gT"))
        v["mT"] = (g[0:d, :] * v.pop("od") + g[d:2 * d, :] * v.pop("om")).astype(BF)

    def out_proj(v, r):
        v["aT"] = _nn(woutT_ref[...], v.pop("mT"))

    def residual(v, r):
        v["x1"] = v.pop("x") + v.pop("aT").T
        v["h2"] = _rms(v["x1"], gmlp_ref[...]).astype(BF)

    def up(v, r):
        v["u"] = _nn(v.pop("h2"), wup_ref[...])

    def act(v, r):
        u = jnp.maximum(v.pop("u"), 0.0)
        v["uu"] = (u * u).astype(BF)

    def down(v, r):
        v["x2"] = v.pop("x1") + _nn(v.pop("uu"), wdown_ref[...])

    def norm_out(v, r):
        y_ref[r, :] = _rms(v.pop("x2"), gfin_ref[...])

    values = [{} for _ in groups]
    for stage in (norm_in, branches, merge, out_proj, residual, up, act, down, norm_out):
        for v, r in zip(values, groups):
            stage(v, r)


def _post(x, oTd, oTm, w, tm, n_groups):
    nb, t, _ = x.shape
    tok = pl.BlockSpec((None, tm, D_MODEL), lambda b, i: (b, i, 0))
    feat = pl.BlockSpec((None, D_MODEL, tm), lambda b, i: (b, 0, i))
    weights = [w["gmix"], w["wgT"], w["wodT"], w["womT"], w["woutT"], w["gmlp"], w["wup"],
               w["wdown"], w["gfin"]]
    return pl.pallas_call(
        functools.partial(_post_kernel, n_groups=n_groups), grid=(nb, t // tm),
        in_specs=[tok, feat, feat] + [_const_spec(a.shape) for a in weights],
        out_specs=tok, out_shape=jax.ShapeDtypeStruct((nb, t, D_MODEL), F32),
        compiler_params=_params(2), name="post",
    )(x, oTd, oTm, *weights)


def _prep_weights(norm_mix, w_in, mla_q_norm, mla_w_uq, mla_kv_norm, mla_w_uk, mla_w_uv, w_o_diff,
                  w_o_mla, w_out, norm_mlp, w_up, w_down, norm_final):
    d = D_MODEL
    o_cq = 3 * d
    o_ckv = o_cq + MLA_Q_LORA
    o_kr = o_ckv + MLA_KV_LORA
    o_g = o_kr + MLA_ROPE_DIM
    half = MLA_ROPE_DIM // 2
    rot = lambda a: jnp.concatenate([-a[..., half:], a[..., :half]], axis=-1)
    zpad = lambda a, n: jnp.zeros(a.shape[:-1] + (n,), a.dtype)

    wkr = w_in[:, o_kr:o_g]
    place = lambda a: jnp.concatenate([a, zpad(a, 32), a, zpad(a, 32)], axis=-1)
    wlat = jnp.concatenate([w_in[:, o_cq:o_kr], place(wkr), place(rot(wkr))], axis=-1)

    nope = mla_w_uq[:, :, :MLA_NOPE_DIM]
    rope = mla_w_uq[:, :, MLA_NOPE_DIM:]

    def arrange(n, r):
        even = jnp.concatenate([n[:, 0::2], r[:, 0::2], zpad(r[:, 0::2], 32)], axis=-1)
        odd = jnp.concatenate([r[:, 1::2], zpad(r[:, 1::2], 32), n[:, 1::2]], axis=-1)
        both = jnp.stack([even, odd], axis=2)
        return both.reshape(MLA_Q_LORA, MLA_HEADS * LANES)

    row = lambda a: a.reshape(1, -1).astype(F32)
    return {
        "gmix": row(norm_mix), "gq": row(mla_q_norm), "gkv": row(mla_kv_norm),
        "gmlp": row(norm_mlp), "gfin": row(norm_final),
        "wqT": w_in[:, 0:d].T.astype(BF),
        "wk": w_in[:, d:2 * d].astype(BF),
        "wv": w_in[:, 2 * d:3 * d].astype(BF),
        "wlat": wlat.astype(BF),
        "wgT": w_in[:, o_g:].T.astype(BF),
        "wuqaT": arrange(nope, rope).T.astype(BF),
        "wuqbT": arrange(jnp.zeros_like(nope), rot(rope)).T.astype(BF),
        "wuk": mla_w_uk.reshape(MLA_KV_LORA, d).astype(BF),
        "wuvT": mla_w_uv.reshape(MLA_KV_LORA, d).T.astype(BF),
        "wodT": w_o_diff.reshape(d, d).T.astype(BF),
        "womT": w_o_mla.reshape(d, d).T.astype(BF),
        "woutT": w_out.T.astype(BF),
        "wup": w_up.astype(BF),
        "wdown": w_down.astype(BF),
    }


def _rope_tables(pos):
    half = MLA_ROPE_DIM // 2
    inv = jnp.power(ROPE_THETA, -jnp.arange(half, dtype=F32) * 2.0 / MLA_ROPE_DIM)
    ang = pos.astype(F32)[:, None] * inv[None, :]
    cos2 = jnp.tile(jnp.cos(ang), (1, 2))
    sin2 = jnp.tile(jnp.sin(ang), (1, 2))
    t = pos.shape[0]
    z32 = jnp.zeros((t, 32), F32)
    place = lambda a: jnp.concatenate([a, z32, a, z32], axis=1)
    scale = (MLA_NOPE_DIM + MLA_ROPE_DIM) ** -0.5 * LOG2E
    ones = jnp.ones((t, MLA_NOPE_DIM), F32)
    z64 = jnp.zeros((t, MLA_NOPE_DIM), F32)
    ct = jnp.concatenate([ones, cos2, z32, cos2, z32, ones], axis=1) * scale
    st = jnp.concatenate([z64, sin2, z32, sin2, z32, z64], axis=1) * scale
    return {"cosp": place(cos2), "sinp": place(sin2), "ct": ct.T, "st": st.T}


def _score_tiles(rel_bias, k_pos_tiles, q_pos, shift_bucket):
    k_pos = jnp.stack(k_pos_tiles)
    rel = k_pos[:, :, None] - q_pos[None, None, :]
    mask = (k_pos // CHUNK)[:, :, None] <= (q_pos // CHUNK)[None, None, :]
    return _bias_tiles(rel_bias, _t5_bucket(rel).astype(jnp.int32), mask.astype(jnp.int32), shift_bucket)


def _decode_score_tiles(rel_bias, k_pos, q_pos, new, shift_bucket):
    lane = jnp.arange(2 * DIFF_HEADS * new, dtype=jnp.int32)
    qp = q_pos[lane % new]
    rel = k_pos[:, None] - qp[None, :]
    mask = (k_pos // CHUNK)[:, None] <= (qp // CHUNK)[None, :]
    tab = rel_bias[:, (lane // new) % DIFF_HEADS].astype(F32)
    return _bias_lanes(tab, _t5_bucket(rel).astype(jnp.int32), mask.astype(jnp.int32), shift_bucket)


def kernel(x_prompt, x_sample, cache_diff_k, cache_diff_v, cache_mla_ckv, cache_mla_krope, rel_bias,
           norm_mix, w_in, lam_q1, lam_k1, lam_q2, lam_k2, diff_subln, mla_q_norm, mla_w_uq, mla_kv_norm,
           mla_w_uk, mla_w_uv, w_o_diff, w_o_mla, w_out, norm_mlp, w_up, w_down, norm_final):
    assert norm_mix.shape[0] == 1, "single-layer model"
    nb, t, _ = x_prompt.shape
    ns, new, _ = x_sample.shape
    past = cache_diff_k.shape[2]
    assert t % TQ == 0 and past % CACHE_BLK == 0 and CACHE_BLK >= TK and new <= DEC_PAD and new % 16 == 0
    assert past % CHUNK == 0 and new <= CHUNK

    w = _prep_weights(norm_mix[0], w_in[0], mla_q_norm[0], mla_w_uq[0], mla_kv_norm[0], mla_w_uk[0],
                      mla_w_uv[0], w_o_diff[0], w_o_mla[0], w_out[0], norm_mlp[0], w_up[0], w_down[0],
                      norm_final)
    lamp = jnp.stack([lam_q1[0], lam_k1[0], lam_q2[0], lam_k2[0]]).astype(F32)
    g_sub = diff_subln[0].reshape(DIFF_V_DIM, 1).astype(F32)

    pos_p = jnp.arange(t, dtype=jnp.int32)
    p = _project(x_prompt, w, _rope_tables(pos_p), tm=256)
    blk = jnp.arange(TQ, dtype=jnp.int32)
    tiles_p = _score_tiles(rel_bias, [TQ + blk, blk], TQ + blk, _far_bucket())
    mask_p = jnp.where((blk // CHUNK)[:, None] <= (blk // CHUNK)[None, :], 0.0, NEG_INF).astype(F32)
    oTd = _diff_prompt(p, tiles_p, lamp, g_sub)
    oTm = _mla_prompt(p, mask_p)
    y_prompt = _post(x_prompt, oTd, oTm, w, tm=512, n_groups=2)

    xs = jnp.pad(x_sample, ((0, 0), (0, DEC_PAD - new), (0, 0)))
    pos_s = past + jnp.arange(DEC_PAD, dtype=jnp.int32)
    s = _project(xs, w, _rope_tables(pos_s), tm=DEC_PAD)
    bias_c = _decode_score_tiles(rel_bias, jnp.arange(past - CACHE_BLK, past, dtype=jnp.int32), pos_s, new,
                                 _far_bucket())
    bias_n = _decode_score_tiles(rel_bias, pos_s[:new], pos_s, new, _far_bucket())
    oTd_s = _diff_decode(s, cache_diff_k[0], cache_diff_v[0], bias_c, bias_n, lamp, g_sub, new)
    oTm_s = _mla_decode(s, cache_mla_ckv[0], cache_mla_krope[0], w, new)
    y_sample = _post(xs, oTd_s, oTm_s, w, tm=DEC_PAD, n_groups=1)[:, :new]

    heads = lambda a, n: a.reshape((1,) + a.shape[:2] + (DIFF_HEADS, n))
    return (y_prompt, y_sample,
            heads(p["kf"], 2 * DIFF_HEAD_DIM), heads(p["vf"], DIFF_V_DIM), p["ckv"][None], p["kr"][None],
            heads(s["kf"][:, :new], 2 * DIFF_HEAD_DIM), heads(s["vf"][:, :new], DIFF_V_DIM),
            s["ckv"][:, :new][None], s["kr"][:, :new][None])
```

```python
import functools
import math

import numpy as np
import jax
import jax.numpy as jnp
from jax import lax
from jax.experimental import pallas as pl
from jax.experimental.pallas import tpu as pltpu

D_MODEL = 1024
CHUNK = 64
DIFF_HEADS = 8
DIFF_HEAD_DIM = 64
DIFF_V_DIM = 128
MLA_HEADS = 16
MLA_NOPE_DIM = 64
MLA_ROPE_DIM = 32
MLA_V_DIM = 64
MLA_Q_LORA = 256
MLA_KV_LORA = 256
D_FF = 4 * D_MODEL
NUM_BUCKETS = 32
MAX_DISTANCE = 128
ROPE_THETA = 10000.0
EPS = 1e-6
NEG_INF = -1e30
LAMBDA_INIT = 0.8 - 0.6 * math.exp(-0.3 * 0)

BF = jnp.bfloat16
F32 = jnp.float32

LANES = 128
TQ = 256
TK = 256
DEC_PAD = LANES
CACHE_BLK = 512
VMEM_LIMIT = 56 * 1024 * 1024
LOG2E = math.log2(math.e)
ONES_ROWS = 16


def _nn(a, b):
    return jnp.dot(a, b, preferred_element_type=F32)


def _nt(a, b):
    return lax.dot_general(a, b, (((1,), (1,)), ((), ())), preferred_element_type=F32)


def _tn(a, b):
    return lax.dot_general(a, b, (((0,), (0,)), ((), ())), preferred_element_type=F32)


def _rms(x, g):
    return x * lax.rsqrt(jnp.mean(x * x, axis=-1, keepdims=True) + EPS) * g


def _const_spec(shape):
    zeros = (0,) * len(shape)
    return pl.BlockSpec(shape, lambda *_: zeros, pipeline_mode=pl.Buffered(1))


def _params(n_axes):
    return pltpu.CompilerParams(dimension_semantics=("arbitrary",) * n_axes,
                                vmem_limit_bytes=VMEM_LIMIT)


def _bias_kernel(tab_ref, bkt_ref, msk_ref, o_ref, *, shift_bucket):
    h = pl.program_id(0)
    bkt = bkt_ref[...]
    acc = jnp.zeros(bkt.shape, F32)
    for b in range(NUM_BUCKETS):
        acc = jnp.where(bkt == b, tab_ref[b, h], acc)
    if shift_bucket is not None:
        acc = acc - tab_ref[shift_bucket, h]
    o_ref[...] = jnp.where(msk_ref[...] != 0, acc * LOG2E, NEG_INF)


def _bias_tiles(rel_bias, bucket, mask, shift_bucket):
    n, r, c = bucket.shape
    return pl.pallas_call(
        functools.partial(_bias_kernel, shift_bucket=shift_bucket),
        grid=(DIFF_HEADS,),
        in_specs=[pl.BlockSpec(memory_space=pltpu.SMEM),
                  pl.BlockSpec((n, r, c), lambda h: (0, 0, 0)),
                  pl.BlockSpec((n, r, c), lambda h: (0, 0, 0))],
        out_specs=pl.BlockSpec((None, n, r, c), lambda h: (h, 0, 0, 0)),
        out_shape=jax.ShapeDtypeStruct((DIFF_HEADS, n, r, c), F32),
        compiler_params=_params(1),
        name="bias_tiles",
    )(rel_bias, bucket, mask)


def _bias_lanes_kernel(tab_ref, bkt_ref, msk_ref, o_ref, *, shift_bucket):
    bkt = bkt_ref[...]
    acc = jnp.zeros(bkt.shape, F32)
    for b in range(NUM_BUCKETS):
        acc = jnp.where(bkt == b, tab_ref[b:b + 1, :], acc)
    if shift_bucket is not None:
        acc = acc - tab_ref[shift_bucket:shift_bucket + 1, :]
    o_ref[...] = jnp.where(msk_ref[...] != 0, acc * LOG2E, NEG_INF)


def _bias_lanes(tab, bucket, mask, shift_bucket):
    return pl.pallas_call(
        functools.partial(_bias_lanes_kernel, shift_bucket=shift_bucket),
        out_shape=jax.ShapeDtypeStruct(bucket.shape, F32), name="bias_lanes",
    )(tab, bucket, mask)


def _t5_bucket(rel):
    half = NUM_BUCKETS // 2
    max_exact = half // 2
    n = jnp.abs(rel)
    nf = jnp.maximum(n, max_exact).astype(F32)
    large = max_exact + (jnp.log(nf / max_exact) / math.log(MAX_DISTANCE / max_exact)
                         * (half - max_exact)).astype(jnp.int32)
    large = jnp.minimum(large, half - 1)
    return jnp.where(rel > 0, half, 0) + jnp.where(n < max_exact, n, large)


def _far_bucket():
    half = NUM_BUCKETS // 2
    max_exact = half // 2
    assert max_exact + math.log(TK / max_exact) / math.log(MAX_DISTANCE / max_exact) * (half - max_exact) >= half
    return half - 1


def _proj_kernel(x_ref, gmix_ref, wqT_ref, wk_ref, wv_ref, wlat_ref, gq_ref, gkv_ref,
                 wuqaT_ref, wuqbT_ref, wuk_ref, wuvT_ref, cosp_ref, sinp_ref, ct_ref, st_ref,
                 qTd_ref, kf_ref, kb_ref, vf_ref, vb_ref, qTm_ref, ckv_ref, kn_ref, vTm_ref,
                 r_ref, kr_ref):
    hb = _rms(x_ref[...], gmix_ref[...]).astype(BF)
    qTd_ref[...] = (_nt(wqT_ref[...], hb) * (DIFF_HEAD_DIM ** -0.5 * LOG2E)).astype(BF)
    k = _nn(hb, wk_ref[...])
    kb_ref[...] = k.astype(BF)
    v = _nn(hb, wv_ref[...])
    vb_ref[...] = v.astype(BF)
    tm = k.shape[0]
    for h in range(DIFF_HEADS):
        rows = pl.ds(h, tm, stride=DIFF_HEADS)
        kf_ref[rows, :] = k[:, h * DIFF_V_DIM:(h + 1) * DIFF_V_DIM]
        vf_ref[rows, :] = v[:, h * DIFF_V_DIM:(h + 1) * DIFF_V_DIM]

    lat = _nn(hb, wlat_ref[...])
    cq = _rms(lat[:, 0:MLA_Q_LORA], gq_ref[...]).astype(BF)
    qa = _nt(wuqaT_ref[...], cq)
    qb = _nt(wuqbT_ref[...], cq)
    ct = ct_ref[...]
    st = st_ref[...]
    pair = 2 * LANES
    for p in range(MLA_HEADS // 2):
        sl = slice(p * pair, (p + 1) * pair)
        qTm_ref[sl, :] = (qa[sl, :] * ct + qb[sl, :] * st).astype(BF)

    ckv = _rms(lat[:, MLA_Q_LORA:MLA_Q_LORA + MLA_KV_LORA], gkv_ref[...])
    ckv_ref[...] = ckv
    cb = ckv.astype(BF)
    kn_ref[...] = _nn(cb, wuk_ref[...]).astype(BF)
    vTm_ref[...] = _nt(wuvT_ref[...], cb).astype(BF)

    o = MLA_Q_LORA + MLA_KV_LORA
    r = lat[:, o:o + LANES] * cosp_ref[...] + lat[:, o + LANES:o + 2 * LANES] * sinp_ref[...]
    r_ref[...] = r.astype(BF)
    kr_ref[...] = r[:, 0:MLA_ROPE_DIM]


def _project(x, w, tabs, tm):
    nb, t, _ = x.shape
    nt = t // tm
    tok = lambda width: pl.BlockSpec((None, tm, width), lambda b, i: (b, i, 0))
    feat = lambda rows: pl.BlockSpec((None, rows, tm), lambda b, i: (b, 0, i))
    sds = jax.ShapeDtypeStruct
    weights = [w["gmix"], w["wqT"], w["wk"], w["wv"], w["wlat"], w["gq"], w["gkv"],
               w["wuqaT"], w["wuqbT"], w["wuk"], w["wuvT"]]
    in_specs = ([tok(D_MODEL)] + [_const_spec(a.shape) for a in weights]
                + [pl.BlockSpec((tm, LANES), lambda b, i: (i, 0)),
                   pl.BlockSpec((tm, LANES), lambda b, i: (i, 0)),
                   pl.BlockSpec((2 * LANES, tm), lambda b, i: (0, i)),
                   pl.BlockSpec((2 * LANES, tm), lambda b, i: (0, i))])
    head_rows = pl.BlockSpec((None, tm * DIFF_HEADS, DIFF_V_DIM), lambda b, i: (b, i, 0))
    out_shape = [sds((nb, D_MODEL, t), BF), sds((nb, t * DIFF_HEADS, DIFF_V_DIM), F32), sds((nb, t, D_MODEL), BF),
                 sds((nb, t * DIFF_HEADS, DIFF_V_DIM), F32), sds((nb, t, D_MODEL), BF),
                 sds((nb, MLA_HEADS * LANES, t), BF), sds((nb, t, MLA_KV_LORA), F32),
                 sds((nb, t, D_MODEL), BF), sds((nb, D_MODEL, t), BF),
                 sds((nb, t, LANES), BF), sds((nb, t, MLA_ROPE_DIM), F32)]
    out_specs = [feat(D_MODEL), head_rows, tok(D_MODEL), head_rows, tok(D_MODEL),
                 feat(MLA_HEADS * LANES), tok(MLA_KV_LORA), tok(D_MODEL), feat(D_MODEL),
                 tok(LANES), tok(MLA_ROPE_DIM)]
    outs = pl.pallas_call(
        _proj_kernel, grid=(nb, nt), in_specs=in_specs, out_specs=out_specs, out_shape=out_shape,
        compiler_params=_params(2), name="project",
    )(x, *weights, tabs["cosp"], tabs["sinp"], tabs["ct"], tabs["st"])
    names = ["qTd", "kf", "kb", "vf", "vb", "qTm", "ckv", "kn", "vTm", "r", "kr"]
    return dict(zip(names, outs))


def _flash_step(state, s, pv):
    bm = jnp.max(s, axis=0, keepdims=True)
    if state is None:
        p = jnp.exp2(s - bm)
        return bm, jnp.sum(p, axis=0, keepdims=True), pv(p.astype(BF))
    m, l, acc = state
    m_new = jnp.maximum(m, bm)
    alpha = jnp.exp2(m - m_new)
    p = jnp.exp2(s - m_new)
    return m_new, alpha * l + jnp.sum(p, axis=0, keepdims=True), alpha * acc + pv(p.astype(BF))


def _normalized(state):
    _, l, acc = state
    return acc * (1.0 / l)


def _split_maps(qT):
    rows = lax.broadcasted_iota(jnp.int32, qT.shape, 0)
    qf = qT.astype(F32)
    zero = jnp.zeros_like(qf)
    return (jnp.where(rows < DIFF_HEAD_DIM, qf, zero).astype(BF),
            jnp.where(rows >= DIFF_HEAD_DIM, qf, zero).astype(BF))


def _lambda(lam_ref):
    lp = lam_ref[...]
    a = jnp.sum(lp[0:1, :] * lp[1:2, :], axis=1, keepdims=True)
    b = jnp.sum(lp[2:3, :] * lp[3:4, :], axis=1, keepdims=True)
    return jnp.exp(a) - jnp.exp(b) + LAMBDA_INIT


def _diff_finish(o1, o2, lam, g):
    o = o1 - lam * o2
    ms = jnp.mean(o * o, axis=0, keepdims=True)
    return (o * lax.rsqrt(ms + EPS) * g * (1.0 - LAMBDA_INIT)).astype(BF)


def _mla_keys(kp, r):
    lane = lax.broadcasted_iota(jnp.int32, kp.shape, 1)
    kpf = kp.astype(F32)
    rf = r.astype(F32)
    return (jnp.where(lane < MLA_NOPE_DIM, kpf, rf).astype(BF),
            jnp.where(lane >= MLA_NOPE_DIM, kpf, rf).astype(BF))


def _qslice(qb):
    return slice(qb * TQ, (qb + 1) * TQ)


def _kslice(kb):
    return slice(kb * TK, (kb + 1) * TK)


def _with_ones(va_ref, vT):
    dv, t = vT.shape
    va_ref[0:dv, :] = vT
    va_ref[dv:dv + ONES_ROWS, :] = jnp.ones((ONES_ROWS, t), BF)


def _weighted_mean(acc):
    dv = acc.shape[0] - ONES_ROWS
    return acc[0:dv, :] * (1.0 / acc[dv:dv + 1, :])


def _causal_two_pass(nq, prep, scores, va_refs, s_ref, p_ref, emit):
    n_maps = len(va_refs)

    def score_pass(qb):
        ops = prep(qb)
        m8 = [None] * n_maps
        for kb in range(qb + 1):
            for i, s in enumerate(scores(ops, qb, kb)):
                s_ref[qb % 2, i, _kslice(kb), :] = s
                c = jnp.max(s.reshape(TK // 8, 8, TQ), axis=0)
                m8[i] = c if m8[i] is None else jnp.maximum(m8[i], c)
        return [jnp.max(m, axis=0, keepdims=True) for m in m8]

    def exp_pass(qb, ms):
        for kb in range(qb + 1):
            for i in range(n_maps):
                p_ref[qb % 2, i, _kslice(kb), :] = jnp.exp2(s_ref[qb % 2, i, _kslice(kb), :] - ms[i]).astype(BF)

    def value_pass(qb):
        klen = (qb + 1) * TK
        emit(qb, [_nn(va_refs[i][:, 0:klen], p_ref[qb % 2, i, 0:klen, :]) for i in range(n_maps)])

    ms = score_pass(0)
    for qb in range(nq + 1):
        ms_next = score_pass(qb + 1) if qb + 1 < nq else None
        if qb < nq:
            exp_pass(qb, ms)
        if qb >= 1:
            value_pass(qb - 1)
        ms = ms_next


def _diff_prompt_kernel(qT_ref, k_ref, v_ref, bias_ref, lam_ref, g_ref, o_ref, va_ref, s_ref, p_ref, *, t):
    lam = _lambda(lam_ref)
    g = g_ref[...]
    _with_ones(va_ref, v_ref[...].astype(F32).T.astype(BF))

    def prep(qb):
        return _split_maps(qT_ref[:, _qslice(qb)])

    def scores(ops, qb, kb):
        kblk = k_ref[_kslice(kb), :]
        out = [_nn(kblk, q) for q in ops]
        if kb >= qb - 1:
            out = [s + bias_ref[qb - kb] for s in out]
        return out

    def emit(qb, accs):
        o_ref[:, _qslice(qb)] = _diff_finish(_weighted_mean(accs[0]), _weighted_mean(accs[1]), lam, g)

    _causal_two_pass(t // TQ, prep, scores, [va_ref, va_ref], s_ref, p_ref, emit)


def _score_scratch(n_maps, t):
    return [pltpu.VMEM((2, n_maps, t, TQ), F32), pltpu.VMEM((2, n_maps, t, TQ), BF)]


def _diff_prompt(p, bias, lamp, g):
    nb, _, t = p["qTd"].shape
    head_feat = pl.BlockSpec((None, DIFF_V_DIM, t), lambda b, h: (b, h, 0))
    head_tok = pl.BlockSpec((None, t, DIFF_V_DIM), lambda b, h: (b, 0, h))
    return pl.pallas_call(
        functools.partial(_diff_prompt_kernel, t=t),
        grid=(nb, DIFF_HEADS),
        in_specs=[head_feat, head_tok, head_tok,
                  pl.BlockSpec((None, 2, TK, TQ), lambda b, h: (h, 0, 0, 0)),
                  _const_spec(lamp.shape), _const_spec(g.shape)],
        out_specs=head_feat,
        out_shape=jax.ShapeDtypeStruct((nb, D_MODEL, t), BF),
        scratch_shapes=[pltpu.VMEM((DIFF_V_DIM + ONES_ROWS, t), BF)] + _score_scratch(2, t),
        compiler_params=_params(2), name="diff_prompt",
    )(p["qTd"], p["kb"], p["vb"], bias, lamp, g)


def _mla_prompt_kernel(qT_ref, kp_ref, r_ref, vT_ref, mask_ref, o_ref, ke_ref, ko_ref, vae_ref, vao_ref,
                       s_ref, p_ref, *, t):
    ke, ko = _mla_keys(kp_ref[...], r_ref[...])
    ke_ref[...] = ke
    ko_ref[...] = ko
    _with_ones(vae_ref, vT_ref[0:MLA_V_DIM, :])
    _with_ones(vao_ref, vT_ref[MLA_V_DIM:2 * MLA_V_DIM, :])

    def prep(qb):
        return qT_ref[0:LANES, _qslice(qb)], qT_ref[LANES:2 * LANES, _qslice(qb)]

    def scores(ops, qb, kb):
        out = [_nn(ke_ref[_kslice(kb), :], ops[0]), _nn(ko_ref[_kslice(kb), :], ops[1])]
        if kb == qb:
            out = [s + mask_ref[...] for s in out]
        return out

    def emit(qb, accs):
        o_ref[0:MLA_V_DIM, _qslice(qb)] = _weighted_mean(accs[0]).astype(BF)
        o_ref[MLA_V_DIM:2 * MLA_V_DIM, _qslice(qb)] = _weighted_mean(accs[1]).astype(BF)

    _causal_two_pass(t // TQ, prep, scores, [vae_ref, vao_ref], s_ref, p_ref, emit)


def _mla_prompt(p, mask):
    nb, _, t = p["qTm"].shape
    return pl.pallas_call(
        functools.partial(_mla_prompt_kernel, t=t),
        grid=(nb, MLA_HEADS // 2),
        in_specs=[pl.BlockSpec((None, 2 * LANES, t), lambda b, h: (b, h, 0)),
                  pl.BlockSpec((None, t, LANES), lambda b, h: (b, 0, h)),
                  pl.BlockSpec((None, t, LANES), lambda b, h: (b, 0, 0)),
                  pl.BlockSpec((None, 2 * MLA_V_DIM, t), lambda b, h: (b, h, 0)),
                  _const_spec(mask.shape)],
        out_specs=pl.BlockSpec((None, 2 * MLA_V_DIM, t), lambda b, h: (b, h, 0)),
        out_shape=jax.ShapeDtypeStruct((nb, D_MODEL, t), BF),
        scratch_shapes=[pltpu.VMEM((t, LANES), BF), pltpu.VMEM((t, LANES), BF),
                        pltpu.VMEM((MLA_V_DIM + ONES_ROWS, t), BF),
                        pltpu.VMEM((MLA_V_DIM + ONES_ROWS, t), BF)] + _score_scratch(2, t),
        compiler_params=_params(2), name="mla_prompt",
    )(p["qTm"], p["kn"], p["r"], p["vTm"], mask)


def _diff_decode_kernel(qT_ref, ck_ref, cv_ref, kn_ref, vn_ref, bc_ref, bn_ref, lam_ref, g_ref, o_ref,
                        qbd_ref, m_ref, l_ref, acc_ref, *, new):
    nq = 2 * DIFF_HEADS * new
    assert nq == 2 * LANES and new & (new - 1) == 0
    c = pl.program_id(1)
    last = pl.num_programs(1) - 1
    log2 = lambda n: n.bit_length() - 1

    def lane_head(shape):
        lane = lax.broadcasted_iota(jnp.int32, shape, 1)
        return lax.shift_right_logical(lane, log2(new)) & (DIFF_HEADS - 1)

    @pl.when(c == 0)
    def _():
        m_ref[...] = jnp.full(m_ref.shape, NEG_INF, F32)
        l_ref[...] = jnp.zeros(l_ref.shape, F32)
        acc_ref[...] = jnp.zeros(acc_ref.shape, F32)
        shape = (2 * DIFF_HEAD_DIM, nq)
        row = lax.broadcasted_iota(jnp.int32, shape, 0)
        lane = lax.broadcasted_iota(jnp.int32, shape, 1)
        frame = lax.broadcasted_iota(jnp.int32, (DEC_PAD, nq), 0)
        spread = jnp.where((lax.broadcasted_iota(jnp.int32, (DEC_PAD, nq), 1) & (new - 1)) == frame,
                           1.0, 0.0).astype(BF)
        same_map = lax.shift_right_logical(row, log2(DIFF_HEAD_DIM)) == lax.shift_right_logical(lane, log2(LANES))
        for h in range(DIFF_HEADS):
            rep = _nn(qT_ref[h * DIFF_V_DIM:(h + 1) * DIFF_V_DIM, :], spread)
            qbd_ref[h] = jnp.where(same_map & (lane_head(shape) == h), rep, 0.0).astype(BF)

    def update(keys, vals, bias):
        s = None
        for h in range(DIFF_HEADS):
            sh = _nn(keys(h), qbd_ref[h])
            s = sh if s is None else s + sh
        if bias is not None:
            s = s + bias
        m_new = jnp.maximum(m_ref[...], jnp.max(s, axis=0, keepdims=True))
        alpha = jnp.exp2(m_ref[...] - m_new)
        p = jnp.exp2(s - m_new)
        l_ref[...] = alpha * l_ref[...] + jnp.sum(p, axis=0, keepdims=True)
        m_ref[...] = m_new
        pb = p.astype(BF)
        head = lane_head((DIFF_V_DIM, nq))
        pv = jnp.zeros((DIFF_V_DIM, nq), F32)
        for h in range(DIFF_HEADS):
            pv = jnp.where(head == h, _tn(vals(h), pb), pv)
        acc_ref[...] = alpha * acc_ref[...] + pv

    def cache_rows(ref):
        return lambda h: ref[pl.ds(h, CACHE_BLK, stride=DIFF_HEADS), :].astype(BF)

    @pl.when(c < last)
    def _():
        update(cache_rows(ck_ref), cache_rows(cv_ref), None)

    @pl.when(c == last)
    def _():
        update(cache_rows(ck_ref), cache_rows(cv_ref), bc_ref[...])
        hs = lambda h: slice(h * DIFF_V_DIM, (h + 1) * DIFF_V_DIM)
        update(lambda h: kn_ref[0:new, hs(h)],
               lambda h: vn_ref[pl.ds(h, new, stride=DIFF_HEADS), :].astype(BF), bn_ref[...])
        o = acc_ref[...] * (1.0 / l_ref[...])
        y = _diff_finish(o[:, 0:LANES], o[:, LANES:nq], _lambda(lam_ref), g_ref[...])
        r = lax.broadcasted_iota(jnp.int32, (LANES, DEC_PAD), 0)
        col = lax.broadcasted_iota(jnp.int32, (LANES, DEC_PAD), 1)
        for h in range(DIFF_HEADS):
            fold = jnp.where(r == col + h * new, 1.0, 0.0).astype(BF)
            o_ref[hs(h), :] = _nn(y, jnp.where(col < new, fold, jnp.zeros_like(fold))).astype(BF)


def _diff_decode(s, cache_k, cache_v, bias_c, bias_n, lamp, g, new):
    nb, past = cache_k.shape[:2]
    nq = 2 * DIFF_HEADS * new
    cache_k, cache_v = (a.reshape(nb, past * DIFF_HEADS, DIFF_V_DIM) for a in (cache_k, cache_v))
    stream = lambda rows: pl.BlockSpec((None, rows, D_MODEL), lambda b, c: (b, 0, 0))
    chunk = pl.BlockSpec((None, CACHE_BLK * DIFF_HEADS, DIFF_V_DIM), lambda b, c: (b, c, 0))
    return pl.pallas_call(
        functools.partial(_diff_decode_kernel, new=new),
        grid=(nb, past // CACHE_BLK),
        in_specs=[pl.BlockSpec((None, D_MODEL, DEC_PAD), lambda b, c: (b, 0, 0)), chunk, chunk,
                  stream(DEC_PAD),
                  pl.BlockSpec((None, DEC_PAD * DIFF_HEADS, DIFF_V_DIM), lambda b, c: (b, 0, 0)),
                  _const_spec(bias_c.shape), _const_spec(bias_n.shape),
                  _const_spec(lamp.shape), _const_spec(g.shape)],
        out_specs=pl.BlockSpec((None, D_MODEL, DEC_PAD), lambda b, c: (b, 0, 0)),
        out_shape=jax.ShapeDtypeStruct((nb, D_MODEL, DEC_PAD), BF),
        scratch_shapes=[pltpu.VMEM((DIFF_HEADS, 2 * DIFF_HEAD_DIM, nq), BF),
                        pltpu.VMEM((1, nq), F32), pltpu.VMEM((1, nq), F32),
                        pltpu.VMEM((DIFF_V_DIM, nq), F32)],
        compiler_params=_params(2), name="diff_decode",
    )(s["qTd"], cache_k, cache_v, s["kb"], s["vf"], bias_c, bias_n, lamp, g)


def _mla_decode_kernel(qT_ref, ckv_ref, kr_ref, ckvn_ref, krn_ref, wuk_ref, wuvT_ref, o_ref, s_ref,
                       *, past, new):
    nq = MLA_HEADS * new
    assert nq == 2 * LANES and new & (new - 1) == 0
    log2 = lambda n: n.bit_length() - 1

    def groups(shape, rows_per_group):
        r = lax.broadcasted_iota(jnp.int32, shape, 0)
        c = lax.broadcasted_iota(jnp.int32, shape, 1)
        return lax.shift_right_logical(r, log2(rows_per_group)) == lax.shift_right_logical(c, log2(new))

    def frame_match(shape, lane_axis):
        a = lax.broadcasted_iota(jnp.int32, shape, lane_axis)
        b = lax.broadcasted_iota(jnp.int32, shape, 1 - lane_axis)
        return jnp.where((a & (new - 1)) == b, 1.0, 0.0).astype(BF)

    nope, rope = [], []
    for h in range(MLA_HEADS):
        base = h * LANES
        o_n, o_r = (0, MLA_NOPE_DIM) if h % 2 == 0 else (MLA_NOPE_DIM, 0)
        nope.append(qT_ref[base + o_n:base + o_n + MLA_NOPE_DIM, :])
        rope.append(qT_ref[base + o_r:base + o_r + MLA_ROPE_DIM, :])
    spread = frame_match((DEC_PAD, nq), 1)
    qn = _nn(jnp.concatenate(nope, axis=0), spread)
    qn_bd = jnp.where(groups(qn.shape, MLA_NOPE_DIM), qn, 0.0).astype(BF)
    qr = _nn(jnp.concatenate(rope, axis=0), spread)
    qr_bd = jnp.where(groups(qr.shape, MLA_ROPE_DIM), qr, 0.0)
    qrT = jnp.sum(qr_bd.reshape(MLA_HEADS, MLA_ROPE_DIM, nq), axis=0).astype(BF)
    qlatT = _nn(wuk_ref[...], qn_bd).astype(BF)

    def keys(lo, n):
        if lo < past:
            return ckv_ref[lo:lo + n, :].astype(BF), kr_ref[lo:lo + n, :].astype(BF)
        return ckvn_ref[0:new, :].astype(BF), krn_ref[0:new, :].astype(BF)

    spans = [(c0, CACHE_BLK) for c0 in range(0, past, CACHE_BLK)] + [(past, new)]
    m = None
    for lo, n in spans:
        ck, kr = keys(lo, n)
        s = _nn(ck, qlatT) + _nn(kr, qrT)
        s_ref[lo:lo + n, :] = s
        c = jnp.max(s.reshape(n // 8, 8, nq), axis=0)
        m = c if m is None else jnp.maximum(m, c)
    m = jnp.max(m, axis=0, keepdims=True)
    l = acc = None
    for lo, n in spans:
        p = jnp.exp2(s_ref[lo:lo + n, :] - m)
        ps = jnp.sum(p.reshape(n // 8, 8, nq), axis=0)
        pv = _tn(keys(lo, n)[0], p.astype(BF))
        l, acc = (ps, pv) if l is None else (l + ps, acc + pv)
    o_latT = (acc * (1.0 / jnp.sum(l, axis=0, keepdims=True))).astype(BF)
    y = _nn(wuvT_ref[...], o_latT)
    y_bd = jnp.where(groups(y.shape, MLA_V_DIM), y, 0.0).astype(BF)
    o_ref[...] = _nn(y_bd, frame_match((nq, DEC_PAD), 0)).astype(BF)


def _mla_decode(s, cache_ckv, cache_kr, w, new):
    nb, past, _ = cache_ckv.shape
    per_stream = lambda rows, cols: pl.BlockSpec((None, rows, cols), lambda b: (b, 0, 0))
    return pl.pallas_call(
        functools.partial(_mla_decode_kernel, past=past, new=new),
        grid=(nb,),
        in_specs=[per_stream(MLA_HEADS * LANES, DEC_PAD), per_stream(past, MLA_KV_LORA),
                  per_stream(past, MLA_ROPE_DIM), per_stream(DEC_PAD, MLA_KV_LORA),
                  per_stream(DEC_PAD, MLA_ROPE_DIM), _const_spec(w["wuk"].shape),
                  _const_spec(w["wuvT"].shape)],
        out_specs=per_stream(D_MODEL, DEC_PAD),
        out_shape=jax.ShapeDtypeStruct((nb, D_MODEL, DEC_PAD), BF),
        scratch_shapes=[pltpu.VMEM((past + new, MLA_HEADS * new), F32)],
        compiler_params=_params(1), name="mla_decode",
    )(s["qTm"], cache_ckv, cache_kr, s["ckv"], s["kr"], w["wuk"], w["wuvT"])


def _post_kernel(x_ref, oTd_ref, oTm_ref, gmix_ref, wgT_ref, wodT_ref, womT_ref, woutT_ref,
                 gmlp_ref, wup_ref, wdown_ref, gfin_ref, y_ref, *, n_groups):
    rows = x_ref.shape[0] // n_groups
    groups = [slice(i * rows, (i + 1) * rows) for i in range(n_groups)]
    d = D_MODEL

    def norm_in(v, r):
        v["x"] = x_ref[r, :]
        v["hb"] = _rms(v["x"], gmix_ref[...]).astype(BF)

    def branches(v, r):
        v["gT"] = _nt(wgT_ref[...], v.pop("hb"))
        v["od"] = _nn(wodT_ref[...], oTd_ref[:, r])
        v["om"] = _nn(womT_ref[...], oTm_ref[:, r])

    def merge(v, r):
        g = jax.nn.sigmoid(v.pop("gT"))
        v["mT"] = (g[0:d, :] * v.pop("od") + g[d:2 * d, :] * v.pop("om")).astype(BF)

    def out_proj(v, r):
        v["aT"] = _nn(woutT_ref[...], v.pop("mT"))

    def residual(v, r):
        v["x1"] = v.pop("x") + v.pop("aT").T
        v["h2"] = _rms(v["x1"], gmlp_ref[...]).astype(BF)

    def up(v, r):
        v["u"] = _nn(v.pop("h2"), wup_ref[...])

    def act(v, r):
        u = jnp.maximum(v.pop("u"), 0.0)
        v["uu"] = (u * u).astype(BF)

    def down(v, r):
        v["x2"] = v.pop("x1") + _nn(v.pop("uu"), wdown_ref[...])

    def norm_out(v, r):
        y_ref[r, :] = _rms(v.pop("x2"), gfin_ref[...])

    values = [{} for _ in groups]
    for stage in (norm_in, branches, merge, out_proj, residual, up, act, down, norm_out):
        for v, r in zip(values, groups):
            stage(v, r)


def _post(x, oTd, oTm, w, tm, n_groups):
    nb, t, _ = x.shape
    tok = pl.BlockSpec((None, tm, D_MODEL), lambda b, i: (b, i, 0))
    feat = pl.BlockSpec((None, D_MODEL, tm), lambda b, i: (b, 0, i))
    weights = [w["gmix"], w["wgT"], w["wodT"], w["womT"], w["woutT"], w["gmlp"], w["wup"],
               w["wdown"], w["gfin"]]
    return pl.pallas_call(
        functools.partial(_post_kernel, n_groups=n_groups), grid=(nb, t // tm),
        in_specs=[tok, feat, feat] + [_const_spec(a.shape) for a in weights],
        out_specs=tok, out_shape=jax.ShapeDtypeStruct((nb, t, D_MODEL), F32),
        compiler_params=_params(2), name="post",
    )(x, oTd, oTm, *weights)


def _prep_weights(norm_mix, w_in, mla_q_norm, mla_w_uq, mla_kv_norm, mla_w_uk, mla_w_uv, w_o_diff,
                  w_o_mla, w_out, norm_mlp, w_up, w_down, norm_final):
    d = D_MODEL
    o_cq = 3 * d
    o_ckv = o_cq + MLA_Q_LORA
    o_kr = o_ckv + MLA_KV_LORA
    o_g = o_kr + MLA_ROPE_DIM
    half = MLA_ROPE_DIM // 2
    rot = lambda a: jnp.concatenate([-a[..., half:], a[..., :half]], axis=-1)
    zpad = lambda a, n: jnp.zeros(a.shape[:-1] + (n,), a.dtype)

    wkr = w_in[:, o_kr:o_g]
    place = lambda a: jnp.concatenate([a, zpad(a, 32), a, zpad(a, 32)], axis=-1)
    wlat = jnp.concatenate([w_in[:, o_cq:o_kr], place(wkr), place(rot(wkr))], axis=-1)

    nope = mla_w_uq[:, :, :MLA_NOPE_DIM]
    rope = mla_w_uq[:, :, MLA_NOPE_DIM:]

    def arrange(n, r):
        even = jnp.concatenate([n[:, 0::2], r[:, 0::2], zpad(r[:, 0::2], 32)], axis=-1)
        odd = jnp.concatenate([r[:, 1::2], zpad(r[:, 1::2], 32), n[:, 1::2]], axis=-1)
        both = jnp.stack([even, odd], axis=2)
        return both.reshape(MLA_Q_LORA, MLA_HEADS * LANES)

    row = lambda a: a.reshape(1, -1).astype(F32)
    return {
        "gmix": row(norm_mix), "gq": row(mla_q_norm), "gkv": row(mla_kv_norm),
        "gmlp": row(norm_mlp), "gfin": row(norm_final),
        "wqT": w_in[:, 0:d].T.astype(BF),
        "wk": w_in[:, d:2 * d].astype(BF),
        "wv": w_in[:, 2 * d:3 * d].astype(BF),
        "wlat": wlat.astype(BF),
        "wgT": w_in[:, o_g:].T.astype(BF),
        "wuqaT": arrange(nope, rope).T.astype(BF),
        "wuqbT": arrange(jnp.zeros_like(nope), rot(rope)).T.astype(BF),
        "wuk": mla_w_uk.reshape(MLA_KV_LORA, d).astype(BF),
        "wuvT": mla_w_uv.reshape(MLA_KV_LORA, d).T.astype(BF),
        "wodT": w_o_diff.reshape(d, d).T.astype(BF),
        "womT": w_o_mla.reshape(d, d).T.astype(BF),
        "woutT": w_out.T.astype(BF),
        "wup": w_up.astype(BF),
        "wdown": w_down.astype(BF),
    }


def _rope_tables(pos):
    half = MLA_ROPE_DIM // 2
    inv = jnp.power(ROPE_THETA, -jnp.arange(half, dtype=F32) * 2.0 / MLA_ROPE_DIM)
    ang = pos.astype(F32)[:, None] * inv[None, :]
    cos2 = jnp.tile(jnp.cos(ang), (1, 2))
    sin2 = jnp.tile(jnp.sin(ang), (1, 2))
    t = pos.shape[0]
    z32 = jnp.zeros((t, 32), F32)
    place = lambda a: jnp.concatenate([a, z32, a, z32], axis=1)
    scale = (MLA_NOPE_DIM + MLA_ROPE_DIM) ** -0.5 * LOG2E
    ones = jnp.ones((t, MLA_NOPE_DIM), F32)
    z64 = jnp.zeros((t, MLA_NOPE_DIM), F32)
    ct = jnp.concatenate([ones, cos2, z32, cos2, z32, ones], axis=1) * scale
    st = jnp.concatenate([z64, sin2, z32, sin2, z32, z64], axis=1) * scale
    return {"cosp": place(cos2), "sinp": place(sin2), "ct": ct.T, "st": st.T}


def _score_tiles(rel_bias, k_pos_tiles, q_pos, shift_bucket):
    k_pos = jnp.stack(k_pos_tiles)
    rel = k_pos[:, :, None] - q_pos[None, None, :]
    mask = (k_pos // CHUNK)[:, :, None] <= (q_pos // CHUNK)[None, None, :]
    return _bias_tiles(rel_bias, _t5_bucket(rel).astype(jnp.int32), mask.astype(jnp.int32), shift_bucket)


def _decode_score_tiles(rel_bias, k_pos, q_pos, new, shift_bucket):
    lane = jnp.arange(2 * DIFF_HEADS * new, dtype=jnp.int32)
    qp = q_pos[lane % new]
    rel = k_pos[:, None] - qp[None, :]
    mask = (k_pos // CHUNK)[:, None] <= (qp // CHUNK)[None, :]
    tab = rel_bias[:, (lane // new) % DIFF_HEADS].astype(F32)
    return _bias_lanes(tab, _t5_bucket(rel).astype(jnp.int32), mask.astype(jnp.int32), shift_bucket)


def kernel(x_prompt, x_sample, cache_diff_k, cache_diff_v, cache_mla_ckv, cache_mla_krope, rel_bias,
           norm_mix, w_in, lam_q1, lam_k1, lam_q2, lam_k2, diff_subln, mla_q_norm, mla_w_uq, mla_kv_norm,
           mla_w_uk, mla_w_uv, w_o_diff, w_o_mla, w_out, norm_mlp, w_up, w_down, norm_final):
    assert norm_mix.shape[0] == 1, "single-layer model"
    nb, t, _ = x_prompt.shape
    ns, new, _ = x_sample.shape
    past = cache_diff_k.shape[2]
    assert t % TQ == 0 and past % CACHE_BLK == 0 and CACHE_BLK >= TK and new <= DEC_PAD and new % 16 == 0
    assert past % CHUNK == 0 and new <= CHUNK

    w = _prep_weights(norm_mix[0], w_in[0], mla_q_norm[0], mla_w_uq[0], mla_kv_norm[0], mla_w_uk[0],
                      mla_w_uv[0], w_o_diff[0], w_o_mla[0], w_out[0], norm_mlp[0], w_up[0], w_down[0],
                      norm_final)
    lamp = jnp.stack([lam_q1[0], lam_k1[0], lam_q2[0], lam_k2[0]]).astype(F32)
    g_sub = diff_subln[0].reshape(DIFF_V_DIM, 1).astype(F32)

    pos_p = jnp.arange(t, dtype=jnp.int32)
    p = _project(x_prompt, w, _rope_tables(pos_p), tm=256)
    blk = jnp.arange(TQ, dtype=jnp.int32)
    tiles_p = _score_tiles(rel_bias, [TQ + blk, blk], TQ + blk, _far_bucket())
    mask_p = jnp.where((blk // CHUNK)[:, None] <= (blk // CHUNK)[None, :], 0.0, NEG_INF).astype(F32)
    oTd = _diff_prompt(p, tiles_p, lamp, g_sub)
    oTm = _mla_prompt(p, mask_p)
    y_prompt = _post(x_prompt, oTd, oTm, w, tm=512, n_groups=2)

    xs = jnp.pad(x_sample, ((0, 0), (0, DEC_PAD - new), (0, 0)))
    pos_s = past + jnp.arange(DEC_PAD, dtype=jnp.int32)
    s = _project(xs, w, _rope_tables(pos_s), tm=DEC_PAD)
    bias_c = _decode_score_tiles(rel_bias, jnp.arange(past - CACHE_BLK, past, dtype=jnp.int32), pos_s, new,
                                 _far_bucket())
    bias_n = _decode_score_tiles(rel_bias, pos_s[:new], pos_s, new, _far_bucket())
    oTd_s = _diff_decode(s, cache_diff_k[0], cache_diff_v[0], bias_c, bias_n, lamp, g_sub, new)
    oTm_s = _mla_decode(s, cache_mla_ckv[0], cache_mla_krope[0], w, new)
    y_sample = _post(xs, oTd_s, oTm_s, w, tm=DEC_PAD, n_groups=1)[:, :new]

    heads = lambda a, frames: a.reshape(1, a.shape[0], -1, DIFF_HEADS, DIFF_V_DIM)[:, :, :frames]
    return (y_prompt, y_sample,
            heads(p["kf"], t), heads(p["vf"], t), p["ckv"][None], p["kr"][None],
            heads(s["kf"], new), heads(s["vf"], new),
            s["ckv"][:, :new][None], s["kr"][:, :new][None])
```

```python
import functools
import math

import numpy as np
import jax
import jax.numpy as jnp
from jax import lax
from jax.experimental import pallas as pl
from jax.experimental.pallas import tpu as pltpu

D_MODEL = 1024
CHUNK = 64
DIFF_HEADS = 8
DIFF_HEAD_DIM = 64
DIFF_V_DIM = 128
MLA_HEADS = 16
MLA_NOPE_DIM = 64
MLA_ROPE_DIM = 32
MLA_V_DIM = 64
MLA_Q_LORA = 256
MLA_KV_LORA = 256
D_FF = 4 * D_MODEL
NUM_BUCKETS = 32
MAX_DISTANCE = 128
ROPE_THETA = 10000.0
EPS = 1e-6
NEG_INF = -1e30
LAMBDA_INIT = 0.8 - 0.6 * math.exp(-0.3 * 0)

BF = jnp.bfloat16
F32 = jnp.float32

LANES = 128
TQ = 256
TK = 256
DEC_PAD = LANES
CACHE_BLK = 512
VMEM_LIMIT = 56 * 1024 * 1024
LOG2E = math.log2(math.e)
ONES_ROWS = 16


def _nn(a, b):
    return jnp.dot(a, b, preferred_element_type=F32)


def _nt(a, b):
    return lax.dot_general(a, b, (((1,), (1,)), ((), ())), preferred_element_type=F32)


def _tn(a, b):
    return lax.dot_general(a, b, (((0,), (0,)), ((), ())), preferred_element_type=F32)


def _rms(x, g):
    return x * lax.rsqrt(jnp.mean(x * x, axis=-1, keepdims=True) + EPS) * g


def _const_spec(shape):
    zeros = (0,) * len(shape)
    return pl.BlockSpec(shape, lambda *_: zeros, pipeline_mode=pl.Buffered(1))


def _params(n_axes):
    return pltpu.CompilerParams(dimension_semantics=("arbitrary",) * n_axes,
                                vmem_limit_bytes=VMEM_LIMIT)


def _bias_kernel(tab_ref, bkt_ref, msk_ref, o_ref, *, shift_bucket):
    h = pl.program_id(0)
    bkt = bkt_ref[...]
    acc = jnp.zeros(bkt.shape, F32)
    for b in range(NUM_BUCKETS):
        acc = jnp.where(bkt == b, tab_ref[b, h], acc)
    if shift_bucket is not None:
        acc = acc - tab_ref[shift_bucket, h]
    o_ref[...] = jnp.where(msk_ref[...] != 0, acc * LOG2E, NEG_INF)


def _bias_tiles(rel_bias, bucket, mask, shift_bucket):
    n, r, c = bucket.shape
    return pl.pallas_call(
        functools.partial(_bias_kernel, shift_bucket=shift_bucket),
        grid=(DIFF_HEADS,),
        in_specs=[pl.BlockSpec(memory_space=pltpu.SMEM),
                  pl.BlockSpec((n, r, c), lambda h: (0, 0, 0)),
                  pl.BlockSpec((n, r, c), lambda h: (0, 0, 0))],
        out_specs=pl.BlockSpec((None, n, r, c), lambda h: (h, 0, 0, 0)),
        out_shape=jax.ShapeDtypeStruct((DIFF_HEADS, n, r, c), F32),
        compiler_params=_params(1),
        name="bias_tiles",
    )(rel_bias, bucket, mask)


def _bias_lanes_kernel(tab_ref, bkt_ref, msk_ref, o_ref, *, shift_bucket):
    bkt = bkt_ref[...]
    acc = jnp.zeros(bkt.shape, F32)
    for b in range(NUM_BUCKETS):
        acc = jnp.where(bkt == b, tab_ref[b:b + 1, :], acc)
    if shift_bucket is not None:
        acc = acc - tab_ref[shift_bucket:shift_bucket + 1, :]
    o_ref[...] = jnp.where(msk_ref[...] != 0, acc * LOG2E, NEG_INF)


def _bias_lanes(tab, bucket, mask, shift_bucket):
    return pl.pallas_call(
        functools.partial(_bias_lanes_kernel, shift_bucket=shift_bucket),
        out_shape=jax.ShapeDtypeStruct(bucket.shape, F32), name="bias_lanes",
    )(tab, bucket, mask)


def _t5_bucket(rel):
    half = NUM_BUCKETS // 2
    max_exact = half // 2
    n = jnp.abs(rel)
    nf = jnp.maximum(n, max_exact).astype(F32)
    large = max_exact + (jnp.log(nf / max_exact) / math.log(MAX_DISTANCE / max_exact)
                         * (half - max_exact)).astype(jnp.int32)
    large = jnp.minimum(large, half - 1)
    return jnp.where(rel > 0, half, 0) + jnp.where(n < max_exact, n, large)


def _far_bucket():
    half = NUM_BUCKETS // 2
    max_exact = half // 2
    assert max_exact + math.log(TK / max_exact) / math.log(MAX_DISTANCE / max_exact) * (half - max_exact) >= half
    return half - 1


def _proj_kernel(x_ref, gmix_ref, wqT_ref, wk_ref, wv_ref, wlat_ref, gq_ref, gkv_ref,
                 wuqaT_ref, wuqbT_ref, wuk_ref, wuvT_ref, cosp_ref, sinp_ref, ct_ref, st_ref,
                 qTd_ref, kf_ref, kb_ref, vf_ref, vb_ref, qTm_ref, ckv_ref, kn_ref, vTm_ref,
                 r_ref, kr_ref):
    hb = _rms(x_ref[...], gmix_ref[...]).astype(BF)
    lat = _nn(hb, wlat_ref[...])
    qTd_ref[...] = (_nt(wqT_ref[...], hb) * (DIFF_HEAD_DIM ** -0.5 * LOG2E)).astype(BF)
    k = _nn(hb, wk_ref[...])
    kb_ref[...] = k.astype(BF)
    v = _nn(hb, wv_ref[...])
    vb_ref[...] = v.astype(BF)
    tm = k.shape[0]
    for h in range(DIFF_HEADS):
        rows = pl.ds(h, tm, stride=DIFF_HEADS)
        kf_ref[rows, :] = k[:, h * DIFF_V_DIM:(h + 1) * DIFF_V_DIM]
        vf_ref[rows, :] = v[:, h * DIFF_V_DIM:(h + 1) * DIFF_V_DIM]

    cq = _rms(lat[:, 0:MLA_Q_LORA], gq_ref[...]).astype(BF)
    qa = _nt(wuqaT_ref[...], cq)
    qb = _nt(wuqbT_ref[...], cq)
    ct = ct_ref[...]
    st = st_ref[...]
    pair = 2 * LANES
    rd = MLA_ROPE_DIM
    for p in range(MLA_HEADS // 2):
        sl = slice(p * pair, (p + 1) * pair)
        z = jnp.zeros((MLA_NOPE_DIM, tm), F32)
        partner = jnp.concatenate([z, qb[2 * p * rd:(2 * p + 1) * rd, :], z[0:rd, :],
                                   qb[(2 * p + 1) * rd:(2 * p + 2) * rd, :], z[0:rd, :], z], axis=0)
        qTm_ref[sl, :] = (qa[sl, :] * ct + partner * st).astype(BF)

    ckv = _rms(lat[:, MLA_Q_LORA:MLA_Q_LORA + MLA_KV_LORA], gkv_ref[...])
    ckv_ref[...] = ckv
    cb = ckv.astype(BF)
    kn_ref[...] = _nn(cb, wuk_ref[...]).astype(BF)
    vTm_ref[...] = _nt(wuvT_ref[...], cb).astype(BF)

    o = MLA_Q_LORA + MLA_KV_LORA
    r = lat[:, o:o + LANES] * cosp_ref[...] + lat[:, o + LANES:o + 2 * LANES] * sinp_ref[...]
    r_ref[...] = r.astype(BF)
    kr_ref[...] = r[:, 0:MLA_ROPE_DIM]


def _project(x, w, tabs, tm):
    nb, t, _ = x.shape
    nt = t // tm
    tok = lambda width: pl.BlockSpec((None, tm, width), lambda b, i: (b, i, 0))
    feat = lambda rows: pl.BlockSpec((None, rows, tm), lambda b, i: (b, 0, i))
    sds = jax.ShapeDtypeStruct
    weights = [w["gmix"], w["wqT"], w["wk"], w["wv"], w["wlat"], w["gq"], w["gkv"],
               w["wuqaT"], w["wuqbT"], w["wuk"], w["wuvT"]]
    in_specs = ([tok(D_MODEL)] + [_const_spec(a.shape) for a in weights]
                + [pl.BlockSpec((tm, LANES), lambda b, i: (i, 0)),
                   pl.BlockSpec((tm, LANES), lambda b, i: (i, 0)),
                   pl.BlockSpec((2 * LANES, tm), lambda b, i: (0, i)),
                   pl.BlockSpec((2 * LANES, tm), lambda b, i: (0, i))])
    head_rows = pl.BlockSpec((None, tm * DIFF_HEADS, DIFF_V_DIM), lambda b, i: (b, i, 0))
    out_shape = [sds((nb, D_MODEL, t), BF), sds((nb, t * DIFF_HEADS, DIFF_V_DIM), F32), sds((nb, t, D_MODEL), BF),
                 sds((nb, t * DIFF_HEADS, DIFF_V_DIM), F32), sds((nb, t, D_MODEL), BF),
                 sds((nb, MLA_HEADS * LANES, t), BF), sds((nb, t, MLA_KV_LORA), F32),
                 sds((nb, t, D_MODEL), BF), sds((nb, D_MODEL, t), BF),
                 sds((nb, t, LANES), BF), sds((nb, t, MLA_ROPE_DIM), F32)]
    out_specs = [feat(D_MODEL), head_rows, tok(D_MODEL), head_rows, tok(D_MODEL),
                 feat(MLA_HEADS * LANES), tok(MLA_KV_LORA), tok(D_MODEL), feat(D_MODEL),
                 tok(LANES), tok(MLA_ROPE_DIM)]
    outs = pl.pallas_call(
        _proj_kernel, grid=(nb, nt), in_specs=in_specs, out_specs=out_specs, out_shape=out_shape,
        compiler_params=_params(2), name="project",
    )(x, *weights, tabs["cosp"], tabs["sinp"], tabs["ct"], tabs["st"])
    names = ["qTd", "kf", "kb", "vf", "vb", "qTm", "ckv", "kn", "vTm", "r", "kr"]
    return dict(zip(names, outs))


def _flash_step(state, s, pv):
    bm = jnp.max(s, axis=0, keepdims=True)
    if state is None:
        p = jnp.exp2(s - bm)
        return bm, jnp.sum(p, axis=0, keepdims=True), pv(p.astype(BF))
    m, l, acc = state
    m_new = jnp.maximum(m, bm)
    alpha = jnp.exp2(m - m_new)
    p = jnp.exp2(s - m_new)
    return m_new, alpha * l + jnp.sum(p, axis=0, keepdims=True), alpha * acc + pv(p.astype(BF))


def _normalized(state):
    _, l, acc = state
    return acc * (1.0 / l)


def _split_maps(qT):
    rows = lax.broadcasted_iota(jnp.int32, qT.shape, 0)
    qf = qT.astype(F32)
    zero = jnp.zeros_like(qf)
    return (jnp.where(rows < DIFF_HEAD_DIM, qf, zero).astype(BF),
            jnp.where(rows >= DIFF_HEAD_DIM, qf, zero).astype(BF))


def _lambda(lam_ref):
    lp = lam_ref[...]
    a = jnp.sum(lp[0:1, :] * lp[1:2, :], axis=1, keepdims=True)
    b = jnp.sum(lp[2:3, :] * lp[3:4, :], axis=1, keepdims=True)
    return jnp.exp(a) - jnp.exp(b) + LAMBDA_INIT


def _diff_finish(o1, o2, lam, g):
    o = o1 - lam * o2
    ms = jnp.mean(o * o, axis=0, keepdims=True)
    return (o * lax.rsqrt(ms + EPS) * g * (1.0 - LAMBDA_INIT)).astype(BF)


def _mla_keys(kp, r):
    lane = lax.broadcasted_iota(jnp.int32, kp.shape, 1)
    kpf = kp.astype(F32)
    rf = r.astype(F32)
    return (jnp.where(lane < MLA_NOPE_DIM, kpf, rf).astype(BF),
            jnp.where(lane >= MLA_NOPE_DIM, kpf, rf).astype(BF))


def _qslice(qb):
    return slice(qb * TQ, (qb + 1) * TQ)


def _kslice(kb):
    return slice(kb * TK, (kb + 1) * TK)


def _with_ones(va_ref, vT):
    dv, t = vT.shape
    va_ref[0:dv, :] = vT
    va_ref[dv:dv + ONES_ROWS, :] = jnp.ones((ONES_ROWS, t), BF)


def _weighted_mean(acc):
    dv = acc.shape[0] - ONES_ROWS
    return acc[0:dv, :] * (1.0 / acc[dv:dv + 1, :])


def _causal_two_pass(nq, prep, scores, va_refs, s_ref, p_ref, emit):
    n_maps = len(va_refs)

    def score_pass(qb):
        ops = prep(qb)
        m8 = [None] * n_maps
        for kb in range(qb + 1):
            for i, s in enumerate(scores(ops, qb, kb)):
                s_ref[qb % 2, i, _kslice(kb), :] = s
                c = jnp.max(s.reshape(TK // 8, 8, TQ), axis=0)
                m8[i] = c if m8[i] is None else jnp.maximum(m8[i], c)
        return [jnp.max(m, axis=0, keepdims=True) for m in m8]

    def exp_pass(qb, ms):
        for kb in range(qb + 1):
            for i in range(n_maps):
                p_ref[qb % 2, i, _kslice(kb), :] = jnp.exp2(s_ref[qb % 2, i, _kslice(kb), :] - ms[i]).astype(BF)

    def value_pass(qb):
        klen = (qb + 1) * TK
        emit(qb, [_nn(va_refs[i][:, 0:klen], p_ref[qb % 2, i, 0:klen, :]) for i in range(n_maps)])

    ms = score_pass(0)
    for qb in range(nq + 1):
        ms_next = score_pass(qb + 1) if qb + 1 < nq else None
        if qb < nq:
            exp_pass(qb, ms)
        if qb >= 1:
            value_pass(qb - 1)
        ms = ms_next


def _diff_prompt_kernel(qT_ref, k_ref, v_ref, bias_ref, lam_ref, g_ref, o_ref, va_ref, s_ref, p_ref, *, t):
    lam = _lambda(lam_ref)
    g = g_ref[...]
    _with_ones(va_ref, v_ref[...].astype(F32).T.astype(BF))

    def prep(qb):
        return _split_maps(qT_ref[:, _qslice(qb)])

    def scores(ops, qb, kb):
        kblk = k_ref[_kslice(kb), :]
        out = [_nn(kblk, q) for q in ops]
        if kb >= qb - 1:
            out = [s + bias_ref[qb - kb] for s in out]
        return out

    def emit(qb, accs):
        o_ref[:, _qslice(qb)] = _diff_finish(_weighted_mean(accs[0]), _weighted_mean(accs[1]), lam, g)

    _causal_two_pass(t // TQ, prep, scores, [va_ref, va_ref], s_ref, p_ref, emit)


def _score_scratch(n_maps, t):
    return [pltpu.VMEM((2, n_maps, t, TQ), F32), pltpu.VMEM((2, n_maps, t, TQ), BF)]


def _diff_prompt(p, bias, lamp, g):
    nb, _, t = p["qTd"].shape
    head_feat = pl.BlockSpec((None, DIFF_V_DIM, t), lambda b, h: (b, h, 0))
    head_tok = pl.BlockSpec((None, t, DIFF_V_DIM), lambda b, h: (b, 0, h))
    return pl.pallas_call(
        functools.partial(_diff_prompt_kernel, t=t),
        grid=(nb, DIFF_HEADS),
        in_specs=[head_feat, head_tok, head_tok,
                  pl.BlockSpec((None, 2, TK, TQ), lambda b, h: (h, 0, 0, 0)),
                  _const_spec(lamp.shape), _const_spec(g.shape)],
        out_specs=head_feat,
        out_shape=jax.ShapeDtypeStruct((nb, D_MODEL, t), BF),
        scratch_shapes=[pltpu.VMEM((DIFF_V_DIM + ONES_ROWS, t), BF)] + _score_scratch(2, t),
        compiler_params=_params(2), name="diff_prompt",
    )(p["qTd"], p["kb"], p["vb"], bias, lamp, g)


def _mla_prompt_kernel(qT_ref, kp_ref, r_ref, vT_ref, mask_ref, o_ref, ke_ref, ko_ref, vae_ref, vao_ref,
                       s_ref, p_ref, *, t):
    ke, ko = _mla_keys(kp_ref[...], r_ref[...])
    ke_ref[...] = ke
    ko_ref[...] = ko
    _with_ones(vae_ref, vT_ref[0:MLA_V_DIM, :])
    _with_ones(vao_ref, vT_ref[MLA_V_DIM:2 * MLA_V_DIM, :])

    def prep(qb):
        return qT_ref[0:LANES, _qslice(qb)], qT_ref[LANES:2 * LANES, _qslice(qb)]

    def scores(ops, qb, kb):
        out = [_nn(ke_ref[_kslice(kb), :], ops[0]), _nn(ko_ref[_kslice(kb), :], ops[1])]
        if kb == qb:
            out = [s + mask_ref[...] for s in out]
        return out

    def emit(qb, accs):
        o_ref[0:MLA_V_DIM, _qslice(qb)] = _weighted_mean(accs[0]).astype(BF)
        o_ref[MLA_V_DIM:2 * MLA_V_DIM, _qslice(qb)] = _weighted_mean(accs[1]).astype(BF)

    _causal_two_pass(t // TQ, prep, scores, [vae_ref, vao_ref], s_ref, p_ref, emit)


def _mla_prompt(p, mask):
    nb, _, t = p["qTm"].shape
    return pl.pallas_call(
        functools.partial(_mla_prompt_kernel, t=t),
        grid=(nb, MLA_HEADS // 2),
        in_specs=[pl.BlockSpec((None, 2 * LANES, t), lambda b, h: (b, h, 0)),
                  pl.BlockSpec((None, t, LANES), lambda b, h: (b, 0, h)),
                  pl.BlockSpec((None, t, LANES), lambda b, h: (b, 0, 0)),
                  pl.BlockSpec((None, 2 * MLA_V_DIM, t), lambda b, h: (b, h, 0)),
                  _const_spec(mask.shape)],
        out_specs=pl.BlockSpec((None, 2 * MLA_V_DIM, t), lambda b, h: (b, h, 0)),
        out_shape=jax.ShapeDtypeStruct((nb, D_MODEL, t), BF),
        scratch_shapes=[pltpu.VMEM((t, LANES), BF), pltpu.VMEM((t, LANES), BF),
                        pltpu.VMEM((MLA_V_DIM + ONES_ROWS, t), BF),
                        pltpu.VMEM((MLA_V_DIM + ONES_ROWS, t), BF)] + _score_scratch(2, t),
        compiler_params=_params(2), name="mla_prompt",
    )(p["qTm"], p["kn"], p["r"], p["vTm"], mask)


def _diff_decode_kernel(qT_ref, ck_ref, cv_ref, kn_ref, vn_ref, bc_ref, bn_ref, lam_ref, g_ref, o_ref,
                        qbd_ref, m_ref, l_ref, acc_ref, *, new):
    nq = 2 * DIFF_HEADS * new
    assert nq == 2 * LANES and new & (new - 1) == 0
    c = pl.program_id(1)
    last = pl.num_programs(1) - 1
    log2 = lambda n: n.bit_length() - 1

    def lane_head(shape):
        lane = lax.broadcasted_iota(jnp.int32, shape, 1)
        return lax.shift_right_logical(lane, log2(new)) & (DIFF_HEADS - 1)

    @pl.when(c == 0)
    def _():
        m_ref[...] = jnp.full(m_ref.shape, NEG_INF, F32)
        l_ref[...] = jnp.zeros(l_ref.shape, F32)
        acc_ref[...] = jnp.zeros(acc_ref.shape, F32)
        shape = (2 * DIFF_HEAD_DIM, nq)
        row = lax.broadcasted_iota(jnp.int32, shape, 0)
        lane = lax.broadcasted_iota(jnp.int32, shape, 1)
        frame = lax.broadcasted_iota(jnp.int32, (DEC_PAD, nq), 0)
        spread = jnp.where((lax.broadcasted_iota(jnp.int32, (DEC_PAD, nq), 1) & (new - 1)) == frame,
                           1.0, 0.0).astype(BF)
        same_map = lax.shift_right_logical(row, log2(DIFF_HEAD_DIM)) == lax.shift_right_logical(lane, log2(LANES))
        for h in range(DIFF_HEADS):
            rep = _nn(qT_ref[h * DIFF_V_DIM:(h + 1) * DIFF_V_DIM, :], spread)
            qbd_ref[h] = jnp.where(same_map & (lane_head(shape) == h), rep, 0.0).astype(BF)

    def update(keys, vals, bias):
        s = None
        for h in range(DIFF_HEADS):
            sh = _nn(keys(h), qbd_ref[h])
            s = sh if s is None else s + sh
        if bias is not None:
            s = s + bias
        m_new = jnp.maximum(m_ref[...], jnp.max(s, axis=0, keepdims=True))
        alpha = jnp.exp2(m_ref[...] - m_new)
        p = jnp.exp2(s - m_new)
        l_ref[...] = alpha * l_ref[...] + jnp.sum(p, axis=0, keepdims=True)
        m_ref[...] = m_new
        pb = p.astype(BF)
        head = lane_head((DIFF_V_DIM, nq))
        pv = jnp.zeros((DIFF_V_DIM, nq), F32)
        for h in range(DIFF_HEADS):
            pv = jnp.where(head == h, _tn(vals(h), pb), pv)
        acc_ref[...] = alpha * acc_ref[...] + pv

    def cache_rows(ref):
        return lambda h: ref[pl.ds(h, CACHE_BLK, stride=DIFF_HEADS), :].astype(BF)

    @pl.when(c < last)
    def _():
        update(cache_rows(ck_ref), cache_rows(cv_ref), None)

    @pl.when(c == last)
    def _():
        update(cache_rows(ck_ref), cache_rows(cv_ref), bc_ref[...])
        hs = lambda h: slice(h * DIFF_V_DIM, (h + 1) * DIFF_V_DIM)
        update(lambda h: kn_ref[0:new, hs(h)],
               lambda h: vn_ref[pl.ds(h, new, stride=DIFF_HEADS), :].astype(BF), bn_ref[...])
        o = acc_ref[...] * (1.0 / l_ref[...])
        y = _diff_finish(o[:, 0:LANES], o[:, LANES:nq], _lambda(lam_ref), g_ref[...])
        r = lax.broadcasted_iota(jnp.int32, (LANES, DEC_PAD), 0)
        col = lax.broadcasted_iota(jnp.int32, (LANES, DEC_PAD), 1)
        for h in range(DIFF_HEADS):
            fold = jnp.where(r == col + h * new, 1.0, 0.0).astype(BF)
            o_ref[hs(h), :] = _nn(y, jnp.where(col < new, fold, jnp.zeros_like(fold))).astype(BF)


def _diff_decode(s, cache_k, cache_v, bias_c, bias_n, lamp, g, new):
    nb, past = cache_k.shape[:2]
    nq = 2 * DIFF_HEADS * new
    cache_k, cache_v = (a.reshape(nb, past * DIFF_HEADS, DIFF_V_DIM) for a in (cache_k, cache_v))
    stream = lambda rows: pl.BlockSpec((None, rows, D_MODEL), lambda b, c: (b, 0, 0))
    chunk = pl.BlockSpec((None, CACHE_BLK * DIFF_HEADS, DIFF_V_DIM), lambda b, c: (b, c, 0))
    return pl.pallas_call(
        functools.partial(_diff_decode_kernel, new=new),
        grid=(nb, past // CACHE_BLK),
        in_specs=[pl.BlockSpec((None, D_MODEL, DEC_PAD), lambda b, c: (b, 0, 0)), chunk, chunk,
                  stream(DEC_PAD),
                  pl.BlockSpec((None, DEC_PAD * DIFF_HEADS, DIFF_V_DIM), lambda b, c: (b, 0, 0)),
                  _const_spec(bias_c.shape), _const_spec(bias_n.shape),
                  _const_spec(lamp.shape), _const_spec(g.shape)],
        out_specs=pl.BlockSpec((None, D_MODEL, DEC_PAD), lambda b, c: (b, 0, 0)),
        out_shape=jax.ShapeDtypeStruct((nb, D_MODEL, DEC_PAD), BF),
        scratch_shapes=[pltpu.VMEM((DIFF_HEADS, 2 * DIFF_HEAD_DIM, nq), BF),
                        pltpu.VMEM((1, nq), F32), pltpu.VMEM((1, nq), F32),
                        pltpu.VMEM((DIFF_V_DIM, nq), F32)],
        compiler_params=_params(2), name="diff_decode",
    )(s["qTd"], cache_k, cache_v, s["kb"], s["vf"], bias_c, bias_n, lamp, g)


def _mla_decode_kernel(qT_ref, ckv_ref, kr_ref, ckvn_ref, krn_ref, wuk_ref, wuvT_ref, o_ref, s_ref,
                       *, past, new):
    nq = MLA_HEADS * new
    assert nq == 2 * LANES and new & (new - 1) == 0
    log2 = lambda n: n.bit_length() - 1

    def groups(shape, rows_per_group):
        r = lax.broadcasted_iota(jnp.int32, shape, 0)
        c = lax.broadcasted_iota(jnp.int32, shape, 1)
        return lax.shift_right_logical(r, log2(rows_per_group)) == lax.shift_right_logical(c, log2(new))

    def frame_match(shape, lane_axis):
        a = lax.broadcasted_iota(jnp.int32, shape, lane_axis)
        b = lax.broadcasted_iota(jnp.int32, shape, 1 - lane_axis)
        return jnp.where((a & (new - 1)) == b, 1.0, 0.0).astype(BF)

    nope, rope = [], []
    for h in range(MLA_HEADS):
        base = h * LANES
        o_n, o_r = (0, MLA_NOPE_DIM) if h % 2 == 0 else (MLA_NOPE_DIM, 0)
        nope.append(qT_ref[base + o_n:base + o_n + MLA_NOPE_DIM, :])
        rope.append(qT_ref[base + o_r:base + o_r + MLA_ROPE_DIM, :])
    spread = frame_match((DEC_PAD, nq), 1)
    qn = _nn(jnp.concatenate(nope, axis=0), spread)
    qn_bd = jnp.where(groups(qn.shape, MLA_NOPE_DIM), qn, 0.0).astype(BF)
    qr = _nn(jnp.concatenate(rope, axis=0), spread)
    qr_bd = jnp.where(groups(qr.shape, MLA_ROPE_DIM), qr, 0.0)
    qrT = jnp.sum(qr_bd.reshape(MLA_HEADS, MLA_ROPE_DIM, nq), axis=0).astype(BF)
    qlatT = _nn(wuk_ref[...], qn_bd).astype(BF)

    def keys(lo, n):
        if lo < past:
            return ckv_ref[lo:lo + n, :].astype(BF), kr_ref[lo:lo + n, :].astype(BF)
        return ckvn_ref[0:new, :].astype(BF), krn_ref[0:new, :].astype(BF)

    spans = [(c0, CACHE_BLK) for c0 in range(0, past, CACHE_BLK)] + [(past, new)]
    m = None
    for lo, n in spans:
        ck, kr = keys(lo, n)
        s = _nn(ck, qlatT) + _nn(kr, qrT)
        s_ref[lo:lo + n, :] = s
        c = jnp.max(s.reshape(n // 8, 8, nq), axis=0)
        m = c if m is None else jnp.maximum(m, c)
    m = jnp.max(m, axis=0, keepdims=True)
    l = acc = None
    for lo, n in spans:
        p = jnp.exp2(s_ref[lo:lo + n, :] - m)
        ps = jnp.sum(p.reshape(n // 8, 8, nq), axis=0)
        pv = _tn(keys(lo, n)[0], p.astype(BF))
        l, acc = (ps, pv) if l is None else (l + ps, acc + pv)
    o_latT = (acc * (1.0 / jnp.sum(l, axis=0, keepdims=True))).astype(BF)
    y = _nn(wuvT_ref[...], o_latT)
    y_bd = jnp.where(groups(y.shape, MLA_V_DIM), y, 0.0).astype(BF)
    o_ref[...] = _nn(y_bd, frame_match((nq, DEC_PAD), 0)).astype(BF)


def _mla_decode(s, cache_ckv, cache_kr, w, new):
    nb, past, _ = cache_ckv.shape
    per_stream = lambda rows, cols: pl.BlockSpec((None, rows, cols), lambda b: (b, 0, 0))
    return pl.pallas_call(
        functools.partial(_mla_decode_kernel, past=past, new=new),
        grid=(nb,),
        in_specs=[per_stream(MLA_HEADS * LANES, DEC_PAD), per_stream(past, MLA_KV_LORA),
                  per_stream(past, MLA_ROPE_DIM), per_stream(DEC_PAD, MLA_KV_LORA),
                  per_stream(DEC_PAD, MLA_ROPE_DIM), _const_spec(w["wuk"].shape),
                  _const_spec(w["wuvT"].shape)],
        out_specs=per_stream(D_MODEL, DEC_PAD),
        out_shape=jax.ShapeDtypeStruct((nb, D_MODEL, DEC_PAD), BF),
        scratch_shapes=[pltpu.VMEM((past + new, MLA_HEADS * new), F32)],
        compiler_params=_params(1), name="mla_decode",
    )(s["qTm"], cache_ckv, cache_kr, s["ckv"], s["kr"], w["wuk"], w["wuvT"])


def _post_kernel(x_ref, oTd_ref, oTm_ref, gmix_ref, wgT_ref, wodT_ref, womT_ref, woutT_ref,
                 gmlp_ref, wup_ref, wdown_ref, gfin_ref, y_ref, *, n_groups):
    rows = x_ref.shape[0] // n_groups
    groups = [slice(i * rows, (i + 1) * rows) for i in range(n_groups)]
    d = D_MODEL

    def norm_in(v, r):
        v["x"] = x_ref[r, :]
        v["hb"] = _rms(v["x"], gmix_ref[...]).astype(BF)

    def branches(v, r):
        v["gT"] = _nt(wgT_ref[...], v.pop("hb"))
        v["od"] = _nn(wodT_ref[...], oTd_ref[:, r])
        v["om"] = _nn(womT_ref[...], oTm_ref[:, r])

    def merge(v, r):
        g = jax.nn.sigmoid(v.pop("gT"))
        v["mT"] = (g[0:d, :] * v.pop("od") + g[d:2 * d, :] * v.pop("om")).astype(BF)

    def out_proj(v, r):
        v["aT"] = _nn(woutT_ref[...], v.pop("mT"))

    def residual(v, r):
        v["x1"] = v.pop("x") + v.pop("aT").T
        v["h2"] = _rms(v["x1"], gmlp_ref[...]).astype(BF)

    def up(v, r):
        v["u"] = _nn(v.pop("h2"), wup_ref[...])

    def act(v, r):
        u = jnp.maximum(v.pop("u"), 0.0)
        v["uu"] = (u * u).astype(BF)

    def down(v, r):
        v["x2"] = v.pop("x1") + _nn(v.pop("uu"), wdown_ref[...])

    def norm_out(v, r):
        y_ref[r, :] = _rms(v.pop("x2"), gfin_ref[...])

    values = [{} for _ in groups]
    for stage in (norm_in, branches, merge, out_proj, residual, up, act, down, norm_out):
        for v, r in zip(values, groups):
            stage(v, r)


def _post(x, oTd, oTm, w, tm, n_groups):
    nb, t, _ = x.shape
    tok = pl.BlockSpec((None, tm, D_MODEL), lambda b, i: (b, i, 0))
    feat = pl.BlockSpec((None, D_MODEL, tm), lambda b, i: (b, 0, i))
    weights = [w["gmix"], w["wgT"], w["wodT"], w["womT"], w["woutT"], w["gmlp"], w["wup"],
               w["wdown"], w["gfin"]]
    return pl.pallas_call(
        functools.partial(_post_kernel, n_groups=n_groups), grid=(nb, t // tm),
        in_specs=[tok, feat, feat] + [_const_spec(a.shape) for a in weights],
        out_specs=tok, out_shape=jax.ShapeDtypeStruct((nb, t, D_MODEL), F32),
        compiler_params=_params(2), name="post",
    )(x, oTd, oTm, *weights)


def _prep_weights(norm_mix, w_in, mla_q_norm, mla_w_uq, mla_kv_norm, mla_w_uk, mla_w_uv, w_o_diff,
                  w_o_mla, w_out, norm_mlp, w_up, w_down, norm_final):
    d = D_MODEL
    o_cq = 3 * d
    o_ckv = o_cq + MLA_Q_LORA
    o_kr = o_ckv + MLA_KV_LORA
    o_g = o_kr + MLA_ROPE_DIM
    half = MLA_ROPE_DIM // 2
    rot = lambda a: jnp.concatenate([-a[..., half:], a[..., :half]], axis=-1)
    zpad = lambda a, n: jnp.zeros(a.shape[:-1] + (n,), a.dtype)

    wkr = w_in[:, o_kr:o_g]
    place = lambda a: jnp.concatenate([a, zpad(a, 32), a, zpad(a, 32)], axis=-1)
    wlat = jnp.concatenate([w_in[:, o_cq:o_kr], place(wkr), place(rot(wkr))], axis=-1)

    nope = mla_w_uq[:, :, :MLA_NOPE_DIM]
    rope = mla_w_uq[:, :, MLA_NOPE_DIM:]

    def arrange(n, r):
        even = jnp.concatenate([n[:, 0::2], r[:, 0::2], zpad(r[:, 0::2], 32)], axis=-1)
        odd = jnp.concatenate([r[:, 1::2], zpad(r[:, 1::2], 32), n[:, 1::2]], axis=-1)
        both = jnp.stack([even, odd], axis=2)
        return both.reshape(MLA_Q_LORA, MLA_HEADS * LANES)

    row = lambda a: a.reshape(1, -1).astype(F32)
    return {
        "gmix": row(norm_mix), "gq": row(mla_q_norm), "gkv": row(mla_kv_norm),
        "gmlp": row(norm_mlp), "gfin": row(norm_final),
        "wqT": w_in[:, 0:d].T.astype(BF),
        "wk": w_in[:, d:2 * d].astype(BF),
        "wv": w_in[:, 2 * d:3 * d].astype(BF),
        "wlat": wlat.astype(BF),
        "wgT": w_in[:, o_g:].T.astype(BF),
        "wuqaT": arrange(nope, rope).T.astype(BF),
        "wuqbT": rot(rope).reshape(MLA_Q_LORA, MLA_HEADS * MLA_ROPE_DIM).T.astype(BF),
        "wuk": mla_w_uk.reshape(MLA_KV_LORA, d).astype(BF),
        "wuvT": mla_w_uv.reshape(MLA_KV_LORA, d).T.astype(BF),
        "wodT": w_o_diff.reshape(d, d).T.astype(BF),
        "womT": w_o_mla.reshape(d, d).T.astype(BF),
        "woutT": w_out.T.astype(BF),
        "wup": w_up.astype(BF),
        "wdown": w_down.astype(BF),
    }


def _rope_tables(pos):
    half = MLA_ROPE_DIM // 2
    inv = jnp.power(ROPE_THETA, -jnp.arange(half, dtype=F32) * 2.0 / MLA_ROPE_DIM)
    ang = pos.astype(F32)[:, None] * inv[None, :]
    cos2 = jnp.tile(jnp.cos(ang), (1, 2))
    sin2 = jnp.tile(jnp.sin(ang), (1, 2))
    t = pos.shape[0]
    z32 = jnp.zeros((t, 32), F32)
    place = lambda a: jnp.concatenate([a, z32, a, z32], axis=1)
    scale = (MLA_NOPE_DIM + MLA_ROPE_DIM) ** -0.5 * LOG2E
    ones = jnp.ones((t, MLA_NOPE_DIM), F32)
    z64 = jnp.zeros((t, MLA_NOPE_DIM), F32)
    ct = jnp.concatenate([ones, cos2, z32, cos2, z32, ones], axis=1) * scale
    st = jnp.concatenate([z64, sin2, z32, sin2, z32, z64], axis=1) * scale
    return {"cosp": place(cos2), "sinp": place(sin2), "ct": ct.T, "st": st.T}


def _score_tiles(rel_bias, k_pos_tiles, q_pos, shift_bucket):
    k_pos = jnp.stack(k_pos_tiles)
    rel = k_pos[:, :, None] - q_pos[None, None, :]
    mask = (k_pos // CHUNK)[:, :, None] <= (q_pos // CHUNK)[None, None, :]
    return _bias_tiles(rel_bias, _t5_bucket(rel).astype(jnp.int32), mask.astype(jnp.int32), shift_bucket)


def _decode_score_tiles(rel_bias, k_pos, q_pos, new, shift_bucket):
    lane = jnp.arange(2 * DIFF_HEADS * new, dtype=jnp.int32)
    qp = q_pos[lane % new]
    rel = k_pos[:, None] - qp[None, :]
    mask = (k_pos // CHUNK)[:, None] <= (qp // CHUNK)[None, :]
    tab = rel_bias[:, (lane // new) % DIFF_HEADS].astype(F32)
    return _bias_lanes(tab, _t5_bucket(rel).astype(jnp.int32), mask.astype(jnp.int32), shift_bucket)


def kernel(x_prompt, x_sample, cache_diff_k, cache_diff_v, cache_mla_ckv, cache_mla_krope, rel_bias,
           norm_mix, w_in, lam_q1, lam_k1, lam_q2, lam_k2, diff_subln, mla_q_norm, mla_w_uq, mla_kv_norm,
           mla_w_uk, mla_w_uv, w_o_diff, w_o_mla, w_out, norm_mlp, w_up, w_down, norm_final):
    assert norm_mix.shape[0] == 1, "single-layer model"
    nb, t, _ = x_prompt.shape
    ns, new, _ = x_sample.shape
    past = cache_diff_k.shape[2]
    assert t % TQ == 0 and past % CACHE_BLK == 0 and CACHE_BLK >= TK and new <= DEC_PAD and new % 16 == 0
    assert past % CHUNK == 0 and new <= CHUNK

    w = _prep_weights(norm_mix[0], w_in[0], mla_q_norm[0], mla_w_uq[0], mla_kv_norm[0], mla_w_uk[0],
                      mla_w_uv[0], w_o_diff[0], w_o_mla[0], w_out[0], norm_mlp[0], w_up[0], w_down[0],
                      norm_final)
    lamp = jnp.stack([lam_q1[0], lam_k1[0], lam_q2[0], lam_k2[0]]).astype(F32)
    g_sub = diff_subln[0].reshape(DIFF_V_DIM, 1).astype(F32)

    pos_p = jnp.arange(t, dtype=jnp.int32)
    p = _project(x_prompt, w, _rope_tables(pos_p), tm=256)
    blk = jnp.arange(TQ, dtype=jnp.int32)
    tiles_p = _score_tiles(rel_bias, [TQ + blk, blk], TQ + blk, _far_bucket())
    mask_p = jnp.where((blk // CHUNK)[:, None] <= (blk // CHUNK)[None, :], 0.0, NEG_INF).astype(F32)
    oTd = _diff_prompt(p, tiles_p, lamp, g_sub)
    oTm = _mla_prompt(p, mask_p)
    y_prompt = _post(x_prompt, oTd, oTm, w, tm=512, n_groups=2)

    xs = jnp.pad(x_sample, ((0, 0), (0, DEC_PAD - new), (0, 0)))
    pos_s = past + jnp.arange(DEC_PAD, dtype=jnp.int32)
    s = _project(xs, w, _rope_tables(pos_s), tm=DEC_PAD)
    bias_c = _decode_score_tiles(rel_bias, jnp.arange(past - CACHE_BLK, past, dtype=jnp.int32), pos_s, new,
                                 _far_bucket())
    bias_n = _decode_score_tiles(rel_bias, pos_s[:new], pos_s, new, _far_bucket())
    oTd_s = _diff_decode(s, cache_diff_k[0], cache_diff_v[0], bias_c, bias_n, lamp, g_sub, new)
    oTm_s = _mla_decode(s, cache_mla_ckv[0], cache_mla_krope[0], w, new)
    y_sample = _post(xs, oTd_s, oTm_s, w, tm=DEC_PAD, n_groups=1)[:, :new]

    heads = lambda a, frames: a.reshape(1, a.shape[0], -1, DIFF_HEADS, DIFF_V_DIM)[:, :, :frames]
    return (y_prompt, y_sample,
            heads(p["kf"], t), heads(p["vf"], t), p["ckv"][None], p["kr"][None],
            heads(s["kf"], new), heads(s["vf"], new),
            s["ckv"][:, :new][None], s["kr"][:, :new][None])
```

```python
import functools
import math

import numpy as np
import jax
import jax.numpy as jnp
from jax import lax
from jax.experimental import pallas as pl
from jax.experimental.pallas import tpu as pltpu

D_MODEL = 1024
CHUNK = 64
DIFF_HEADS = 8
DIFF_HEAD_DIM = 64
DIFF_V_DIM = 128
MLA_HEADS = 16
MLA_NOPE_DIM = 64
MLA_ROPE_DIM = 32
MLA_V_DIM = 64
MLA_Q_LORA = 256
MLA_KV_LORA = 256
D_FF = 4 * D_MODEL
NUM_BUCKETS = 32
MAX_DISTANCE = 128
ROPE_THETA = 10000.0
EPS = 1e-6
NEG_INF = -1e30
LAMBDA_INIT = 0.8 - 0.6 * math.exp(-0.3 * 0)

BF = jnp.bfloat16
F32 = jnp.float32

LANES = 128
TQ = 256
TK = 256
DEC_PAD = LANES
CACHE_BLK = 512
VMEM_LIMIT = 56 * 1024 * 1024
LOG2E = math.log2(math.e)
ONES_ROWS = 16


def _nn(a, b):
    return jnp.dot(a, b, preferred_element_type=F32)


def _nt(a, b):
    return lax.dot_general(a, b, (((1,), (1,)), ((), ())), preferred_element_type=F32)


def _tn(a, b):
    return lax.dot_general(a, b, (((0,), (0,)), ((), ())), preferred_element_type=F32)


def _rms(x, g):
    return x * lax.rsqrt(jnp.mean(x * x, axis=-1, keepdims=True) + EPS) * g


def _const_spec(shape):
    zeros = (0,) * len(shape)
    return pl.BlockSpec(shape, lambda *_: zeros, pipeline_mode=pl.Buffered(1))


def _params(n_axes):
    return pltpu.CompilerParams(dimension_semantics=("arbitrary",) * n_axes,
                                vmem_limit_bytes=VMEM_LIMIT)


def _bias_kernel(tab_ref, bkt_ref, msk_ref, o_ref, *, shift_bucket):
    h = pl.program_id(0)
    bkt = bkt_ref[...]
    acc = jnp.zeros(bkt.shape, F32)
    for b in range(NUM_BUCKETS):
        acc = jnp.where(bkt == b, tab_ref[b, h], acc)
    if shift_bucket is not None:
        acc = acc - tab_ref[shift_bucket, h]
    o_ref[...] = jnp.where(msk_ref[...] != 0, acc * LOG2E, NEG_INF)


def _bias_tiles(rel_bias, bucket, mask, shift_bucket):
    n, r, c = bucket.shape
    return pl.pallas_call(
        functools.partial(_bias_kernel, shift_bucket=shift_bucket),
        grid=(DIFF_HEADS,),
        in_specs=[pl.BlockSpec(memory_space=pltpu.SMEM),
                  pl.BlockSpec((n, r, c), lambda h: (0, 0, 0)),
                  pl.BlockSpec((n, r, c), lambda h: (0, 0, 0))],
        out_specs=pl.BlockSpec((None, n, r, c), lambda h: (h, 0, 0, 0)),
        out_shape=jax.ShapeDtypeStruct((DIFF_HEADS, n, r, c), F32),
        compiler_params=_params(1),
        name="bias_tiles",
    )(rel_bias, bucket, mask)


def _bias_lanes_kernel(tab_ref, bkt_ref, msk_ref, o_ref, *, shift_bucket):
    bkt = bkt_ref[...]
    acc = jnp.zeros(bkt.shape, F32)
    for b in range(NUM_BUCKETS):
        acc = jnp.where(bkt == b, tab_ref[b:b + 1, :], acc)
    if shift_bucket is not None:
        acc = acc - tab_ref[shift_bucket:shift_bucket + 1, :]
    o_ref[...] = jnp.where(msk_ref[...] != 0, acc * LOG2E, NEG_INF)


def _bias_lanes(tab, bucket, mask, shift_bucket):
    return pl.pallas_call(
        functools.partial(_bias_lanes_kernel, shift_bucket=shift_bucket),
        out_shape=jax.ShapeDtypeStruct(bucket.shape, F32), name="bias_lanes",
    )(tab, bucket, mask)


def _t5_bucket(rel):
    half = NUM_BUCKETS // 2
    max_exact = half // 2
    n = jnp.abs(rel)
    nf = jnp.maximum(n, max_exact).astype(F32)
    large = max_exact + (jnp.log(nf / max_exact) / math.log(MAX_DISTANCE / max_exact)
                         * (half - max_exact)).astype(jnp.int32)
    large = jnp.minimum(large, half - 1)
    return jnp.where(rel > 0, half, 0) + jnp.where(n < max_exact, n, large)


def _far_bucket():
    half = NUM_BUCKETS // 2
    max_exact = half // 2
    assert max_exact + math.log(TK / max_exact) / math.log(MAX_DISTANCE / max_exact) * (half - max_exact) >= half
    return half - 1


def _proj_kernel(x_ref, gmix_ref, wqT_ref, wk_ref, wv_ref, wlat_ref, gq_ref, gkv_ref,
                 wuqaT_ref, wuqbT_ref, wuk_ref, wuvT_ref, cosp_ref, sinp_ref, ct_ref, st_ref,
                 qTd_ref, kf_ref, kb_ref, vf_ref, vb_ref, qTm_ref, ckv_ref, kn_ref, vTm_ref,
                 r_ref, kr_ref):
    hb = _rms(x_ref[...], gmix_ref[...]).astype(BF)
    lat = _nn(hb, wlat_ref[...])
    qTd_ref[...] = (_nt(wqT_ref[...], hb) * (DIFF_HEAD_DIM ** -0.5 * LOG2E)).astype(BF)
    k = _nn(hb, wk_ref[...])
    kb_ref[...] = k.astype(BF)
    v = _nn(hb, wv_ref[...])
    vb_ref[...] = v.astype(BF)
    tm = k.shape[0]
    for h in range(DIFF_HEADS):
        rows = pl.ds(h, tm, stride=DIFF_HEADS)
        kf_ref[rows, :] = k[:, h * DIFF_V_DIM:(h + 1) * DIFF_V_DIM]
        vf_ref[rows, :] = v[:, h * DIFF_V_DIM:(h + 1) * DIFF_V_DIM]

    cq = _rms(lat[:, 0:MLA_Q_LORA], gq_ref[...]).astype(BF)
    qa = _nt(wuqaT_ref[...], cq)
    qb = _nt(wuqbT_ref[...], cq)
    ct = ct_ref[...]
    st = st_ref[...]
    pair = 2 * LANES
    rd = MLA_ROPE_DIM
    for p in range(MLA_HEADS // 2):
        sl = slice(p * pair, (p + 1) * pair)
        z = jnp.zeros((MLA_NOPE_DIM, tm), F32)
        partner = jnp.concatenate([z, qb[2 * p * rd:(2 * p + 1) * rd, :], z[0:rd, :],
                                   qb[(2 * p + 1) * rd:(2 * p + 2) * rd, :], z[0:rd, :], z], axis=0)
        qTm_ref[sl, :] = (qa[sl, :] * ct + partner * st).astype(BF)

    ckv = _rms(lat[:, MLA_Q_LORA:MLA_Q_LORA + MLA_KV_LORA], gkv_ref[...])
    ckv_ref[...] = ckv
    cb = ckv.astype(BF)
    kn_ref[...] = _nn(cb, wuk_ref[...]).astype(BF)
    vTm_ref[...] = _nt(wuvT_ref[...], cb).astype(BF)

    o = MLA_Q_LORA + MLA_KV_LORA
    r = lat[:, o:o + LANES] * cosp_ref[...] + lat[:, o + LANES:o + 2 * LANES] * sinp_ref[...]
    r_ref[...] = r.astype(BF)
    kr_ref[...] = r[:, 0:MLA_ROPE_DIM]


def _project(x, w, tabs, tm):
    nb, t, _ = x.shape
    nt = t // tm
    tok = lambda width: pl.BlockSpec((None, tm, width), lambda b, i: (b, i, 0))
    feat = lambda rows: pl.BlockSpec((None, rows, tm), lambda b, i: (b, 0, i))
    sds = jax.ShapeDtypeStruct
    weights = [w["gmix"], w["wqT"], w["wk"], w["wv"], w["wlat"], w["gq"], w["gkv"],
               w["wuqaT"], w["wuqbT"], w["wuk"], w["wuvT"]]
    in_specs = ([tok(D_MODEL)] + [_const_spec(a.shape) for a in weights]
                + [pl.BlockSpec((tm, LANES), lambda b, i: (i, 0)),
                   pl.BlockSpec((tm, LANES), lambda b, i: (i, 0)),
                   pl.BlockSpec((2 * LANES, tm), lambda b, i: (0, i)),
                   pl.BlockSpec((2 * LANES, tm), lambda b, i: (0, i))])
    head_rows = pl.BlockSpec((None, tm * DIFF_HEADS, DIFF_V_DIM), lambda b, i: (b, i, 0))
    out_shape = [sds((nb, D_MODEL, t), BF), sds((nb, t * DIFF_HEADS, DIFF_V_DIM), F32), sds((nb, t, D_MODEL), BF),
                 sds((nb, t * DIFF_HEADS, DIFF_V_DIM), F32), sds((nb, t, D_MODEL), BF),
                 sds((nb, MLA_HEADS * LANES, t), BF), sds((nb, t, MLA_KV_LORA), F32),
                 sds((nb, t, D_MODEL), BF), sds((nb, D_MODEL, t), BF),
                 sds((nb, t, LANES), BF), sds((nb, t, MLA_ROPE_DIM), F32)]
    out_specs = [feat(D_MODEL), head_rows, tok(D_MODEL), head_rows, tok(D_MODEL),
                 feat(MLA_HEADS * LANES), tok(MLA_KV_LORA), tok(D_MODEL), feat(D_MODEL),
                 tok(LANES), tok(MLA_ROPE_DIM)]
    outs = pl.pallas_call(
        _proj_kernel, grid=(nb, nt), in_specs=in_specs, out_specs=out_specs, out_shape=out_shape,
        compiler_params=_params(2), name="project",
    )(x, *weights, tabs["cosp"], tabs["sinp"], tabs["ct"], tabs["st"])
    names = ["qTd", "kf", "kb", "vf", "vb", "qTm", "ckv", "kn", "vTm", "r", "kr"]
    return dict(zip(names, outs))


def _flash_step(state, s, pv):
    bm = jnp.max(s, axis=0, keepdims=True)
    if state is None:
        p = jnp.exp2(s - bm)
        return bm, jnp.sum(p, axis=0, keepdims=True), pv(p.astype(BF))
    m, l, acc = state
    m_new = jnp.maximum(m, bm)
    alpha = jnp.exp2(m - m_new)
    p = jnp.exp2(s - m_new)
    return m_new, alpha * l + jnp.sum(p, axis=0, keepdims=True), alpha * acc + pv(p.astype(BF))


def _normalized(state):
    _, l, acc = state
    return acc * (1.0 / l)


def _split_maps(qT):
    rows = lax.broadcasted_iota(jnp.int32, qT.shape, 0)
    qf = qT.astype(F32)
    zero = jnp.zeros_like(qf)
    return (jnp.where(rows < DIFF_HEAD_DIM, qf, zero).astype(BF),
            jnp.where(rows >= DIFF_HEAD_DIM, qf, zero).astype(BF))


def _lambda(lam_ref):
    lp = lam_ref[...]
    a = jnp.sum(lp[0:1, :] * lp[1:2, :], axis=1, keepdims=True)
    b = jnp.sum(lp[2:3, :] * lp[3:4, :], axis=1, keepdims=True)
    return jnp.exp(a) - jnp.exp(b) + LAMBDA_INIT


def _diff_finish(o1, o2, lam, g):
    o = o1 - lam * o2
    ms = jnp.mean(o * o, axis=0, keepdims=True)
    return (o * lax.rsqrt(ms + EPS) * g * (1.0 - LAMBDA_INIT)).astype(BF)


def _mla_keys(kp, r):
    lane = lax.broadcasted_iota(jnp.int32, kp.shape, 1)
    kpf = kp.astype(F32)
    rf = r.astype(F32)
    return (jnp.where(lane < MLA_NOPE_DIM, kpf, rf).astype(BF),
            jnp.where(lane >= MLA_NOPE_DIM, kpf, rf).astype(BF))


def _qslice(qb):
    return slice(qb * TQ, (qb + 1) * TQ)


def _kslice(kb):
    return slice(kb * TK, (kb + 1) * TK)


def _with_ones(va_ref, vT):
    dv, t = vT.shape
    va_ref[0:dv, :] = vT
    va_ref[dv:dv + ONES_ROWS, :] = jnp.ones((ONES_ROWS, t), BF)


def _weighted_mean(acc):
    dv = acc.shape[0] - ONES_ROWS
    return acc[0:dv, :] * (1.0 / acc[dv:dv + 1, :])


def _causal_two_pass(nq, prep, scores, va_refs, s_ref, p_ref, emit):
    n_maps = len(va_refs)

    def score_pass(qb):
        ops = prep(qb)
        m8 = [None] * n_maps
        for kb in range(qb + 1):
            for i, s in enumerate(scores(ops, qb, kb)):
                s_ref[qb % 2, i, _kslice(kb), :] = s
                c = jnp.max(s.reshape(TK // 8, 8, TQ), axis=0)
                m8[i] = c if m8[i] is None else jnp.maximum(m8[i], c)
        return [jnp.max(m, axis=0, keepdims=True) for m in m8]

    def exp_pass(qb, ms):
        for kb in range(qb + 1):
            for i in range(n_maps):
                p_ref[qb % 2, i, _kslice(kb), :] = jnp.exp2(s_ref[qb % 2, i, _kslice(kb), :] - ms[i]).astype(BF)

    def value_pass(qb):
        klen = (qb + 1) * TK
        emit(qb, [_nn(va_refs[i][:, 0:klen], p_ref[qb % 2, i, 0:klen, :]) for i in range(n_maps)])

    ms = score_pass(0)
    for qb in range(nq + 1):
        ms_next = score_pass(qb + 1) if qb + 1 < nq else None
        if qb < nq:
            exp_pass(qb, ms)
        if qb >= 1:
            value_pass(qb - 1)
        ms = ms_next


def _diff_prompt_kernel(qT_ref, k_ref, v_ref, bias_ref, lam_ref, g_ref, o_ref, va_ref, s_ref, p_ref, *, t):
    lam = _lambda(lam_ref)
    g = g_ref[...]
    _with_ones(va_ref, v_ref[...].astype(F32).T.astype(BF))

    def prep(qb):
        return _split_maps(qT_ref[:, _qslice(qb)])

    def scores(ops, qb, kb):
        kblk = k_ref[_kslice(kb), :]
        out = [_nn(kblk, q) for q in ops]
        if kb >= qb - 1:
            out = [s + bias_ref[qb - kb] for s in out]
        return out

    def emit(qb, accs):
        o_ref[:, _qslice(qb)] = _diff_finish(_weighted_mean(accs[0]), _weighted_mean(accs[1]), lam, g)

    _causal_two_pass(t // TQ, prep, scores, [va_ref, va_ref], s_ref, p_ref, emit)


def _score_scratch(n_maps, t):
    return [pltpu.VMEM((2, n_maps, t, TQ), F32), pltpu.VMEM((2, n_maps, t, TQ), BF)]


def _diff_prompt(p, bias, lamp, g):
    nb, _, t = p["qTd"].shape
    head_feat = pl.BlockSpec((None, DIFF_V_DIM, t), lambda b, h: (b, h, 0))
    head_tok = pl.BlockSpec((None, t, DIFF_V_DIM), lambda b, h: (b, 0, h))
    return pl.pallas_call(
        functools.partial(_diff_prompt_kernel, t=t),
        grid=(nb, DIFF_HEADS),
        in_specs=[head_feat, head_tok, head_tok,
                  pl.BlockSpec((None, 2, TK, TQ), lambda b, h: (h, 0, 0, 0)),
                  _const_spec(lamp.shape), _const_spec(g.shape)],
        out_specs=head_feat,
        out_shape=jax.ShapeDtypeStruct((nb, D_MODEL, t), BF),
        scratch_shapes=[pltpu.VMEM((DIFF_V_DIM + ONES_ROWS, t), BF)] + _score_scratch(2, t),
        compiler_params=_params(2), name="diff_prompt",
    )(p["qTd"], p["kb"], p["vb"], bias, lamp, g)


def _mla_prompt_kernel(qT_ref, kp_ref, r_ref, vT_ref, mask_ref, o_ref, ke_ref, ko_ref, vae_ref, vao_ref,
                       s_ref, p_ref, *, t):
    ke, ko = _mla_keys(kp_ref[...], r_ref[...])
    ke_ref[...] = ke
    ko_ref[...] = ko
    _with_ones(vae_ref, vT_ref[0:MLA_V_DIM, :])
    _with_ones(vao_ref, vT_ref[MLA_V_DIM:2 * MLA_V_DIM, :])

    def prep(qb):
        return qT_ref[0:LANES, _qslice(qb)], qT_ref[LANES:2 * LANES, _qslice(qb)]

    def scores(ops, qb, kb):
        out = [_nn(ke_ref[_kslice(kb), :], ops[0]), _nn(ko_ref[_kslice(kb), :], ops[1])]
        if kb == qb:
            out = [s + mask_ref[...] for s in out]
        return out

    def emit(qb, accs):
        o_ref[0:MLA_V_DIM, _qslice(qb)] = _weighted_mean(accs[0]).astype(BF)
        o_ref[MLA_V_DIM:2 * MLA_V_DIM, _qslice(qb)] = _weighted_mean(accs[1]).astype(BF)

    _causal_two_pass(t // TQ, prep, scores, [vae_ref, vao_ref], s_ref, p_ref, emit)


def _mla_prompt(p, mask):
    nb, _, t = p["qTm"].shape
    return pl.pallas_call(
        functools.partial(_mla_prompt_kernel, t=t),
        grid=(nb, MLA_HEADS // 2),
        in_specs=[pl.BlockSpec((None, 2 * LANES, t), lambda b, h: (b, h, 0)),
                  pl.BlockSpec((None, t, LANES), lambda b, h: (b, 0, h)),
                  pl.BlockSpec((None, t, LANES), lambda b, h: (b, 0, 0)),
                  pl.BlockSpec((None, 2 * MLA_V_DIM, t), lambda b, h: (b, h, 0)),
                  _const_spec(mask.shape)],
        out_specs=pl.BlockSpec((None, 2 * MLA_V_DIM, t), lambda b, h: (b, h, 0)),
        out_shape=jax.ShapeDtypeStruct((nb, D_MODEL, t), BF),
        scratch_shapes=[pltpu.VMEM((t, LANES), BF), pltpu.VMEM((t, LANES), BF),
                        pltpu.VMEM((MLA_V_DIM + ONES_ROWS, t), BF),
                        pltpu.VMEM((MLA_V_DIM + ONES_ROWS, t), BF)] + _score_scratch(2, t),
        compiler_params=_params(2), name="mla_prompt",
    )(p["qTm"], p["kn"], p["r"], p["vTm"], mask)


def _diff_decode_kernel(qT_ref, ck_ref, cv_ref, kn_ref, vn_ref, bc_ref, bn_ref, lam_ref, g_ref, o_ref,
                        qbd_ref, m_ref, l_ref, acc_ref, *, new):
    nq = 2 * DIFF_HEADS * new
    assert nq == 2 * LANES and new & (new - 1) == 0
    c = pl.program_id(1)
    last = pl.num_programs(1) - 1
    log2 = lambda n: n.bit_length() - 1

    def lane_head(shape):
        lane = lax.broadcasted_iota(jnp.int32, shape, 1)
        return lax.shift_right_logical(lane, log2(new)) & (DIFF_HEADS - 1)

    @pl.when(c == 0)
    def _():
        m_ref[...] = jnp.full(m_ref.shape, NEG_INF, F32)
        l_ref[...] = jnp.zeros(l_ref.shape, F32)
        acc_ref[...] = jnp.zeros(acc_ref.shape, F32)
        shape = (2 * DIFF_HEAD_DIM, nq)
        row = lax.broadcasted_iota(jnp.int32, shape, 0)
        lane = lax.broadcasted_iota(jnp.int32, shape, 1)
        frame = lax.broadcasted_iota(jnp.int32, (DEC_PAD, nq), 0)
        spread = jnp.where((lax.broadcasted_iota(jnp.int32, (DEC_PAD, nq), 1) & (new - 1)) == frame,
                           1.0, 0.0).astype(BF)
        same_map = lax.shift_right_logical(row, log2(DIFF_HEAD_DIM)) == lax.shift_right_logical(lane, log2(LANES))
        for h in range(DIFF_HEADS):
            rep = _nn(qT_ref[h * DIFF_V_DIM:(h + 1) * DIFF_V_DIM, :], spread)
            qbd_ref[h] = jnp.where(same_map & (lane_head(shape) == h), rep, 0.0).astype(BF)

    def update(keys, vals, bias):
        s = None
        for h in range(DIFF_HEADS):
            sh = _nn(keys(h), qbd_ref[h])
            s = sh if s is None else s + sh
        if bias is not None:
            s = s + bias
        m_new = jnp.maximum(m_ref[...], jnp.max(s, axis=0, keepdims=True))
        alpha = jnp.exp2(m_ref[...] - m_new)
        p = jnp.exp2(s - m_new)
        l_ref[...] = alpha * l_ref[...] + jnp.sum(p, axis=0, keepdims=True)
        m_ref[...] = m_new
        pb = p.astype(BF)
        head = lane_head((DIFF_V_DIM, nq))
        pv = jnp.zeros((DIFF_V_DIM, nq), F32)
        for h in range(DIFF_HEADS):
            pv = jnp.where(head == h, _tn(vals(h), pb), pv)
        acc_ref[...] = alpha * acc_ref[...] + pv

    def cache_rows(ref):
        return lambda h: ref[pl.ds(h, CACHE_BLK, stride=DIFF_HEADS), :].astype(BF)

    @pl.when(c < last)
    def _():
        update(cache_rows(ck_ref), cache_rows(cv_ref), None)

    @pl.when(c == last)
    def _():
        update(cache_rows(ck_ref), cache_rows(cv_ref), bc_ref[...])
        hs = lambda h: slice(h * DIFF_V_DIM, (h + 1) * DIFF_V_DIM)
        update(lambda h: kn_ref[0:new, hs(h)],
               lambda h: vn_ref[pl.ds(h, new, stride=DIFF_HEADS), :].astype(BF), bn_ref[...])
        o = acc_ref[...] * (1.0 / l_ref[...])
        y = _diff_finish(o[:, 0:LANES], o[:, LANES:nq], _lambda(lam_ref), g_ref[...])
        r = lax.broadcasted_iota(jnp.int32, (LANES, DEC_PAD), 0)
        col = lax.broadcasted_iota(jnp.int32, (LANES, DEC_PAD), 1)
        for h in range(DIFF_HEADS):
            fold = jnp.where(r == col + h * new, 1.0, 0.0).astype(BF)
            o_ref[hs(h), :] = _nn(y, jnp.where(col < new, fold, jnp.zeros_like(fold))).astype(BF)


def _diff_decode(s, cache_k, cache_v, bias_c, bias_n, lamp, g, new):
    nb, past = cache_k.shape[:2]
    nq = 2 * DIFF_HEADS * new
    cache_k, cache_v = (a.reshape(nb, past * DIFF_HEADS, DIFF_V_DIM) for a in (cache_k, cache_v))
    stream = lambda rows: pl.BlockSpec((None, rows, D_MODEL), lambda b, c: (b, 0, 0))
    chunk = pl.BlockSpec((None, CACHE_BLK * DIFF_HEADS, DIFF_V_DIM), lambda b, c: (b, c, 0))
    return pl.pallas_call(
        functools.partial(_diff_decode_kernel, new=new),
        grid=(nb, past // CACHE_BLK),
        in_specs=[pl.BlockSpec((None, D_MODEL, DEC_PAD), lambda b, c: (b, 0, 0)), chunk, chunk,
                  stream(new),
                  pl.BlockSpec((None, new * DIFF_HEADS, DIFF_V_DIM), lambda b, c: (b, 0, 0)),
                  _const_spec(bias_c.shape), _const_spec(bias_n.shape),
                  _const_spec(lamp.shape), _const_spec(g.shape)],
        out_specs=pl.BlockSpec((None, D_MODEL, DEC_PAD), lambda b, c: (b, 0, 0)),
        out_shape=jax.ShapeDtypeStruct((nb, D_MODEL, DEC_PAD), BF),
        scratch_shapes=[pltpu.VMEM((DIFF_HEADS, 2 * DIFF_HEAD_DIM, nq), BF),
                        pltpu.VMEM((1, nq), F32), pltpu.VMEM((1, nq), F32),
                        pltpu.VMEM((DIFF_V_DIM, nq), F32)],
        compiler_params=_params(2), name="diff_decode",
    )(s["qTd"], cache_k, cache_v, s["kb"], s["vf_s"], bias_c, bias_n, lamp, g)


def _mla_decode_kernel(qT_ref, ckv_ref, kr_ref, ckvn_ref, krn_ref, wuk_ref, wuvT_ref, o_ref, s_ref,
                       *, past, new):
    nq = MLA_HEADS * new
    assert nq == 2 * LANES and new & (new - 1) == 0
    log2 = lambda n: n.bit_length() - 1

    def groups(shape, rows_per_group):
        r = lax.broadcasted_iota(jnp.int32, shape, 0)
        c = lax.broadcasted_iota(jnp.int32, shape, 1)
        return lax.shift_right_logical(r, log2(rows_per_group)) == lax.shift_right_logical(c, log2(new))

    def frame_match(shape, lane_axis):
        a = lax.broadcasted_iota(jnp.int32, shape, lane_axis)
        b = lax.broadcasted_iota(jnp.int32, shape, 1 - lane_axis)
        return jnp.where((a & (new - 1)) == b, 1.0, 0.0).astype(BF)

    nope, rope = [], []
    for h in range(MLA_HEADS):
        base = h * LANES
        o_n, o_r = (0, MLA_NOPE_DIM) if h % 2 == 0 else (MLA_NOPE_DIM, 0)
        nope.append(qT_ref[base + o_n:base + o_n + MLA_NOPE_DIM, :])
        rope.append(qT_ref[base + o_r:base + o_r + MLA_ROPE_DIM, :])
    spread = frame_match((DEC_PAD, nq), 1)
    qn = _nn(jnp.concatenate(nope, axis=0), spread)
    qn_bd = jnp.where(groups(qn.shape, MLA_NOPE_DIM), qn, 0.0).astype(BF)
    qr = _nn(jnp.concatenate(rope, axis=0), spread)
    qr_bd = jnp.where(groups(qr.shape, MLA_ROPE_DIM), qr, 0.0)
    qrT = jnp.sum(qr_bd.reshape(MLA_HEADS, MLA_ROPE_DIM, nq), axis=0).astype(BF)
    qlatT = _nn(wuk_ref[...], qn_bd).astype(BF)

    def keys(lo, n):
        if lo < past:
            return ckv_ref[lo:lo + n, :].astype(BF), kr_ref[lo:lo + n, :].astype(BF)
        return ckvn_ref[0:new, :].astype(BF), krn_ref[0:new, :].astype(BF)

    spans = [(c0, CACHE_BLK) for c0 in range(0, past, CACHE_BLK)] + [(past, new)]
    m = None
    for lo, n in spans:
        ck, kr = keys(lo, n)
        s = _nn(ck, qlatT) + _nn(kr, qrT)
        s_ref[lo:lo + n, :] = s
        c = jnp.max(s.reshape(n // 8, 8, nq), axis=0)
        m = c if m is None else jnp.maximum(m, c)
    m = jnp.max(m, axis=0, keepdims=True)
    l = acc = None
    for lo, n in spans:
        p = jnp.exp2(s_ref[lo:lo + n, :] - m)
        ps = jnp.sum(p.reshape(n // 8, 8, nq), axis=0)
        pv = _tn(keys(lo, n)[0], p.astype(BF))
        l, acc = (ps, pv) if l is None else (l + ps, acc + pv)
    o_latT = (acc * (1.0 / jnp.sum(l, axis=0, keepdims=True))).astype(BF)
    y = _nn(wuvT_ref[...], o_latT)
    y_bd = jnp.where(groups(y.shape, MLA_V_DIM), y, 0.0).astype(BF)
    o_ref[...] = _nn(y_bd, frame_match((nq, DEC_PAD), 0)).astype(BF)


def _mla_decode(s, cache_ckv, cache_kr, w, new):
    nb, past, _ = cache_ckv.shape
    per_stream = lambda rows, cols: pl.BlockSpec((None, rows, cols), lambda b: (b, 0, 0))
    return pl.pallas_call(
        functools.partial(_mla_decode_kernel, past=past, new=new),
        grid=(nb,),
        in_specs=[per_stream(MLA_HEADS * LANES, DEC_PAD), per_stream(past, MLA_KV_LORA),
                  per_stream(past, MLA_ROPE_DIM), per_stream(new, MLA_KV_LORA),
                  per_stream(new, MLA_ROPE_DIM), _const_spec(w["wuk"].shape),
                  _const_spec(w["wuvT"].shape)],
        out_specs=per_stream(D_MODEL, DEC_PAD),
        out_shape=jax.ShapeDtypeStruct((nb, D_MODEL, DEC_PAD), BF),
        scratch_shapes=[pltpu.VMEM((past + new, MLA_HEADS * new), F32)],
        compiler_params=_params(1), name="mla_decode",
    )(s["qTm"], cache_ckv, cache_kr, s["ckv_s"], s["kr_s"], w["wuk"], w["wuvT"])


def _post_kernel(x_ref, oTd_ref, oTm_ref, gmix_ref, wgT_ref, wodT_ref, womT_ref, woutT_ref,
                 gmlp_ref, wup_ref, wdown_ref, gfin_ref, y_ref, *, n_groups):
    rows = x_ref.shape[0] // n_groups
    groups = [slice(i * rows, (i + 1) * rows) for i in range(n_groups)]
    d = D_MODEL

    def norm_in(v, r):
        v["x"] = x_ref[r, :]
        v["hb"] = _rms(v["x"], gmix_ref[...]).astype(BF)

    def branches(v, r):
        v["gT"] = _nt(wgT_ref[...], v.pop("hb"))
        v["od"] = _nn(wodT_ref[...], oTd_ref[:, r])
        v["om"] = _nn(womT_ref[...], oTm_ref[:, r])

    def merge(v, r):
        g = jax.nn.sigmoid(v.pop("gT"))
        v["mT"] = (g[0:d, :] * v.pop("od") + g[d:2 * d, :] * v.pop("om")).astype(BF)

    def out_proj(v, r):
        v["aT"] = _nn(woutT_ref[...], v.pop("mT"))

    def residual(v, r):
        v["x1"] = v.pop("x") + v.pop("aT").T
        v["h2"] = _rms(v["x1"], gmlp_ref[...]).astype(BF)

    def up(v, r):
        v["u"] = _nn(v.pop("h2"), wup_ref[...])

    def act(v, r):
        u = jnp.maximum(v.pop("u"), 0.0)
        v["uu"] = (u * u).astype(BF)

    def down(v, r):
        v["x2"] = v.pop("x1") + _nn(v.pop("uu"), wdown_ref[...])

    def norm_out(v, r):
        y_ref[r, :] = _rms(v.pop("x2"), gfin_ref[...])

    values = [{} for _ in groups]
    for stage in (norm_in, branches, merge, out_proj, residual, up, act, down, norm_out):
        for v, r in zip(values, groups):
            stage(v, r)


def _post(x, oTd, oTm, w, tm, n_groups):
    nb, t, _ = x.shape
    tok = pl.BlockSpec((None, tm, D_MODEL), lambda b, i: (b, i, 0))
    feat = pl.BlockSpec((None, D_MODEL, tm), lambda b, i: (b, 0, i))
    weights = [w["gmix"], w["wgT"], w["wodT"], w["womT"], w["woutT"], w["gmlp"], w["wup"],
               w["wdown"], w["gfin"]]
    return pl.pallas_call(
        functools.partial(_post_kernel, n_groups=n_groups), grid=(nb, t // tm),
        in_specs=[tok, feat, feat] + [_const_spec(a.shape) for a in weights],
        out_specs=tok, out_shape=jax.ShapeDtypeStruct((nb, t, D_MODEL), F32),
        compiler_params=_params(2), name="post",
    )(x, oTd, oTm, *weights)


def _prep_weights(norm_mix, w_in, mla_q_norm, mla_w_uq, mla_kv_norm, mla_w_uk, mla_w_uv, w_o_diff,
                  w_o_mla, w_out, norm_mlp, w_up, w_down, norm_final):
    d = D_MODEL
    o_cq = 3 * d
    o_ckv = o_cq + MLA_Q_LORA
    o_kr = o_ckv + MLA_KV_LORA
    o_g = o_kr + MLA_ROPE_DIM
    half = MLA_ROPE_DIM // 2
    rot = lambda a: jnp.concatenate([-a[..., half:], a[..., :half]], axis=-1)
    zpad = lambda a, n: jnp.zeros(a.shape[:-1] + (n,), a.dtype)

    wkr = w_in[:, o_kr:o_g]
    place = lambda a: jnp.concatenate([a, zpad(a, 32), a, zpad(a, 32)], axis=-1)
    wlat = jnp.concatenate([w_in[:, o_cq:o_kr], place(wkr), place(rot(wkr))], axis=-1)

    nope = mla_w_uq[:, :, :MLA_NOPE_DIM]
    rope = mla_w_uq[:, :, MLA_NOPE_DIM:]

    def arrange(n, r):
        even = jnp.concatenate([n[:, 0::2], r[:, 0::2], zpad(r[:, 0::2], 32)], axis=-1)
        odd = jnp.concatenate([r[:, 1::2], zpad(r[:, 1::2], 32), n[:, 1::2]], axis=-1)
        both = jnp.stack([even, odd], axis=2)
        return both.reshape(MLA_Q_LORA, MLA_HEADS * LANES)

    row = lambda a: a.reshape(1, -1).astype(F32)
    return {
        "gmix": row(norm_mix), "gq": row(mla_q_norm), "gkv": row(mla_kv_norm),
        "gmlp": row(norm_mlp), "gfin": row(norm_final),
        "wqT": w_in[:, 0:d].T.astype(BF),
        "wk": w_in[:, d:2 * d].astype(BF),
        "wv": w_in[:, 2 * d:3 * d].astype(BF),
        "wlat": wlat.astype(BF),
        "wgT": w_in[:, o_g:].T.astype(BF),
        "wuqaT": arrange(nope, rope).T.astype(BF),
        "wuqbT": rot(rope).reshape(MLA_Q_LORA, MLA_HEADS * MLA_ROPE_DIM).T.astype(BF),
        "wuk": mla_w_uk.reshape(MLA_KV_LORA, d).astype(BF),
        "wuvT": mla_w_uv.reshape(MLA_KV_LORA, d).T.astype(BF),
        "wodT": w_o_diff.reshape(d, d).T.astype(BF),
        "womT": w_o_mla.reshape(d, d).T.astype(BF),
        "woutT": w_out.T.astype(BF),
        "wup": w_up.astype(BF),
        "wdown": w_down.astype(BF),
    }


def _rope_tables(pos):
    half = MLA_ROPE_DIM // 2
    inv = jnp.power(ROPE_THETA, -jnp.arange(half, dtype=F32) * 2.0 / MLA_ROPE_DIM)
    ang = pos.astype(F32)[:, None] * inv[None, :]
    cos2 = jnp.tile(jnp.cos(ang), (1, 2))
    sin2 = jnp.tile(jnp.sin(ang), (1, 2))
    t = pos.shape[0]
    z32 = jnp.zeros((t, 32), F32)
    place = lambda a: jnp.concatenate([a, z32, a, z32], axis=1)
    scale = (MLA_NOPE_DIM + MLA_ROPE_DIM) ** -0.5 * LOG2E
    ones = jnp.ones((t, MLA_NOPE_DIM), F32)
    z64 = jnp.zeros((t, MLA_NOPE_DIM), F32)
    ct = jnp.concatenate([ones, cos2, z32, cos2, z32, ones], axis=1) * scale
    st = jnp.concatenate([z64, sin2, z32, sin2, z32, z64], axis=1) * scale
    return {"cosp": place(cos2), "sinp": place(sin2), "ct": ct.T, "st": st.T}


def _score_tiles(rel_bias, k_pos_tiles, q_pos, shift_bucket):
    k_pos = jnp.stack(k_pos_tiles)
    rel = k_pos[:, :, None] - q_pos[None, None, :]
    mask = (k_pos // CHUNK)[:, :, None] <= (q_pos // CHUNK)[None, None, :]
    return _bias_tiles(rel_bias, _t5_bucket(rel).astype(jnp.int32), mask.astype(jnp.int32), shift_bucket)


def _decode_score_tiles(rel_bias, k_pos, q_pos, new, shift_bucket):
    lane = jnp.arange(2 * DIFF_HEADS * new, dtype=jnp.int32)
    qp = q_pos[lane % new]
    rel = k_pos[:, None] - qp[None, :]
    mask = (k_pos // CHUNK)[:, None] <= (qp // CHUNK)[None, :]
    tab = rel_bias[:, (lane // new) % DIFF_HEADS].astype(F32)
    return _bias_lanes(tab, _t5_bucket(rel).astype(jnp.int32), mask.astype(jnp.int32), shift_bucket)


def kernel(x_prompt, x_sample, cache_diff_k, cache_diff_v, cache_mla_ckv, cache_mla_krope, rel_bias,
           norm_mix, w_in, lam_q1, lam_k1, lam_q2, lam_k2, diff_subln, mla_q_norm, mla_w_uq, mla_kv_norm,
           mla_w_uk, mla_w_uv, w_o_diff, w_o_mla, w_out, norm_mlp, w_up, w_down, norm_final):
    assert norm_mix.shape[0] == 1, "single-layer model"
    nb, t, _ = x_prompt.shape
    ns, new, _ = x_sample.shape
    past = cache_diff_k.shape[2]
    assert t % TQ == 0 and past % CACHE_BLK == 0 and CACHE_BLK >= TK and new <= DEC_PAD and new % 16 == 0
    assert past % CHUNK == 0 and new <= CHUNK

    w = _prep_weights(norm_mix[0], w_in[0], mla_q_norm[0], mla_w_uq[0], mla_kv_norm[0], mla_w_uk[0],
                      mla_w_uv[0], w_o_diff[0], w_o_mla[0], w_out[0], norm_mlp[0], w_up[0], w_down[0],
                      norm_final)
    lamp = jnp.stack([lam_q1[0], lam_k1[0], lam_q2[0], lam_k2[0]]).astype(F32)
    g_sub = diff_subln[0].reshape(DIFF_V_DIM, 1).astype(F32)

    pos_p = jnp.arange(t, dtype=jnp.int32)
    p = _project(x_prompt, w, _rope_tables(pos_p), tm=256)
    blk = jnp.arange(TQ, dtype=jnp.int32)
    tiles_p = _score_tiles(rel_bias, [TQ + blk, blk], TQ + blk, _far_bucket())
    mask_p = jnp.where((blk // CHUNK)[:, None] <= (blk // CHUNK)[None, :], 0.0, NEG_INF).astype(F32)
    oTd = _diff_prompt(p, tiles_p, lamp, g_sub)
    oTm = _mla_prompt(p, mask_p)
    y_prompt = _post(x_prompt, oTd, oTm, w, tm=512, n_groups=2)

    toks = ns * new
    xs = x_sample.reshape(1, toks, D_MODEL)
    pos_s = past + jnp.arange(DEC_PAD, dtype=jnp.int32)
    s = _project(xs, w, _rope_tables(jnp.tile(pos_s[:new], ns)), tm=toks)
    per_stream_q = lambda a: jnp.pad(a[0].reshape(-1, ns, new).transpose(1, 0, 2),
                                     ((0, 0), (0, 0), (0, DEC_PAD - new)))
    s.update(qTd=per_stream_q(s["qTd"]), qTm=per_stream_q(s["qTm"]),
             kb=s["kb"].reshape(ns, new, D_MODEL), ckv_s=s["ckv"].reshape(ns, new, MLA_KV_LORA),
             kr_s=s["kr"].reshape(ns, new, MLA_ROPE_DIM),
             vf_s=s["vf"].reshape(ns, new * DIFF_HEADS, DIFF_V_DIM))
    bias_c = _decode_score_tiles(rel_bias, jnp.arange(past - CACHE_BLK, past, dtype=jnp.int32), pos_s, new,
                                 _far_bucket())
    bias_n = _decode_score_tiles(rel_bias, pos_s[:new], pos_s, new, _far_bucket())
    oTd_s = _diff_decode(s, cache_diff_k[0], cache_diff_v[0], bias_c, bias_n, lamp, g_sub, new)
    oTm_s = _mla_decode(s, cache_mla_ckv[0], cache_mla_krope[0], w, new)
    token_major = lambda a: a[:, :, :new].transpose(1, 0, 2).reshape(1, D_MODEL, toks)
    y_sample = _post(xs, token_major(oTd_s), token_major(oTm_s), w, tm=toks, n_groups=1)

    heads = lambda a, lead: a.reshape((1,) + lead + (DIFF_HEADS, DIFF_V_DIM))
    return (y_prompt, y_sample.reshape(ns, new, D_MODEL),
            heads(p["kf"], (nb, t)), heads(p["vf"], (nb, t)), p["ckv"][None], p["kr"][None],
            heads(s["kf"], (ns, new)), heads(s["vf"], (ns, new)),
            s["ckv"].reshape(1, ns, new, MLA_KV_LORA), s["kr"].reshape(1, ns, new, MLA_ROPE_DIM))
```

```python
import functools
import math

import numpy as np
import jax
import jax.numpy as jnp
from jax import lax
from jax.experimental import pallas as pl
from jax.experimental.pallas import tpu as pltpu

D_MODEL = 1024
CHUNK = 64
DIFF_HEADS = 8
DIFF_HEAD_DIM = 64
DIFF_V_DIM = 128
MLA_HEADS = 16
MLA_NOPE_DIM = 64
MLA_ROPE_DIM = 32
MLA_V_DIM = 64
MLA_Q_LORA = 256
MLA_KV_LORA = 256
D_FF = 4 * D_MODEL
NUM_BUCKETS = 32
MAX_DISTANCE = 128
ROPE_THETA = 10000.0
EPS = 1e-6
NEG_INF = -1e30
LAMBDA_INIT = 0.8 - 0.6 * math.exp(-0.3 * 0)

BF = jnp.bfloat16
F32 = jnp.float32

LANES = 128
TQ = 256
TK = 256
TR = 256
DEC_PAD = LANES
CACHE_BLK = 512
DIFF_CACHE_BLK = 1024
VMEM_LIMIT = 56 * 1024 * 1024
LOG2E = math.log2(math.e)
ONES_ROWS = 16


def _nn(a, b):
    return jnp.dot(a, b, preferred_element_type=F32)


def _nt(a, b):
    return lax.dot_general(a, b, (((1,), (1,)), ((), ())), preferred_element_type=F32)


def _tn(a, b):
    return lax.dot_general(a, b, (((0,), (0,)), ((), ())), preferred_element_type=F32)


def _rms(x, g):
    return x * lax.rsqrt(jnp.mean(x * x, axis=-1, keepdims=True) + EPS) * g


def _const_spec(shape):
    zeros = (0,) * len(shape)
    return pl.BlockSpec(shape, lambda *_: zeros, pipeline_mode=pl.Buffered(1))


def _params(n_axes):
    return pltpu.CompilerParams(dimension_semantics=("arbitrary",) * n_axes,
                                vmem_limit_bytes=VMEM_LIMIT)


def _bias_kernel(tab_ref, bkt_ref, msk_ref, o_ref, *, shift_bucket):
    h = pl.program_id(0)
    bkt = bkt_ref[...]
    acc = jnp.zeros(bkt.shape, F32)
    for b in range(NUM_BUCKETS):
        acc = jnp.where(bkt == b, tab_ref[b, h], acc)
    if shift_bucket is not None:
        acc = acc - tab_ref[shift_bucket, h]
    o_ref[...] = jnp.where(msk_ref[...] != 0, acc * LOG2E, NEG_INF)


def _bias_tiles(rel_bias, bucket, mask, shift_bucket):
    n, r, c = bucket.shape
    return pl.pallas_call(
        functools.partial(_bias_kernel, shift_bucket=shift_bucket),
        grid=(DIFF_HEADS,),
        in_specs=[pl.BlockSpec(memory_space=pltpu.SMEM),
                  pl.BlockSpec((n, r, c), lambda h: (0, 0, 0)),
                  pl.BlockSpec((n, r, c), lambda h: (0, 0, 0))],
        out_specs=pl.BlockSpec((None, n, r, c), lambda h: (h, 0, 0, 0)),
        out_shape=jax.ShapeDtypeStruct((DIFF_HEADS, n, r, c), F32),
        compiler_params=_params(1),
        name="bias_tiles",
    )(rel_bias, bucket, mask)


def _bias_lanes_kernel(tab_ref, bkt_ref, msk_ref, o_ref, *, shift_bucket):
    bkt = bkt_ref[...]
    acc = jnp.zeros(bkt.shape, F32)
    for b in range(NUM_BUCKETS):
        acc = jnp.where(bkt == b, tab_ref[b:b + 1, :], acc)
    if shift_bucket is not None:
        acc = acc - tab_ref[shift_bucket:shift_bucket + 1, :]
    o_ref[...] = jnp.where(msk_ref[...] != 0, acc * LOG2E, NEG_INF)


def _bias_lanes(tab, bucket, mask, shift_bucket):
    return pl.pallas_call(
        functools.partial(_bias_lanes_kernel, shift_bucket=shift_bucket),
        out_shape=jax.ShapeDtypeStruct(bucket.shape, F32), name="bias_lanes",
    )(tab, bucket, mask)


def _t5_bucket(rel):
    half = NUM_BUCKETS // 2
    max_exact = half // 2
    n = jnp.abs(rel)
    nf = jnp.maximum(n, max_exact).astype(F32)
    large = max_exact + (jnp.log(nf / max_exact) / math.log(MAX_DISTANCE / max_exact)
                         * (half - max_exact)).astype(jnp.int32)
    large = jnp.minimum(large, half - 1)
    return jnp.where(rel > 0, half, 0) + jnp.where(n < max_exact, n, large)


def _far_bucket():
    half = NUM_BUCKETS // 2
    max_exact = half // 2
    assert max_exact + math.log(TK / max_exact) / math.log(MAX_DISTANCE / max_exact) * (half - max_exact) >= half
    return half - 1


def _proj_kernel(x_ref, gmix_ref, wqT_ref, wk_ref, wv_ref, wlat_ref, gq_ref, gkv_ref,
                 wuqaT_ref, wuqbT_ref, wuk_ref, wuvT_ref, cosp_ref, sinp_ref, ct_ref, st_ref,
                 qTd_ref, kf_ref, kb_ref, vf_ref, vb_ref, qTm_ref, ckv_ref, kn_ref, vTm_ref,
                 r_ref, kr_ref):
    hb = _rms(x_ref[...], gmix_ref[...]).astype(BF)
    lat = _nn(hb, wlat_ref[...])
    qTd_ref[...] = (_nt(wqT_ref[...], hb) * (DIFF_HEAD_DIM ** -0.5 * LOG2E)).astype(BF)
    k = _nn(hb, wk_ref[...])
    kb_ref[...] = k.astype(BF)
    v = _nn(hb, wv_ref[...])
    vb_ref[...] = v.astype(BF)
    tm = k.shape[0]
    for h in range(DIFF_HEADS):
        rows = pl.ds(h, tm, stride=DIFF_HEADS)
        kf_ref[rows, :] = k[:, h * DIFF_V_DIM:(h + 1) * DIFF_V_DIM]
        vf_ref[rows, :] = v[:, h * DIFF_V_DIM:(h + 1) * DIFF_V_DIM]

    cq = _rms(lat[:, 0:MLA_Q_LORA], gq_ref[...]).astype(BF)
    qa = _nt(wuqaT_ref[...], cq)
    qb = _nt(wuqbT_ref[...], cq)
    ct = ct_ref[...]
    st = st_ref[...]
    pair = 2 * LANES
    rd = MLA_ROPE_DIM
    for p in range(MLA_HEADS // 2):
        sl = slice(p * pair, (p + 1) * pair)
        z = jnp.zeros((MLA_NOPE_DIM, tm), F32)
        partner = jnp.concatenate([z, qb[2 * p * rd:(2 * p + 1) * rd, :], z[0:rd, :],
                                   qb[(2 * p + 1) * rd:(2 * p + 2) * rd, :], z[0:rd, :], z], axis=0)
        qTm_ref[sl, :] = (qa[sl, :] * ct + partner * st).astype(BF)

    ckv = _rms(lat[:, MLA_Q_LORA:MLA_Q_LORA + MLA_KV_LORA], gkv_ref[...])
    ckv_ref[...] = ckv
    cb = ckv.astype(BF)
    kn_ref[...] = _nn(cb, wuk_ref[...]).astype(BF)
    vTm_ref[...] = _nt(wuvT_ref[...], cb).astype(BF)

    o = MLA_Q_LORA + MLA_KV_LORA
    r = lat[:, o:o + LANES] * cosp_ref[...] + lat[:, o + LANES:o + 2 * LANES] * sinp_ref[...]
    r_ref[...] = r.astype(BF)
    kr_ref[...] = r[:, 0:MLA_ROPE_DIM]


def _project(x, w, tabs, tm):
    nb, t, _ = x.shape
    nt = t // tm
    tok = lambda width: pl.BlockSpec((None, tm, width), lambda b, i: (b, i, 0))
    feat = lambda rows: pl.BlockSpec((None, rows, tm), lambda b, i: (b, 0, i))
    sds = jax.ShapeDtypeStruct
    weights = [w["gmix"], w["wqT"], w["wk"], w["wv"], w["wlat"], w["gq"], w["gkv"],
               w["wuqaT"], w["wuqbT"], w["wuk"], w["wuvT"]]
    in_specs = ([tok(D_MODEL)] + [_const_spec(a.shape) for a in weights]
                + [pl.BlockSpec((tm, LANES), lambda b, i: (i, 0)),
                   pl.BlockSpec((tm, LANES), lambda b, i: (i, 0)),
                   pl.BlockSpec((2 * LANES, tm), lambda b, i: (0, i)),
                   pl.BlockSpec((2 * LANES, tm), lambda b, i: (0, i))])
    head_rows = pl.BlockSpec((None, tm * DIFF_HEADS, DIFF_V_DIM), lambda b, i: (b, i, 0))
    out_shape = [sds((nb, D_MODEL, t), BF), sds((nb, t * DIFF_HEADS, DIFF_V_DIM), F32), sds((nb, t, D_MODEL), BF),
                 sds((nb, t * DIFF_HEADS, DIFF_V_DIM), F32), sds((nb, t, D_MODEL), BF),
                 sds((nb, MLA_HEADS * LANES, t), BF), sds((nb, t, MLA_KV_LORA), F32),
                 sds((nb, t, D_MODEL), BF), sds((nb, D_MODEL, t), BF),
                 sds((nb, t, LANES), BF), sds((nb, t, MLA_ROPE_DIM), F32)]
    out_specs = [feat(D_MODEL), head_rows, tok(D_MODEL), head_rows, tok(D_MODEL),
                 feat(MLA_HEADS * LANES), tok(MLA_KV_LORA), tok(D_MODEL), feat(D_MODEL),
                 tok(LANES), tok(MLA_ROPE_DIM)]
    outs = pl.pallas_call(
        _proj_kernel, grid=(nb, nt), in_specs=in_specs, out_specs=out_specs, out_shape=out_shape,
        compiler_params=_params(2), name="project",
    )(x, *weights, tabs["cosp"], tabs["sinp"], tabs["ct"], tabs["st"])
    names = ["qTd", "kf", "kb", "vf", "vb", "qTm", "ckv", "kn", "vTm", "r", "kr"]
    return dict(zip(names, outs))


def _split_maps(qT):
    rows = lax.broadcasted_iota(jnp.int32, qT.shape, 0)
    qf = qT.astype(F32)
    zero = jnp.zeros_like(qf)
    return (jnp.where(rows < DIFF_HEAD_DIM, qf, zero).astype(BF),
            jnp.where(rows >= DIFF_HEAD_DIM, qf, zero).astype(BF))


def _lambda(lam_ref):
    lp = lam_ref[...]
    a = jnp.sum(lp[0:1, :] * lp[1:2, :], axis=1, keepdims=True)
    b = jnp.sum(lp[2:3, :] * lp[3:4, :], axis=1, keepdims=True)
    return jnp.exp(a) - jnp.exp(b) + LAMBDA_INIT


def _diff_finish(o1, o2, lam, g):
    o = o1 - lam * o2
    ms = jnp.mean(o * o, axis=0, keepdims=True)
    return (o * lax.rsqrt(ms + EPS) * g * (1.0 - LAMBDA_INIT)).astype(BF)


def _mla_keys(kp, r):
    lane = lax.broadcasted_iota(jnp.int32, kp.shape, 1)
    kpf = kp.astype(F32)
    rf = r.astype(F32)
    return (jnp.where(lane < MLA_NOPE_DIM, kpf, rf).astype(BF),
            jnp.where(lane >= MLA_NOPE_DIM, kpf, rf).astype(BF))


def _qslice(qb):
    return slice(qb * TQ, (qb + 1) * TQ)


def _kslice(kb):
    return slice(kb * TR, (kb + 1) * TR)


def _with_ones(va_ref, vT):
    dv, t = vT.shape
    va_ref[0:dv, :] = vT
    va_ref[dv:dv + ONES_ROWS, :] = jnp.ones((ONES_ROWS, t), BF)


def _weighted_mean(acc):
    dv = acc.shape[0] - ONES_ROWS
    return acc[0:dv, :] * (1.0 / acc[dv:dv + 1, :])


def _causal_two_pass(nq, prep, scores, va_refs, s_ref, p_ref, emit):
    n_maps = len(va_refs)

    def score_pass(qb):
        ops = prep(qb)
        m8 = [None] * n_maps
        for kb in range((qb + 1) * TK // TR):
            for i, s in enumerate(scores(ops, qb, kb)):
                s_ref[qb % 2, i, _kslice(kb), :] = s
                c = jnp.max(s.reshape(TR // 8, 8, TQ), axis=0)
                m8[i] = c if m8[i] is None else jnp.maximum(m8[i], c)
        return [jnp.max(m, axis=0, keepdims=True) for m in m8]

    def exp_pass(qb, ms):
        for kb in range((qb + 1) * TK // TR):
            for i in range(n_maps):
                p_ref[qb % 2, i, _kslice(kb), :] = jnp.exp2(s_ref[qb % 2, i, _kslice(kb), :] - ms[i]).astype(BF)

    def value_pass(qb):
        klen = (qb + 1) * TK
        emit(qb, [_nn(va_refs[i][:, 0:klen], p_ref[qb % 2, i, 0:klen, :]) for i in range(n_maps)])

    ms = score_pass(0)
    for qb in range(nq + 1):
        ms_next = score_pass(qb + 1) if qb + 1 < nq else None
        if qb < nq:
            exp_pass(qb, ms)
        if qb >= 1:
            value_pass(qb - 1)
        ms = ms_next


def _diff_prompt_kernel(qT_ref, k_ref, v_ref, bias_ref, lam_ref, g_ref, o_ref, va_ref, s_ref, p_ref, *, t):
    lam = _lambda(lam_ref)
    g = g_ref[...]
    _with_ones(va_ref, v_ref[...].astype(F32).T.astype(BF))

    def prep(qb):
        return _split_maps(qT_ref[:, _qslice(qb)])

    def scores(ops, qb, kb):
        kblk = k_ref[_kslice(kb), :]
        out = [_nn(kblk, q) for q in ops]
        g, r0 = divmod(kb * TR, TK)
        if g >= qb - 1:
            out = [s + bias_ref[qb - g, r0:r0 + TR, :] for s in out]
        return out

    def emit(qb, accs):
        o_ref[:, _qslice(qb)] = _diff_finish(_weighted_mean(accs[0]), _weighted_mean(accs[1]), lam, g)

    _causal_two_pass(t // TQ, prep, scores, [va_ref, va_ref], s_ref, p_ref, emit)


def _score_scratch(n_maps, t):
    return [pltpu.VMEM((2, n_maps, t, TQ), F32), pltpu.VMEM((2, n_maps, t, TQ), BF)]


def _diff_prompt(p, bias, lamp, g):
    nb, _, t = p["qTd"].shape
    head_feat = pl.BlockSpec((None, DIFF_V_DIM, t), lambda b, h: (b, h, 0))
    head_tok = pl.BlockSpec((None, t, DIFF_V_DIM), lambda b, h: (b, 0, h))
    return pl.pallas_call(
        functools.partial(_diff_prompt_kernel, t=t),
        grid=(nb, DIFF_HEADS),
        in_specs=[head_feat, head_tok, head_tok,
                  pl.BlockSpec((None, 2, TK, TQ), lambda b, h: (h, 0, 0, 0)),
                  _const_spec(lamp.shape), _const_spec(g.shape)],
        out_specs=head_feat,
        out_shape=jax.ShapeDtypeStruct((nb, D_MODEL, t), BF),
        scratch_shapes=[pltpu.VMEM((DIFF_V_DIM + ONES_ROWS, t), BF)] + _score_scratch(2, t),
        compiler_params=_params(2), name="diff_prompt",
    )(p["qTd"], p["kb"], p["vb"], bias, lamp, g)


def _mla_prompt_kernel(qT_ref, kp_ref, r_ref, vT_ref, mask_ref, o_ref, ke_ref, ko_ref, vae_ref, vao_ref,
                       s_ref, p_ref, *, t):
    ke, ko = _mla_keys(kp_ref[...], r_ref[...])
    ke_ref[...] = ke
    ko_ref[...] = ko
    _with_ones(vae_ref, vT_ref[0:MLA_V_DIM, :])
    _with_ones(vao_ref, vT_ref[MLA_V_DIM:2 * MLA_V_DIM, :])

    def prep(qb):
        return qT_ref[0:LANES, _qslice(qb)], qT_ref[LANES:2 * LANES, _qslice(qb)]

    def scores(ops, qb, kb):
        out = [_nn(ke_ref[_kslice(kb), :], ops[0]), _nn(ko_ref[_kslice(kb), :], ops[1])]
        g, r0 = divmod(kb * TR, TK)
        if g == qb:
            out = [s + mask_ref[r0:r0 + TR, :] for s in out]
        return out

    def emit(qb, accs):
        o_ref[0:MLA_V_DIM, _qslice(qb)] = _weighted_mean(accs[0]).astype(BF)
        o_ref[MLA_V_DIM:2 * MLA_V_DIM, _qslice(qb)] = _weighted_mean(accs[1]).astype(BF)

    _causal_two_pass(t // TQ, prep, scores, [vae_ref, vao_ref], s_ref, p_ref, emit)


def _mla_prompt(p, mask):
    nb, _, t = p["qTm"].shape
    return pl.pallas_call(
        functools.partial(_mla_prompt_kernel, t=t),
        grid=(nb, MLA_HEADS // 2),
        in_specs=[pl.BlockSpec((None, 2 * LANES, t), lambda b, h: (b, h, 0)),
                  pl.BlockSpec((None, t, LANES), lambda b, h: (b, 0, h)),
                  pl.BlockSpec((None, t, LANES), lambda b, h: (b, 0, 0)),
                  pl.BlockSpec((None, 2 * MLA_V_DIM, t), lambda b, h: (b, h, 0)),
                  _const_spec(mask.shape)],
        out_specs=pl.BlockSpec((None, 2 * MLA_V_DIM, t), lambda b, h: (b, h, 0)),
        out_shape=jax.ShapeDtypeStruct((nb, D_MODEL, t), BF),
        scratch_shapes=[pltpu.VMEM((t, LANES), BF), pltpu.VMEM((t, LANES), BF),
                        pltpu.VMEM((MLA_V_DIM + ONES_ROWS, t), BF),
                        pltpu.VMEM((MLA_V_DIM + ONES_ROWS, t), BF)] + _score_scratch(2, t),
        compiler_params=_params(2), name="mla_prompt",
    )(p["qTm"], p["kn"], p["r"], p["vTm"], mask)


def _diff_decode_kernel(qT_ref, ck_ref, cv_ref, kn_ref, vn_ref, bc_ref, bn_ref, lam_ref, g_ref, o_ref,
                        qbd_ref, m_ref, l_ref, acc_ref, *, new, blk):
    nq = 2 * DIFF_HEADS * new
    assert nq == 2 * LANES and new & (new - 1) == 0
    c = pl.program_id(1)
    last = pl.num_programs(1) - 1
    log2 = lambda n: n.bit_length() - 1

    def lane_head(shape):
        lane = lax.broadcasted_iota(jnp.int32, shape, 1)
        return lax.shift_right_logical(lane, log2(new)) & (DIFF_HEADS - 1)

    @pl.when(c == 0)
    def _():
        m_ref[...] = jnp.full(m_ref.shape, NEG_INF, F32)
        l_ref[...] = jnp.zeros(l_ref.shape, F32)
        acc_ref[...] = jnp.zeros(acc_ref.shape, F32)
        shape = (2 * DIFF_HEAD_DIM, nq)
        row = lax.broadcasted_iota(jnp.int32, shape, 0)
        lane = lax.broadcasted_iota(jnp.int32, shape, 1)
        frame = lax.broadcasted_iota(jnp.int32, (DEC_PAD, nq), 0)
        spread = jnp.where((lax.broadcasted_iota(jnp.int32, (DEC_PAD, nq), 1) & (new - 1)) == frame,
                           1.0, 0.0).astype(BF)
        same_map = lax.shift_right_logical(row, log2(DIFF_HEAD_DIM)) == lax.shift_right_logical(lane, log2(LANES))
        for h in range(DIFF_HEADS):
            rep = _nn(qT_ref[h * DIFF_V_DIM:(h + 1) * DIFF_V_DIM, :], spread)
            qbd_ref[h * DIFF_V_DIM:(h + 1) * DIFF_V_DIM, :] = jnp.where(
                same_map & (lane_head(shape) == h), rep, 0.0).astype(BF)

    def update(keys, vals, bias):
        heads = range(DIFF_HEADS)
        s = _nn(jnp.concatenate([keys(h) for h in heads], axis=1), qbd_ref[...])
        if bias is not None:
            s = s + bias
        m_new = jnp.maximum(m_ref[...], jnp.max(s, axis=0, keepdims=True))
        alpha = jnp.exp2(m_ref[...] - m_new)
        p = jnp.exp2(s - m_new)
        l_ref[...] = alpha * l_ref[...] + jnp.sum(p, axis=0, keepdims=True)
        m_ref[...] = m_new
        pb = p.astype(BF)
        head = lane_head((DIFF_V_DIM, nq))
        full = _tn(jnp.concatenate([vals(h) for h in heads], axis=1), pb)
        pv = jnp.zeros((DIFF_V_DIM, nq), F32)
        for h in heads:
            pv = jnp.where(head == h, full[h * DIFF_V_DIM:(h + 1) * DIFF_V_DIM, :], pv)
        acc_ref[...] = alpha * acc_ref[...] + pv

    def cache_rows(ref):
        return lambda h: ref[pl.ds(h, blk, stride=DIFF_HEADS), :].astype(BF)

    @pl.when(c < last)
    def _():
        update(cache_rows(ck_ref), cache_rows(cv_ref), None)

    @pl.when(c == last)
    def _():
        update(cache_rows(ck_ref), cache_rows(cv_ref), bc_ref[...])
        hs = lambda h: slice(h * DIFF_V_DIM, (h + 1) * DIFF_V_DIM)
        update(lambda h: kn_ref[0:new, hs(h)],
               lambda h: vn_ref[pl.ds(h, new, stride=DIFF_HEADS), :].astype(BF), bn_ref[...])
        o = acc_ref[...] * (1.0 / l_ref[...])
        y = _diff_finish(o[:, 0:LANES], o[:, LANES:nq], _lambda(lam_ref), g_ref[...])
        r = lax.broadcasted_iota(jnp.int32, (LANES, DEC_PAD), 0)
        col = lax.broadcasted_iota(jnp.int32, (LANES, DEC_PAD), 1)
        for h in range(DIFF_HEADS):
            fold = jnp.where(r == col + h * new, 1.0, 0.0).astype(BF)
            o_ref[hs(h), :] = _nn(y, jnp.where(col < new, fold, jnp.zeros_like(fold))).astype(BF)


def _diff_decode(s, cache_k, cache_v, bias_c, bias_n, lamp, g, new, blk):
    nb, past = cache_k.shape[:2]
    nq = 2 * DIFF_HEADS * new
    cache_k, cache_v = (a.reshape(nb, past * DIFF_HEADS, DIFF_V_DIM) for a in (cache_k, cache_v))
    stream = lambda rows: pl.BlockSpec((None, rows, D_MODEL), lambda b, c: (b, 0, 0))
    chunk = pl.BlockSpec((None, blk * DIFF_HEADS, DIFF_V_DIM), lambda b, c: (b, c, 0))
    return pl.pallas_call(
        functools.partial(_diff_decode_kernel, new=new, blk=blk),
        grid=(nb, past // blk),
        in_specs=[pl.BlockSpec((None, D_MODEL, DEC_PAD), lambda b, c: (b, 0, 0)), chunk, chunk,
                  stream(new),
                  pl.BlockSpec((None, new * DIFF_HEADS, DIFF_V_DIM), lambda b, c: (b, 0, 0)),
                  _const_spec(bias_c.shape), _const_spec(bias_n.shape),
                  _const_spec(lamp.shape), _const_spec(g.shape)],
        out_specs=pl.BlockSpec((None, D_MODEL, DEC_PAD), lambda b, c: (b, 0, 0)),
        out_shape=jax.ShapeDtypeStruct((nb, D_MODEL, DEC_PAD), BF),
        scratch_shapes=[pltpu.VMEM((D_MODEL, nq), BF),
                        pltpu.VMEM((1, nq), F32), pltpu.VMEM((1, nq), F32),
                        pltpu.VMEM((DIFF_V_DIM, nq), F32)],
        compiler_params=_params(2), name="diff_decode",
    )(s["qTd"], cache_k, cache_v, s["kb"], s["vf_s"], bias_c, bias_n, lamp, g)


def _mla_decode_kernel(qT_ref, ckv_ref, kr_ref, ckvn_ref, krn_ref, wuk_ref, wuvT_ref, o_ref, s_ref,
                       *, past, new):
    nq = MLA_HEADS * new
    assert nq == 2 * LANES and new & (new - 1) == 0
    log2 = lambda n: n.bit_length() - 1

    def groups(shape, rows_per_group):
        r = lax.broadcasted_iota(jnp.int32, shape, 0)
        c = lax.broadcasted_iota(jnp.int32, shape, 1)
        return lax.shift_right_logical(r, log2(rows_per_group)) == lax.shift_right_logical(c, log2(new))

    def frame_match(shape, lane_axis):
        a = lax.broadcasted_iota(jnp.int32, shape, lane_axis)
        b = lax.broadcasted_iota(jnp.int32, shape, 1 - lane_axis)
        return jnp.where((a & (new - 1)) == b, 1.0, 0.0).astype(BF)

    nope, rope = [], []
    for h in range(MLA_HEADS):
        base = h * LANES
        o_n, o_r = (0, MLA_NOPE_DIM) if h % 2 == 0 else (MLA_NOPE_DIM, 0)
        nope.append(qT_ref[base + o_n:base + o_n + MLA_NOPE_DIM, :])
        rope.append(qT_ref[base + o_r:base + o_r + MLA_ROPE_DIM, :])
    spread = frame_match((DEC_PAD, nq), 1)
    qn = _nn(jnp.concatenate(nope, axis=0), spread)
    qn_bd = jnp.where(groups(qn.shape, MLA_NOPE_DIM), qn, 0.0).astype(BF)
    qr = _nn(jnp.concatenate(rope, axis=0), spread)
    qr_bd = jnp.where(groups(qr.shape, MLA_ROPE_DIM), qr, 0.0)
    qrT = jnp.sum(qr_bd.reshape(MLA_HEADS, MLA_ROPE_DIM, nq), axis=0).astype(BF)
    qlatT = _nn(wuk_ref[...], qn_bd).astype(BF)

    def keys(lo, n):
        if lo < past:
            return ckv_ref[lo:lo + n, :].astype(BF), kr_ref[lo:lo + n, :].astype(BF)
        return ckvn_ref[0:new, :].astype(BF), krn_ref[0:new, :].astype(BF)

    spans = [(c0, CACHE_BLK) for c0 in range(0, past, CACHE_BLK)] + [(past, new)]
    m = None
    for lo, n in spans:
        ck, kr = keys(lo, n)
        s = _nn(ck, qlatT) + _nn(kr, qrT)
        s_ref[lo:lo + n, :] = s
        c = jnp.max(s.reshape(n // 8, 8, nq), axis=0)
        m = c if m is None else jnp.maximum(m, c)
    m = jnp.max(m, axis=0, keepdims=True)
    l = acc = None
    for lo, n in spans:
        p = jnp.exp2(s_ref[lo:lo + n, :] - m)
        ps = jnp.sum(p.reshape(n // 8, 8, nq), axis=0)
        pv = _tn(keys(lo, n)[0], p.astype(BF))
        l, acc = (ps, pv) if l is None else (l + ps, acc + pv)
    o_latT = (acc * (1.0 / jnp.sum(l, axis=0, keepdims=True))).astype(BF)
    y = _nn(wuvT_ref[...], o_latT)
    y_bd = jnp.where(groups(y.shape, MLA_V_DIM), y, 0.0).astype(BF)
    o_ref[...] = _nn(y_bd, frame_match((nq, DEC_PAD), 0)).astype(BF)


def _mla_decode(s, cache_ckv, cache_kr, w, new):
    nb, past, _ = cache_ckv.shape
    per_stream = lambda rows, cols: pl.BlockSpec((None, rows, cols), lambda b: (b, 0, 0))
    return pl.pallas_call(
        functools.partial(_mla_decode_kernel, past=past, new=new),
        grid=(nb,),
        in_specs=[per_stream(MLA_HEADS * LANES, DEC_PAD), per_stream(past, MLA_KV_LORA),
                  per_stream(past, MLA_ROPE_DIM), per_stream(new, MLA_KV_LORA),
                  per_stream(new, MLA_ROPE_DIM), _const_spec(w["wuk"].shape),
                  _const_spec(w["wuvT"].shape)],
        out_specs=per_stream(D_MODEL, DEC_PAD),
        out_shape=jax.ShapeDtypeStruct((nb, D_MODEL, DEC_PAD), BF),
        scratch_shapes=[pltpu.VMEM((past + new, MLA_HEADS * new), F32)],
        compiler_params=_params(1), name="mla_decode",
    )(s["qTm"], cache_ckv, cache_kr, s["ckv_s"], s["kr_s"], w["wuk"], w["wuvT"])


def _post_kernel(x_ref, oTd_ref, oTm_ref, gmix_ref, wgT_ref, wodT_ref, womT_ref, woutT_ref,
                 gmlp_ref, wup_ref, wdown_ref, gfin_ref, y_ref, *, n_groups):
    rows = x_ref.shape[0] // n_groups
    groups = [slice(i * rows, (i + 1) * rows) for i in range(n_groups)]
    d = D_MODEL

    def norm_in(v, r):
        v["x"] = x_ref[r, :]
        v["hb"] = _rms(v["x"], gmix_ref[...]).astype(BF)

    def branches(v, r):
        v["gT"] = _nt(wgT_ref[...], v.pop("hb"))
        v["od"] = _nn(wodT_ref[...], oTd_ref[:, r])
        v["om"] = _nn(womT_ref[...], oTm_ref[:, r])

    def merge(v, r):
        g = jax.nn.sigmoid(v.pop("gT"))
        v["mT"] = (g[0:d, :] * v.pop("od") + g[d:2 * d, :] * v.pop("om")).astype(BF)

    def out_proj(v, r):
        v["aT"] = _nn(woutT_ref[...], v.pop("mT"))

    def residual(v, r):
        v["x1"] = v.pop("x") + v.pop("aT").T
        v["h2"] = _rms(v["x1"], gmlp_ref[...]).astype(BF)

    def up(v, r):
        v["u"] = _nn(v.pop("h2"), wup_ref[...])

    def act(v, r):
        u = jnp.maximum(v.pop("u"), 0.0)
        v["uu"] = (u * u).astype(BF)

    def down(v, r):
        v["x2"] = v.pop("x1") + _nn(v.pop("uu"), wdown_ref[...])

    def norm_out(v, r):
        y_ref[r, :] = _rms(v.pop("x2"), gfin_ref[...])

    values = [{} for _ in groups]
    for stage in (norm_in, branches, merge, out_proj, residual, up, act, down, norm_out):
        for v, r in zip(values, groups):
            stage(v, r)


def _post(x, oTd, oTm, w, tm, n_groups):
    nb, t, _ = x.shape
    tok = pl.BlockSpec((None, tm, D_MODEL), lambda b, i: (b, i, 0))
    feat = pl.BlockSpec((None, D_MODEL, tm), lambda b, i: (b, 0, i))
    weights = [w["gmix"], w["wgT"], w["wodT"], w["womT"], w["woutT"], w["gmlp"], w["wup"],
               w["wdown"], w["gfin"]]
    return pl.pallas_call(
        functools.partial(_post_kernel, n_groups=n_groups), grid=(nb, t // tm),
        in_specs=[tok, feat, feat] + [_const_spec(a.shape) for a in weights],
        out_specs=tok, out_shape=jax.ShapeDtypeStruct((nb, t, D_MODEL), F32),
        compiler_params=_params(2), name="post",
    )(x, oTd, oTm, *weights)


def _prep_weights(norm_mix, w_in, mla_q_norm, mla_w_uq, mla_kv_norm, mla_w_uk, mla_w_uv, w_o_diff,
                  w_o_mla, w_out, norm_mlp, w_up, w_down, norm_final):
    d = D_MODEL
    o_cq = 3 * d
    o_ckv = o_cq + MLA_Q_LORA
    o_kr = o_ckv + MLA_KV_LORA
    o_g = o_kr + MLA_ROPE_DIM
    half = MLA_ROPE_DIM // 2
    rot = lambda a: jnp.concatenate([-a[..., half:], a[..., :half]], axis=-1)
    zpad = lambda a, n: jnp.zeros(a.shape[:-1] + (n,), a.dtype)

    wkr = w_in[:, o_kr:o_g]
    place = lambda a: jnp.concatenate([a, zpad(a, 32), a, zpad(a, 32)], axis=-1)
    wlat = jnp.concatenate([w_in[:, o_cq:o_kr], place(wkr), place(rot(wkr))], axis=-1)

    nope = mla_w_uq[:, :, :MLA_NOPE_DIM]
    rope = mla_w_uq[:, :, MLA_NOPE_DIM:]

    def arrange(n, r):
        even = jnp.concatenate([n[:, 0::2], r[:, 0::2], zpad(r[:, 0::2], 32)], axis=-1)
        odd = jnp.concatenate([r[:, 1::2], zpad(r[:, 1::2], 32), n[:, 1::2]], axis=-1)
        both = jnp.stack([even, odd], axis=2)
        return both.reshape(MLA_Q_LORA, MLA_HEADS * LANES)

    row = lambda a: a.reshape(1, -1).astype(F32)
    return {
        "gmix": row(norm_mix), "gq": row(mla_q_norm), "gkv": row(mla_kv_norm),
        "gmlp": row(norm_mlp), "gfin": row(norm_final),
        "wqT": w_in[:, 0:d].T.astype(BF),
        "wk": w_in[:, d:2 * d].astype(BF),
        "wv": w_in[:, 2 * d:3 * d].astype(BF),
        "wlat": wlat.astype(BF),
        "wgT": w_in[:, o_g:].T.astype(BF),
        "wuqaT": arrange(nope, rope).T.astype(BF),
        "wuqbT": rot(rope).reshape(MLA_Q_LORA, MLA_HEADS * MLA_ROPE_DIM).T.astype(BF),
        "wuk": mla_w_uk.reshape(MLA_KV_LORA, d).astype(BF),
        "wuvT": mla_w_uv.reshape(MLA_KV_LORA, d).T.astype(BF),
        "wodT": w_o_diff.reshape(d, d).T.astype(BF),
        "womT": w_o_mla.reshape(d, d).T.astype(BF),
        "woutT": w_out.T.astype(BF),
        "wup": w_up.astype(BF),
        "wdown": w_down.astype(BF),
    }


def _rope_tables(pos):
    half = MLA_ROPE_DIM // 2
    inv = jnp.power(ROPE_THETA, -jnp.arange(half, dtype=F32) * 2.0 / MLA_ROPE_DIM)
    ang = pos.astype(F32)[:, None] * inv[None, :]
    cos2 = jnp.tile(jnp.cos(ang), (1, 2))
    sin2 = jnp.tile(jnp.sin(ang), (1, 2))
    t = pos.shape[0]
    z32 = jnp.zeros((t, 32), F32)
    place = lambda a: jnp.concatenate([a, z32, a, z32], axis=1)
    scale = (MLA_NOPE_DIM + MLA_ROPE_DIM) ** -0.5 * LOG2E
    ones = jnp.ones((t, MLA_NOPE_DIM), F32)
    z64 = jnp.zeros((t, MLA_NOPE_DIM), F32)
    ct = jnp.concatenate([ones, cos2, z32, cos2, z32, ones], axis=1) * scale
    st = jnp.concatenate([z64, sin2, z32, sin2, z32, z64], axis=1) * scale
    return {"cosp": place(cos2), "sinp": place(sin2), "ct": ct.T, "st": st.T}


def _score_tiles(rel_bias, k_pos_tiles, q_pos, shift_bucket):
    k_pos = jnp.stack(k_pos_tiles)
    rel = k_pos[:, :, None] - q_pos[None, None, :]
    mask = (k_pos // CHUNK)[:, :, None] <= (q_pos // CHUNK)[None, None, :]
    return _bias_tiles(rel_bias, _t5_bucket(rel).astype(jnp.int32), mask.astype(jnp.int32), shift_bucket)


def _decode_score_tiles(rel_bias, k_pos, q_pos, new, shift_bucket):
    lane = jnp.arange(2 * DIFF_HEADS * new, dtype=jnp.int32)
    qp = q_pos[lane % new]
    rel = k_pos[:, None] - qp[None, :]
    mask = (k_pos // CHUNK)[:, None] <= (qp // CHUNK)[None, :]
    tab = rel_bias[:, (lane // new) % DIFF_HEADS].astype(F32)
    return _bias_lanes(tab, _t5_bucket(rel).astype(jnp.int32), mask.astype(jnp.int32), shift_bucket)


def kernel(x_prompt, x_sample, cache_diff_k, cache_diff_v, cache_mla_ckv, cache_mla_krope, rel_bias,
           norm_mix, w_in, lam_q1, lam_k1, lam_q2, lam_k2, diff_subln, mla_q_norm, mla_w_uq, mla_kv_norm,
           mla_w_uk, mla_w_uv, w_o_diff, w_o_mla, w_out, norm_mlp, w_up, w_down, norm_final):
    assert norm_mix.shape[0] == 1, "single-layer model"
    nb, t, _ = x_prompt.shape
    ns, new, _ = x_sample.shape
    past = cache_diff_k.shape[2]
    diff_blk = min(DIFF_CACHE_BLK, past)
    assert t % TQ == 0 and past % CACHE_BLK == 0 and new <= DEC_PAD and new % 16 == 0
    assert past % diff_blk == 0 and diff_blk >= TK
    assert past % CHUNK == 0 and new <= CHUNK

    w = _prep_weights(norm_mix[0], w_in[0], mla_q_norm[0], mla_w_uq[0], mla_kv_norm[0], mla_w_uk[0],
                      mla_w_uv[0], w_o_diff[0], w_o_mla[0], w_out[0], norm_mlp[0], w_up[0], w_down[0],
                      norm_final)
    lamp = jnp.stack([lam_q1[0], lam_k1[0], lam_q2[0], lam_k2[0]]).astype(F32)
    g_sub = diff_subln[0].reshape(DIFF_V_DIM, 1).astype(F32)

    pos_p = jnp.arange(t, dtype=jnp.int32)
    p = _project(x_prompt, w, _rope_tables(pos_p), tm=256)
    blk = jnp.arange(TQ, dtype=jnp.int32)
    tiles_p = _score_tiles(rel_bias, [TQ + blk, blk], TQ + blk, _far_bucket())
    mask_p = jnp.where((blk // CHUNK)[:, None] <= (blk // CHUNK)[None, :], 0.0, NEG_INF).astype(F32)
    oTd = _diff_prompt(p, tiles_p, lamp, g_sub)
    oTm = _mla_prompt(p, mask_p)
    y_prompt = _post(x_prompt, oTd, oTm, w, tm=512, n_groups=2)

    toks = ns * new
    xs = x_sample.reshape(1, toks, D_MODEL)
    pos_s = past + jnp.arange(DEC_PAD, dtype=jnp.int32)
    s = _project(xs, w, _rope_tables(jnp.tile(pos_s[:new], ns)), tm=toks)
    per_stream_q = lambda a: jnp.pad(a[0].reshape(-1, ns, new).transpose(1, 0, 2),
                                     ((0, 0), (0, 0), (0, DEC_PAD - new)))
    s.update(qTd=per_stream_q(s["qTd"]), qTm=per_stream_q(s["qTm"]),
             kb=s["kb"].reshape(ns, new, D_MODEL), ckv_s=s["ckv"].reshape(ns, new, MLA_KV_LORA),
             kr_s=s["kr"].reshape(ns, new, MLA_ROPE_DIM),
             vf_s=s["vf"].reshape(ns, new * DIFF_HEADS, DIFF_V_DIM))
    bias_c = _decode_score_tiles(rel_bias, jnp.arange(past - diff_blk, past, dtype=jnp.int32), pos_s, new,
                                 _far_bucket())
    bias_n = _decode_score_tiles(rel_bias, pos_s[:new], pos_s, new, _far_bucket())
    oTd_s = _diff_decode(s, cache_diff_k[0], cache_diff_v[0], bias_c, bias_n, lamp, g_sub, new, diff_blk)
    oTm_s = _mla_decode(s, cache_mla_ckv[0], cache_mla_krope[0], w, new)
    token_major = lambda a: a[:, :, :new].transpose(1, 0, 2).reshape(1, D_MODEL, toks)
    y_sample = _post(xs, token_major(oTd_s), token_major(oTm_s), w, tm=toks, n_groups=1)

    heads = lambda a, lead: a.reshape((1,) + lead + (DIFF_HEADS, DIFF_V_DIM))
    return (y_prompt, y_sample.reshape(ns, new, D_MODEL),
            heads(p["kf"], (nb, t)), heads(p["vf"], (nb, t)), p["ckv"][None], p["kr"][None],
            heads(s["kf"], (ns, new)), heads(s["vf"], (ns, new)),
            s["ckv"].reshape(1, ns, new, MLA_KV_LORA), s["kr"].reshape(1, ns, new, MLA_ROPE_DIM))
```

```python
import functools
import math

import numpy as np
import jax
import jax.numpy as jnp
from jax import lax
from jax.experimental import pallas as pl
from jax.experimental.pallas import tpu as pltpu

D_MODEL = 1024
CHUNK = 64
DIFF_HEADS = 8
DIFF_HEAD_DIM = 64
DIFF_V_DIM = 128
MLA_HEADS = 16
MLA_NOPE_DIM = 64
MLA_ROPE_DIM = 32
MLA_V_DIM = 64
MLA_Q_LORA = 256
MLA_KV_LORA = 256
D_FF = 4 * D_MODEL
NUM_BUCKETS = 32
MAX_DISTANCE = 128
ROPE_THETA = 10000.0
EPS = 1e-6
NEG_INF = -1e30
LAMBDA_INIT = 0.8 - 0.6 * math.exp(-0.3 * 0)

BF = jnp.bfloat16
F32 = jnp.float32

LANES = 128
TQ = 256
TK = 256
TR = 256
HEADS_PER_STEP = 2
DEC_PAD = LANES
CACHE_BLK = 512
DIFF_CACHE_BLK = 1024
VMEM_LIMIT = 56 * 1024 * 1024
LOG2E = math.log2(math.e)
ONES_ROWS = 16


def _nn(a, b):
    return jnp.dot(a, b, preferred_element_type=F32)


def _nt(a, b):
    return lax.dot_general(a, b, (((1,), (1,)), ((), ())), preferred_element_type=F32)


def _tn(a, b):
    return lax.dot_general(a, b, (((0,), (0,)), ((), ())), preferred_element_type=F32)


def _rms(x, g):
    return x * lax.rsqrt(jnp.mean(x * x, axis=-1, keepdims=True) + EPS) * g


def _const_spec(shape):
    zeros = (0,) * len(shape)
    return pl.BlockSpec(shape, lambda *_: zeros, pipeline_mode=pl.Buffered(1))


def _params(n_axes):
    return pltpu.CompilerParams(dimension_semantics=("arbitrary",) * n_axes,
                                vmem_limit_bytes=VMEM_LIMIT)


def _bias_kernel(tab_ref, bkt_ref, msk_ref, o_ref, *, shift_bucket):
    h = pl.program_id(0)
    bkt = bkt_ref[...]
    acc = jnp.zeros(bkt.shape, F32)
    for b in range(NUM_BUCKETS):
        acc = jnp.where(bkt == b, tab_ref[b, h], acc)
    if shift_bucket is not None:
        acc = acc - tab_ref[shift_bucket, h]
    o_ref[...] = jnp.where(msk_ref[...] != 0, acc * LOG2E, NEG_INF)


def _bias_tiles(rel_bias, bucket, mask, shift_bucket):
    n, r, c = bucket.shape
    return pl.pallas_call(
        functools.partial(_bias_kernel, shift_bucket=shift_bucket),
        grid=(DIFF_HEADS,),
        in_specs=[pl.BlockSpec(memory_space=pltpu.SMEM),
                  pl.BlockSpec((n, r, c), lambda h: (0, 0, 0)),
                  pl.BlockSpec((n, r, c), lambda h: (0, 0, 0))],
        out_specs=pl.BlockSpec((None, n, r, c), lambda h: (h, 0, 0, 0)),
        out_shape=jax.ShapeDtypeStruct((DIFF_HEADS, n, r, c), F32),
        compiler_params=_params(1),
        name="bias_tiles",
    )(rel_bias, bucket, mask)


def _bias_lanes_kernel(tab_ref, bkt_ref, msk_ref, o_ref, *, shift_bucket):
    bkt = bkt_ref[...]
    acc = jnp.zeros(bkt.shape, F32)
    for b in range(NUM_BUCKETS):
        acc = jnp.where(bkt == b, tab_ref[b:b + 1, :], acc)
    if shift_bucket is not None:
        acc = acc - tab_ref[shift_bucket:shift_bucket + 1, :]
    o_ref[...] = jnp.where(msk_ref[...] != 0, acc * LOG2E, NEG_INF)


def _bias_lanes(tab, bucket, mask, shift_bucket):
    return pl.pallas_call(
        functools.partial(_bias_lanes_kernel, shift_bucket=shift_bucket),
        out_shape=jax.ShapeDtypeStruct(bucket.shape, F32), name="bias_lanes",
    )(tab, bucket, mask)


def _t5_bucket(rel):
    half = NUM_BUCKETS // 2
    max_exact = half // 2
    n = jnp.abs(rel)
    nf = jnp.maximum(n, max_exact).astype(F32)
    large = max_exact + (jnp.log(nf / max_exact) / math.log(MAX_DISTANCE / max_exact)
                         * (half - max_exact)).astype(jnp.int32)
    large = jnp.minimum(large, half - 1)
    return jnp.where(rel > 0, half, 0) + jnp.where(n < max_exact, n, large)


def _far_bucket():
    half = NUM_BUCKETS // 2
    max_exact = half // 2
    assert max_exact + math.log(TK / max_exact) / math.log(MAX_DISTANCE / max_exact) * (half - max_exact) >= half
    return half - 1


def _proj_kernel(x_ref, gmix_ref, wqT_ref, wk_ref, wv_ref, wlat_ref, gq_ref, gkv_ref,
                 wuqaT_ref, wuqbT_ref, wuk_ref, wuvT_ref, cosp_ref, sinp_ref, ct_ref, st_ref,
                 qTd_ref, kf_ref, kb_ref, vf_ref, vb_ref, qTm_ref, ckv_ref, kn_ref, vTm_ref,
                 r_ref, kr_ref):
    hb = _rms(x_ref[...], gmix_ref[...]).astype(BF)
    lat = _nn(hb, wlat_ref[...])
    qTd_ref[...] = (_nt(wqT_ref[...], hb) * (DIFF_HEAD_DIM ** -0.5 * LOG2E)).astype(BF)
    k = _nn(hb, wk_ref[...])
    kb_ref[...] = k.astype(BF)
    v = _nn(hb, wv_ref[...])
    vb_ref[...] = v.astype(BF)
    tm = k.shape[0]
    for h in range(DIFF_HEADS):
        rows = pl.ds(h, tm, stride=DIFF_HEADS)
        kf_ref[rows, :] = k[:, h * DIFF_V_DIM:(h + 1) * DIFF_V_DIM]
        vf_ref[rows, :] = v[:, h * DIFF_V_DIM:(h + 1) * DIFF_V_DIM]

    cq = _rms(lat[:, 0:MLA_Q_LORA], gq_ref[...]).astype(BF)
    qa = _nt(wuqaT_ref[...], cq)
    qb = _nt(wuqbT_ref[...], cq)
    ct = ct_ref[...]
    st = st_ref[...]
    pair = 2 * LANES
    rd = MLA_ROPE_DIM
    for p in range(MLA_HEADS // 2):
        sl = slice(p * pair, (p + 1) * pair)
        z = jnp.zeros((MLA_NOPE_DIM, tm), F32)
        partner = jnp.concatenate([z, qb[2 * p * rd:(2 * p + 1) * rd, :], z[0:rd, :],
                                   qb[(2 * p + 1) * rd:(2 * p + 2) * rd, :], z[0:rd, :], z], axis=0)
        qTm_ref[sl, :] = (qa[sl, :] * ct + partner * st).astype(BF)

    ckv = _rms(lat[:, MLA_Q_LORA:MLA_Q_LORA + MLA_KV_LORA], gkv_ref[...])
    ckv_ref[...] = ckv
    cb = ckv.astype(BF)
    kn_ref[...] = _nn(cb, wuk_ref[...]).astype(BF)
    vTm_ref[...] = _nt(wuvT_ref[...], cb).astype(BF)

    o = MLA_Q_LORA + MLA_KV_LORA
    r = lat[:, o:o + LANES] * cosp_ref[...] + lat[:, o + LANES:o + 2 * LANES] * sinp_ref[...]
    r_ref[...] = r.astype(BF)
    kr_ref[...] = r[:, 0:MLA_ROPE_DIM]


def _project(x, w, tabs, tm):
    nb, t, _ = x.shape
    nt = t // tm
    tok = lambda width: pl.BlockSpec((None, tm, width), lambda b, i: (b, i, 0))
    feat = lambda rows: pl.BlockSpec((None, rows, tm), lambda b, i: (b, 0, i))
    sds = jax.ShapeDtypeStruct
    weights = [w["gmix"], w["wqT"], w["wk"], w["wv"], w["wlat"], w["gq"], w["gkv"],
               w["wuqaT"], w["wuqbT"], w["wuk"], w["wuvT"]]
    in_specs = ([tok(D_MODEL)] + [_const_spec(a.shape) for a in weights]
                + [pl.BlockSpec((tm, LANES), lambda b, i: (i, 0)),
                   pl.BlockSpec((tm, LANES), lambda b, i: (i, 0)),
                   pl.BlockSpec((2 * LANES, tm), lambda b, i: (0, i)),
                   pl.BlockSpec((2 * LANES, tm), lambda b, i: (0, i))])
    head_rows = pl.BlockSpec((None, tm * DIFF_HEADS, DIFF_V_DIM), lambda b, i: (b, i, 0))
    out_shape = [sds((nb, D_MODEL, t), BF), sds((nb, t * DIFF_HEADS, DIFF_V_DIM), F32), sds((nb, t, D_MODEL), BF),
                 sds((nb, t * DIFF_HEADS, DIFF_V_DIM), F32), sds((nb, t, D_MODEL), BF),
                 sds((nb, MLA_HEADS * LANES, t), BF), sds((nb, t, MLA_KV_LORA), F32),
                 sds((nb, t, D_MODEL), BF), sds((nb, D_MODEL, t), BF),
                 sds((nb, t, LANES), BF), sds((nb, t, MLA_ROPE_DIM), F32)]
    out_specs = [feat(D_MODEL), head_rows, tok(D_MODEL), head_rows, tok(D_MODEL),
                 feat(MLA_HEADS * LANES), tok(MLA_KV_LORA), tok(D_MODEL), feat(D_MODEL),
                 tok(LANES), tok(MLA_ROPE_DIM)]
    outs = pl.pallas_call(
        _proj_kernel, grid=(nb, nt), in_specs=in_specs, out_specs=out_specs, out_shape=out_shape,
        compiler_params=_params(2), name="project",
    )(x, *weights, tabs["cosp"], tabs["sinp"], tabs["ct"], tabs["st"])
    names = ["qTd", "kf", "kb", "vf", "vb", "qTm", "ckv", "kn", "vTm", "r", "kr"]
    return dict(zip(names, outs))


def _split_maps(qT):
    rows = lax.broadcasted_iota(jnp.int32, qT.shape, 0)
    qf = qT.astype(F32)
    zero = jnp.zeros_like(qf)
    return (jnp.where(rows < DIFF_HEAD_DIM, qf, zero).astype(BF),
            jnp.where(rows >= DIFF_HEAD_DIM, qf, zero).astype(BF))


def _lambda(lam_ref):
    lp = lam_ref[...]
    a = jnp.sum(lp[0:1, :] * lp[1:2, :], axis=1, keepdims=True)
    b = jnp.sum(lp[2:3, :] * lp[3:4, :], axis=1, keepdims=True)
    return jnp.exp(a) - jnp.exp(b) + LAMBDA_INIT


def _diff_finish(o1, o2, lam, g):
    o = o1 - lam * o2
    ms = jnp.mean(o * o, axis=0, keepdims=True)
    return (o * lax.rsqrt(ms + EPS) * g * (1.0 - LAMBDA_INIT)).astype(BF)


def _mla_keys(kp, r):
    lane = lax.broadcasted_iota(jnp.int32, kp.shape, 1)
    kpf = kp.astype(F32)
    rf = r.astype(F32)
    return (jnp.where(lane < MLA_NOPE_DIM, kpf, rf).astype(BF),
            jnp.where(lane >= MLA_NOPE_DIM, kpf, rf).astype(BF))


def _qslice(qb):
    return slice(qb * TQ, (qb + 1) * TQ)


def _kslice(kb):
    return slice(kb * TR, (kb + 1) * TR)


def _with_ones(va_ref, vT):
    dv, t = vT.shape
    va_ref[0:dv, :] = vT
    va_ref[dv:dv + ONES_ROWS, :] = jnp.ones((ONES_ROWS, t), BF)


def _weighted_mean(acc):
    dv = acc.shape[0] - ONES_ROWS
    return acc[0:dv, :] * (1.0 / acc[dv:dv + 1, :])


def _causal_two_pass(nq, streams, s_ref, p_ref):
    items = [(stream, qb) for stream in streams for qb in range(nq)]

    def score_pass(n):
        (prep, scores, va_refs, _), qb = items[n]
        ops = prep(qb)
        m8 = [None] * len(va_refs)
        for kb in range((qb + 1) * TK // TR):
            for i, s in enumerate(scores(ops, qb, kb)):
                s_ref[n % 2, i, _kslice(kb), :] = s
                c = jnp.max(s.reshape(TR // 8, 8, TQ), axis=0)
                m8[i] = c if m8[i] is None else jnp.maximum(m8[i], c)
        return [jnp.max(m, axis=0, keepdims=True) for m in m8]

    def exp_pass(n, ms):
        qb = items[n][1]
        for kb in range((qb + 1) * TK // TR):
            for i in range(len(ms)):
                p_ref[n % 2, i, _kslice(kb), :] = jnp.exp2(s_ref[n % 2, i, _kslice(kb), :] - ms[i]).astype(BF)

    def value_pass(n):
        (_, _, va_refs, emit), qb = items[n]
        klen = (qb + 1) * TK
        emit(qb, [_nn(va[:, 0:klen], p_ref[n % 2, i, 0:klen, :]) for i, va in enumerate(va_refs)])

    ms = score_pass(0)
    for n in range(len(items) + 1):
        ms_next = score_pass(n + 1) if n + 1 < len(items) else None
        if n < len(items):
            exp_pass(n, ms)
        if n >= 1:
            value_pass(n - 1)
        ms = ms_next


def _diff_prompt_kernel(qT_ref, k_ref, v_ref, bias_ref, lam_ref, g_ref, o_ref, va_ref, s_ref, p_ref, *, t):
    lam = _lambda(lam_ref)
    gain = g_ref[...]

    def head_stream(j):
        hs = slice(j * DIFF_V_DIM, (j + 1) * DIFF_V_DIM)
        _with_ones(va_ref.at[j], v_ref[:, hs].astype(F32).T.astype(BF))

        def prep(qb):
            return _split_maps(qT_ref[hs, _qslice(qb)])

        def scores(ops, qb, kb):
            kblk = k_ref[_kslice(kb), hs]
            out = [_nn(kblk, q) for q in ops]
            grp, r0 = divmod(kb * TR, TK)
            if grp >= qb - 1:
                out = [s + bias_ref[j, qb - grp, r0:r0 + TR, :] for s in out]
            return out

        def emit(qb, accs):
            o_ref[hs, _qslice(qb)] = _diff_finish(_weighted_mean(accs[0]), _weighted_mean(accs[1]), lam, gain)

        return prep, scores, [va_ref.at[j], va_ref.at[j]], emit

    _causal_two_pass(t // TQ, [head_stream(j) for j in range(HEADS_PER_STEP)], s_ref, p_ref)


def _score_scratch(n_maps, t):
    return [pltpu.VMEM((2, n_maps, t, TQ), F32), pltpu.VMEM((2, n_maps, t, TQ), BF)]


def _diff_prompt(p, bias, lamp, g):
    nb, _, t = p["qTd"].shape
    n = HEADS_PER_STEP
    head_feat = pl.BlockSpec((None, n * DIFF_V_DIM, t), lambda b, h: (b, h, 0))
    head_tok = pl.BlockSpec((None, t, n * DIFF_V_DIM), lambda b, h: (b, 0, h))
    return pl.pallas_call(
        functools.partial(_diff_prompt_kernel, t=t),
        grid=(nb, DIFF_HEADS // n),
        in_specs=[head_feat, head_tok, head_tok,
                  pl.BlockSpec((n, 2, TK, TQ), lambda b, h: (h, 0, 0, 0)),
                  _const_spec(lamp.shape), _const_spec(g.shape)],
        out_specs=head_feat,
        out_shape=jax.ShapeDtypeStruct((nb, D_MODEL, t), BF),
        scratch_shapes=[pltpu.VMEM((n, DIFF_V_DIM + ONES_ROWS, t), BF)] + _score_scratch(2, t),
        compiler_params=_params(2), name="diff_prompt",
    )(p["qTd"], p["kb"], p["vb"], bias, lamp, g)


def _mla_prompt_kernel(qT_ref, kp_ref, r_ref, vT_ref, mask_ref, o_ref, k_ref, va_ref, s_ref, p_ref, *, t):
    def pair_stream(j):
        for i, keys in enumerate(_mla_keys(kp_ref[:, j * LANES:(j + 1) * LANES], r_ref[...])):
            k_ref[j, i] = keys
            vs = slice((2 * j + i) * MLA_V_DIM, (2 * j + i + 1) * MLA_V_DIM)
            _with_ones(va_ref.at[j, i], vT_ref[vs, :])

        def prep(qb):
            return [qT_ref[(2 * j + i) * LANES:(2 * j + i + 1) * LANES, _qslice(qb)] for i in range(2)]

        def scores(ops, qb, kb):
            out = [_nn(k_ref[j, i, _kslice(kb), :], ops[i]) for i in range(2)]
            grp, r0 = divmod(kb * TR, TK)
            if grp == qb:
                out = [s + mask_ref[r0:r0 + TR, :] for s in out]
            return out

        def emit(qb, accs):
            for i in range(2):
                vs = slice((2 * j + i) * MLA_V_DIM, (2 * j + i + 1) * MLA_V_DIM)
                o_ref[vs, _qslice(qb)] = _weighted_mean(accs[i]).astype(BF)

        return prep, scores, [va_ref.at[j, 0], va_ref.at[j, 1]], emit

    _causal_two_pass(t // TQ, [pair_stream(j) for j in range(HEADS_PER_STEP)], s_ref, p_ref)


def _mla_prompt(p, mask):
    nb, _, t = p["qTm"].shape
    n = HEADS_PER_STEP
    return pl.pallas_call(
        functools.partial(_mla_prompt_kernel, t=t),
        grid=(nb, MLA_HEADS // (2 * n)),
        in_specs=[pl.BlockSpec((None, 2 * n * LANES, t), lambda b, h: (b, h, 0)),
                  pl.BlockSpec((None, t, n * LANES), lambda b, h: (b, 0, h)),
                  pl.BlockSpec((None, t, LANES), lambda b, h: (b, 0, 0)),
                  pl.BlockSpec((None, 2 * n * MLA_V_DIM, t), lambda b, h: (b, h, 0)),
                  _const_spec(mask.shape)],
        out_specs=pl.BlockSpec((None, 2 * n * MLA_V_DIM, t), lambda b, h: (b, h, 0)),
        out_shape=jax.ShapeDtypeStruct((nb, D_MODEL, t), BF),
        scratch_shapes=[pltpu.VMEM((n, 2, t, LANES), BF),
                        pltpu.VMEM((n, 2, MLA_V_DIM + ONES_ROWS, t), BF)] + _score_scratch(2, t),
        compiler_params=_params(2), name="mla_prompt",
    )(p["qTm"], p["kn"], p["r"], p["vTm"], mask)


def _diff_decode_kernel(qT_ref, ck_ref, cv_ref, kn_ref, vn_ref, bc_ref, bn_ref, lam_ref, g_ref, o_ref,
                        qbd_ref, m_ref, l_ref, acc_ref, *, new, blk):
    nq = 2 * DIFF_HEADS * new
    assert nq == 2 * LANES and new & (new - 1) == 0
    c = pl.program_id(1)
    last = pl.num_programs(1) - 1
    log2 = lambda n: n.bit_length() - 1

    def lane_head(shape):
        lane = lax.broadcasted_iota(jnp.int32, shape, 1)
        return lax.shift_right_logical(lane, log2(new)) & (DIFF_HEADS - 1)

    @pl.when(c == 0)
    def _():
        m_ref[...] = jnp.full(m_ref.shape, NEG_INF, F32)
        l_ref[...] = jnp.zeros(l_ref.shape, F32)
        acc_ref[...] = jnp.zeros(acc_ref.shape, F32)
        shape = (2 * DIFF_HEAD_DIM, nq)
        row = lax.broadcasted_iota(jnp.int32, shape, 0)
        lane = lax.broadcasted_iota(jnp.int32, shape, 1)
        frame = lax.broadcasted_iota(jnp.int32, (DEC_PAD, nq), 0)
        spread = jnp.where((lax.broadcasted_iota(jnp.int32, (DEC_PAD, nq), 1) & (new - 1)) == frame,
                           1.0, 0.0).astype(BF)
        same_map = lax.shift_right_logical(row, log2(DIFF_HEAD_DIM)) == lax.shift_right_logical(lane, log2(LANES))
        for h in range(DIFF_HEADS):
            rep = _nn(qT_ref[h * DIFF_V_DIM:(h + 1) * DIFF_V_DIM, :], spread)
            qbd_ref[h * DIFF_V_DIM:(h + 1) * DIFF_V_DIM, :] = jnp.where(
                same_map & (lane_head(shape) == h), rep, 0.0).astype(BF)

    def update(keys, vals, bias):
        heads = range(DIFF_HEADS)
        s = _nn(jnp.concatenate([keys(h) for h in heads], axis=1), qbd_ref[...])
        if bias is not None:
            s = s + bias
        m_new = jnp.maximum(m_ref[...], jnp.max(s, axis=0, keepdims=True))
        alpha = jnp.exp2(m_ref[...] - m_new)
        p = jnp.exp2(s - m_new)
        l_ref[...] = alpha * l_ref[...] + jnp.sum(p, axis=0, keepdims=True)
        m_ref[...] = m_new
        pb = p.astype(BF)
        head = lane_head((DIFF_V_DIM, nq))
        full = _tn(jnp.concatenate([vals(h) for h in heads], axis=1), pb)
        pv = jnp.zeros((DIFF_V_DIM, nq), F32)
        for h in heads:
            pv = jnp.where(head == h, full[h * DIFF_V_DIM:(h + 1) * DIFF_V_DIM, :], pv)
        acc_ref[...] = alpha * acc_ref[...] + pv

    def cache_rows(ref):
        return lambda h: ref[pl.ds(h, blk, stride=DIFF_HEADS), :].astype(BF)

    @pl.when(c < last)
    def _():
        update(cache_rows(ck_ref), cache_rows(cv_ref), None)

    @pl.when(c == last)
    def _():
        update(cache_rows(ck_ref), cache_rows(cv_ref), bc_ref[...])
        hs = lambda h: slice(h * DIFF_V_DIM, (h + 1) * DIFF_V_DIM)
        update(lambda h: kn_ref[0:new, hs(h)],
               lambda h: vn_ref[pl.ds(h, new, stride=DIFF_HEADS), :].astype(BF), bn_ref[...])
        o = acc_ref[...] * (1.0 / l_ref[...])
        y = _diff_finish(o[:, 0:LANES], o[:, LANES:nq], _lambda(lam_ref), g_ref[...])
        r = lax.broadcasted_iota(jnp.int32, (LANES, DEC_PAD), 0)
        col = lax.broadcasted_iota(jnp.int32, (LANES, DEC_PAD), 1)
        for h in range(DIFF_HEADS):
            fold = jnp.where(r == col + h * new, 1.0, 0.0).astype(BF)
            o_ref[hs(h), :] = _nn(y, jnp.where(col < new, fold, jnp.zeros_like(fold))).astype(BF)


def _diff_decode(s, cache_k, cache_v, bias_c, bias_n, lamp, g, new, blk):
    nb, past = cache_k.shape[:2]
    nq = 2 * DIFF_HEADS * new
    cache_k, cache_v = (a.reshape(nb, past * DIFF_HEADS, DIFF_V_DIM) for a in (cache_k, cache_v))
    stream = lambda rows: pl.BlockSpec((None, rows, D_MODEL), lambda b, c: (b, 0, 0))
    chunk = pl.BlockSpec((None, blk * DIFF_HEADS, DIFF_V_DIM), lambda b, c: (b, c, 0))
    return pl.pallas_call(
        functools.partial(_diff_decode_kernel, new=new, blk=blk),
        grid=(nb, past // blk),
        in_specs=[pl.BlockSpec((None, D_MODEL, DEC_PAD), lambda b, c: (b, 0, 0)), chunk, chunk,
                  stream(new),
                  pl.BlockSpec((None, new * DIFF_HEADS, DIFF_V_DIM), lambda b, c: (b, 0, 0)),
                  _const_spec(bias_c.shape), _const_spec(bias_n.shape),
                  _const_spec(lamp.shape), _const_spec(g.shape)],
        out_specs=pl.BlockSpec((None, D_MODEL, DEC_PAD), lambda b, c: (b, 0, 0)),
        out_shape=jax.ShapeDtypeStruct((nb, D_MODEL, DEC_PAD), BF),
        scratch_shapes=[pltpu.VMEM((D_MODEL, nq), BF),
                        pltpu.VMEM((1, nq), F32), pltpu.VMEM((1, nq), F32),
                        pltpu.VMEM((DIFF_V_DIM, nq), F32)],
        compiler_params=_params(2), name="diff_decode",
    )(s["qTd"], cache_k, cache_v, s["kb"], s["vf_s"], bias_c, bias_n, lamp, g)


def _mla_decode_kernel(qT_ref, ckv_ref, kr_ref, ckvn_ref, krn_ref, wuk_ref, wuvT_ref, o_ref, s_ref,
                       *, past, new):
    nq = MLA_HEADS * new
    assert nq == 2 * LANES and new & (new - 1) == 0
    log2 = lambda n: n.bit_length() - 1

    def groups(shape, rows_per_group):
        r = lax.broadcasted_iota(jnp.int32, shape, 0)
        c = lax.broadcasted_iota(jnp.int32, shape, 1)
        return lax.shift_right_logical(r, log2(rows_per_group)) == lax.shift_right_logical(c, log2(new))

    def frame_match(shape, lane_axis):
        a = lax.broadcasted_iota(jnp.int32, shape, lane_axis)
        b = lax.broadcasted_iota(jnp.int32, shape, 1 - lane_axis)
        return jnp.where((a & (new - 1)) == b, 1.0, 0.0).astype(BF)

    nope, rope = [], []
    for h in range(MLA_HEADS):
        base = h * LANES
        o_n, o_r = (0, MLA_NOPE_DIM) if h % 2 == 0 else (MLA_NOPE_DIM, 0)
        nope.append(qT_ref[base + o_n:base + o_n + MLA_NOPE_DIM, :])
        rope.append(qT_ref[base + o_r:base + o_r + MLA_ROPE_DIM, :])
    spread = frame_match((DEC_PAD, nq), 1)
    qn = _nn(jnp.concatenate(nope, axis=0), spread)
    qn_bd = jnp.where(groups(qn.shape, MLA_NOPE_DIM), qn, 0.0).astype(BF)
    qr = _nn(jnp.concatenate(rope, axis=0), spread)
    qr_bd = jnp.where(groups(qr.shape, MLA_ROPE_DIM), qr, 0.0)
    qrT = jnp.sum(qr_bd.reshape(MLA_HEADS, MLA_ROPE_DIM, nq), axis=0).astype(BF)
    qlatT = _nn(wuk_ref[...], qn_bd).astype(BF)

    def keys(lo, n):
        if lo < past:
            return ckv_ref[lo:lo + n, :].astype(BF), kr_ref[lo:lo + n, :].astype(BF)
        return ckvn_ref[0:new, :].astype(BF), krn_ref[0:new, :].astype(BF)

    spans = [(c0, CACHE_BLK) for c0 in range(0, past, CACHE_BLK)] + [(past, new)]
    m = None
    for lo, n in spans:
        ck, kr = keys(lo, n)
        s = _nn(ck, qlatT) + _nn(kr, qrT)
        s_ref[lo:lo + n, :] = s
        c = jnp.max(s.reshape(n // 8, 8, nq), axis=0)
        m = c if m is None else jnp.maximum(m, c)
    m = jnp.max(m, axis=0, keepdims=True)
    l = acc = None
    for lo, n in spans:
        p = jnp.exp2(s_ref[lo:lo + n, :] - m)
        ps = jnp.sum(p.reshape(n // 8, 8, nq), axis=0)
        pv = _tn(keys(lo, n)[0], p.astype(BF))
        l, acc = (ps, pv) if l is None else (l + ps, acc + pv)
    o_latT = (acc * (1.0 / jnp.sum(l, axis=0, keepdims=True))).astype(BF)
    y = _nn(wuvT_ref[...], o_latT)
    y_bd = jnp.where(groups(y.shape, MLA_V_DIM), y, 0.0).astype(BF)
    o_ref[...] = _nn(y_bd, frame_match((nq, DEC_PAD), 0)).astype(BF)


def _mla_decode(s, cache_ckv, cache_kr, w, new):
    nb, past, _ = cache_ckv.shape
    per_stream = lambda rows, cols: pl.BlockSpec((None, rows, cols), lambda b: (b, 0, 0))
    return pl.pallas_call(
        functools.partial(_mla_decode_kernel, past=past, new=new),
        grid=(nb,),
        in_specs=[per_stream(MLA_HEADS * LANES, DEC_PAD), per_stream(past, MLA_KV_LORA),
                  per_stream(past, MLA_ROPE_DIM), per_stream(new, MLA_KV_LORA),
                  per_stream(new, MLA_ROPE_DIM), _const_spec(w["wuk"].shape),
                  _const_spec(w["wuvT"].shape)],
        out_specs=per_stream(D_MODEL, DEC_PAD),
        out_shape=jax.ShapeDtypeStruct((nb, D_MODEL, DEC_PAD), BF),
        scratch_shapes=[pltpu.VMEM((past + new, MLA_HEADS * new), F32)],
        compiler_params=_params(1), name="mla_decode",
    )(s["qTm"], cache_ckv, cache_kr, s["ckv_s"], s["kr_s"], w["wuk"], w["wuvT"])


def _post_kernel(x_ref, oTd_ref, oTm_ref, gmix_ref, wgT_ref, wodT_ref, womT_ref, woutT_ref,
                 gmlp_ref, wup_ref, wdown_ref, gfin_ref, y_ref, *, n_groups):
    rows = x_ref.shape[0] // n_groups
    groups = [slice(i * rows, (i + 1) * rows) for i in range(n_groups)]
    d = D_MODEL

    def norm_in(v, r):
        v["x"] = x_ref[r, :]
        v["hb"] = _rms(v["x"], gmix_ref[...]).astype(BF)

    def branches(v, r):
        v["gT"] = _nt(wgT_ref[...], v.pop("hb"))
        v["od"] = _nn(wodT_ref[...], oTd_ref[:, r])
        v["om"] = _nn(womT_ref[...], oTm_ref[:, r])

    def merge(v, r):
        g = jax.nn.sigmoid(v.pop("gT"))
        v["mT"] = (g[0:d, :] * v.pop("od") + g[d:2 * d, :] * v.pop("om")).astype(BF)

    def out_proj(v, r):
        v["aT"] = _nn(woutT_ref[...], v.pop("mT"))

    def residual(v, r):
        v["x1"] = v.pop("x") + v.pop("aT").T
        v["h2"] = _rms(v["x1"], gmlp_ref[...]).astype(BF)

    def up(v, r):
        v["u"] = _nn(v.pop("h2"), wup_ref[...])

    def act(v, r):
        u = jnp.maximum(v.pop("u"), 0.0)
        v["uu"] = (u * u).astype(BF)

    def down(v, r):
        v["x2"] = v.pop("x1") + _nn(v.pop("uu"), wdown_ref[...])

    def norm_out(v, r):
        y_ref[r, :] = _rms(v.pop("x2"), gfin_ref[...])

    values = [{} for _ in groups]
    for stage in (norm_in, branches, merge, out_proj, residual, up, act, down, norm_out):
        for v, r in zip(values, groups):
            stage(v, r)


def _post(x, oTd, oTm, w, tm, n_groups):
    nb, t, _ = x.shape
    tok = pl.BlockSpec((None, tm, D_MODEL), lambda b, i: (b, i, 0))
    feat = pl.BlockSpec((None, D_MODEL, tm), lambda b, i: (b, 0, i))
    weights = [w["gmix"], w["wgT"], w["wodT"], w["womT"], w["woutT"], w["gmlp"], w["wup"],
               w["wdown"], w["gfin"]]
    return pl.pallas_call(
        functools.partial(_post_kernel, n_groups=n_groups), grid=(nb, t // tm),
        in_specs=[tok, feat, feat] + [_const_spec(a.shape) for a in weights],
        out_specs=tok, out_shape=jax.ShapeDtypeStruct((nb, t, D_MODEL), F32),
        compiler_params=_params(2), name="post",
    )(x, oTd, oTm, *weights)


def _prep_weights(norm_mix, w_in, mla_q_norm, mla_w_uq, mla_kv_norm, mla_w_uk, mla_w_uv, w_o_diff,
                  w_o_mla, w_out, norm_mlp, w_up, w_down, norm_final):
    d = D_MODEL
    o_cq = 3 * d
    o_ckv = o_cq + MLA_Q_LORA
    o_kr = o_ckv + MLA_KV_LORA
    o_g = o_kr + MLA_ROPE_DIM
    half = MLA_ROPE_DIM // 2
    rot = lambda a: jnp.concatenate([-a[..., half:], a[..., :half]], axis=-1)
    zpad = lambda a, n: jnp.zeros(a.shape[:-1] + (n,), a.dtype)

    wkr = w_in[:, o_kr:o_g]
    place = lambda a: jnp.concatenate([a, zpad(a, 32), a, zpad(a, 32)], axis=-1)
    wlat = jnp.concatenate([w_in[:, o_cq:o_kr], place(wkr), place(rot(wkr))], axis=-1)

    nope = mla_w_uq[:, :, :MLA_NOPE_DIM]
    rope = mla_w_uq[:, :, MLA_NOPE_DIM:]

    def arrange(n, r):
        even = jnp.concatenate([n[:, 0::2], r[:, 0::2], zpad(r[:, 0::2], 32)], axis=-1)
        odd = jnp.concatenate([r[:, 1::2], zpad(r[:, 1::2], 32), n[:, 1::2]], axis=-1)
        both = jnp.stack([even, odd], axis=2)
        return both.reshape(MLA_Q_LORA, MLA_HEADS * LANES)

    row = lambda a: a.reshape(1, -1).astype(F32)
    return {
        "gmix": row(norm_mix), "gq": row(mla_q_norm), "gkv": row(mla_kv_norm),
        "gmlp": row(norm_mlp), "gfin": row(norm_final),
        "wqT": w_in[:, 0:d].T.astype(BF),
        "wk": w_in[:, d:2 * d].astype(BF),
        "wv": w_in[:, 2 * d:3 * d].astype(BF),
        "wlat": wlat.astype(BF),
        "wgT": w_in[:, o_g:].T.astype(BF),
        "wuqaT": arrange(nope, rope).T.astype(BF),
        "wuqbT": rot(rope).reshape(MLA_Q_LORA, MLA_HEADS * MLA_ROPE_DIM).T.astype(BF),
        "wuk": mla_w_uk.reshape(MLA_KV_LORA, d).astype(BF),
        "wuvT": mla_w_uv.reshape(MLA_KV_LORA, d).T.astype(BF),
        "wodT": w_o_diff.reshape(d, d).T.astype(BF),
        "womT": w_o_mla.reshape(d, d).T.astype(BF),
        "woutT": w_out.T.astype(BF),
        "wup": w_up.astype(BF),
        "wdown": w_down.astype(BF),
    }


def _rope_tables(pos):
    half = MLA_ROPE_DIM // 2
    inv = jnp.power(ROPE_THETA, -jnp.arange(half, dtype=F32) * 2.0 / MLA_ROPE_DIM)
    ang = pos.astype(F32)[:, None] * inv[None, :]
    cos2 = jnp.tile(jnp.cos(ang), (1, 2))
    sin2 = jnp.tile(jnp.sin(ang), (1, 2))
    t = pos.shape[0]
    z32 = jnp.zeros((t, 32), F32)
    place = lambda a: jnp.concatenate([a, z32, a, z32], axis=1)
    scale = (MLA_NOPE_DIM + MLA_ROPE_DIM) ** -0.5 * LOG2E
    ones = jnp.ones((t, MLA_NOPE_DIM), F32)
    z64 = jnp.zeros((t, MLA_NOPE_DIM), F32)
    ct = jnp.concatenate([ones, cos2, z32, cos2, z32, ones], axis=1) * scale
    st = jnp.concatenate([z64, sin2, z32, sin2, z32, z64], axis=1) * scale
    return {"cosp": place(cos2), "sinp": place(sin2), "ct": ct.T, "st": st.T}


def _score_tiles(rel_bias, k_pos_tiles, q_pos, shift_bucket):
    k_pos = jnp.stack(k_pos_tiles)
    rel = k_pos[:, :, None] - q_pos[None, None, :]
    mask = (k_pos // CHUNK)[:, :, None] <= (q_pos // CHUNK)[None, None, :]
    return _bias_tiles(rel_bias, _t5_bucket(rel).astype(jnp.int32), mask.astype(jnp.int32), shift_bucket)


def _decode_score_tiles(rel_bias, k_pos, q_pos, new, shift_bucket):
    lane = jnp.arange(2 * DIFF_HEADS * new, dtype=jnp.int32)
    qp = q_pos[lane % new]
    rel = k_pos[:, None] - qp[None, :]
    mask = (k_pos // CHUNK)[:, None] <= (qp // CHUNK)[None, :]
    tab = rel_bias[:, (lane // new) % DIFF_HEADS].astype(F32)
    return _bias_lanes(tab, _t5_bucket(rel).astype(jnp.int32), mask.astype(jnp.int32), shift_bucket)


def kernel(x_prompt, x_sample, cache_diff_k, cache_diff_v, cache_mla_ckv, cache_mla_krope, rel_bias,
           norm_mix, w_in, lam_q1, lam_k1, lam_q2, lam_k2, diff_subln, mla_q_norm, mla_w_uq, mla_kv_norm,
           mla_w_uk, mla_w_uv, w_o_diff, w_o_mla, w_out, norm_mlp, w_up, w_down, norm_final):
    assert norm_mix.shape[0] == 1, "single-layer model"
    nb, t, _ = x_prompt.shape
    ns, new, _ = x_sample.shape
    past = cache_diff_k.shape[2]
    diff_blk = min(DIFF_CACHE_BLK, past)
    assert t % TQ == 0 and past % CACHE_BLK == 0 and new <= DEC_PAD and new % 16 == 0
    assert past % diff_blk == 0 and diff_blk >= TK
    assert past % CHUNK == 0 and new <= CHUNK

    w = _prep_weights(norm_mix[0], w_in[0], mla_q_norm[0], mla_w_uq[0], mla_kv_norm[0], mla_w_uk[0],
                      mla_w_uv[0], w_o_diff[0], w_o_mla[0], w_out[0], norm_mlp[0], w_up[0], w_down[0],
                      norm_final)
    lamp = jnp.stack([lam_q1[0], lam_k1[0], lam_q2[0], lam_k2[0]]).astype(F32)
    g_sub = diff_subln[0].reshape(DIFF_V_DIM, 1).astype(F32)

    pos_p = jnp.arange(t, dtype=jnp.int32)
    p = _project(x_prompt, w, _rope_tables(pos_p), tm=256)
    blk = jnp.arange(TQ, dtype=jnp.int32)
    tiles_p = _score_tiles(rel_bias, [TQ + blk, blk], TQ + blk, _far_bucket())
    mask_p = jnp.where((blk // CHUNK)[:, None] <= (blk // CHUNK)[None, :], 0.0, NEG_INF).astype(F32)
    oTd = _diff_prompt(p, tiles_p, lamp, g_sub)
    oTm = _mla_prompt(p, mask_p)
    y_prompt = _post(x_prompt, oTd, oTm, w, tm=512, n_groups=2)

    toks = ns * new
    xs = x_sample.reshape(1, toks, D_MODEL)
    pos_s = past + jnp.arange(DEC_PAD, dtype=jnp.int32)
    s = _project(xs, w, _rope_tables(jnp.tile(pos_s[:new], ns)), tm=toks)
    per_stream_q = lambda a: jnp.pad(a[0].reshape(-1, ns, new).transpose(1, 0, 2),
                                     ((0, 0), (0, 0), (0, DEC_PAD - new)))
    s.update(qTd=per_stream_q(s["qTd"]), qTm=per_stream_q(s["qTm"]),
             kb=s["kb"].reshape(ns, new, D_MODEL), ckv_s=s["ckv"].reshape(ns, new, MLA_KV_LORA),
             kr_s=s["kr"].reshape(ns, new, MLA_ROPE_DIM),
             vf_s=s["vf"].reshape(ns, new * DIFF_HEADS, DIFF_V_DIM))
    bias_c = _decode_score_tiles(rel_bias, jnp.arange(past - diff_blk, past, dtype=jnp.int32), pos_s, new,
                                 _far_bucket())
    bias_n = _decode_score_tiles(rel_bias, pos_s[:new], pos_s, new, _far_bucket())
    oTd_s = _diff_decode(s, cache_diff_k[0], cache_diff_v[0], bias_c, bias_n, lamp, g_sub, new, diff_blk)
    oTm_s = _mla_decode(s, cache_mla_ckv[0], cache_mla_krope[0], w, new)
    token_major = lambda a: a[:, :, :new].transpose(1, 0, 2).reshape(1, D_MODEL, toks)
    y_sample = _post(xs, token_major(oTd_s), token_major(oTm_s), w, tm=toks, n_groups=1)

    heads = lambda a, lead: a.reshape((1,) + lead + (DIFF_HEADS, DIFF_V_DIM))
    return (y_prompt, y_sample.reshape(ns, new, D_MODEL),
            heads(p["kf"], (nb, t)), heads(p["vf"], (nb, t)), p["ckv"][None], p["kr"][None],
            heads(s["kf"], (ns, new)), heads(s["vf"], (ns, new)),
            s["ckv"].reshape(1, ns, new, MLA_KV_LORA), s["kr"].reshape(1, ns, new, MLA_ROPE_DIM))
```

```python
import functools
import math

import numpy as np
import jax
import jax.numpy as jnp
from jax import lax
from jax.experimental import pallas as pl
from jax.experimental.pallas import tpu as pltpu

D_MODEL = 1024
CHUNK = 64
DIFF_HEADS = 8
DIFF_HEAD_DIM = 64
DIFF_V_DIM = 128
MLA_HEADS = 16
MLA_NOPE_DIM = 64
MLA_ROPE_DIM = 32
MLA_V_DIM = 64
MLA_Q_LORA = 256
MLA_KV_LORA = 256
D_FF = 4 * D_MODEL
NUM_BUCKETS = 32
MAX_DISTANCE = 128
ROPE_THETA = 10000.0
EPS = 1e-6
NEG_INF = -1e30
LAMBDA_INIT = 0.8 - 0.6 * math.exp(-0.3 * 0)

BF = jnp.bfloat16
F32 = jnp.float32

LANES = 128
TQ = 256
TK = 256
TR = 256
HEADS_PER_STEP = 2
DEC_PAD = LANES
CACHE_BLK = 512
DIFF_CACHE_BLK = 1024
VMEM_LIMIT = 56 * 1024 * 1024
LOG2E = math.log2(math.e)
ONES_ROWS = 16


def _nn(a, b):
    return jnp.dot(a, b, preferred_element_type=F32)


def _nt(a, b):
    return lax.dot_general(a, b, (((1,), (1,)), ((), ())), preferred_element_type=F32)


def _tn(a, b):
    return lax.dot_general(a, b, (((0,), (0,)), ((), ())), preferred_element_type=F32)


def _rms(x, g):
    return x * lax.rsqrt(jnp.mean(x * x, axis=-1, keepdims=True) + EPS) * g


def _const_spec(shape):
    zeros = (0,) * len(shape)
    return pl.BlockSpec(shape, lambda *_: zeros, pipeline_mode=pl.Buffered(1))


def _params(n_axes):
    return pltpu.CompilerParams(dimension_semantics=("arbitrary",) * n_axes,
                                vmem_limit_bytes=VMEM_LIMIT)


def _bias_kernel(tab_ref, bkt_ref, msk_ref, o_ref, *, shift_bucket):
    h = pl.program_id(0)
    bkt = bkt_ref[...]
    acc = jnp.zeros(bkt.shape, F32)
    for b in range(NUM_BUCKETS):
        acc = jnp.where(bkt == b, tab_ref[b, h], acc)
    if shift_bucket is not None:
        acc = acc - tab_ref[shift_bucket, h]
    o_ref[...] = jnp.where(msk_ref[...] != 0, acc * LOG2E, NEG_INF)


def _bias_tiles(rel_bias, bucket, mask, shift_bucket):
    n, r, c = bucket.shape
    return pl.pallas_call(
        functools.partial(_bias_kernel, shift_bucket=shift_bucket),
        grid=(DIFF_HEADS,),
        in_specs=[pl.BlockSpec(memory_space=pltpu.SMEM),
                  pl.BlockSpec((n, r, c), lambda h: (0, 0, 0)),
                  pl.BlockSpec((n, r, c), lambda h: (0, 0, 0))],
        out_specs=pl.BlockSpec((None, n, r, c), lambda h: (h, 0, 0, 0)),
        out_shape=jax.ShapeDtypeStruct((DIFF_HEADS, n, r, c), F32),
        compiler_params=_params(1),
        name="bias_tiles",
    )(rel_bias, bucket, mask)


def _bias_lanes_kernel(tab_ref, bkt_ref, msk_ref, o_ref, *, shift_bucket):
    bkt = bkt_ref[...]
    acc = jnp.zeros(bkt.shape, F32)
    for b in range(NUM_BUCKETS):
        acc = jnp.where(bkt == b, tab_ref[b:b + 1, :], acc)
    if shift_bucket is not None:
        acc = acc - tab_ref[shift_bucket:shift_bucket + 1, :]
    o_ref[...] = jnp.where(msk_ref[...] != 0, acc * LOG2E, NEG_INF)


def _bias_lanes(tab, bucket, mask, shift_bucket):
    return pl.pallas_call(
        functools.partial(_bias_lanes_kernel, shift_bucket=shift_bucket),
        out_shape=jax.ShapeDtypeStruct(bucket.shape, F32), name="bias_lanes",
    )(tab, bucket, mask)


def _t5_bucket(rel):
    half = NUM_BUCKETS // 2
    max_exact = half // 2
    n = jnp.abs(rel)
    nf = jnp.maximum(n, max_exact).astype(F32)
    large = max_exact + (jnp.log(nf / max_exact) / math.log(MAX_DISTANCE / max_exact)
                         * (half - max_exact)).astype(jnp.int32)
    large = jnp.minimum(large, half - 1)
    return jnp.where(rel > 0, half, 0) + jnp.where(n < max_exact, n, large)


def _far_bucket():
    half = NUM_BUCKETS // 2
    max_exact = half // 2
    assert max_exact + math.log(TK / max_exact) / math.log(MAX_DISTANCE / max_exact) * (half - max_exact) >= half
    return half - 1


def _proj_kernel(x_ref, gmix_ref, wqT_ref, wk_ref, wv_ref, wlat_ref, gq_ref, gkv_ref,
                 wuqaT_ref, wuqbT_ref, wuk_ref, wuvT_ref, cosp_ref, sinp_ref, ct_ref, st_ref,
                 qTd_ref, kf_ref, kb_ref, vf_ref, vb_ref, qTm_ref, ckv_ref, kn_ref, vTm_ref,
                 r_ref, kr_ref, *, n_groups):
    rows = x_ref.shape[0] // n_groups
    groups = [slice(i * rows, (i + 1) * rows) for i in range(n_groups)]
    pair = 2 * LANES
    rd = MLA_ROPE_DIM

    def norm_in(v, r):
        v["hb"] = _rms(x_ref[r, :], gmix_ref[...]).astype(BF)

    def latent(v, r):
        v["lat"] = _nn(v["hb"], wlat_ref[...])

    def qkv(v, r):
        hb = v.pop("hb")
        qTd_ref[:, r] = (_nt(wqT_ref[...], hb) * (DIFF_HEAD_DIM ** -0.5 * LOG2E)).astype(BF)
        k = _nn(hb, wk_ref[...])
        kb_ref[r, :] = k.astype(BF)
        val = _nn(hb, wv_ref[...])
        vb_ref[r, :] = val.astype(BF)
        for h in range(DIFF_HEADS):
            dst = pl.ds(r.start * DIFF_HEADS + h, rows, stride=DIFF_HEADS)
            kf_ref[dst, :] = k[:, h * DIFF_V_DIM:(h + 1) * DIFF_V_DIM]
            vf_ref[dst, :] = val[:, h * DIFF_V_DIM:(h + 1) * DIFF_V_DIM]

    def latent_norms(v, r):
        lat = v["lat"]
        v["cq"] = _rms(lat[:, 0:MLA_Q_LORA], gq_ref[...]).astype(BF)
        ckv = _rms(lat[:, MLA_Q_LORA:MLA_Q_LORA + MLA_KV_LORA], gkv_ref[...])
        ckv_ref[r, :] = ckv
        v["cb"] = ckv.astype(BF)

    def up(v, r):
        cq, cb = v.pop("cq"), v.pop("cb")
        v["qa"] = _nt(wuqaT_ref[...], cq)
        v["qb"] = _nt(wuqbT_ref[...], cq)
        kn_ref[r, :] = _nn(cb, wuk_ref[...]).astype(BF)
        vTm_ref[:, r] = _nt(wuvT_ref[...], cb).astype(BF)

    def rope(v, r):
        qa, qb, lat = v.pop("qa"), v.pop("qb"), v.pop("lat")
        ct = ct_ref[:, r]
        st = st_ref[:, r]
        z = jnp.zeros((MLA_NOPE_DIM, rows), F32)
        for p in range(MLA_HEADS // 2):
            sl = slice(p * pair, (p + 1) * pair)
            partner = jnp.concatenate([z, qb[2 * p * rd:(2 * p + 1) * rd, :], z[0:rd, :],
                                       qb[(2 * p + 1) * rd:(2 * p + 2) * rd, :], z[0:rd, :], z], axis=0)
            qTm_ref[sl, r] = (qa[sl, :] * ct + partner * st).astype(BF)
        o = MLA_Q_LORA + MLA_KV_LORA
        kr = lat[:, o:o + LANES] * cosp_ref[r, :] + lat[:, o + LANES:o + 2 * LANES] * sinp_ref[r, :]
        r_ref[r, :] = kr.astype(BF)
        kr_ref[r, :] = kr[:, 0:MLA_ROPE_DIM]

    values = [{} for _ in groups]
    for stage in (norm_in, latent, qkv, latent_norms, up, rope):
        for v, r in zip(values, groups):
            stage(v, r)


def _project(x, w, tabs, tm, n_groups):
    nb, t, _ = x.shape
    nt = t // tm
    tok = lambda width: pl.BlockSpec((None, tm, width), lambda b, i: (b, i, 0))
    feat = lambda rows: pl.BlockSpec((None, rows, tm), lambda b, i: (b, 0, i))
    sds = jax.ShapeDtypeStruct
    weights = [w["gmix"], w["wqT"], w["wk"], w["wv"], w["wlat"], w["gq"], w["gkv"],
               w["wuqaT"], w["wuqbT"], w["wuk"], w["wuvT"]]
    in_specs = ([tok(D_MODEL)] + [_const_spec(a.shape) for a in weights]
                + [pl.BlockSpec((tm, LANES), lambda b, i: (i, 0)),
                   pl.BlockSpec((tm, LANES), lambda b, i: (i, 0)),
                   pl.BlockSpec((2 * LANES, tm), lambda b, i: (0, i)),
                   pl.BlockSpec((2 * LANES, tm), lambda b, i: (0, i))])
    head_rows = pl.BlockSpec((None, tm * DIFF_HEADS, DIFF_V_DIM), lambda b, i: (b, i, 0))
    out_shape = [sds((nb, D_MODEL, t), BF), sds((nb, t * DIFF_HEADS, DIFF_V_DIM), F32), sds((nb, t, D_MODEL), BF),
                 sds((nb, t * DIFF_HEADS, DIFF_V_DIM), F32), sds((nb, t, D_MODEL), BF),
                 sds((nb, MLA_HEADS * LANES, t), BF), sds((nb, t, MLA_KV_LORA), F32),
                 sds((nb, t, D_MODEL), BF), sds((nb, D_MODEL, t), BF),
                 sds((nb, t, LANES), BF), sds((nb, t, MLA_ROPE_DIM), F32)]
    out_specs = [feat(D_MODEL), head_rows, tok(D_MODEL), head_rows, tok(D_MODEL),
                 feat(MLA_HEADS * LANES), tok(MLA_KV_LORA), tok(D_MODEL), feat(D_MODEL),
                 tok(LANES), tok(MLA_ROPE_DIM)]
    outs = pl.pallas_call(
        functools.partial(_proj_kernel, n_groups=n_groups), grid=(nb, nt), in_specs=in_specs,
        out_specs=out_specs, out_shape=out_shape,
        compiler_params=_params(2), name="project",
    )(x, *weights, tabs["cosp"], tabs["sinp"], tabs["ct"], tabs["st"])
    names = ["qTd", "kf", "kb", "vf", "vb", "qTm", "ckv", "kn", "vTm", "r", "kr"]
    return dict(zip(names, outs))


def _split_maps(qT):
    rows = lax.broadcasted_iota(jnp.int32, qT.shape, 0)
    qf = qT.astype(F32)
    zero = jnp.zeros_like(qf)
    return (jnp.where(rows < DIFF_HEAD_DIM, qf, zero).astype(BF),
            jnp.where(rows >= DIFF_HEAD_DIM, qf, zero).astype(BF))


def _lambda(lam_ref):
    lp = lam_ref[...]
    a = jnp.sum(lp[0:1, :] * lp[1:2, :], axis=1, keepdims=True)
    b = jnp.sum(lp[2:3, :] * lp[3:4, :], axis=1, keepdims=True)
    return jnp.exp(a) - jnp.exp(b) + LAMBDA_INIT


def _diff_finish(o1, o2, lam, g):
    o = o1 - lam * o2
    ms = jnp.mean(o * o, axis=0, keepdims=True)
    return (o * lax.rsqrt(ms + EPS) * g * (1.0 - LAMBDA_INIT)).astype(BF)


def _mla_keys(kp, r):
    lane = lax.broadcasted_iota(jnp.int32, kp.shape, 1)
    kpf = kp.astype(F32)
    rf = r.astype(F32)
    return (jnp.where(lane < MLA_NOPE_DIM, kpf, rf).astype(BF),
            jnp.where(lane >= MLA_NOPE_DIM, kpf, rf).astype(BF))


def _qslice(qb):
    return slice(qb * TQ, (qb + 1) * TQ)


def _kslice(kb):
    return slice(kb * TR, (kb + 1) * TR)


def _with_ones(va_ref, vT):
    dv, t = vT.shape
    va_ref[0:dv, :] = vT
    va_ref[dv:dv + ONES_ROWS, :] = jnp.ones((ONES_ROWS, t), BF)


def _weighted_mean(acc):
    dv = acc.shape[0] - ONES_ROWS
    return acc[0:dv, :] * (1.0 / acc[dv:dv + 1, :])


def _causal_two_pass(nq, streams, s_ref, p_ref):
    items = [(stream, qb) for stream in streams for qb in range(nq)]

    def score_pass(n):
        (prep, scores, va_refs, _), qb = items[n]
        ops = prep(qb)
        m8 = [None] * len(va_refs)
        for kb in range((qb + 1) * TK // TR):
            for i, s in enumerate(scores(ops, qb, kb)):
                s_ref[n % 2, i, _kslice(kb), :] = s
                c = jnp.max(s.reshape(TR // 8, 8, TQ), axis=0)
                m8[i] = c if m8[i] is None else jnp.maximum(m8[i], c)
        return [jnp.max(m, axis=0, keepdims=True) for m in m8]

    def exp_pass(n, ms):
        qb = items[n][1]
        for kb in range((qb + 1) * TK // TR):
            for i in range(len(ms)):
                p_ref[n % 2, i, _kslice(kb), :] = jnp.exp2(s_ref[n % 2, i, _kslice(kb), :] - ms[i]).astype(BF)

    def value_pass(n):
        (_, _, va_refs, emit), qb = items[n]
        klen = (qb + 1) * TK
        emit(qb, [_nn(va[:, 0:klen], p_ref[n % 2, i, 0:klen, :]) for i, va in enumerate(va_refs)])

    ms = score_pass(0)
    for n in range(len(items) + 1):
        ms_next = score_pass(n + 1) if n + 1 < len(items) else None
        if n < len(items):
            exp_pass(n, ms)
        if n >= 1:
            value_pass(n - 1)
        ms = ms_next


def _diff_prompt_kernel(qT_ref, k_ref, v_ref, bias_ref, lam_ref, g_ref, o_ref, va_ref, s_ref, p_ref, *, t):
    lam = _lambda(lam_ref)
    gain = g_ref[...]

    def head_stream(j):
        hs = slice(j * DIFF_V_DIM, (j + 1) * DIFF_V_DIM)
        _with_ones(va_ref.at[j], v_ref[:, hs].astype(F32).T.astype(BF))

        def prep(qb):
            return _split_maps(qT_ref[hs, _qslice(qb)])

        def scores(ops, qb, kb):
            kblk = k_ref[_kslice(kb), hs]
            out = [_nn(kblk, q) for q in ops]
            grp, r0 = divmod(kb * TR, TK)
            if grp >= qb - 1:
                out = [s + bias_ref[j, qb - grp, r0:r0 + TR, :] for s in out]
            return out

        def emit(qb, accs):
            o_ref[hs, _qslice(qb)] = _diff_finish(_weighted_mean(accs[0]), _weighted_mean(accs[1]), lam, gain)

        return prep, scores, [va_ref.at[j], va_ref.at[j]], emit

    _causal_two_pass(t // TQ, [head_stream(j) for j in range(HEADS_PER_STEP)], s_ref, p_ref)


def _score_scratch(n_maps, t):
    return [pltpu.VMEM((2, n_maps, t, TQ), F32), pltpu.VMEM((2, n_maps, t, TQ), BF)]


def _diff_prompt(p, bias, lamp, g):
    nb, _, t = p["qTd"].shape
    n = HEADS_PER_STEP
    head_feat = pl.BlockSpec((None, n * DIFF_V_DIM, t), lambda b, h: (b, h, 0))
    head_tok = pl.BlockSpec((None, t, n * DIFF_V_DIM), lambda b, h: (b, 0, h))
    return pl.pallas_call(
        functools.partial(_diff_prompt_kernel, t=t),
        grid=(nb, DIFF_HEADS // n),
        in_specs=[head_feat, head_tok, head_tok,
                  pl.BlockSpec((n, 2, TK, TQ), lambda b, h: (h, 0, 0, 0)),
                  _const_spec(lamp.shape), _const_spec(g.shape)],
        out_specs=head_feat,
        out_shape=jax.ShapeDtypeStruct((nb, D_MODEL, t), BF),
        scratch_shapes=[pltpu.VMEM((n, DIFF_V_DIM + ONES_ROWS, t), BF)] + _score_scratch(2, t),
        compiler_params=_params(2), name="diff_prompt",
    )(p["qTd"], p["kb"], p["vb"], bias, lamp, g)


def _mla_prompt_kernel(qT_ref, kp_ref, r_ref, vT_ref, mask_ref, o_ref, k_ref, va_ref, s_ref, p_ref, *, t):
    def pair_stream(j):
        for i, keys in enumerate(_mla_keys(kp_ref[:, j * LANES:(j + 1) * LANES], r_ref[...])):
            k_ref[j, i] = keys
            vs = slice((2 * j + i) * MLA_V_DIM, (2 * j + i + 1) * MLA_V_DIM)
            _with_ones(va_ref.at[j, i], vT_ref[vs, :])

        def prep(qb):
            return [qT_ref[(2 * j + i) * LANES:(2 * j + i + 1) * LANES, _qslice(qb)] for i in range(2)]

        def scores(ops, qb, kb):
            out = [_nn(k_ref[j, i, _kslice(kb), :], ops[i]) for i in range(2)]
            grp, r0 = divmod(kb * TR, TK)
            if grp == qb:
                out = [s + mask_ref[r0:r0 + TR, :] for s in out]
            return out

        def emit(qb, accs):
            for i in range(2):
                vs = slice((2 * j + i) * MLA_V_DIM, (2 * j + i + 1) * MLA_V_DIM)
                o_ref[vs, _qslice(qb)] = _weighted_mean(accs[i]).astype(BF)

        return prep, scores, [va_ref.at[j, 0], va_ref.at[j, 1]], emit

    _causal_two_pass(t // TQ, [pair_stream(j) for j in range(HEADS_PER_STEP)], s_ref, p_ref)


def _mla_prompt(p, mask):
    nb, _, t = p["qTm"].shape
    n = HEADS_PER_STEP
    return pl.pallas_call(
        functools.partial(_mla_prompt_kernel, t=t),
        grid=(nb, MLA_HEADS // (2 * n)),
        in_specs=[pl.BlockSpec((None, 2 * n * LANES, t), lambda b, h: (b, h, 0)),
                  pl.BlockSpec((None, t, n * LANES), lambda b, h: (b, 0, h)),
                  pl.BlockSpec((None, t, LANES), lambda b, h: (b, 0, 0)),
                  pl.BlockSpec((None, 2 * n * MLA_V_DIM, t), lambda b, h: (b, h, 0)),
                  _const_spec(mask.shape)],
        out_specs=pl.BlockSpec((None, 2 * n * MLA_V_DIM, t), lambda b, h: (b, h, 0)),
        out_shape=jax.ShapeDtypeStruct((nb, D_MODEL, t), BF),
        scratch_shapes=[pltpu.VMEM((n, 2, t, LANES), BF),
                        pltpu.VMEM((n, 2, MLA_V_DIM + ONES_ROWS, t), BF)] + _score_scratch(2, t),
        compiler_params=_params(2), name="mla_prompt",
    )(p["qTm"], p["kn"], p["r"], p["vTm"], mask)


def _diff_decode_kernel(qT_ref, ck_ref, cv_ref, kn_ref, vn_ref, bc_ref, bn_ref, lam_ref, g_ref, o_ref,
                        qbd_ref, m_ref, l_ref, acc_ref, *, new, blk):
    nq = 2 * DIFF_HEADS * new
    assert nq == 2 * LANES and new & (new - 1) == 0
    c = pl.program_id(1)
    last = pl.num_programs(1) - 1
    log2 = lambda n: n.bit_length() - 1

    def lane_head(shape):
        lane = lax.broadcasted_iota(jnp.int32, shape, 1)
        return lax.shift_right_logical(lane, log2(new)) & (DIFF_HEADS - 1)

    @pl.when(c == 0)
    def _():
        m_ref[...] = jnp.full(m_ref.shape, NEG_INF, F32)
        l_ref[...] = jnp.zeros(l_ref.shape, F32)
        acc_ref[...] = jnp.zeros(acc_ref.shape, F32)
        shape = (2 * DIFF_HEAD_DIM, nq)
        row = lax.broadcasted_iota(jnp.int32, shape, 0)
        lane = lax.broadcasted_iota(jnp.int32, shape, 1)
        frame = lax.broadcasted_iota(jnp.int32, (DEC_PAD, nq), 0)
        spread = jnp.where((lax.broadcasted_iota(jnp.int32, (DEC_PAD, nq), 1) & (new - 1)) == frame,
                           1.0, 0.0).astype(BF)
        same_map = lax.shift_right_logical(row, log2(DIFF_HEAD_DIM)) == lax.shift_right_logical(lane, log2(LANES))
        for h in range(DIFF_HEADS):
            rep = _nn(qT_ref[h * DIFF_V_DIM:(h + 1) * DIFF_V_DIM, :], spread)
            qbd_ref[h * DIFF_V_DIM:(h + 1) * DIFF_V_DIM, :] = jnp.where(
                same_map & (lane_head(shape) == h), rep, 0.0).astype(BF)

    def update(keys, vals, bias):
        heads = range(DIFF_HEADS)
        s = _nn(jnp.concatenate([keys(h) for h in heads], axis=1), qbd_ref[...])
        if bias is not None:
            s = s + bias
        m_new = jnp.maximum(m_ref[...], jnp.max(s, axis=0, keepdims=True))
        alpha = jnp.exp2(m_ref[...] - m_new)
        p = jnp.exp2(s - m_new)
        l_ref[...] = alpha * l_ref[...] + jnp.sum(p, axis=0, keepdims=True)
        m_ref[...] = m_new
        pb = p.astype(BF)
        head = lane_head((DIFF_V_DIM, nq))
        full = _tn(jnp.concatenate([vals(h) for h in heads], axis=1), pb)
        pv = jnp.zeros((DIFF_V_DIM, nq), F32)
        for h in heads:
            pv = jnp.where(head == h, full[h * DIFF_V_DIM:(h + 1) * DIFF_V_DIM, :], pv)
        acc_ref[...] = alpha * acc_ref[...] + pv

    def cache_rows(ref):
        return lambda h: ref[pl.ds(h, blk, stride=DIFF_HEADS), :].astype(BF)

    @pl.when(c < last)
    def _():
        update(cache_rows(ck_ref), cache_rows(cv_ref), None)

    @pl.when(c == last)
    def _():
        update(cache_rows(ck_ref), cache_rows(cv_ref), bc_ref[...])
        hs = lambda h: slice(h * DIFF_V_DIM, (h + 1) * DIFF_V_DIM)
        update(lambda h: kn_ref[0:new, hs(h)],
               lambda h: vn_ref[pl.ds(h, new, stride=DIFF_HEADS), :].astype(BF), bn_ref[...])
        o = acc_ref[...] * (1.0 / l_ref[...])
        y = _diff_finish(o[:, 0:LANES], o[:, LANES:nq], _lambda(lam_ref), g_ref[...])
        r = lax.broadcasted_iota(jnp.int32, (LANES, DEC_PAD), 0)
        col = lax.broadcasted_iota(jnp.int32, (LANES, DEC_PAD), 1)
        for h in range(DIFF_HEADS):
            fold = jnp.where(r == col + h * new, 1.0, 0.0).astype(BF)
            o_ref[hs(h), :] = _nn(y, jnp.where(col < new, fold, jnp.zeros_like(fold))).astype(BF)


def _diff_decode(s, cache_k, cache_v, bias_c, bias_n, lamp, g, new, blk):
    nb, past = cache_k.shape[:2]
    nq = 2 * DIFF_HEADS * new
    cache_k, cache_v = (a.reshape(nb, past * DIFF_HEADS, DIFF_V_DIM) for a in (cache_k, cache_v))
    stream = lambda rows: pl.BlockSpec((None, rows, D_MODEL), lambda b, c: (b, 0, 0))
    chunk = pl.BlockSpec((None, blk * DIFF_HEADS, DIFF_V_DIM), lambda b, c: (b, c, 0))
    return pl.pallas_call(
        functools.partial(_diff_decode_kernel, new=new, blk=blk),
        grid=(nb, past // blk),
        in_specs=[pl.BlockSpec((None, D_MODEL, DEC_PAD), lambda b, c: (b, 0, 0)), chunk, chunk,
                  stream(new),
                  pl.BlockSpec((None, new * DIFF_HEADS, DIFF_V_DIM), lambda b, c: (b, 0, 0)),
                  _const_spec(bias_c.shape), _const_spec(bias_n.shape),
                  _const_spec(lamp.shape), _const_spec(g.shape)],
        out_specs=pl.BlockSpec((None, D_MODEL, DEC_PAD), lambda b, c: (b, 0, 0)),
        out_shape=jax.ShapeDtypeStruct((nb, D_MODEL, DEC_PAD), BF),
        scratch_shapes=[pltpu.VMEM((D_MODEL, nq), BF),
                        pltpu.VMEM((1, nq), F32), pltpu.VMEM((1, nq), F32),
                        pltpu.VMEM((DIFF_V_DIM, nq), F32)],
        compiler_params=_params(2), name="diff_decode",
    )(s["qTd"], cache_k, cache_v, s["kb"], s["vf_s"], bias_c, bias_n, lamp, g)


def _mla_decode_kernel(qT_ref, ckv_ref, kr_ref, ckvn_ref, krn_ref, wuk_ref, wuvT_ref, o_ref, s_ref,
                       *, past, new):
    nq = MLA_HEADS * new
    assert nq == 2 * LANES and new & (new - 1) == 0
    log2 = lambda n: n.bit_length() - 1

    def groups(shape, rows_per_group):
        r = lax.broadcasted_iota(jnp.int32, shape, 0)
        c = lax.broadcasted_iota(jnp.int32, shape, 1)
        return lax.shift_right_logical(r, log2(rows_per_group)) == lax.shift_right_logical(c, log2(new))

    def frame_match(shape, lane_axis):
        a = lax.broadcasted_iota(jnp.int32, shape, lane_axis)
        b = lax.broadcasted_iota(jnp.int32, shape, 1 - lane_axis)
        return jnp.where((a & (new - 1)) == b, 1.0, 0.0).astype(BF)

    nope, rope = [], []
    for h in range(MLA_HEADS):
        base = h * LANES
        o_n, o_r = (0, MLA_NOPE_DIM) if h % 2 == 0 else (MLA_NOPE_DIM, 0)
        nope.append(qT_ref[base + o_n:base + o_n + MLA_NOPE_DIM, :])
        rope.append(qT_ref[base + o_r:base + o_r + MLA_ROPE_DIM, :])
    spread = frame_match((DEC_PAD, nq), 1)
    qn = _nn(jnp.concatenate(nope, axis=0), spread)
    qn_bd = jnp.where(groups(qn.shape, MLA_NOPE_DIM), qn, 0.0).astype(BF)
    qr = _nn(jnp.concatenate(rope, axis=0), spread)
    qr_bd = jnp.where(groups(qr.shape, MLA_ROPE_DIM), qr, 0.0)
    qrT = jnp.sum(qr_bd.reshape(MLA_HEADS, MLA_ROPE_DIM, nq), axis=0).astype(BF)
    qlatT = _nn(wuk_ref[...], qn_bd).astype(BF)

    def keys(lo, n):
        if lo < past:
            return ckv_ref[lo:lo + n, :].astype(BF), kr_ref[lo:lo + n, :].astype(BF)
        return ckvn_ref[0:new, :].astype(BF), krn_ref[0:new, :].astype(BF)

    spans = [(c0, CACHE_BLK) for c0 in range(0, past, CACHE_BLK)] + [(past, new)]
    m = None
    for lo, n in spans:
        ck, kr = keys(lo, n)
        s = _nn(ck, qlatT) + _nn(kr, qrT)
        s_ref[lo:lo + n, :] = s
        c = jnp.max(s.reshape(n // 8, 8, nq), axis=0)
        m = c if m is None else jnp.maximum(m, c)
    m = jnp.max(m, axis=0, keepdims=True)
    l = acc = None
    for lo, n in spans:
        p = jnp.exp2(s_ref[lo:lo + n, :] - m)
        ps = jnp.sum(p.reshape(n // 8, 8, nq), axis=0)
        pv = _tn(keys(lo, n)[0], p.astype(BF))
        l, acc = (ps, pv) if l is None else (l + ps, acc + pv)
    o_latT = (acc * (1.0 / jnp.sum(l, axis=0, keepdims=True))).astype(BF)
    y = _nn(wuvT_ref[...], o_latT)
    y_bd = jnp.where(groups(y.shape, MLA_V_DIM), y, 0.0).astype(BF)
    o_ref[...] = _nn(y_bd, frame_match((nq, DEC_PAD), 0)).astype(BF)


def _mla_decode(s, cache_ckv, cache_kr, w, new):
    nb, past, _ = cache_ckv.shape
    per_stream = lambda rows, cols: pl.BlockSpec((None, rows, cols), lambda b: (b, 0, 0))
    return pl.pallas_call(
        functools.partial(_mla_decode_kernel, past=past, new=new),
        grid=(nb,),
        in_specs=[per_stream(MLA_HEADS * LANES, DEC_PAD), per_stream(past, MLA_KV_LORA),
                  per_stream(past, MLA_ROPE_DIM), per_stream(new, MLA_KV_LORA),
                  per_stream(new, MLA_ROPE_DIM), _const_spec(w["wuk"].shape),
                  _const_spec(w["wuvT"].shape)],
        out_specs=per_stream(D_MODEL, DEC_PAD),
        out_shape=jax.ShapeDtypeStruct((nb, D_MODEL, DEC_PAD), BF),
        scratch_shapes=[pltpu.VMEM((past + new, MLA_HEADS * new), F32)],
        compiler_params=_params(1), name="mla_decode",
    )(s["qTm"], cache_ckv, cache_kr, s["ckv_s"], s["kr_s"], w["wuk"], w["wuvT"])


def _post_kernel(x_ref, oTd_ref, oTm_ref, gmix_ref, wgT_ref, wodT_ref, womT_ref, woutT_ref,
                 gmlp_ref, wup_ref, wdown_ref, gfin_ref, y_ref, *, n_groups):
    rows = x_ref.shape[0] // n_groups
    groups = [slice(i * rows, (i + 1) * rows) for i in range(n_groups)]
    d = D_MODEL

    def norm_in(v, r):
        v["x"] = x_ref[r, :]
        v["hb"] = _rms(v["x"], gmix_ref[...]).astype(BF)

    def branches(v, r):
        v["gT"] = _nt(wgT_ref[...], v.pop("hb"))
        v["od"] = _nn(wodT_ref[...], oTd_ref[:, r])
        v["om"] = _nn(womT_ref[...], oTm_ref[:, r])

    def merge(v, r):
        g = jax.nn.sigmoid(v.pop("gT"))
        v["mT"] = (g[0:d, :] * v.pop("od") + g[d:2 * d, :] * v.pop("om")).astype(BF)

    def out_proj(v, r):
        v["aT"] = _nn(woutT_ref[...], v.pop("mT"))

    def residual(v, r):
        v["x1"] = v.pop("x") + v.pop("aT").T
        v["h2"] = _rms(v["x1"], gmlp_ref[...]).astype(BF)

    def up(v, r):
        v["u"] = _nn(v.pop("h2"), wup_ref[...])

    def act(v, r):
        u = jnp.maximum(v.pop("u"), 0.0)
        v["uu"] = (u * u).astype(BF)

    def down(v, r):
        v["x2"] = v.pop("x1") + _nn(v.pop("uu"), wdown_ref[...])

    def norm_out(v, r):
        y_ref[r, :] = _rms(v.pop("x2"), gfin_ref[...])

    values = [{} for _ in groups]
    for stage in (norm_in, branches, merge, out_proj, residual, up, act, down, norm_out):
        for v, r in zip(values, groups):
            stage(v, r)


def _post(x, oTd, oTm, w, tm, n_groups):
    nb, t, _ = x.shape
    tok = pl.BlockSpec((None, tm, D_MODEL), lambda b, i: (b, i, 0))
    feat = pl.BlockSpec((None, D_MODEL, tm), lambda b, i: (b, 0, i))
    weights = [w["gmix"], w["wgT"], w["wodT"], w["womT"], w["woutT"], w["gmlp"], w["wup"],
               w["wdown"], w["gfin"]]
    return pl.pallas_call(
        functools.partial(_post_kernel, n_groups=n_groups), grid=(nb, t // tm),
        in_specs=[tok, feat, feat] + [_const_spec(a.shape) for a in weights],
        out_specs=tok, out_shape=jax.ShapeDtypeStruct((nb, t, D_MODEL), F32),
        compiler_params=_params(2), name="post",
    )(x, oTd, oTm, *weights)


def _prep_weights(norm_mix, w_in, mla_q_norm, mla_w_uq, mla_kv_norm, mla_w_uk, mla_w_uv, w_o_diff,
                  w_o_mla, w_out, norm_mlp, w_up, w_down, norm_final):
    d = D_MODEL
    o_cq = 3 * d
    o_ckv = o_cq + MLA_Q_LORA
    o_kr = o_ckv + MLA_KV_LORA
    o_g = o_kr + MLA_ROPE_DIM
    half = MLA_ROPE_DIM // 2
    rot = lambda a: jnp.concatenate([-a[..., half:], a[..., :half]], axis=-1)
    zpad = lambda a, n: jnp.zeros(a.shape[:-1] + (n,), a.dtype)

    wkr = w_in[:, o_kr:o_g]
    place = lambda a: jnp.concatenate([a, zpad(a, 32), a, zpad(a, 32)], axis=-1)
    wlat = jnp.concatenate([w_in[:, o_cq:o_kr], place(wkr), place(rot(wkr))], axis=-1)

    nope = mla_w_uq[:, :, :MLA_NOPE_DIM]
    rope = mla_w_uq[:, :, MLA_NOPE_DIM:]

    def arrange(n, r):
        even = jnp.concatenate([n[:, 0::2], r[:, 0::2], zpad(r[:, 0::2], 32)], axis=-1)
        odd = jnp.concatenate([r[:, 1::2], zpad(r[:, 1::2], 32), n[:, 1::2]], axis=-1)
        both = jnp.stack([even, odd], axis=2)
        return both.reshape(MLA_Q_LORA, MLA_HEADS * LANES)

    row = lambda a: a.reshape(1, -1).astype(F32)
    return {
        "gmix": row(norm_mix), "gq": row(mla_q_norm), "gkv": row(mla_kv_norm),
        "gmlp": row(norm_mlp), "gfin": row(norm_final),
        "wqT": w_in[:, 0:d].T.astype(BF),
        "wk": w_in[:, d:2 * d].astype(BF),
        "wv": w_in[:, 2 * d:3 * d].astype(BF),
        "wlat": wlat.astype(BF),
        "wgT": w_in[:, o_g:].T.astype(BF),
        "wuqaT": arrange(nope, rope).T.astype(BF),
        "wuqbT": rot(rope).reshape(MLA_Q_LORA, MLA_HEADS * MLA_ROPE_DIM).T.astype(BF),
        "wuk": mla_w_uk.reshape(MLA_KV_LORA, d).astype(BF),
        "wuvT": mla_w_uv.reshape(MLA_KV_LORA, d).T.astype(BF),
        "wodT": w_o_diff.reshape(d, d).T.astype(BF),
        "womT": w_o_mla.reshape(d, d).T.astype(BF),
        "woutT": w_out.T.astype(BF),
        "wup": w_up.astype(BF),
        "wdown": w_down.astype(BF),
    }


def _rope_tables(pos):
    half = MLA_ROPE_DIM // 2
    inv = jnp.power(ROPE_THETA, -jnp.arange(half, dtype=F32) * 2.0 / MLA_ROPE_DIM)
    ang = pos.astype(F32)[:, None] * inv[None, :]
    cos2 = jnp.tile(jnp.cos(ang), (1, 2))
    sin2 = jnp.tile(jnp.sin(ang), (1, 2))
    t = pos.shape[0]
    z32 = jnp.zeros((t, 32), F32)
    place = lambda a: jnp.concatenate([a, z32, a, z32], axis=1)
    scale = (MLA_NOPE_DIM + MLA_ROPE_DIM) ** -0.5 * LOG2E
    ones = jnp.ones((t, MLA_NOPE_DIM), F32)
    z64 = jnp.zeros((t, MLA_NOPE_DIM), F32)
    ct = jnp.concatenate([ones, cos2, z32, cos2, z32, ones], axis=1) * scale
    st = jnp.concatenate([z64, sin2, z32, sin2, z32, z64], axis=1) * scale
    return {"cosp": place(cos2), "sinp": place(sin2), "ct": ct.T, "st": st.T}


def _score_tiles(rel_bias, k_pos_tiles, q_pos, shift_bucket):
    k_pos = jnp.stack(k_pos_tiles)
    rel = k_pos[:, :, None] - q_pos[None, None, :]
    mask = (k_pos // CHUNK)[:, :, None] <= (q_pos // CHUNK)[None, None, :]
    return _bias_tiles(rel_bias, _t5_bucket(rel).astype(jnp.int32), mask.astype(jnp.int32), shift_bucket)


def _decode_score_tiles(rel_bias, k_pos, q_pos, new, shift_bucket):
    lane = jnp.arange(2 * DIFF_HEADS * new, dtype=jnp.int32)
    qp = q_pos[lane % new]
    rel = k_pos[:, None] - qp[None, :]
    mask = (k_pos // CHUNK)[:, None] <= (qp // CHUNK)[None, :]
    tab = rel_bias[:, (lane // new) % DIFF_HEADS].astype(F32)
    return _bias_lanes(tab, _t5_bucket(rel).astype(jnp.int32), mask.astype(jnp.int32), shift_bucket)


def kernel(x_prompt, x_sample, cache_diff_k, cache_diff_v, cache_mla_ckv, cache_mla_krope, rel_bias,
           norm_mix, w_in, lam_q1, lam_k1, lam_q2, lam_k2, diff_subln, mla_q_norm, mla_w_uq, mla_kv_norm,
           mla_w_uk, mla_w_uv, w_o_diff, w_o_mla, w_out, norm_mlp, w_up, w_down, norm_final):
    assert norm_mix.shape[0] == 1, "single-layer model"
    nb, t, _ = x_prompt.shape
    ns, new, _ = x_sample.shape
    past = cache_diff_k.shape[2]
    diff_blk = min(DIFF_CACHE_BLK, past)
    assert t % TQ == 0 and past % CACHE_BLK == 0 and new <= DEC_PAD and new % 16 == 0
    assert past % diff_blk == 0 and diff_blk >= TK
    assert past % CHUNK == 0 and new <= CHUNK

    w = _prep_weights(norm_mix[0], w_in[0], mla_q_norm[0], mla_w_uq[0], mla_kv_norm[0], mla_w_uk[0],
                      mla_w_uv[0], w_o_diff[0], w_o_mla[0], w_out[0], norm_mlp[0], w_up[0], w_down[0],
                      norm_final)
    lamp = jnp.stack([lam_q1[0], lam_k1[0], lam_q2[0], lam_k2[0]]).astype(F32)
    g_sub = diff_subln[0].reshape(DIFF_V_DIM, 1).astype(F32)

    pos_p = jnp.arange(t, dtype=jnp.int32)
    p = _project(x_prompt, w, _rope_tables(pos_p), tm=512, n_groups=2)
    blk = jnp.arange(TQ, dtype=jnp.int32)
    tiles_p = _score_tiles(rel_bias, [TQ + blk, blk], TQ + blk, _far_bucket())
    mask_p = jnp.where((blk // CHUNK)[:, None] <= (blk // CHUNK)[None, :], 0.0, NEG_INF).astype(F32)
    oTd = _diff_prompt(p, tiles_p, lamp, g_sub)
    oTm = _mla_prompt(p, mask_p)
    y_prompt = _post(x_prompt, oTd, oTm, w, tm=512, n_groups=2)

    toks = ns * new
    xs = x_sample.reshape(1, toks, D_MODEL)
    pos_s = past + jnp.arange(DEC_PAD, dtype=jnp.int32)
    s = _project(xs, w, _rope_tables(jnp.tile(pos_s[:new], ns)), tm=toks, n_groups=1)
    per_stream_q = lambda a: jnp.pad(a[0].reshape(-1, ns, new).transpose(1, 0, 2),
                                     ((0, 0), (0, 0), (0, DEC_PAD - new)))
    s.update(qTd=per_stream_q(s["qTd"]), qTm=per_stream_q(s["qTm"]),
             kb=s["kb"].reshape(ns, new, D_MODEL), ckv_s=s["ckv"].reshape(ns, new, MLA_KV_LORA),
             kr_s=s["kr"].reshape(ns, new, MLA_ROPE_DIM),
             vf_s=s["vf"].reshape(ns, new * DIFF_HEADS, DIFF_V_DIM))
    bias_c = _decode_score_tiles(rel_bias, jnp.arange(past - diff_blk, past, dtype=jnp.int32), pos_s, new,
                                 _far_bucket())
    bias_n = _decode_score_tiles(rel_bias, pos_s[:new], pos_s, new, _far_bucket())
    oTd_s = _diff_decode(s, cache_diff_k[0], cache_diff_v[0], bias_c, bias_n, lamp, g_sub, new, diff_blk)
    oTm_s = _mla_decode(s, cache_mla_ckv[0], cache_mla_krope[0], w, new)
    token_major = lambda a: a[:, :, :new].transpose(1, 0, 2).reshape(1, D_MODEL, toks)
    y_sample = _post(xs, token_major(oTd_s), token_major(oTm_s), w, tm=toks, n_groups=1)

    heads = lambda a, lead: a.reshape((1,) + lead + (DIFF_HEADS, DIFF_V_DIM))
    return (y_prompt, y_sample.reshape(ns, new, D_MODEL),
            heads(p["kf"], (nb, t)), heads(p["vf"], (nb, t)), p["ckv"][None], p["kr"][None],
            heads(s["kf"], (ns, new)), heads(s["vf"], (ns, new)),
            s["ckv"].reshape(1, ns, new, MLA_KV_LORA), s["kr"].reshape(1, ns, new, MLA_ROPE_DIM))
```

```python
import functools
import math

import jax
import jax.numpy as jnp
from jax import lax
from jax.experimental import pallas as pl
from jax.experimental.pallas import tpu as pltpu

D_MODEL = 1024
CHUNK = 64
DIFF_HEADS = 8
DIFF_HEAD_DIM = 64
DIFF_V_DIM = 128
MLA_HEADS = 16
MLA_NOPE_DIM = 64
MLA_ROPE_DIM = 32
MLA_V_DIM = 64
MLA_Q_LORA = 256
MLA_KV_LORA = 256
NUM_BUCKETS = 32
MAX_DISTANCE = 128
ROPE_THETA = 10000.0
EPS = 1e-6
NEG_INF = -1e30
LAMBDA_INIT = 0.8 - 0.6 * math.exp(-0.3 * 0)

BF = jnp.bfloat16
F32 = jnp.float32

LANES = 128
TQ = 256
TK = 256
TR = 256
HEADS_PER_STEP = 2
DEC_PAD = LANES
CACHE_BLK = 512
DIFF_CACHE_BLK = 1024
VMEM_LIMIT = 56 * 1024 * 1024
LOG2E = math.log2(math.e)
ONES_ROWS = 16


def _nn(a, b):
    return jnp.dot(a, b, preferred_element_type=F32)


def _nt(a, b):
    return lax.dot_general(a, b, (((1,), (1,)), ((), ())), preferred_element_type=F32)


def _tn(a, b):
    return lax.dot_general(a, b, (((0,), (0,)), ((), ())), preferred_element_type=F32)


def _rms(x, g):
    return x * lax.rsqrt(jnp.mean(x * x, axis=-1, keepdims=True) + EPS) * g


def _const_spec(shape):
    zeros = (0,) * len(shape)
    return pl.BlockSpec(shape, lambda *_: zeros, pipeline_mode=pl.Buffered(1))


def _params(n_axes):
    return pltpu.CompilerParams(dimension_semantics=("arbitrary",) * n_axes,
                                vmem_limit_bytes=VMEM_LIMIT)


def _bias_kernel(tab_ref, bkt_ref, msk_ref, o_ref, *, shift_bucket):
    h = pl.program_id(0)
    bkt = bkt_ref[...]
    acc = jnp.zeros(bkt.shape, F32)
    for b in range(NUM_BUCKETS):
        acc = jnp.where(bkt == b, tab_ref[b, h], acc)
    if shift_bucket is not None:
        acc = acc - tab_ref[shift_bucket, h]
    o_ref[...] = jnp.where(msk_ref[...] != 0, acc * LOG2E, NEG_INF)


def _bias_tiles(rel_bias, bucket, mask, shift_bucket):
    n, r, c = bucket.shape
    return pl.pallas_call(
        functools.partial(_bias_kernel, shift_bucket=shift_bucket),
        grid=(DIFF_HEADS,),
        in_specs=[pl.BlockSpec(memory_space=pltpu.SMEM),
                  pl.BlockSpec((n, r, c), lambda h: (0, 0, 0)),
                  pl.BlockSpec((n, r, c), lambda h: (0, 0, 0))],
        out_specs=pl.BlockSpec((None, n, r, c), lambda h: (h, 0, 0, 0)),
        out_shape=jax.ShapeDtypeStruct((DIFF_HEADS, n, r, c), F32),
        compiler_params=_params(1),
        name="bias_tiles",
    )(rel_bias, bucket, mask)


def _bias_lanes_kernel(tab_ref, bkt_ref, msk_ref, o_ref, *, shift_bucket):
    bkt = bkt_ref[...]
    acc = jnp.zeros(bkt.shape, F32)
    for b in range(NUM_BUCKETS):
        acc = jnp.where(bkt == b, tab_ref[b:b + 1, :], acc)
    if shift_bucket is not None:
        acc = acc - tab_ref[shift_bucket:shift_bucket + 1, :]
    o_ref[...] = jnp.where(msk_ref[...] != 0, acc * LOG2E, NEG_INF)


def _bias_lanes(tab, bucket, mask, shift_bucket):
    return pl.pallas_call(
        functools.partial(_bias_lanes_kernel, shift_bucket=shift_bucket),
        out_shape=jax.ShapeDtypeStruct(bucket.shape, F32), name="bias_lanes",
    )(tab, bucket, mask)


def _t5_bucket(rel):
    half = NUM_BUCKETS // 2
    max_exact = half // 2
    n = jnp.abs(rel)
    nf = jnp.maximum(n, max_exact).astype(F32)
    large = max_exact + (jnp.log(nf / max_exact) / math.log(MAX_DISTANCE / max_exact)
                         * (half - max_exact)).astype(jnp.int32)
    large = jnp.minimum(large, half - 1)
    return jnp.where(rel > 0, half, 0) + jnp.where(n < max_exact, n, large)


def _far_bucket():
    half = NUM_BUCKETS // 2
    max_exact = half // 2
    assert max_exact + math.log(TK / max_exact) / math.log(MAX_DISTANCE / max_exact) * (half - max_exact) >= half
    return half - 1


def _proj_kernel(x_ref, gmix_ref, wqT_ref, wk_ref, wv_ref, wlat_ref, gq_ref, gkv_ref,
                 wuqaT_ref, wuqbT_ref, wuk_ref, wuvT_ref, cosp_ref, sinp_ref, ct_ref, st_ref,
                 qTd_ref, kf_ref, kb_ref, vf_ref, vb_ref, qTm_ref, ckv_ref, kn_ref, vTm_ref,
                 r_ref, kr_ref, *, n_groups):
    rows = x_ref.shape[0] // n_groups
    groups = [slice(i * rows, (i + 1) * rows) for i in range(n_groups)]
    pair = 2 * LANES
    rd = MLA_ROPE_DIM

    def norm_in(v, r):
        v["hb"] = _rms(x_ref[r, :], gmix_ref[...]).astype(BF)

    def latent(v, r):
        v["lat"] = _nn(v["hb"], wlat_ref[...])

    def qkv(v, r):
        hb = v.pop("hb")
        qTd_ref[:, r] = (_nt(wqT_ref[...], hb) * (DIFF_HEAD_DIM ** -0.5 * LOG2E)).astype(BF)
        k = _nn(hb, wk_ref[...])
        kb_ref[r, :] = k.astype(BF)
        val = _nn(hb, wv_ref[...])
        vb_ref[r, :] = val.astype(BF)
        for h in range(DIFF_HEADS):
            dst = pl.ds(r.start * DIFF_HEADS + h, rows, stride=DIFF_HEADS)
            kf_ref[dst, :] = k[:, h * DIFF_V_DIM:(h + 1) * DIFF_V_DIM]
            vf_ref[dst, :] = val[:, h * DIFF_V_DIM:(h + 1) * DIFF_V_DIM]

    def latent_norms(v, r):
        lat = v["lat"]
        v["cq"] = _rms(lat[:, 0:MLA_Q_LORA], gq_ref[...]).astype(BF)
        ckv = _rms(lat[:, MLA_Q_LORA:MLA_Q_LORA + MLA_KV_LORA], gkv_ref[...])
        ckv_ref[r, :] = ckv
        v["cb"] = ckv.astype(BF)

    def up(v, r):
        cq, cb = v.pop("cq"), v.pop("cb")
        v["qa"] = _nt(wuqaT_ref[...], cq)
        v["qb"] = _nt(wuqbT_ref[...], cq)
        kn_ref[r, :] = _nn(cb, wuk_ref[...]).astype(BF)
        vTm_ref[:, r] = _nt(wuvT_ref[...], cb).astype(BF)

    def rope(v, r):
        qa, qb, lat = v.pop("qa"), v.pop("qb"), v.pop("lat")
        ct = ct_ref[:, r]
        st = st_ref[:, r]
        z = jnp.zeros((MLA_NOPE_DIM, rows), F32)
        for p in range(MLA_HEADS // 2):
            sl = slice(p * pair, (p + 1) * pair)
            partner = jnp.concatenate([z, qb[2 * p * rd:(2 * p + 1) * rd, :], z[0:rd, :],
                                       qb[(2 * p + 1) * rd:(2 * p + 2) * rd, :], z[0:rd, :], z], axis=0)
            qTm_ref[sl, r] = (qa[sl, :] * ct + partner * st).astype(BF)
        o = MLA_Q_LORA + MLA_KV_LORA
        kr = lat[:, o:o + LANES] * cosp_ref[r, :] + lat[:, o + LANES:o + 2 * LANES] * sinp_ref[r, :]
        r_ref[r, :] = kr.astype(BF)
        kr_ref[r, :] = kr[:, 0:MLA_ROPE_DIM]

    values = [{} for _ in groups]
    for stage in (norm_in, latent, qkv, latent_norms, up, rope):
        for v, r in zip(values, groups):
            stage(v, r)


def _project(x, w, tabs, tm, n_groups):
    nb, t, _ = x.shape
    nt = t // tm
    tok = lambda width: pl.BlockSpec((None, tm, width), lambda b, i: (b, i, 0))
    feat = lambda rows: pl.BlockSpec((None, rows, tm), lambda b, i: (b, 0, i))
    sds = jax.ShapeDtypeStruct
    weights = [w["gmix"], w["wqT"], w["wk"], w["wv"], w["wlat"], w["gq"], w["gkv"],
               w["wuqaT"], w["wuqbT"], w["wuk"], w["wuvT"]]
    in_specs = ([tok(D_MODEL)] + [_const_spec(a.shape) for a in weights]
                + [pl.BlockSpec((tm, LANES), lambda b, i: (i, 0)),
                   pl.BlockSpec((tm, LANES), lambda b, i: (i, 0)),
                   pl.BlockSpec((2 * LANES, tm), lambda b, i: (0, i)),
                   pl.BlockSpec((2 * LANES, tm), lambda b, i: (0, i))])
    head_rows = pl.BlockSpec((None, tm * DIFF_HEADS, DIFF_V_DIM), lambda b, i: (b, i, 0))
    out_shape = [sds((nb, D_MODEL, t), BF), sds((nb, t * DIFF_HEADS, DIFF_V_DIM), F32), sds((nb, t, D_MODEL), BF),
                 sds((nb, t * DIFF_HEADS, DIFF_V_DIM), F32), sds((nb, t, D_MODEL), BF),
                 sds((nb, MLA_HEADS * LANES, t), BF), sds((nb, t, MLA_KV_LORA), F32),
                 sds((nb, t, D_MODEL), BF), sds((nb, D_MODEL, t), BF),
                 sds((nb, t, LANES), BF), sds((nb, t, MLA_ROPE_DIM), F32)]
    out_specs = [feat(D_MODEL), head_rows, tok(D_MODEL), head_rows, tok(D_MODEL),
                 feat(MLA_HEADS * LANES), tok(MLA_KV_LORA), tok(D_MODEL), feat(D_MODEL),
                 tok(LANES), tok(MLA_ROPE_DIM)]
    outs = pl.pallas_call(
        functools.partial(_proj_kernel, n_groups=n_groups), grid=(nb, nt), in_specs=in_specs,
        out_specs=out_specs, out_shape=out_shape,
        compiler_params=_params(2), name="project",
    )(x, *weights, tabs["cosp"], tabs["sinp"], tabs["ct"], tabs["st"])
    names = ["qTd", "kf", "kb", "vf", "vb", "qTm", "ckv", "kn", "vTm", "r", "kr"]
    return dict(zip(names, outs))


def _split_maps(qT):
    rows = lax.broadcasted_iota(jnp.int32, qT.shape, 0)
    qf = qT.astype(F32)
    zero = jnp.zeros_like(qf)
    return (jnp.where(rows < DIFF_HEAD_DIM, qf, zero).astype(BF),
            jnp.where(rows >= DIFF_HEAD_DIM, qf, zero).astype(BF))


def _lambda(lam_ref):
    lp = lam_ref[...]
    a = jnp.sum(lp[0:1, :] * lp[1:2, :], axis=1, keepdims=True)
    b = jnp.sum(lp[2:3, :] * lp[3:4, :], axis=1, keepdims=True)
    return jnp.exp(a) - jnp.exp(b) + LAMBDA_INIT


def _diff_finish(o1, o2, lam, g):
    o = o1 - lam * o2
    ms = jnp.mean(o * o, axis=0, keepdims=True)
    return (o * lax.rsqrt(ms + EPS) * g * (1.0 - LAMBDA_INIT)).astype(BF)


def _mla_keys(kp, r):
    lane = lax.broadcasted_iota(jnp.int32, kp.shape, 1)
    kpf = kp.astype(F32)
    rf = r.astype(F32)
    return (jnp.where(lane < MLA_NOPE_DIM, kpf, rf).astype(BF),
            jnp.where(lane >= MLA_NOPE_DIM, kpf, rf).astype(BF))


def _qslice(qb):
    return slice(qb * TQ, (qb + 1) * TQ)


def _kslice(kb):
    return slice(kb * TR, (kb + 1) * TR)


def _with_ones(va_ref, vT):
    dv, t = vT.shape
    va_ref[0:dv, :] = vT
    va_ref[dv:dv + ONES_ROWS, :] = jnp.ones((ONES_ROWS, t), BF)


def _weighted_mean(acc):
    dv = acc.shape[0] - ONES_ROWS
    return acc[0:dv, :] * (1.0 / acc[dv:dv + 1, :])


def _causal_two_pass(nq, streams, s_ref, p_ref):
    items = [(stream, qb) for stream in streams for qb in range(nq)]

    def score_pass(n):
        (prep, scores, va_refs, _), qb = items[n]
        ops = prep(qb)
        m8 = [None] * len(va_refs)
        for kb in range((qb + 1) * TK // TR):
            for i, s in enumerate(scores(ops, qb, kb)):
                s_ref[n % 2, i, _kslice(kb), :] = s
                c = jnp.max(s.reshape(TR // 8, 8, TQ), axis=0)
                m8[i] = c if m8[i] is None else jnp.maximum(m8[i], c)
        return [jnp.max(m, axis=0, keepdims=True) for m in m8]

    def exp_pass(n, ms):
        qb = items[n][1]
        for kb in range((qb + 1) * TK // TR):
            for i in range(len(ms)):
                p_ref[n % 2, i, _kslice(kb), :] = jnp.exp2(s_ref[n % 2, i, _kslice(kb), :] - ms[i]).astype(BF)

    def value_pass(n):
        (_, _, va_refs, emit), qb = items[n]
        klen = (qb + 1) * TK
        emit(qb, [_nn(va[:, 0:klen], p_ref[n % 2, i, 0:klen, :]) for i, va in enumerate(va_refs)])

    ms = score_pass(0)
    for n in range(len(items) + 1):
        ms_next = score_pass(n + 1) if n + 1 < len(items) else None
        if n < len(items):
            exp_pass(n, ms)
        if n >= 1:
            value_pass(n - 1)
        ms = ms_next


def _diff_prompt_kernel(qT_ref, k_ref, v_ref, bias_ref, lam_ref, g_ref, o_ref, va_ref, s_ref, p_ref, *, t):
    lam = _lambda(lam_ref)
    gain = g_ref[...]

    def head_stream(j):
        hs = slice(j * DIFF_V_DIM, (j + 1) * DIFF_V_DIM)
        _with_ones(va_ref.at[j], v_ref[:, hs].astype(F32).T.astype(BF))

        def prep(qb):
            return _split_maps(qT_ref[hs, _qslice(qb)])

        def scores(ops, qb, kb):
            kblk = k_ref[_kslice(kb), hs]
            out = [_nn(kblk, q) for q in ops]
            grp, r0 = divmod(kb * TR, TK)
            if grp >= qb - 1:
                out = [s + bias_ref[j, qb - grp, r0:r0 + TR, :] for s in out]
            return out

        def emit(qb, accs):
            o_ref[hs, _qslice(qb)] = _diff_finish(_weighted_mean(accs[0]), _weighted_mean(accs[1]), lam, gain)

        return prep, scores, [va_ref.at[j], va_ref.at[j]], emit

    _causal_two_pass(t // TQ, [head_stream(j) for j in range(HEADS_PER_STEP)], s_ref, p_ref)


def _score_scratch(n_maps, t):
    return [pltpu.VMEM((2, n_maps, t, TQ), F32), pltpu.VMEM((2, n_maps, t, TQ), BF)]


def _diff_prompt(p, bias, lamp, g):
    nb, _, t = p["qTd"].shape
    n = HEADS_PER_STEP
    head_feat = pl.BlockSpec((None, n * DIFF_V_DIM, t), lambda b, h: (b, h, 0))
    head_tok = pl.BlockSpec((None, t, n * DIFF_V_DIM), lambda b, h: (b, 0, h))
    return pl.pallas_call(
        functools.partial(_diff_prompt_kernel, t=t),
        grid=(nb, DIFF_HEADS // n),
        in_specs=[head_feat, head_tok, head_tok,
                  pl.BlockSpec((n, 2, TK, TQ), lambda b, h: (h, 0, 0, 0)),
                  _const_spec(lamp.shape), _const_spec(g.shape)],
        out_specs=head_feat,
        out_shape=jax.ShapeDtypeStruct((nb, D_MODEL, t), BF),
        scratch_shapes=[pltpu.VMEM((n, DIFF_V_DIM + ONES_ROWS, t), BF)] + _score_scratch(2, t),
        compiler_params=_params(2), name="diff_prompt",
    )(p["qTd"], p["kb"], p["vb"], bias, lamp, g)


def _mla_prompt_kernel(qT_ref, kp_ref, r_ref, vT_ref, mask_ref, o_ref, k_ref, va_ref, s_ref, p_ref, *, t):
    def pair_stream(j):
        for i, keys in enumerate(_mla_keys(kp_ref[:, j * LANES:(j + 1) * LANES], r_ref[...])):
            k_ref[j, i] = keys
            vs = slice((2 * j + i) * MLA_V_DIM, (2 * j + i + 1) * MLA_V_DIM)
            _with_ones(va_ref.at[j, i], vT_ref[vs, :])

        def prep(qb):
            return [qT_ref[(2 * j + i) * LANES:(2 * j + i + 1) * LANES, _qslice(qb)] for i in range(2)]

        def scores(ops, qb, kb):
            out = [_nn(k_ref[j, i, _kslice(kb), :], ops[i]) for i in range(2)]
            grp, r0 = divmod(kb * TR, TK)
            if grp == qb:
                out = [s + mask_ref[r0:r0 + TR, :] for s in out]
            return out

        def emit(qb, accs):
            for i in range(2):
                vs = slice((2 * j + i) * MLA_V_DIM, (2 * j + i + 1) * MLA_V_DIM)
                o_ref[vs, _qslice(qb)] = _weighted_mean(accs[i]).astype(BF)

        return prep, scores, [va_ref.at[j, 0], va_ref.at[j, 1]], emit

    _causal_two_pass(t // TQ, [pair_stream(j) for j in range(HEADS_PER_STEP)], s_ref, p_ref)


def _mla_prompt(p, mask):
    nb, _, t = p["qTm"].shape
    n = HEADS_PER_STEP
    return pl.pallas_call(
        functools.partial(_mla_prompt_kernel, t=t),
        grid=(nb, MLA_HEADS // (2 * n)),
        in_specs=[pl.BlockSpec((None, 2 * n * LANES, t), lambda b, h: (b, h, 0)),
                  pl.BlockSpec((None, t, n * LANES), lambda b, h: (b, 0, h)),
                  pl.BlockSpec((None, t, LANES), lambda b, h: (b, 0, 0)),
                  pl.BlockSpec((None, 2 * n * MLA_V_DIM, t), lambda b, h: (b, h, 0)),
                  _const_spec(mask.shape)],
        out_specs=pl.BlockSpec((None, 2 * n * MLA_V_DIM, t), lambda b, h: (b, h, 0)),
        out_shape=jax.ShapeDtypeStruct((nb, D_MODEL, t), BF),
        scratch_shapes=[pltpu.VMEM((n, 2, t, LANES), BF),
                        pltpu.VMEM((n, 2, MLA_V_DIM + ONES_ROWS, t), BF)] + _score_scratch(2, t),
        compiler_params=_params(2), name="mla_prompt",
    )(p["qTm"], p["kn"], p["r"], p["vTm"], mask)


def _diff_decode_kernel(qT_ref, ck_ref, cv_ref, kn_ref, vn_ref, bc_ref, bn_ref, lam_ref, g_ref, o_ref,
                        qbd_ref, m_ref, l_ref, acc_ref, *, new, blk):
    nq = 2 * DIFF_HEADS * new
    assert nq == 2 * LANES and new & (new - 1) == 0
    c = pl.program_id(1)
    last = pl.num_programs(1) - 1
    log2 = lambda n: n.bit_length() - 1

    def lane_head(shape):
        lane = lax.broadcasted_iota(jnp.int32, shape, 1)
        return lax.shift_right_logical(lane, log2(new)) & (DIFF_HEADS - 1)

    @pl.when(c == 0)
    def _():
        m_ref[...] = jnp.full(m_ref.shape, NEG_INF, F32)
        l_ref[...] = jnp.zeros(l_ref.shape, F32)
        acc_ref[...] = jnp.zeros(acc_ref.shape, F32)
        shape = (2 * DIFF_HEAD_DIM, nq)
        row = lax.broadcasted_iota(jnp.int32, shape, 0)
        lane = lax.broadcasted_iota(jnp.int32, shape, 1)
        frame = lax.broadcasted_iota(jnp.int32, (DEC_PAD, nq), 0)
        spread = jnp.where((lax.broadcasted_iota(jnp.int32, (DEC_PAD, nq), 1) & (new - 1)) == frame,
                           1.0, 0.0).astype(BF)
        same_map = lax.shift_right_logical(row, log2(DIFF_HEAD_DIM)) == lax.shift_right_logical(lane, log2(LANES))
        for h in range(DIFF_HEADS):
            rep = _nn(qT_ref[h * DIFF_V_DIM:(h + 1) * DIFF_V_DIM, :], spread)
            qbd_ref[h * DIFF_V_DIM:(h + 1) * DIFF_V_DIM, :] = jnp.where(
                same_map & (lane_head(shape) == h), rep, 0.0).astype(BF)

    def update(keys, vals, bias):
        heads = range(DIFF_HEADS)
        s = _nn(jnp.concatenate([keys(h) for h in heads], axis=1), qbd_ref[...])
        if bias is not None:
            s = s + bias
        m_new = jnp.maximum(m_ref[...], jnp.max(s, axis=0, keepdims=True))
        alpha = jnp.exp2(m_ref[...] - m_new)
        p = jnp.exp2(s - m_new)
        l_ref[...] = alpha * l_ref[...] + jnp.sum(p, axis=0, keepdims=True)
        m_ref[...] = m_new
        pb = p.astype(BF)
        head = lane_head((DIFF_V_DIM, nq))
        full = _tn(jnp.concatenate([vals(h) for h in heads], axis=1), pb)
        pv = jnp.zeros((DIFF_V_DIM, nq), F32)
        for h in heads:
            pv = jnp.where(head == h, full[h * DIFF_V_DIM:(h + 1) * DIFF_V_DIM, :], pv)
        acc_ref[...] = alpha * acc_ref[...] + pv

    def cache_rows(ref):
        return lambda h: ref[pl.ds(h, blk, stride=DIFF_HEADS), :].astype(BF)

    @pl.when(c < last)
    def _():
        update(cache_rows(ck_ref), cache_rows(cv_ref), None)

    @pl.when(c == last)
    def _():
        update(cache_rows(ck_ref), cache_rows(cv_ref), bc_ref[...])
        hs = lambda h: slice(h * DIFF_V_DIM, (h + 1) * DIFF_V_DIM)
        update(lambda h: kn_ref[0:new, hs(h)],
               lambda h: vn_ref[pl.ds(h, new, stride=DIFF_HEADS), :].astype(BF), bn_ref[...])
        o = acc_ref[...] * (1.0 / l_ref[...])
        y = _diff_finish(o[:, 0:LANES], o[:, LANES:nq], _lambda(lam_ref), g_ref[...])
        r = lax.broadcasted_iota(jnp.int32, (LANES, DEC_PAD), 0)
        col = lax.broadcasted_iota(jnp.int32, (LANES, DEC_PAD), 1)
        for h in range(DIFF_HEADS):
            fold = jnp.where(r == col + h * new, 1.0, 0.0).astype(BF)
            o_ref[hs(h), :] = _nn(y, jnp.where(col < new, fold, jnp.zeros_like(fold))).astype(BF)


def _diff_decode(s, cache_k, cache_v, bias_c, bias_n, lamp, g, new, blk):
    nb, past = cache_k.shape[:2]
    nq = 2 * DIFF_HEADS * new
    cache_k, cache_v = (a.reshape(nb, past * DIFF_HEADS, DIFF_V_DIM) for a in (cache_k, cache_v))
    stream = lambda rows: pl.BlockSpec((None, rows, D_MODEL), lambda b, c: (b, 0, 0))
    chunk = pl.BlockSpec((None, blk * DIFF_HEADS, DIFF_V_DIM), lambda b, c: (b, c, 0))
    return pl.pallas_call(
        functools.partial(_diff_decode_kernel, new=new, blk=blk),
        grid=(nb, past // blk),
        in_specs=[pl.BlockSpec((None, D_MODEL, DEC_PAD), lambda b, c: (b, 0, 0)), chunk, chunk,
                  stream(new),
                  pl.BlockSpec((None, new * DIFF_HEADS, DIFF_V_DIM), lambda b, c: (b, 0, 0)),
                  _const_spec(bias_c.shape), _const_spec(bias_n.shape),
                  _const_spec(lamp.shape), _const_spec(g.shape)],
        out_specs=pl.BlockSpec((None, D_MODEL, DEC_PAD), lambda b, c: (b, 0, 0)),
        out_shape=jax.ShapeDtypeStruct((nb, D_MODEL, DEC_PAD), BF),
        scratch_shapes=[pltpu.VMEM((D_MODEL, nq), BF),
                        pltpu.VMEM((1, nq), F32), pltpu.VMEM((1, nq), F32),
                        pltpu.VMEM((DIFF_V_DIM, nq), F32)],
        compiler_params=_params(2), name="diff_decode",
    )(s["qTd"], cache_k, cache_v, s["kb"], s["vf_s"], bias_c, bias_n, lamp, g)


def _mla_decode_kernel(qT_ref, ckv_ref, kr_ref, ckvn_ref, krn_ref, wuk_ref, wuvT_ref, o_ref, s_ref,
                       *, past, new):
    nq = MLA_HEADS * new
    assert nq == 2 * LANES and new & (new - 1) == 0
    log2 = lambda n: n.bit_length() - 1

    def groups(shape, rows_per_group):
        r = lax.broadcasted_iota(jnp.int32, shape, 0)
        c = lax.broadcasted_iota(jnp.int32, shape, 1)
        return lax.shift_right_logical(r, log2(rows_per_group)) == lax.shift_right_logical(c, log2(new))

    def frame_match(shape, lane_axis):
        a = lax.broadcasted_iota(jnp.int32, shape, lane_axis)
        b = lax.broadcasted_iota(jnp.int32, shape, 1 - lane_axis)
        return jnp.where((a & (new - 1)) == b, 1.0, 0.0).astype(BF)

    nope, rope = [], []
    for h in range(MLA_HEADS):
        base = h * LANES
        o_n, o_r = (0, MLA_NOPE_DIM) if h % 2 == 0 else (MLA_NOPE_DIM, 0)
        nope.append(qT_ref[base + o_n:base + o_n + MLA_NOPE_DIM, :])
        rope.append(qT_ref[base + o_r:base + o_r + MLA_ROPE_DIM, :])
    spread = frame_match((DEC_PAD, nq), 1)
    qn = _nn(jnp.concatenate(nope, axis=0), spread)
    qn_bd = jnp.where(groups(qn.shape, MLA_NOPE_DIM), qn, 0.0).astype(BF)
    qr = _nn(jnp.concatenate(rope, axis=0), spread)
    qr_bd = jnp.where(groups(qr.shape, MLA_ROPE_DIM), qr, 0.0)
    qrT = jnp.sum(qr_bd.reshape(MLA_HEADS, MLA_ROPE_DIM, nq), axis=0).astype(BF)
    qlatT = _nn(wuk_ref[...], qn_bd).astype(BF)

    def keys(lo, n):
        if lo < past:
            return ckv_ref[lo:lo + n, :].astype(BF), kr_ref[lo:lo + n, :].astype(BF)
        return ckvn_ref[0:new, :].astype(BF), krn_ref[0:new, :].astype(BF)

    spans = [(c0, CACHE_BLK) for c0 in range(0, past, CACHE_BLK)] + [(past, new)]
    m = None
    for lo, n in spans:
        ck, kr = keys(lo, n)
        s = _nn(ck, qlatT) + _nn(kr, qrT)
        s_ref[lo:lo + n, :] = s
        c = jnp.max(s.reshape(n // 8, 8, nq), axis=0)
        m = c if m is None else jnp.maximum(m, c)
    m = jnp.max(m, axis=0, keepdims=True)
    l = acc = None
    for lo, n in spans:
        p = jnp.exp2(s_ref[lo:lo + n, :] - m)
        ps = jnp.sum(p.reshape(n // 8, 8, nq), axis=0)
        pv = _tn(keys(lo, n)[0], p.astype(BF))
        l, acc = (ps, pv) if l is None else (l + ps, acc + pv)
    o_latT = (acc * (1.0 / jnp.sum(l, axis=0, keepdims=True))).astype(BF)
    y = _nn(wuvT_ref[...], o_latT)
    y_bd = jnp.where(groups(y.shape, MLA_V_DIM), y, 0.0).astype(BF)
    o_ref[...] = _nn(y_bd, frame_match((nq, DEC_PAD), 0)).astype(BF)


def _mla_decode(s, cache_ckv, cache_kr, w, new):
    nb, past, _ = cache_ckv.shape
    per_stream = lambda rows, cols: pl.BlockSpec((None, rows, cols), lambda b: (b, 0, 0))
    return pl.pallas_call(
        functools.partial(_mla_decode_kernel, past=past, new=new),
        grid=(nb,),
        in_specs=[per_stream(MLA_HEADS * LANES, DEC_PAD), per_stream(past, MLA_KV_LORA),
                  per_stream(past, MLA_ROPE_DIM), per_stream(new, MLA_KV_LORA),
                  per_stream(new, MLA_ROPE_DIM), _const_spec(w["wuk"].shape),
                  _const_spec(w["wuvT"].shape)],
        out_specs=per_stream(D_MODEL, DEC_PAD),
        out_shape=jax.ShapeDtypeStruct((nb, D_MODEL, DEC_PAD), BF),
        scratch_shapes=[pltpu.VMEM((past + new, MLA_HEADS * new), F32)],
        compiler_params=_params(1), name="mla_decode",
    )(s["qTm"], cache_ckv, cache_kr, s["ckv_s"], s["kr_s"], w["wuk"], w["wuvT"])


def _post_kernel(x_ref, oTd_ref, oTm_ref, gmix_ref, wgT_ref, wodT_ref, womT_ref, woutT_ref,
                 gmlp_ref, wup_ref, wdown_ref, gfin_ref, y_ref, *, n_groups):
    rows = x_ref.shape[0] // n_groups
    groups = [slice(i * rows, (i + 1) * rows) for i in range(n_groups)]
    d = D_MODEL

    def norm_in(v, r):
        v["x"] = x_ref[r, :]
        v["hb"] = _rms(v["x"], gmix_ref[...]).astype(BF)

    def branches(v, r):
        v["gT"] = _nt(wgT_ref[...], v.pop("hb"))
        v["od"] = _nn(wodT_ref[...], oTd_ref[:, r])
        v["om"] = _nn(womT_ref[...], oTm_ref[:, r])

    def merge(v, r):
        g = jax.nn.sigmoid(v.pop("gT"))
        v["mT"] = (g[0:d, :] * v.pop("od") + g[d:2 * d, :] * v.pop("om")).astype(BF)

    def out_proj(v, r):
        v["aT"] = _nn(woutT_ref[...], v.pop("mT"))

    def residual(v, r):
        v["x1"] = v.pop("x") + v.pop("aT").T
        v["h2"] = _rms(v["x1"], gmlp_ref[...]).astype(BF)

    def up(v, r):
        v["u"] = _nn(v.pop("h2"), wup_ref[...])

    def act(v, r):
        u = jnp.maximum(v.pop("u"), 0.0)
        v["uu"] = (u * u).astype(BF)

    def down(v, r):
        v["x2"] = v.pop("x1") + _nn(v.pop("uu"), wdown_ref[...])

    def norm_out(v, r):
        y_ref[r, :] = _rms(v.pop("x2"), gfin_ref[...])

    values = [{} for _ in groups]
    for stage in (norm_in, branches, merge, out_proj, residual, up, act, down, norm_out):
        for v, r in zip(values, groups):
            stage(v, r)


def _post(x, oTd, oTm, w, tm, n_groups):
    nb, t, _ = x.shape
    tok = pl.BlockSpec((None, tm, D_MODEL), lambda b, i: (b, i, 0))
    feat = pl.BlockSpec((None, D_MODEL, tm), lambda b, i: (b, 0, i))
    weights = [w["gmix"], w["wgT"], w["wodT"], w["womT"], w["woutT"], w["gmlp"], w["wup"],
               w["wdown"], w["gfin"]]
    return pl.pallas_call(
        functools.partial(_post_kernel, n_groups=n_groups), grid=(nb, t // tm),
        in_specs=[tok, feat, feat] + [_const_spec(a.shape) for a in weights],
        out_specs=tok, out_shape=jax.ShapeDtypeStruct((nb, t, D_MODEL), F32),
        compiler_params=_params(2), name="post",
    )(x, oTd, oTm, *weights)


def _prep_weights(norm_mix, w_in, mla_q_norm, mla_w_uq, mla_kv_norm, mla_w_uk, mla_w_uv, w_o_diff,
                  w_o_mla, w_out, norm_mlp, w_up, w_down, norm_final):
    d = D_MODEL
    o_cq = 3 * d
    o_ckv = o_cq + MLA_Q_LORA
    o_kr = o_ckv + MLA_KV_LORA
    o_g = o_kr + MLA_ROPE_DIM
    half = MLA_ROPE_DIM // 2
    rot = lambda a: jnp.concatenate([-a[..., half:], a[..., :half]], axis=-1)
    zpad = lambda a, n: jnp.zeros(a.shape[:-1] + (n,), a.dtype)

    wkr = w_in[:, o_kr:o_g]
    place = lambda a: jnp.concatenate([a, zpad(a, 32), a, zpad(a, 32)], axis=-1)
    wlat = jnp.concatenate([w_in[:, o_cq:o_kr], place(wkr), place(rot(wkr))], axis=-1)

    nope = mla_w_uq[:, :, :MLA_NOPE_DIM]
    rope = mla_w_uq[:, :, MLA_NOPE_DIM:]

    def arrange(n, r):
        even = jnp.concatenate([n[:, 0::2], r[:, 0::2], zpad(r[:, 0::2], 32)], axis=-1)
        odd = jnp.concatenate([r[:, 1::2], zpad(r[:, 1::2], 32), n[:, 1::2]], axis=-1)
        both = jnp.stack([even, odd], axis=2)
        return both.reshape(MLA_Q_LORA, MLA_HEADS * LANES)

    row = lambda a: a.reshape(1, -1).astype(F32)
    return {
        "gmix": row(norm_mix), "gq": row(mla_q_norm), "gkv": row(mla_kv_norm),
        "gmlp": row(norm_mlp), "gfin": row(norm_final),
        "wqT": w_in[:, 0:d].T.astype(BF),
        "wk": w_in[:, d:2 * d].astype(BF),
        "wv": w_in[:, 2 * d:3 * d].astype(BF),
        "wlat": wlat.astype(BF),
        "wgT": w_in[:, o_g:].T.astype(BF),
        "wuqaT": arrange(nope, rope).T.astype(BF),
        "wuqbT": rot(rope).reshape(MLA_Q_LORA, MLA_HEADS * MLA_ROPE_DIM).T.astype(BF),
        "wuk": mla_w_uk.reshape(MLA_KV_LORA, d).astype(BF),
        "wuvT": mla_w_uv.reshape(MLA_KV_LORA, d).T.astype(BF),
        "wodT": w_o_diff.reshape(d, d).T.astype(BF),
        "womT": w_o_mla.reshape(d, d).T.astype(BF),
        "woutT": w_out.T.astype(BF),
        "wup": w_up.astype(BF),
        "wdown": w_down.astype(BF),
    }


def _rope_tables(pos):
    half = MLA_ROPE_DIM // 2
    inv = jnp.power(ROPE_THETA, -jnp.arange(half, dtype=F32) * 2.0 / MLA_ROPE_DIM)
    ang = pos.astype(F32)[:, None] * inv[None, :]
    cos2 = jnp.tile(jnp.cos(ang), (1, 2))
    sin2 = jnp.tile(jnp.sin(ang), (1, 2))
    t = pos.shape[0]
    z32 = jnp.zeros((t, 32), F32)
    place = lambda a: jnp.concatenate([a, z32, a, z32], axis=1)
    scale = (MLA_NOPE_DIM + MLA_ROPE_DIM) ** -0.5 * LOG2E
    ones = jnp.ones((t, MLA_NOPE_DIM), F32)
    z64 = jnp.zeros((t, MLA_NOPE_DIM), F32)
    ct = jnp.concatenate([ones, cos2, z32, cos2, z32, ones], axis=1) * scale
    st = jnp.concatenate([z64, sin2, z32, sin2, z32, z64], axis=1) * scale
    return {"cosp": place(cos2), "sinp": place(sin2), "ct": ct.T, "st": st.T}


def _score_tiles(rel_bias, k_pos_tiles, q_pos, shift_bucket):
    k_pos = jnp.stack(k_pos_tiles)
    rel = k_pos[:, :, None] - q_pos[None, None, :]
    mask = (k_pos // CHUNK)[:, :, None] <= (q_pos // CHUNK)[None, None, :]
    return _bias_tiles(rel_bias, _t5_bucket(rel).astype(jnp.int32), mask.astype(jnp.int32), shift_bucket)


def _decode_score_tiles(rel_bias, k_pos, q_pos, new, shift_bucket):
    lane = jnp.arange(2 * DIFF_HEADS * new, dtype=jnp.int32)
    qp = q_pos[lane % new]
    rel = k_pos[:, None] - qp[None, :]
    mask = (k_pos // CHUNK)[:, None] <= (qp // CHUNK)[None, :]
    tab = rel_bias[:, (lane // new) % DIFF_HEADS].astype(F32)
    return _bias_lanes(tab, _t5_bucket(rel).astype(jnp.int32), mask.astype(jnp.int32), shift_bucket)


def kernel(x_prompt, x_sample, cache_diff_k, cache_diff_v, cache_mla_ckv, cache_mla_krope, rel_bias,
           norm_mix, w_in, lam_q1, lam_k1, lam_q2, lam_k2, diff_subln, mla_q_norm, mla_w_uq, mla_kv_norm,
           mla_w_uk, mla_w_uv, w_o_diff, w_o_mla, w_out, norm_mlp, w_up, w_down, norm_final):
    assert norm_mix.shape[0] == 1, "single-layer model"
    nb, t, _ = x_prompt.shape
    ns, new, _ = x_sample.shape
    past = cache_diff_k.shape[2]
    diff_blk = min(DIFF_CACHE_BLK, past)
    assert t % TQ == 0 and past % CACHE_BLK == 0 and new <= DEC_PAD and new % 16 == 0
    assert past % diff_blk == 0 and diff_blk >= TK
    assert past % CHUNK == 0 and new <= CHUNK

    w = _prep_weights(norm_mix[0], w_in[0], mla_q_norm[0], mla_w_uq[0], mla_kv_norm[0], mla_w_uk[0],
                      mla_w_uv[0], w_o_diff[0], w_o_mla[0], w_out[0], norm_mlp[0], w_up[0], w_down[0],
                      norm_final)
    lamp = jnp.stack([lam_q1[0], lam_k1[0], lam_q2[0], lam_k2[0]]).astype(F32)
    g_sub = diff_subln[0].reshape(DIFF_V_DIM, 1).astype(F32)

    pos_p = jnp.arange(t, dtype=jnp.int32)
    p = _project(x_prompt, w, _rope_tables(pos_p), tm=512, n_groups=2)
    blk = jnp.arange(TQ, dtype=jnp.int32)
    tiles_p = _score_tiles(rel_bias, [TQ + blk, blk], TQ + blk, _far_bucket())
    mask_p = jnp.where((blk // CHUNK)[:, None] <= (blk // CHUNK)[None, :], 0.0, NEG_INF).astype(F32)
    oTd = _diff_prompt(p, tiles_p, lamp, g_sub)
    oTm = _mla_prompt(p, mask_p)
    y_prompt = _post(x_prompt, oTd, oTm, w, tm=512, n_groups=2)

    toks = ns * new
    xs = x_sample.reshape(1, toks, D_MODEL)
    pos_s = past + jnp.arange(DEC_PAD, dtype=jnp.int32)
    s = _project(xs, w, _rope_tables(jnp.tile(pos_s[:new], ns)), tm=toks, n_groups=1)
    per_stream_q = lambda a: jnp.pad(a[0].reshape(-1, ns, new).transpose(1, 0, 2),
                                     ((0, 0), (0, 0), (0, DEC_PAD - new)))
    s.update(qTd=per_stream_q(s["qTd"]), qTm=per_stream_q(s["qTm"]),
             kb=s["kb"].reshape(ns, new, D_MODEL), ckv_s=s["ckv"].reshape(ns, new, MLA_KV_LORA),
             kr_s=s["kr"].reshape(ns, new, MLA_ROPE_DIM),
             vf_s=s["vf"].reshape(ns, new * DIFF_HEADS, DIFF_V_DIM))
    bias_c = _decode_score_tiles(rel_bias, jnp.arange(past - diff_blk, past, dtype=jnp.int32), pos_s, new,
                                 _far_bucket())
    bias_n = _decode_score_tiles(rel_bias, pos_s[:new], pos_s, new, _far_bucket())
    oTd_s = _diff_decode(s, cache_diff_k[0], cache_diff_v[0], bias_c, bias_n, lamp, g_sub, new, diff_blk)
    oTm_s = _mla_decode(s, cache_mla_ckv[0], cache_mla_krope[0], w, new)
    token_major = lambda a: a[:, :, :new].transpose(1, 0, 2).reshape(1, D_MODEL, toks)
    y_sample = _post(xs, token_major(oTd_s), token_major(oTm_s), w, tm=toks, n_groups=1)

    heads = lambda a, lead: a.reshape((1,) + lead + (DIFF_HEADS, DIFF_V_DIM))
    return (y_prompt, y_sample.reshape(ns, new, D_MODEL),
            heads(p["kf"], (nb, t)), heads(p["vf"], (nb, t)), p["ckv"][None], p["kr"][None],
            heads(s["kf"], (ns, new)), heads(s["vf"], (ns, new)),
            s["ckv"].reshape(1, ns, new, MLA_KV_LORA), s["kr"].reshape(1, ns, new, MLA_ROPE_DIM))
```

```python
import functools
import math

import jax
import jax.numpy as jnp
from jax import lax
from jax.experimental import pallas as pl
from jax.experimental.pallas import tpu as pltpu

D_MODEL = 1024
CHUNK = 64
DIFF_HEADS = 8
DIFF_HEAD_DIM = 64
DIFF_V_DIM = 128
MLA_HEADS = 16
MLA_NOPE_DIM = 64
MLA_ROPE_DIM = 32
MLA_V_DIM = 64
MLA_Q_LORA = 256
MLA_KV_LORA = 256
NUM_BUCKETS = 32
MAX_DISTANCE = 128
ROPE_THETA = 10000.0
EPS = 1e-6
NEG_INF = -1e30
LAMBDA_INIT = 0.8 - 0.6 * math.exp(-0.3 * 0)

BF = jnp.bfloat16
F32 = jnp.float32

LANES = 128
TQ = 256
TK = 256
TR = 256
HEADS_PER_STEP = 2
DEC_PAD = LANES
CACHE_BLK = 512
DIFF_CACHE_BLK = 1024
DECODE_SPLIT = 2
VMEM_LIMIT = 56 * 1024 * 1024
LOG2E = math.log2(math.e)
ONES_ROWS = 16


def _nn(a, b):
    return jnp.dot(a, b, preferred_element_type=F32)


def _nt(a, b):
    return lax.dot_general(a, b, (((1,), (1,)), ((), ())), preferred_element_type=F32)


def _tn(a, b):
    return lax.dot_general(a, b, (((0,), (0,)), ((), ())), preferred_element_type=F32)


def _rms(x, g):
    return x * lax.rsqrt(jnp.mean(x * x, axis=-1, keepdims=True) + EPS) * g


def _const_spec(shape):
    zeros = (0,) * len(shape)
    return pl.BlockSpec(shape, lambda *_: zeros, pipeline_mode=pl.Buffered(1))


def _params(n_axes):
    return pltpu.CompilerParams(dimension_semantics=("arbitrary",) * n_axes,
                                vmem_limit_bytes=VMEM_LIMIT)


def _bias_kernel(tab_ref, bkt_ref, msk_ref, o_ref, *, shift_bucket):
    h = pl.program_id(0)
    bkt = bkt_ref[...]
    acc = jnp.zeros(bkt.shape, F32)
    for b in range(NUM_BUCKETS):
        acc = jnp.where(bkt == b, tab_ref[b, h], acc)
    if shift_bucket is not None:
        acc = acc - tab_ref[shift_bucket, h]
    o_ref[...] = jnp.where(msk_ref[...] != 0, acc * LOG2E, NEG_INF)


def _bias_tiles(rel_bias, bucket, mask, shift_bucket):
    n, r, c = bucket.shape
    return pl.pallas_call(
        functools.partial(_bias_kernel, shift_bucket=shift_bucket),
        grid=(DIFF_HEADS,),
        in_specs=[pl.BlockSpec(memory_space=pltpu.SMEM),
                  pl.BlockSpec((n, r, c), lambda h: (0, 0, 0)),
                  pl.BlockSpec((n, r, c), lambda h: (0, 0, 0))],
        out_specs=pl.BlockSpec((None, n, r, c), lambda h: (h, 0, 0, 0)),
        out_shape=jax.ShapeDtypeStruct((DIFF_HEADS, n, r, c), F32),
        compiler_params=_params(1),
        name="bias_tiles",
    )(rel_bias, bucket, mask)


def _bias_lanes_kernel(tab_ref, bkt_ref, msk_ref, o_ref, *, shift_bucket):
    bkt = bkt_ref[...]
    acc = jnp.zeros(bkt.shape, F32)
    for b in range(NUM_BUCKETS):
        acc = jnp.where(bkt == b, tab_ref[b:b + 1, :], acc)
    if shift_bucket is not None:
        acc = acc - tab_ref[shift_bucket:shift_bucket + 1, :]
    o_ref[...] = jnp.where(msk_ref[...] != 0, acc * LOG2E, NEG_INF)


def _bias_lanes(tab, bucket, mask, shift_bucket):
    return pl.pallas_call(
        functools.partial(_bias_lanes_kernel, shift_bucket=shift_bucket),
        out_shape=jax.ShapeDtypeStruct(bucket.shape, F32), name="bias_lanes",
    )(tab, bucket, mask)


def _t5_bucket(rel):
    half = NUM_BUCKETS // 2
    max_exact = half // 2
    n = jnp.abs(rel)
    nf = jnp.maximum(n, max_exact).astype(F32)
    large = max_exact + (jnp.log(nf / max_exact) / math.log(MAX_DISTANCE / max_exact)
                         * (half - max_exact)).astype(jnp.int32)
    large = jnp.minimum(large, half - 1)
    return jnp.where(rel > 0, half, 0) + jnp.where(n < max_exact, n, large)


def _far_bucket():
    half = NUM_BUCKETS // 2
    max_exact = half // 2
    assert max_exact + math.log(TK / max_exact) / math.log(MAX_DISTANCE / max_exact) * (half - max_exact) >= half
    return half - 1


def _proj_kernel(x_ref, gmix_ref, wqT_ref, wk_ref, wv_ref, wlat_ref, gq_ref, gkv_ref,
                 wuqaT_ref, wuqbT_ref, wuk_ref, wuvT_ref, cosp_ref, sinp_ref, ct_ref, st_ref,
                 qTd_ref, kf_ref, kb_ref, vf_ref, vb_ref, qTm_ref, ckv_ref, kn_ref, vTm_ref,
                 r_ref, kr_ref, *, n_groups):
    rows = x_ref.shape[0] // n_groups
    groups = [slice(i * rows, (i + 1) * rows) for i in range(n_groups)]
    pair = 2 * LANES
    rd = MLA_ROPE_DIM

    def norm_in(v, r):
        v["hb"] = _rms(x_ref[r, :], gmix_ref[...]).astype(BF)

    def latent(v, r):
        v["lat"] = _nn(v["hb"], wlat_ref[...])

    def qkv(v, r):
        hb = v.pop("hb")
        qTd_ref[:, r] = (_nt(wqT_ref[...], hb) * (DIFF_HEAD_DIM ** -0.5 * LOG2E)).astype(BF)
        k = _nn(hb, wk_ref[...])
        kb_ref[r, :] = k.astype(BF)
        val = _nn(hb, wv_ref[...])
        vb_ref[r, :] = val.astype(BF)
        for h in range(DIFF_HEADS):
            dst = pl.ds(r.start * DIFF_HEADS + h, rows, stride=DIFF_HEADS)
            kf_ref[dst, :] = k[:, h * DIFF_V_DIM:(h + 1) * DIFF_V_DIM]
            vf_ref[dst, :] = val[:, h * DIFF_V_DIM:(h + 1) * DIFF_V_DIM]

    def latent_norms(v, r):
        lat = v["lat"]
        v["cq"] = _rms(lat[:, 0:MLA_Q_LORA], gq_ref[...]).astype(BF)
        ckv = _rms(lat[:, MLA_Q_LORA:MLA_Q_LORA + MLA_KV_LORA], gkv_ref[...])
        ckv_ref[r, :] = ckv
        v["cb"] = ckv.astype(BF)

    def up(v, r):
        cq, cb = v.pop("cq"), v.pop("cb")
        v["qa"] = _nt(wuqaT_ref[...], cq)
        v["qb"] = _nt(wuqbT_ref[...], cq)
        kn_ref[r, :] = _nn(cb, wuk_ref[...]).astype(BF)
        vTm_ref[:, r] = _nt(wuvT_ref[...], cb).astype(BF)

    def rope(v, r):
        qa, qb, lat = v.pop("qa"), v.pop("qb"), v.pop("lat")
        ct = ct_ref[:, r]
        st = st_ref[:, r]
        z = jnp.zeros((MLA_NOPE_DIM, rows), F32)
        for p in range(MLA_HEADS // 2):
            sl = slice(p * pair, (p + 1) * pair)
            partner = jnp.concatenate([z, qb[2 * p * rd:(2 * p + 1) * rd, :], z[0:rd, :],
                                       qb[(2 * p + 1) * rd:(2 * p + 2) * rd, :], z[0:rd, :], z], axis=0)
            qTm_ref[sl, r] = (qa[sl, :] * ct + partner * st).astype(BF)
        o = MLA_Q_LORA + MLA_KV_LORA
        kr = lat[:, o:o + LANES] * cosp_ref[r, :] + lat[:, o + LANES:o + 2 * LANES] * sinp_ref[r, :]
        r_ref[r, :] = kr.astype(BF)
        kr_ref[r, :] = kr[:, 0:MLA_ROPE_DIM]

    values = [{} for _ in groups]
    for stage in (norm_in, latent, qkv, latent_norms, up, rope):
        for v, r in zip(values, groups):
            stage(v, r)


def _project(x, w, tabs, tm, n_groups):
    nb, t, _ = x.shape
    nt = t // tm
    tok = lambda width: pl.BlockSpec((None, tm, width), lambda b, i: (b, i, 0))
    feat = lambda rows: pl.BlockSpec((None, rows, tm), lambda b, i: (b, 0, i))
    sds = jax.ShapeDtypeStruct
    weights = [w["gmix"], w["wqT"], w["wk"], w["wv"], w["wlat"], w["gq"], w["gkv"],
               w["wuqaT"], w["wuqbT"], w["wuk"], w["wuvT"]]
    in_specs = ([tok(D_MODEL)] + [_const_spec(a.shape) for a in weights]
                + [pl.BlockSpec((tm, LANES), lambda b, i: (i, 0)),
                   pl.BlockSpec((tm, LANES), lambda b, i: (i, 0)),
                   pl.BlockSpec((2 * LANES, tm), lambda b, i: (0, i)),
                   pl.BlockSpec((2 * LANES, tm), lambda b, i: (0, i))])
    head_rows = pl.BlockSpec((None, tm * DIFF_HEADS, DIFF_V_DIM), lambda b, i: (b, i, 0))
    out_shape = [sds((nb, D_MODEL, t), BF), sds((nb, t * DIFF_HEADS, DIFF_V_DIM), F32), sds((nb, t, D_MODEL), BF),
                 sds((nb, t * DIFF_HEADS, DIFF_V_DIM), F32), sds((nb, t, D_MODEL), BF),
                 sds((nb, MLA_HEADS * LANES, t), BF), sds((nb, t, MLA_KV_LORA), F32),
                 sds((nb, t, D_MODEL), BF), sds((nb, D_MODEL, t), BF),
                 sds((nb, t, LANES), BF), sds((nb, t, MLA_ROPE_DIM), F32)]
    out_specs = [feat(D_MODEL), head_rows, tok(D_MODEL), head_rows, tok(D_MODEL),
                 feat(MLA_HEADS * LANES), tok(MLA_KV_LORA), tok(D_MODEL), feat(D_MODEL),
                 tok(LANES), tok(MLA_ROPE_DIM)]
    outs = pl.pallas_call(
        functools.partial(_proj_kernel, n_groups=n_groups), grid=(nb, nt), in_specs=in_specs,
        out_specs=out_specs, out_shape=out_shape,
        compiler_params=_params(2), name="project",
    )(x, *weights, tabs["cosp"], tabs["sinp"], tabs["ct"], tabs["st"])
    names = ["qTd", "kf", "kb", "vf", "vb", "qTm", "ckv", "kn", "vTm", "r", "kr"]
    return dict(zip(names, outs))


def _split_maps(qT):
    rows = lax.broadcasted_iota(jnp.int32, qT.shape, 0)
    qf = qT.astype(F32)
    zero = jnp.zeros_like(qf)
    return (jnp.where(rows < DIFF_HEAD_DIM, qf, zero).astype(BF),
            jnp.where(rows >= DIFF_HEAD_DIM, qf, zero).astype(BF))


def _lambda(lam_ref):
    lp = lam_ref[...]
    a = jnp.sum(lp[0:1, :] * lp[1:2, :], axis=1, keepdims=True)
    b = jnp.sum(lp[2:3, :] * lp[3:4, :], axis=1, keepdims=True)
    return jnp.exp(a) - jnp.exp(b) + LAMBDA_INIT


def _diff_finish(o1, o2, lam, g):
    o = o1 - lam * o2
    ms = jnp.mean(o * o, axis=0, keepdims=True)
    return (o * lax.rsqrt(ms + EPS) * g * (1.0 - LAMBDA_INIT)).astype(BF)


def _mla_keys(kp, r):
    lane = lax.broadcasted_iota(jnp.int32, kp.shape, 1)
    kpf = kp.astype(F32)
    rf = r.astype(F32)
    return (jnp.where(lane < MLA_NOPE_DIM, kpf, rf).astype(BF),
            jnp.where(lane >= MLA_NOPE_DIM, kpf, rf).astype(BF))


def _qslice(qb):
    return slice(qb * TQ, (qb + 1) * TQ)


def _kslice(kb):
    return slice(kb * TR, (kb + 1) * TR)


def _with_ones(va_ref, vT):
    dv, t = vT.shape
    va_ref[0:dv, :] = vT
    va_ref[dv:dv + ONES_ROWS, :] = jnp.ones((ONES_ROWS, t), BF)


def _weighted_mean(acc):
    dv = acc.shape[0] - ONES_ROWS
    return acc[0:dv, :] * (1.0 / acc[dv:dv + 1, :])


def _causal_two_pass(nq, streams, s_ref, p_ref):
    items = [(stream, qb) for stream in streams for qb in range(nq)]

    def score_pass(n):
        (prep, scores, va_refs, _), qb = items[n]
        ops = prep(qb)
        m8 = [None] * len(va_refs)
        for kb in range((qb + 1) * TK // TR):
            for i, s in enumerate(scores(ops, qb, kb)):
                s_ref[n % 2, i, _kslice(kb), :] = s
                c = jnp.max(s.reshape(TR // 8, 8, TQ), axis=0)
                m8[i] = c if m8[i] is None else jnp.maximum(m8[i], c)
        return [jnp.max(m, axis=0, keepdims=True) for m in m8]

    def exp_pass(n, ms):
        qb = items[n][1]
        for kb in range((qb + 1) * TK // TR):
            for i in range(len(ms)):
                p_ref[n % 2, i, _kslice(kb), :] = jnp.exp2(s_ref[n % 2, i, _kslice(kb), :] - ms[i]).astype(BF)

    def value_pass(n):
        (_, _, va_refs, emit), qb = items[n]
        klen = (qb + 1) * TK
        emit(qb, [_nn(va[:, 0:klen], p_ref[n % 2, i, 0:klen, :]) for i, va in enumerate(va_refs)])

    ms = score_pass(0)
    for n in range(len(items) + 1):
        ms_next = score_pass(n + 1) if n + 1 < len(items) else None
        if n < len(items):
            exp_pass(n, ms)
        if n >= 1:
            value_pass(n - 1)
        ms = ms_next


def _diff_prompt_kernel(qT_ref, k_ref, v_ref, bias_ref, lam_ref, g_ref, o_ref, va_ref, s_ref, p_ref, *, t):
    lam = _lambda(lam_ref)
    gain = g_ref[...]

    def head_stream(j):
        hs = slice(j * DIFF_V_DIM, (j + 1) * DIFF_V_DIM)
        _with_ones(va_ref.at[j], v_ref[:, hs].astype(F32).T.astype(BF))

        def prep(qb):
            return _split_maps(qT_ref[hs, _qslice(qb)])

        def scores(ops, qb, kb):
            kblk = k_ref[_kslice(kb), hs]
            out = [_nn(kblk, q) for q in ops]
            grp, r0 = divmod(kb * TR, TK)
            if grp >= qb - 1:
                out = [s + bias_ref[j, qb - grp, r0:r0 + TR, :] for s in out]
            return out

        def emit(qb, accs):
            o_ref[hs, _qslice(qb)] = _diff_finish(_weighted_mean(accs[0]), _weighted_mean(accs[1]), lam, gain)

        return prep, scores, [va_ref.at[j], va_ref.at[j]], emit

    _causal_two_pass(t // TQ, [head_stream(j) for j in range(HEADS_PER_STEP)], s_ref, p_ref)


def _score_scratch(n_maps, t):
    return [pltpu.VMEM((2, n_maps, t, TQ), F32), pltpu.VMEM((2, n_maps, t, TQ), BF)]


def _diff_prompt(p, bias, lamp, g):
    nb, _, t = p["qTd"].shape
    n = HEADS_PER_STEP
    head_feat = pl.BlockSpec((None, n * DIFF_V_DIM, t), lambda b, h: (b, h, 0))
    head_tok = pl.BlockSpec((None, t, n * DIFF_V_DIM), lambda b, h: (b, 0, h))
    return pl.pallas_call(
        functools.partial(_diff_prompt_kernel, t=t),
        grid=(nb, DIFF_HEADS // n),
        in_specs=[head_feat, head_tok, head_tok,
                  pl.BlockSpec((n, 2, TK, TQ), lambda b, h: (h, 0, 0, 0)),
                  _const_spec(lamp.shape), _const_spec(g.shape)],
        out_specs=head_feat,
        out_shape=jax.ShapeDtypeStruct((nb, D_MODEL, t), BF),
        scratch_shapes=[pltpu.VMEM((n, DIFF_V_DIM + ONES_ROWS, t), BF)] + _score_scratch(2, t),
        compiler_params=_params(2), name="diff_prompt",
    )(p["qTd"], p["kb"], p["vb"], bias, lamp, g)


def _mla_prompt_kernel(qT_ref, kp_ref, r_ref, vT_ref, mask_ref, o_ref, k_ref, va_ref, s_ref, p_ref, *, t):
    def pair_stream(j):
        for i, keys in enumerate(_mla_keys(kp_ref[:, j * LANES:(j + 1) * LANES], r_ref[...])):
            k_ref[j, i] = keys
            vs = slice((2 * j + i) * MLA_V_DIM, (2 * j + i + 1) * MLA_V_DIM)
            _with_ones(va_ref.at[j, i], vT_ref[vs, :])

        def prep(qb):
            return [qT_ref[(2 * j + i) * LANES:(2 * j + i + 1) * LANES, _qslice(qb)] for i in range(2)]

        def scores(ops, qb, kb):
            out = [_nn(k_ref[j, i, _kslice(kb), :], ops[i]) for i in range(2)]
            grp, r0 = divmod(kb * TR, TK)
            if grp == qb:
                out = [s + mask_ref[r0:r0 + TR, :] for s in out]
            return out

        def emit(qb, accs):
            for i in range(2):
                vs = slice((2 * j + i) * MLA_V_DIM, (2 * j + i + 1) * MLA_V_DIM)
                o_ref[vs, _qslice(qb)] = _weighted_mean(accs[i]).astype(BF)

        return prep, scores, [va_ref.at[j, 0], va_ref.at[j, 1]], emit

    _causal_two_pass(t // TQ, [pair_stream(j) for j in range(HEADS_PER_STEP)], s_ref, p_ref)


def _mla_prompt(p, mask):
    nb, _, t = p["qTm"].shape
    n = HEADS_PER_STEP
    return pl.pallas_call(
        functools.partial(_mla_prompt_kernel, t=t),
        grid=(nb, MLA_HEADS // (2 * n)),
        in_specs=[pl.BlockSpec((None, 2 * n * LANES, t), lambda b, h: (b, h, 0)),
                  pl.BlockSpec((None, t, n * LANES), lambda b, h: (b, 0, h)),
                  pl.BlockSpec((None, t, LANES), lambda b, h: (b, 0, 0)),
                  pl.BlockSpec((None, 2 * n * MLA_V_DIM, t), lambda b, h: (b, h, 0)),
                  _const_spec(mask.shape)],
        out_specs=pl.BlockSpec((None, 2 * n * MLA_V_DIM, t), lambda b, h: (b, h, 0)),
        out_shape=jax.ShapeDtypeStruct((nb, D_MODEL, t), BF),
        scratch_shapes=[pltpu.VMEM((n, 2, t, LANES), BF),
                        pltpu.VMEM((n, 2, MLA_V_DIM + ONES_ROWS, t), BF)] + _score_scratch(2, t),
        compiler_params=_params(2), name="mla_prompt",
    )(p["qTm"], p["kn"], p["r"], p["vTm"], mask)


def _diff_decode_kernel(qT_ref, ck_ref, cv_ref, kn_ref, vn_ref, bc_ref, bn_ref, lam_ref, g_ref, o_ref,
                        qbd_ref, m_ref, l_ref, acc_ref, *, new, blk):
    nq = 2 * DIFF_HEADS * new
    assert nq == 2 * LANES and new & (new - 1) == 0
    c = pl.program_id(1)
    last = pl.num_programs(1) - 1
    log2 = lambda n: n.bit_length() - 1

    def lane_head(shape):
        lane = lax.broadcasted_iota(jnp.int32, shape, 1)
        return lax.shift_right_logical(lane, log2(new)) & (DIFF_HEADS - 1)

    @pl.when(c == 0)
    def _():
        m_ref[...] = jnp.full(m_ref.shape, NEG_INF, F32)
        l_ref[...] = jnp.zeros(l_ref.shape, F32)
        acc_ref[...] = jnp.zeros(acc_ref.shape, F32)
        shape = (2 * DIFF_HEAD_DIM, nq)
        row = lax.broadcasted_iota(jnp.int32, shape, 0)
        lane = lax.broadcasted_iota(jnp.int32, shape, 1)
        frame = lax.broadcasted_iota(jnp.int32, (DEC_PAD, nq), 0)
        spread = jnp.where((lax.broadcasted_iota(jnp.int32, (DEC_PAD, nq), 1) & (new - 1)) == frame,
                           1.0, 0.0).astype(BF)
        same_map = lax.shift_right_logical(row, log2(DIFF_HEAD_DIM)) == lax.shift_right_logical(lane, log2(LANES))
        for h in range(DIFF_HEADS):
            rep = _nn(qT_ref[h * DIFF_V_DIM:(h + 1) * DIFF_V_DIM, :], spread)
            qbd_ref[h * DIFF_V_DIM:(h + 1) * DIFF_V_DIM, :] = jnp.where(
                same_map & (lane_head(shape) == h), rep, 0.0).astype(BF)

    heads = range(DIFF_HEADS)

    def update(parts):
        scores = [_nn(jnp.concatenate([keys(h) for h in heads], axis=1), qbd_ref[...]) for keys, _, _ in parts]
        head = lane_head((DIFF_V_DIM, nq))
        for s, (_, vals, bias) in zip(scores, parts):
            if bias is not None:
                s = s + bias
            m_new = jnp.maximum(m_ref[...], jnp.max(s, axis=0, keepdims=True))
            alpha = jnp.exp2(m_ref[...] - m_new)
            p = jnp.exp2(s - m_new)
            l_ref[...] = alpha * l_ref[...] + jnp.sum(p, axis=0, keepdims=True)
            m_ref[...] = m_new
            full = _tn(jnp.concatenate([vals(h) for h in heads], axis=1), p.astype(BF))
            pv = jnp.zeros((DIFF_V_DIM, nq), F32)
            for h in heads:
                pv = jnp.where(head == h, full[h * DIFF_V_DIM:(h + 1) * DIFF_V_DIM, :], pv)
            acc_ref[...] = alpha * acc_ref[...] + pv

    sub = blk // DECODE_SPLIT

    def cache_parts(biased):
        rows = lambda ref, i: (lambda h: ref[pl.ds(i * sub * DIFF_HEADS + h, sub, stride=DIFF_HEADS), :].astype(BF))
        return [(rows(ck_ref, i), rows(cv_ref, i), bc_ref[i * sub:(i + 1) * sub, :] if biased else None)
                for i in range(DECODE_SPLIT)]

    @pl.when(c < last)
    def _():
        update(cache_parts(False))

    @pl.when(c == last)
    def _():
        hs = lambda h: slice(h * DIFF_V_DIM, (h + 1) * DIFF_V_DIM)
        update(cache_parts(True)
               + [(lambda h: kn_ref[0:new, hs(h)],
                   lambda h: vn_ref[pl.ds(h, new, stride=DIFF_HEADS), :].astype(BF), bn_ref[...])])
        o = acc_ref[...] * (1.0 / l_ref[...])
        y = _diff_finish(o[:, 0:LANES], o[:, LANES:nq], _lambda(lam_ref), g_ref[...])
        r = lax.broadcasted_iota(jnp.int32, (LANES, DEC_PAD), 0)
        col = lax.broadcasted_iota(jnp.int32, (LANES, DEC_PAD), 1)
        for h in range(DIFF_HEADS):
            fold = jnp.where(r == col + h * new, 1.0, 0.0).astype(BF)
            o_ref[hs(h), :] = _nn(y, jnp.where(col < new, fold, jnp.zeros_like(fold))).astype(BF)


def _diff_decode(s, cache_k, cache_v, bias_c, bias_n, lamp, g, new, blk):
    nb, past = cache_k.shape[:2]
    nq = 2 * DIFF_HEADS * new
    cache_k, cache_v = (a.reshape(nb, past * DIFF_HEADS, DIFF_V_DIM) for a in (cache_k, cache_v))
    stream = lambda rows: pl.BlockSpec((None, rows, D_MODEL), lambda b, c: (b, 0, 0))
    chunk = pl.BlockSpec((None, blk * DIFF_HEADS, DIFF_V_DIM), lambda b, c: (b, c, 0))
    return pl.pallas_call(
        functools.partial(_diff_decode_kernel, new=new, blk=blk),
        grid=(nb, past // blk),
        in_specs=[pl.BlockSpec((None, D_MODEL, DEC_PAD), lambda b, c: (b, 0, 0)), chunk, chunk,
                  stream(new),
                  pl.BlockSpec((None, new * DIFF_HEADS, DIFF_V_DIM), lambda b, c: (b, 0, 0)),
                  _const_spec(bias_c.shape), _const_spec(bias_n.shape),
                  _const_spec(lamp.shape), _const_spec(g.shape)],
        out_specs=pl.BlockSpec((None, D_MODEL, DEC_PAD), lambda b, c: (b, 0, 0)),
        out_shape=jax.ShapeDtypeStruct((nb, D_MODEL, DEC_PAD), BF),
        scratch_shapes=[pltpu.VMEM((D_MODEL, nq), BF),
                        pltpu.VMEM((1, nq), F32), pltpu.VMEM((1, nq), F32),
                        pltpu.VMEM((DIFF_V_DIM, nq), F32)],
        compiler_params=_params(2), name="diff_decode",
    )(s["qTd"], cache_k, cache_v, s["kb"], s["vf_s"], bias_c, bias_n, lamp, g)


def _mla_decode_kernel(qT_ref, ckv_ref, kr_ref, ckvn_ref, krn_ref, wuk_ref, wuvT_ref, o_ref, s_ref,
                       *, past, new):
    nq = MLA_HEADS * new
    assert nq == 2 * LANES and new & (new - 1) == 0
    log2 = lambda n: n.bit_length() - 1

    def groups(shape, rows_per_group):
        r = lax.broadcasted_iota(jnp.int32, shape, 0)
        c = lax.broadcasted_iota(jnp.int32, shape, 1)
        return lax.shift_right_logical(r, log2(rows_per_group)) == lax.shift_right_logical(c, log2(new))

    def frame_match(shape, lane_axis):
        a = lax.broadcasted_iota(jnp.int32, shape, lane_axis)
        b = lax.broadcasted_iota(jnp.int32, shape, 1 - lane_axis)
        return jnp.where((a & (new - 1)) == b, 1.0, 0.0).astype(BF)

    nope, rope = [], []
    for h in range(MLA_HEADS):
        base = h * LANES
        o_n, o_r = (0, MLA_NOPE_DIM) if h % 2 == 0 else (MLA_NOPE_DIM, 0)
        nope.append(qT_ref[base + o_n:base + o_n + MLA_NOPE_DIM, :])
        rope.append(qT_ref[base + o_r:base + o_r + MLA_ROPE_DIM, :])
    spread = frame_match((DEC_PAD, nq), 1)
    qn = _nn(jnp.concatenate(nope, axis=0), spread)
    qn_bd = jnp.where(groups(qn.shape, MLA_NOPE_DIM), qn, 0.0).astype(BF)
    qr = _nn(jnp.concatenate(rope, axis=0), spread)
    qr_bd = jnp.where(groups(qr.shape, MLA_ROPE_DIM), qr, 0.0)
    qrT = jnp.sum(qr_bd.reshape(MLA_HEADS, MLA_ROPE_DIM, nq), axis=0).astype(BF)
    qlatT = _nn(wuk_ref[...], qn_bd).astype(BF)

    def keys(lo, n):
        if lo < past:
            return ckv_ref[lo:lo + n, :].astype(BF), kr_ref[lo:lo + n, :].astype(BF)
        return ckvn_ref[0:new, :].astype(BF), krn_ref[0:new, :].astype(BF)

    spans = [(c0, CACHE_BLK) for c0 in range(0, past, CACHE_BLK)] + [(past, new)]
    m = None
    for lo, n in spans:
        ck, kr = keys(lo, n)
        s = _nn(ck, qlatT) + _nn(kr, qrT)
        s_ref[lo:lo + n, :] = s
        c = jnp.max(s.reshape(n // 8, 8, nq), axis=0)
        m = c if m is None else jnp.maximum(m, c)
    m = jnp.max(m, axis=0, keepdims=True)
    l = acc = None
    for lo, n in spans:
        p = jnp.exp2(s_ref[lo:lo + n, :] - m)
        ps = jnp.sum(p.reshape(n // 8, 8, nq), axis=0)
        pv = _tn(keys(lo, n)[0], p.astype(BF))
        l, acc = (ps, pv) if l is None else (l + ps, acc + pv)
    o_latT = (acc * (1.0 / jnp.sum(l, axis=0, keepdims=True))).astype(BF)
    y = _nn(wuvT_ref[...], o_latT)
    y_bd = jnp.where(groups(y.shape, MLA_V_DIM), y, 0.0).astype(BF)
    o_ref[...] = _nn(y_bd, frame_match((nq, DEC_PAD), 0)).astype(BF)


def _mla_decode(s, cache_ckv, cache_kr, w, new):
    nb, past, _ = cache_ckv.shape
    per_stream = lambda rows, cols: pl.BlockSpec((None, rows, cols), lambda b: (b, 0, 0))
    return pl.pallas_call(
        functools.partial(_mla_decode_kernel, past=past, new=new),
        grid=(nb,),
        in_specs=[per_stream(MLA_HEADS * LANES, DEC_PAD), per_stream(past, MLA_KV_LORA),
                  per_stream(past, MLA_ROPE_DIM), per_stream(new, MLA_KV_LORA),
                  per_stream(new, MLA_ROPE_DIM), _const_spec(w["wuk"].shape),
                  _const_spec(w["wuvT"].shape)],
        out_specs=per_stream(D_MODEL, DEC_PAD),
        out_shape=jax.ShapeDtypeStruct((nb, D_MODEL, DEC_PAD), BF),
        scratch_shapes=[pltpu.VMEM((past + new, MLA_HEADS * new), F32)],
        compiler_params=_params(1), name="mla_decode",
    )(s["qTm"], cache_ckv, cache_kr, s["ckv_s"], s["kr_s"], w["wuk"], w["wuvT"])


def _post_kernel(x_ref, oTd_ref, oTm_ref, gmix_ref, wgT_ref, wodT_ref, womT_ref, woutT_ref,
                 gmlp_ref, wup_ref, wdown_ref, gfin_ref, y_ref, *, n_groups):
    rows = x_ref.shape[0] // n_groups
    groups = [slice(i * rows, (i + 1) * rows) for i in range(n_groups)]
    d = D_MODEL

    def norm_in(v, r):
        v["x"] = x_ref[r, :]
        v["hb"] = _rms(v["x"], gmix_ref[...]).astype(BF)

    def branches(v, r):
        v["gT"] = _nt(wgT_ref[...], v.pop("hb"))
        v["od"] = _nn(wodT_ref[...], oTd_ref[:, r])
        v["om"] = _nn(womT_ref[...], oTm_ref[:, r])

    def merge(v, r):
        g = jax.nn.sigmoid(v.pop("gT"))
        v["mT"] = (g[0:d, :] * v.pop("od") + g[d:2 * d, :] * v.pop("om")).astype(BF)

    def out_proj(v, r):
        v["aT"] = _nn(woutT_ref[...], v.pop("mT"))

    def residual(v, r):
        v["x1"] = v.pop("x") + v.pop("aT").T
        v["h2"] = _rms(v["x1"], gmlp_ref[...]).astype(BF)

    def up(v, r):
        v["u"] = _nn(v.pop("h2"), wup_ref[...])

    def act(v, r):
        u = jnp.maximum(v.pop("u"), 0.0)
        v["uu"] = (u * u).astype(BF)

    def down(v, r):
        v["x2"] = v.pop("x1") + _nn(v.pop("uu"), wdown_ref[...])

    def norm_out(v, r):
        y_ref[r, :] = _rms(v.pop("x2"), gfin_ref[...])

    values = [{} for _ in groups]
    for stage in (norm_in, branches, merge, out_proj, residual, up, act, down, norm_out):
        for v, r in zip(values, groups):
            stage(v, r)


def _post(x, oTd, oTm, w, tm, n_groups):
    nb, t, _ = x.shape
    tok = pl.BlockSpec((None, tm, D_MODEL), lambda b, i: (b, i, 0))
    feat = pl.BlockSpec((None, D_MODEL, tm), lambda b, i: (b, 0, i))
    weights = [w["gmix"], w["wgT"], w["wodT"], w["womT"], w["woutT"], w["gmlp"], w["wup"],
               w["wdown"], w["gfin"]]
    return pl.pallas_call(
        functools.partial(_post_kernel, n_groups=n_groups), grid=(nb, t // tm),
        in_specs=[tok, feat, feat] + [_const_spec(a.shape) for a in weights],
        out_specs=tok, out_shape=jax.ShapeDtypeStruct((nb, t, D_MODEL), F32),
        compiler_params=_params(2), name="post",
    )(x, oTd, oTm, *weights)


def _prep_weights(norm_mix, w_in, mla_q_norm, mla_w_uq, mla_kv_norm, mla_w_uk, mla_w_uv, w_o_diff,
                  w_o_mla, w_out, norm_mlp, w_up, w_down, norm_final):
    d = D_MODEL
    o_cq = 3 * d
    o_ckv = o_cq + MLA_Q_LORA
    o_kr = o_ckv + MLA_KV_LORA
    o_g = o_kr + MLA_ROPE_DIM
    half = MLA_ROPE_DIM // 2
    rot = lambda a: jnp.concatenate([-a[..., half:], a[..., :half]], axis=-1)
    zpad = lambda a, n: jnp.zeros(a.shape[:-1] + (n,), a.dtype)

    wkr = w_in[:, o_kr:o_g]
    place = lambda a: jnp.concatenate([a, zpad(a, 32), a, zpad(a, 32)], axis=-1)
    wlat = jnp.concatenate([w_in[:, o_cq:o_kr], place(wkr), place(rot(wkr))], axis=-1)

    nope = mla_w_uq[:, :, :MLA_NOPE_DIM]
    rope = mla_w_uq[:, :, MLA_NOPE_DIM:]

    def arrange(n, r):
        even = jnp.concatenate([n[:, 0::2], r[:, 0::2], zpad(r[:, 0::2], 32)], axis=-1)
        odd = jnp.concatenate([r[:, 1::2], zpad(r[:, 1::2], 32), n[:, 1::2]], axis=-1)
        both = jnp.stack([even, odd], axis=2)
        return both.reshape(MLA_Q_LORA, MLA_HEADS * LANES)

    row = lambda a: a.reshape(1, -1).astype(F32)
    return {
        "gmix": row(norm_mix), "gq": row(mla_q_norm), "gkv": row(mla_kv_norm),
        "gmlp": row(norm_mlp), "gfin": row(norm_final),
        "wqT": w_in[:, 0:d].T.astype(BF),
        "wk": w_in[:, d:2 * d].astype(BF),
        "wv": w_in[:, 2 * d:3 * d].astype(BF),
        "wlat": wlat.astype(BF),
        "wgT": w_in[:, o_g:].T.astype(BF),
        "wuqaT": arrange(nope, rope).T.astype(BF),
        "wuqbT": rot(rope).reshape(MLA_Q_LORA, MLA_HEADS * MLA_ROPE_DIM).T.astype(BF),
        "wuk": mla_w_uk.reshape(MLA_KV_LORA, d).astype(BF),
        "wuvT": mla_w_uv.reshape(MLA_KV_LORA, d).T.astype(BF),
        "wodT": w_o_diff.reshape(d, d).T.astype(BF),
        "womT": w_o_mla.reshape(d, d).T.astype(BF),
        "woutT": w_out.T.astype(BF),
        "wup": w_up.astype(BF),
        "wdown": w_down.astype(BF),
    }


def _rope_tables(pos):
    half = MLA_ROPE_DIM // 2
    inv = jnp.power(ROPE_THETA, -jnp.arange(half, dtype=F32) * 2.0 / MLA_ROPE_DIM)
    ang = pos.astype(F32)[:, None] * inv[None, :]
    cos2 = jnp.tile(jnp.cos(ang), (1, 2))
    sin2 = jnp.tile(jnp.sin(ang), (1, 2))
    t = pos.shape[0]
    z32 = jnp.zeros((t, 32), F32)
    place = lambda a: jnp.concatenate([a, z32, a, z32], axis=1)
    scale = (MLA_NOPE_DIM + MLA_ROPE_DIM) ** -0.5 * LOG2E
    ones = jnp.ones((t, MLA_NOPE_DIM), F32)
    z64 = jnp.zeros((t, MLA_NOPE_DIM), F32)
    ct = jnp.concatenate([ones, cos2, z32, cos2, z32, ones], axis=1) * scale
    st = jnp.concatenate([z64, sin2, z32, sin2, z32, z64], axis=1) * scale
    return {"cosp": place(cos2), "sinp": place(sin2), "ct": ct.T, "st": st.T}


def _score_tiles(rel_bias, k_pos_tiles, q_pos, shift_bucket):
    k_pos = jnp.stack(k_pos_tiles)
    rel = k_pos[:, :, None] - q_pos[None, None, :]
    mask = (k_pos // CHUNK)[:, :, None] <= (q_pos // CHUNK)[None, None, :]
    return _bias_tiles(rel_bias, _t5_bucket(rel).astype(jnp.int32), mask.astype(jnp.int32), shift_bucket)


def _decode_score_tiles(rel_bias, k_pos, q_pos, new, shift_bucket):
    lane = jnp.arange(2 * DIFF_HEADS * new, dtype=jnp.int32)
    qp = q_pos[lane % new]
    rel = k_pos[:, None] - qp[None, :]
    mask = (k_pos // CHUNK)[:, None] <= (qp // CHUNK)[None, :]
    tab = rel_bias[:, (lane // new) % DIFF_HEADS].astype(F32)
    return _bias_lanes(tab, _t5_bucket(rel).astype(jnp.int32), mask.astype(jnp.int32), shift_bucket)


def kernel(x_prompt, x_sample, cache_diff_k, cache_diff_v, cache_mla_ckv, cache_mla_krope, rel_bias,
           norm_mix, w_in, lam_q1, lam_k1, lam_q2, lam_k2, diff_subln, mla_q_norm, mla_w_uq, mla_kv_norm,
           mla_w_uk, mla_w_uv, w_o_diff, w_o_mla, w_out, norm_mlp, w_up, w_down, norm_final):
    assert norm_mix.shape[0] == 1, "single-layer model"
    nb, t, _ = x_prompt.shape
    ns, new, _ = x_sample.shape
    past = cache_diff_k.shape[2]
    diff_blk = min(DIFF_CACHE_BLK, past)
    assert t % TQ == 0 and past % CACHE_BLK == 0 and new <= DEC_PAD and new % 16 == 0
    assert past % diff_blk == 0 and diff_blk >= TK
    assert past % CHUNK == 0 and new <= CHUNK

    w = _prep_weights(norm_mix[0], w_in[0], mla_q_norm[0], mla_w_uq[0], mla_kv_norm[0], mla_w_uk[0],
                      mla_w_uv[0], w_o_diff[0], w_o_mla[0], w_out[0], norm_mlp[0], w_up[0], w_down[0],
                      norm_final)
    lamp = jnp.stack([lam_q1[0], lam_k1[0], lam_q2[0], lam_k2[0]]).astype(F32)
    g_sub = diff_subln[0].reshape(DIFF_V_DIM, 1).astype(F32)

    pos_p = jnp.arange(t, dtype=jnp.int32)
    p = _project(x_prompt, w, _rope_tables(pos_p), tm=512, n_groups=2)
    blk = jnp.arange(TQ, dtype=jnp.int32)
    tiles_p = _score_tiles(rel_bias, [TQ + blk, blk], TQ + blk, _far_bucket())
    mask_p = jnp.where((blk // CHUNK)[:, None] <= (blk // CHUNK)[None, :], 0.0, NEG_INF).astype(F32)
    oTd = _diff_prompt(p, tiles_p, lamp, g_sub)
    oTm = _mla_prompt(p, mask_p)
    y_prompt = _post(x_prompt, oTd, oTm, w, tm=512, n_groups=2)

    toks = ns * new
    xs = x_sample.reshape(1, toks, D_MODEL)
    pos_s = past + jnp.arange(DEC_PAD, dtype=jnp.int32)
    s = _project(xs, w, _rope_tables(jnp.tile(pos_s[:new], ns)), tm=toks, n_groups=1)
    per_stream_q = lambda a: jnp.pad(a[0].reshape(-1, ns, new).transpose(1, 0, 2),
                                     ((0, 0), (0, 0), (0, DEC_PAD - new)))
    s.update(qTd=per_stream_q(s["qTd"]), qTm=per_stream_q(s["qTm"]),
             kb=s["kb"].reshape(ns, new, D_MODEL), ckv_s=s["ckv"].reshape(ns, new, MLA_KV_LORA),
             kr_s=s["kr"].reshape(ns, new, MLA_ROPE_DIM),
             vf_s=s["vf"].reshape(ns, new * DIFF_HEADS, DIFF_V_DIM))
    bias_c = _decode_score_tiles(rel_bias, jnp.arange(past - diff_blk, past, dtype=jnp.int32), pos_s, new,
                                 _far_bucket())
    bias_n = _decode_score_tiles(rel_bias, pos_s[:new], pos_s, new, _far_bucket())
    oTd_s = _diff_decode(s, cache_diff_k[0], cache_diff_v[0], bias_c, bias_n, lamp, g_sub, new, diff_blk)
    oTm_s = _mla_decode(s, cache_mla_ckv[0], cache_mla_krope[0], w, new)
    token_major = lambda a: a[:, :, :new].transpose(1, 0, 2).reshape(1, D_MODEL, toks)
    y_sample = _post(xs, token_major(oTd_s), token_major(oTm_s), w, tm=toks, n_groups=1)

    heads = lambda a, lead: a.reshape((1,) + lead + (DIFF_HEADS, DIFF_V_DIM))
    return (y_prompt, y_sample.reshape(ns, new, D_MODEL),
            heads(p["kf"], (nb, t)), heads(p["vf"], (nb, t)), p["ckv"][None], p["kr"][None],
            heads(s["kf"], (ns, new)), heads(s["vf"], (ns, new)),
            s["ckv"].reshape(1, ns, new, MLA_KV_LORA), s["kr"].reshape(1, ns, new, MLA_ROPE_DIM))
```

```python
import functools
import math

import jax
import jax.numpy as jnp
from jax import lax
from jax.experimental import pallas as pl
from jax.experimental.pallas import tpu as pltpu

D_MODEL = 1024
CHUNK = 64
DIFF_HEADS = 8
DIFF_HEAD_DIM = 64
DIFF_V_DIM = 128
MLA_HEADS = 16
MLA_NOPE_DIM = 64
MLA_ROPE_DIM = 32
MLA_V_DIM = 64
MLA_Q_LORA = 256
MLA_KV_LORA = 256
NUM_BUCKETS = 32
MAX_DISTANCE = 128
ROPE_THETA = 10000.0
EPS = 1e-6
NEG_INF = -1e30
LAMBDA_INIT = 0.8 - 0.6 * math.exp(-0.3 * 0)

BF = jnp.bfloat16
F32 = jnp.float32

LANES = 128
TQ = 256
TK = 256
TR = 256
HEADS_PER_STEP = 4
DEC_PAD = LANES
CACHE_BLK = 512
DIFF_CACHE_BLK = 1024
DECODE_SPLIT = 2
VMEM_LIMIT = 56 * 1024 * 1024
LOG2E = math.log2(math.e)
ONES_ROWS = 16


def _nn(a, b):
    return jnp.dot(a, b, preferred_element_type=F32)


def _nt(a, b):
    return lax.dot_general(a, b, (((1,), (1,)), ((), ())), preferred_element_type=F32)


def _tn(a, b):
    return lax.dot_general(a, b, (((0,), (0,)), ((), ())), preferred_element_type=F32)


def _rms(x, g):
    return x * lax.rsqrt(jnp.mean(x * x, axis=-1, keepdims=True) + EPS) * g


def _const_spec(shape):
    zeros = (0,) * len(shape)
    return pl.BlockSpec(shape, lambda *_: zeros, pipeline_mode=pl.Buffered(1))


def _params(n_axes):
    return pltpu.CompilerParams(dimension_semantics=("arbitrary",) * n_axes,
                                vmem_limit_bytes=VMEM_LIMIT)


def _bias_kernel(tab_ref, bkt_ref, msk_ref, o_ref, *, shift_bucket):
    h = pl.program_id(0)
    bkt = bkt_ref[...]
    acc = jnp.zeros(bkt.shape, F32)
    for b in range(NUM_BUCKETS):
        acc = jnp.where(bkt == b, tab_ref[b, h], acc)
    if shift_bucket is not None:
        acc = acc - tab_ref[shift_bucket, h]
    o_ref[...] = jnp.where(msk_ref[...] != 0, acc * LOG2E, NEG_INF)


def _bias_tiles(rel_bias, bucket, mask, shift_bucket):
    n, r, c = bucket.shape
    return pl.pallas_call(
        functools.partial(_bias_kernel, shift_bucket=shift_bucket),
        grid=(DIFF_HEADS,),
        in_specs=[pl.BlockSpec(memory_space=pltpu.SMEM),
                  pl.BlockSpec((n, r, c), lambda h: (0, 0, 0)),
                  pl.BlockSpec((n, r, c), lambda h: (0, 0, 0))],
        out_specs=pl.BlockSpec((None, n, r, c), lambda h: (h, 0, 0, 0)),
        out_shape=jax.ShapeDtypeStruct((DIFF_HEADS, n, r, c), F32),
        compiler_params=_params(1),
        name="bias_tiles",
    )(rel_bias, bucket, mask)


def _bias_lanes_kernel(tab_ref, bkt_ref, msk_ref, o_ref, *, shift_bucket):
    bkt = bkt_ref[...]
    acc = jnp.zeros(bkt.shape, F32)
    for b in range(NUM_BUCKETS):
        acc = jnp.where(bkt == b, tab_ref[b:b + 1, :], acc)
    if shift_bucket is not None:
        acc = acc - tab_ref[shift_bucket:shift_bucket + 1, :]
    o_ref[...] = jnp.where(msk_ref[...] != 0, acc * LOG2E, NEG_INF)


def _bias_lanes(tab, bucket, mask, shift_bucket):
    return pl.pallas_call(
        functools.partial(_bias_lanes_kernel, shift_bucket=shift_bucket),
        out_shape=jax.ShapeDtypeStruct(bucket.shape, F32), name="bias_lanes",
    )(tab, bucket, mask)


def _t5_bucket(rel):
    half = NUM_BUCKETS // 2
    max_exact = half // 2
    n = jnp.abs(rel)
    nf = jnp.maximum(n, max_exact).astype(F32)
    large = max_exact + (jnp.log(nf / max_exact) / math.log(MAX_DISTANCE / max_exact)
                         * (half - max_exact)).astype(jnp.int32)
    large = jnp.minimum(large, half - 1)
    return jnp.where(rel > 0, half, 0) + jnp.where(n < max_exact, n, large)


def _far_bucket():
    half = NUM_BUCKETS // 2
    max_exact = half // 2
    assert max_exact + math.log(TK / max_exact) / math.log(MAX_DISTANCE / max_exact) * (half - max_exact) >= half
    return half - 1


def _proj_kernel(x_ref, gmix_ref, wqT_ref, wk_ref, wv_ref, wlat_ref, gq_ref, gkv_ref,
                 wuqaT_ref, wuqbT_ref, wuk_ref, wuvT_ref, cosp_ref, sinp_ref, ct_ref, st_ref,
                 qTd_ref, kf_ref, kb_ref, vf_ref, vb_ref, qTm_ref, ckv_ref, kn_ref, vTm_ref,
                 r_ref, kr_ref, *, n_groups):
    rows = x_ref.shape[0] // n_groups
    groups = [slice(i * rows, (i + 1) * rows) for i in range(n_groups)]
    pair = 2 * LANES
    rd = MLA_ROPE_DIM

    def norm_in(v, r):
        v["hb"] = _rms(x_ref[r, :], gmix_ref[...]).astype(BF)

    def latent(v, r):
        v["lat"] = _nn(v["hb"], wlat_ref[...])

    def qkv(v, r):
        hb = v.pop("hb")
        qTd_ref[:, r] = (_nt(wqT_ref[...], hb) * (DIFF_HEAD_DIM ** -0.5 * LOG2E)).astype(BF)
        k = _nn(hb, wk_ref[...])
        kb_ref[r, :] = k.astype(BF)
        val = _nn(hb, wv_ref[...])
        vb_ref[r, :] = val.astype(BF)
        for h in range(DIFF_HEADS):
            dst = pl.ds(r.start * DIFF_HEADS + h, rows, stride=DIFF_HEADS)
            kf_ref[dst, :] = k[:, h * DIFF_V_DIM:(h + 1) * DIFF_V_DIM]
            vf_ref[dst, :] = val[:, h * DIFF_V_DIM:(h + 1) * DIFF_V_DIM]

    def latent_norms(v, r):
        lat = v["lat"]
        v["cq"] = _rms(lat[:, 0:MLA_Q_LORA], gq_ref[...]).astype(BF)
        ckv = _rms(lat[:, MLA_Q_LORA:MLA_Q_LORA + MLA_KV_LORA], gkv_ref[...])
        ckv_ref[r, :] = ckv
        v["cb"] = ckv.astype(BF)

    def up(v, r):
        cq, cb = v.pop("cq"), v.pop("cb")
        v["qa"] = _nt(wuqaT_ref[...], cq)
        v["qb"] = _nt(wuqbT_ref[...], cq)
        kn_ref[r, :] = _nn(cb, wuk_ref[...]).astype(BF)
        vTm_ref[:, r] = _nt(wuvT_ref[...], cb).astype(BF)

    def rope(v, r):
        qa, qb, lat = v.pop("qa"), v.pop("qb"), v.pop("lat")
        ct = ct_ref[:, r]
        st = st_ref[:, r]
        z = jnp.zeros((MLA_NOPE_DIM, rows), F32)
        for p in range(MLA_HEADS // 2):
            sl = slice(p * pair, (p + 1) * pair)
            partner = jnp.concatenate([z, qb[2 * p * rd:(2 * p + 1) * rd, :], z[0:rd, :],
                                       qb[(2 * p + 1) * rd:(2 * p + 2) * rd, :], z[0:rd, :], z], axis=0)
            qTm_ref[sl, r] = (qa[sl, :] * ct + partner * st).astype(BF)
        o = MLA_Q_LORA + MLA_KV_LORA
        kr = lat[:, o:o + LANES] * cosp_ref[r, :] + lat[:, o + LANES:o + 2 * LANES] * sinp_ref[r, :]
        r_ref[r, :] = kr.astype(BF)
        kr_ref[r, :] = kr[:, 0:MLA_ROPE_DIM]

    values = [{} for _ in groups]
    for stage in (norm_in, latent, qkv, latent_norms, up, rope):
        for v, r in zip(values, groups):
            stage(v, r)


def _project(x, w, tabs, tm, n_groups):
    nb, t, _ = x.shape
    nt = t // tm
    tok = lambda width: pl.BlockSpec((None, tm, width), lambda b, i: (b, i, 0))
    feat = lambda rows: pl.BlockSpec((None, rows, tm), lambda b, i: (b, 0, i))
    sds = jax.ShapeDtypeStruct
    weights = [w["gmix"], w["wqT"], w["wk"], w["wv"], w["wlat"], w["gq"], w["gkv"],
               w["wuqaT"], w["wuqbT"], w["wuk"], w["wuvT"]]
    in_specs = ([tok(D_MODEL)] + [_const_spec(a.shape) for a in weights]
                + [pl.BlockSpec((tm, LANES), lambda b, i: (i, 0)),
                   pl.BlockSpec((tm, LANES), lambda b, i: (i, 0)),
                   pl.BlockSpec((2 * LANES, tm), lambda b, i: (0, i)),
                   pl.BlockSpec((2 * LANES, tm), lambda b, i: (0, i))])
    head_rows = pl.BlockSpec((None, tm * DIFF_HEADS, DIFF_V_DIM), lambda b, i: (b, i, 0))
    out_shape = [sds((nb, D_MODEL, t), BF), sds((nb, t * DIFF_HEADS, DIFF_V_DIM), F32), sds((nb, t, D_MODEL), BF),
                 sds((nb, t * DIFF_HEADS, DIFF_V_DIM), F32), sds((nb, t, D_MODEL), BF),
                 sds((nb, MLA_HEADS * LANES, t), BF), sds((nb, t, MLA_KV_LORA), F32),
                 sds((nb, t, D_MODEL), BF), sds((nb, D_MODEL, t), BF),
                 sds((nb, t, LANES), BF), sds((nb, t, MLA_ROPE_DIM), F32)]
    out_specs = [feat(D_MODEL), head_rows, tok(D_MODEL), head_rows, tok(D_MODEL),
                 feat(MLA_HEADS * LANES), tok(MLA_KV_LORA), tok(D_MODEL), feat(D_MODEL),
                 tok(LANES), tok(MLA_ROPE_DIM)]
    outs = pl.pallas_call(
        functools.partial(_proj_kernel, n_groups=n_groups), grid=(nb, nt), in_specs=in_specs,
        out_specs=out_specs, out_shape=out_shape,
        compiler_params=_params(2), name="project",
    )(x, *weights, tabs["cosp"], tabs["sinp"], tabs["ct"], tabs["st"])
    names = ["qTd", "kf", "kb", "vf", "vb", "qTm", "ckv", "kn", "vTm", "r", "kr"]
    return dict(zip(names, outs))


def _split_maps(qT):
    rows = lax.broadcasted_iota(jnp.int32, qT.shape, 0)
    qf = qT.astype(F32)
    zero = jnp.zeros_like(qf)
    return (jnp.where(rows < DIFF_HEAD_DIM, qf, zero).astype(BF),
            jnp.where(rows >= DIFF_HEAD_DIM, qf, zero).astype(BF))


def _lambda(lam_ref):
    lp = lam_ref[...]
    a = jnp.sum(lp[0:1, :] * lp[1:2, :], axis=1, keepdims=True)
    b = jnp.sum(lp[2:3, :] * lp[3:4, :], axis=1, keepdims=True)
    return jnp.exp(a) - jnp.exp(b) + LAMBDA_INIT


def _diff_finish(o1, o2, lam, g):
    o = o1 - lam * o2
    ms = jnp.mean(o * o, axis=0, keepdims=True)
    return (o * lax.rsqrt(ms + EPS) * g * (1.0 - LAMBDA_INIT)).astype(BF)


def _mla_keys(kp, r):
    lane = lax.broadcasted_iota(jnp.int32, kp.shape, 1)
    kpf = kp.astype(F32)
    rf = r.astype(F32)
    return (jnp.where(lane < MLA_NOPE_DIM, kpf, rf).astype(BF),
            jnp.where(lane >= MLA_NOPE_DIM, kpf, rf).astype(BF))


def _qslice(qb):
    return slice(qb * TQ, (qb + 1) * TQ)


def _kslice(kb):
    return slice(kb * TR, (kb + 1) * TR)


def _with_ones(va_ref, vT):
    dv, t = vT.shape
    va_ref[0:dv, :] = vT
    va_ref[dv:dv + ONES_ROWS, :] = jnp.ones((ONES_ROWS, t), BF)


def _weighted_mean(acc):
    dv = acc.shape[0] - ONES_ROWS
    return acc[0:dv, :] * (1.0 / acc[dv:dv + 1, :])


def _causal_two_pass(nq, streams, s_ref, p_ref):
    items = [(stream, qb) for stream in streams for qb in range(nq)]

    def score_pass(n):
        (prep, scores, va_refs, _), qb = items[n]
        ops = prep(qb)
        m8 = [None] * len(va_refs)
        for kb in range((qb + 1) * TK // TR):
            for i, s in enumerate(scores(ops, qb, kb)):
                s_ref[n % 2, i, _kslice(kb), :] = s
                c = jnp.max(s.reshape(TR // 8, 8, TQ), axis=0)
                m8[i] = c if m8[i] is None else jnp.maximum(m8[i], c)
        return [jnp.max(m, axis=0, keepdims=True) for m in m8]

    def exp_pass(n, ms):
        qb = items[n][1]
        for kb in range((qb + 1) * TK // TR):
            for i in range(len(ms)):
                p_ref[n % 2, i, _kslice(kb), :] = jnp.exp2(s_ref[n % 2, i, _kslice(kb), :] - ms[i]).astype(BF)

    def value_pass(n):
        (_, _, va_refs, emit), qb = items[n]
        klen = (qb + 1) * TK
        emit(qb, [_nn(va[:, 0:klen], p_ref[n % 2, i, 0:klen, :]) for i, va in enumerate(va_refs)])

    ms = score_pass(0)
    for n in range(len(items) + 1):
        ms_next = score_pass(n + 1) if n + 1 < len(items) else None
        if n < len(items):
            exp_pass(n, ms)
        if n >= 1:
            value_pass(n - 1)
        ms = ms_next


def _diff_prompt_kernel(qT_ref, k_ref, v_ref, bias_ref, lam_ref, g_ref, o_ref, va_ref, s_ref, p_ref, *, t):
    lam = _lambda(lam_ref)
    gain = g_ref[...]

    def head_stream(j):
        hs = slice(j * DIFF_V_DIM, (j + 1) * DIFF_V_DIM)
        _with_ones(va_ref.at[j], v_ref[:, hs].astype(F32).T.astype(BF))

        def prep(qb):
            return _split_maps(qT_ref[hs, _qslice(qb)])

        def scores(ops, qb, kb):
            kblk = k_ref[_kslice(kb), hs]
            out = [_nn(kblk, q) for q in ops]
            grp, r0 = divmod(kb * TR, TK)
            if grp >= qb - 1:
                out = [s + bias_ref[j, qb - grp, r0:r0 + TR, :] for s in out]
            return out

        def emit(qb, accs):
            o_ref[hs, _qslice(qb)] = _diff_finish(_weighted_mean(accs[0]), _weighted_mean(accs[1]), lam, gain)

        return prep, scores, [va_ref.at[j], va_ref.at[j]], emit

    _causal_two_pass(t // TQ, [head_stream(j) for j in range(HEADS_PER_STEP)], s_ref, p_ref)


def _score_scratch(n_maps, t):
    return [pltpu.VMEM((2, n_maps, t, TQ), F32), pltpu.VMEM((2, n_maps, t, TQ), BF)]


def _diff_prompt(p, bias, lamp, g):
    nb, _, t = p["qTd"].shape
    n = HEADS_PER_STEP
    head_feat = pl.BlockSpec((None, n * DIFF_V_DIM, t), lambda b, h: (b, h, 0))
    head_tok = pl.BlockSpec((None, t, n * DIFF_V_DIM), lambda b, h: (b, 0, h))
    return pl.pallas_call(
        functools.partial(_diff_prompt_kernel, t=t),
        grid=(nb, DIFF_HEADS // n),
        in_specs=[head_feat, head_tok, head_tok,
                  pl.BlockSpec((n, 2, TK, TQ), lambda b, h: (h, 0, 0, 0)),
                  _const_spec(lamp.shape), _const_spec(g.shape)],
        out_specs=head_feat,
        out_shape=jax.ShapeDtypeStruct((nb, D_MODEL, t), BF),
        scratch_shapes=[pltpu.VMEM((n, DIFF_V_DIM + ONES_ROWS, t), BF)] + _score_scratch(2, t),
        compiler_params=_params(2), name="diff_prompt",
    )(p["qTd"], p["kb"], p["vb"], bias, lamp, g)


def _mla_prompt_kernel(qT_ref, kp_ref, r_ref, vT_ref, mask_ref, o_ref, k_ref, va_ref, s_ref, p_ref, *, t):
    def pair_stream(j):
        for i, keys in enumerate(_mla_keys(kp_ref[:, j * LANES:(j + 1) * LANES], r_ref[...])):
            k_ref[j, i] = keys
            vs = slice((2 * j + i) * MLA_V_DIM, (2 * j + i + 1) * MLA_V_DIM)
            _with_ones(va_ref.at[j, i], vT_ref[vs, :])

        def prep(qb):
            return [qT_ref[(2 * j + i) * LANES:(2 * j + i + 1) * LANES, _qslice(qb)] for i in range(2)]

        def scores(ops, qb, kb):
            out = [_nn(k_ref[j, i, _kslice(kb), :], ops[i]) for i in range(2)]
            grp, r0 = divmod(kb * TR, TK)
            if grp == qb:
                out = [s + mask_ref[r0:r0 + TR, :] for s in out]
            return out

        def emit(qb, accs):
            for i in range(2):
                vs = slice((2 * j + i) * MLA_V_DIM, (2 * j + i + 1) * MLA_V_DIM)
                o_ref[vs, _qslice(qb)] = _weighted_mean(accs[i]).astype(BF)

        return prep, scores, [va_ref.at[j, 0], va_ref.at[j, 1]], emit

    _causal_two_pass(t // TQ, [pair_stream(j) for j in range(HEADS_PER_STEP)], s_ref, p_ref)


def _mla_prompt(p, mask):
    nb, _, t = p["qTm"].shape
    n = HEADS_PER_STEP
    return pl.pallas_call(
        functools.partial(_mla_prompt_kernel, t=t),
        grid=(nb, MLA_HEADS // (2 * n)),
        in_specs=[pl.BlockSpec((None, 2 * n * LANES, t), lambda b, h: (b, h, 0)),
                  pl.BlockSpec((None, t, n * LANES), lambda b, h: (b, 0, h)),
                  pl.BlockSpec((None, t, LANES), lambda b, h: (b, 0, 0)),
                  pl.BlockSpec((None, 2 * n * MLA_V_DIM, t), lambda b, h: (b, h, 0)),
                  _const_spec(mask.shape)],
        out_specs=pl.BlockSpec((None, 2 * n * MLA_V_DIM, t), lambda b, h: (b, h, 0)),
        out_shape=jax.ShapeDtypeStruct((nb, D_MODEL, t), BF),
        scratch_shapes=[pltpu.VMEM((n, 2, t, LANES), BF),
                        pltpu.VMEM((n, 2, MLA_V_DIM + ONES_ROWS, t), BF)] + _score_scratch(2, t),
        compiler_params=_params(2), name="mla_prompt",
    )(p["qTm"], p["kn"], p["r"], p["vTm"], mask)


def _diff_decode_kernel(qT_ref, ck_ref, cv_ref, kn_ref, vn_ref, bc_ref, bn_ref, lam_ref, g_ref, o_ref,
                        qbd_ref, m_ref, l_ref, acc_ref, *, new, blk):
    nq = 2 * DIFF_HEADS * new
    assert nq == 2 * LANES and new & (new - 1) == 0
    c = pl.program_id(1)
    last = pl.num_programs(1) - 1
    log2 = lambda n: n.bit_length() - 1

    def lane_head(shape):
        lane = lax.broadcasted_iota(jnp.int32, shape, 1)
        return lax.shift_right_logical(lane, log2(new)) & (DIFF_HEADS - 1)

    @pl.when(c == 0)
    def _():
        m_ref[...] = jnp.full(m_ref.shape, NEG_INF, F32)
        l_ref[...] = jnp.zeros(l_ref.shape, F32)
        acc_ref[...] = jnp.zeros(acc_ref.shape, F32)
        shape = (2 * DIFF_HEAD_DIM, nq)
        row = lax.broadcasted_iota(jnp.int32, shape, 0)
        lane = lax.broadcasted_iota(jnp.int32, shape, 1)
        frame = lax.broadcasted_iota(jnp.int32, (DEC_PAD, nq), 0)
        spread = jnp.where((lax.broadcasted_iota(jnp.int32, (DEC_PAD, nq), 1) & (new - 1)) == frame,
                           1.0, 0.0).astype(BF)
        same_map = lax.shift_right_logical(row, log2(DIFF_HEAD_DIM)) == lax.shift_right_logical(lane, log2(LANES))
        for h in range(DIFF_HEADS):
            rep = _nn(qT_ref[h * DIFF_V_DIM:(h + 1) * DIFF_V_DIM, :], spread)
            qbd_ref[h * DIFF_V_DIM:(h + 1) * DIFF_V_DIM, :] = jnp.where(
                same_map & (lane_head(shape) == h), rep, 0.0).astype(BF)

    heads = range(DIFF_HEADS)

    def update(parts):
        scores = [_nn(jnp.concatenate([keys(h) for h in heads], axis=1), qbd_ref[...]) for keys, _, _ in parts]
        head = lane_head((DIFF_V_DIM, nq))
        for s, (_, vals, bias) in zip(scores, parts):
            if bias is not None:
                s = s + bias
            m_new = jnp.maximum(m_ref[...], jnp.max(s, axis=0, keepdims=True))
            alpha = jnp.exp2(m_ref[...] - m_new)
            p = jnp.exp2(s - m_new)
            l_ref[...] = alpha * l_ref[...] + jnp.sum(p, axis=0, keepdims=True)
            m_ref[...] = m_new
            full = _tn(jnp.concatenate([vals(h) for h in heads], axis=1), p.astype(BF))
            pv = jnp.zeros((DIFF_V_DIM, nq), F32)
            for h in heads:
                pv = jnp.where(head == h, full[h * DIFF_V_DIM:(h + 1) * DIFF_V_DIM, :], pv)
            acc_ref[...] = alpha * acc_ref[...] + pv

    sub = blk // DECODE_SPLIT

    def cache_parts(biased):
        rows = lambda ref, i: (lambda h: ref[pl.ds(i * sub * DIFF_HEADS + h, sub, stride=DIFF_HEADS), :].astype(BF))
        return [(rows(ck_ref, i), rows(cv_ref, i), bc_ref[i * sub:(i + 1) * sub, :] if biased else None)
                for i in range(DECODE_SPLIT)]

    @pl.when(c < last)
    def _():
        update(cache_parts(False))

    @pl.when(c == last)
    def _():
        hs = lambda h: slice(h * DIFF_V_DIM, (h + 1) * DIFF_V_DIM)
        update(cache_parts(True)
               + [(lambda h: kn_ref[0:new, hs(h)],
                   lambda h: vn_ref[pl.ds(h, new, stride=DIFF_HEADS), :].astype(BF), bn_ref[...])])
        o = acc_ref[...] * (1.0 / l_ref[...])
        y = _diff_finish(o[:, 0:LANES], o[:, LANES:nq], _lambda(lam_ref), g_ref[...])
        r = lax.broadcasted_iota(jnp.int32, (LANES, DEC_PAD), 0)
        col = lax.broadcasted_iota(jnp.int32, (LANES, DEC_PAD), 1)
        for h in range(DIFF_HEADS):
            fold = jnp.where(r == col + h * new, 1.0, 0.0).astype(BF)
            o_ref[hs(h), :] = _nn(y, jnp.where(col < new, fold, jnp.zeros_like(fold))).astype(BF)


def _diff_decode(s, cache_k, cache_v, bias_c, bias_n, lamp, g, new, blk):
    nb, past = cache_k.shape[:2]
    nq = 2 * DIFF_HEADS * new
    cache_k, cache_v = (a.reshape(nb, past * DIFF_HEADS, DIFF_V_DIM) for a in (cache_k, cache_v))
    stream = lambda rows: pl.BlockSpec((None, rows, D_MODEL), lambda b, c: (b, 0, 0))
    chunk = pl.BlockSpec((None, blk * DIFF_HEADS, DIFF_V_DIM), lambda b, c: (b, c, 0))
    return pl.pallas_call(
        functools.partial(_diff_decode_kernel, new=new, blk=blk),
        grid=(nb, past // blk),
        in_specs=[pl.BlockSpec((None, D_MODEL, DEC_PAD), lambda b, c: (b, 0, 0)), chunk, chunk,
                  stream(new),
                  pl.BlockSpec((None, new * DIFF_HEADS, DIFF_V_DIM), lambda b, c: (b, 0, 0)),
                  _const_spec(bias_c.shape), _const_spec(bias_n.shape),
                  _const_spec(lamp.shape), _const_spec(g.shape)],
        out_specs=pl.BlockSpec((None, D_MODEL, DEC_PAD), lambda b, c: (b, 0, 0)),
        out_shape=jax.ShapeDtypeStruct((nb, D_MODEL, DEC_PAD), BF),
        scratch_shapes=[pltpu.VMEM((D_MODEL, nq), BF),
                        pltpu.VMEM((1, nq), F32), pltpu.VMEM((1, nq), F32),
                        pltpu.VMEM((DIFF_V_DIM, nq), F32)],
        compiler_params=_params(2), name="diff_decode",
    )(s["qTd"], cache_k, cache_v, s["kb"], s["vf_s"], bias_c, bias_n, lamp, g)


def _mla_decode_kernel(qT_ref, ckv_ref, kr_ref, ckvn_ref, krn_ref, wuk_ref, wuvT_ref, o_ref, s_ref,
                       *, past, new):
    nq = MLA_HEADS * new
    assert nq == 2 * LANES and new & (new - 1) == 0
    log2 = lambda n: n.bit_length() - 1

    def groups(shape, rows_per_group):
        r = lax.broadcasted_iota(jnp.int32, shape, 0)
        c = lax.broadcasted_iota(jnp.int32, shape, 1)
        return lax.shift_right_logical(r, log2(rows_per_group)) == lax.shift_right_logical(c, log2(new))

    def frame_match(shape, lane_axis):
        a = lax.broadcasted_iota(jnp.int32, shape, lane_axis)
        b = lax.broadcasted_iota(jnp.int32, shape, 1 - lane_axis)
        return jnp.where((a & (new - 1)) == b, 1.0, 0.0).astype(BF)

    nope, rope = [], []
    for h in range(MLA_HEADS):
        base = h * LANES
        o_n, o_r = (0, MLA_NOPE_DIM) if h % 2 == 0 else (MLA_NOPE_DIM, 0)
        nope.append(qT_ref[base + o_n:base + o_n + MLA_NOPE_DIM, :])
        rope.append(qT_ref[base + o_r:base + o_r + MLA_ROPE_DIM, :])
    spread = frame_match((DEC_PAD, nq), 1)
    qn = _nn(jnp.concatenate(nope, axis=0), spread)
    qn_bd = jnp.where(groups(qn.shape, MLA_NOPE_DIM), qn, 0.0).astype(BF)
    qr = _nn(jnp.concatenate(rope, axis=0), spread)
    qr_bd = jnp.where(groups(qr.shape, MLA_ROPE_DIM), qr, 0.0)
    qrT = jnp.sum(qr_bd.reshape(MLA_HEADS, MLA_ROPE_DIM, nq), axis=0).astype(BF)
    qlatT = _nn(wuk_ref[...], qn_bd).astype(BF)

    def keys(lo, n):
        if lo < past:
            return ckv_ref[lo:lo + n, :].astype(BF), kr_ref[lo:lo + n, :].astype(BF)
        return ckvn_ref[0:new, :].astype(BF), krn_ref[0:new, :].astype(BF)

    spans = [(c0, CACHE_BLK) for c0 in range(0, past, CACHE_BLK)] + [(past, new)]
    m = None
    for lo, n in spans:
        ck, kr = keys(lo, n)
        s = _nn(ck, qlatT) + _nn(kr, qrT)
        s_ref[lo:lo + n, :] = s
        c = jnp.max(s.reshape(n // 8, 8, nq), axis=0)
        m = c if m is None else jnp.maximum(m, c)
    m = jnp.max(m, axis=0, keepdims=True)
    l = acc = None
    for lo, n in spans:
        p = jnp.exp2(s_ref[lo:lo + n, :] - m)
        ps = jnp.sum(p.reshape(n // 8, 8, nq), axis=0)
        pv = _tn(keys(lo, n)[0], p.astype(BF))
        l, acc = (ps, pv) if l is None else (l + ps, acc + pv)
    o_latT = (acc * (1.0 / jnp.sum(l, axis=0, keepdims=True))).astype(BF)
    y = _nn(wuvT_ref[...], o_latT)
    y_bd = jnp.where(groups(y.shape, MLA_V_DIM), y, 0.0).astype(BF)
    o_ref[...] = _nn(y_bd, frame_match((nq, DEC_PAD), 0)).astype(BF)


def _mla_decode(s, cache_ckv, cache_kr, w, new):
    nb, past, _ = cache_ckv.shape
    per_stream = lambda rows, cols: pl.BlockSpec((None, rows, cols), lambda b: (b, 0, 0))
    return pl.pallas_call(
        functools.partial(_mla_decode_kernel, past=past, new=new),
        grid=(nb,),
        in_specs=[per_stream(MLA_HEADS * LANES, DEC_PAD), per_stream(past, MLA_KV_LORA),
                  per_stream(past, MLA_ROPE_DIM), per_stream(new, MLA_KV_LORA),
                  per_stream(new, MLA_ROPE_DIM), _const_spec(w["wuk"].shape),
                  _const_spec(w["wuvT"].shape)],
        out_specs=per_stream(D_MODEL, DEC_PAD),
        out_shape=jax.ShapeDtypeStruct((nb, D_MODEL, DEC_PAD), BF),
        scratch_shapes=[pltpu.VMEM((past + new, MLA_HEADS * new), F32)],
        compiler_params=_params(1), name="mla_decode",
    )(s["qTm"], cache_ckv, cache_kr, s["ckv_s"], s["kr_s"], w["wuk"], w["wuvT"])


def _post_kernel(x_ref, oTd_ref, oTm_ref, gmix_ref, wgT_ref, wodT_ref, womT_ref, woutT_ref,
                 gmlp_ref, wup_ref, wdown_ref, gfin_ref, y_ref, *, n_groups):
    rows = x_ref.shape[0] // n_groups
    groups = [slice(i * rows, (i + 1) * rows) for i in range(n_groups)]
    d = D_MODEL

    def norm_in(v, r):
        v["x"] = x_ref[r, :]
        v["hb"] = _rms(v["x"], gmix_ref[...]).astype(BF)

    def branches(v, r):
        v["gT"] = _nt(wgT_ref[...], v.pop("hb"))
        v["od"] = _nn(wodT_ref[...], oTd_ref[:, r])
        v["om"] = _nn(womT_ref[...], oTm_ref[:, r])

    def merge(v, r):
        g = jax.nn.sigmoid(v.pop("gT"))
        v["mT"] = (g[0:d, :] * v.pop("od") + g[d:2 * d, :] * v.pop("om")).astype(BF)

    def out_proj(v, r):
        v["aT"] = _nn(woutT_ref[...], v.pop("mT"))

    def residual(v, r):
        v["x1"] = v.pop("x") + v.pop("aT").T
        v["h2"] = _rms(v["x1"], gmlp_ref[...]).astype(BF)

    def up(v, r):
        v["u"] = _nn(v.pop("h2"), wup_ref[...])

    def act(v, r):
        u = jnp.maximum(v.pop("u"), 0.0)
        v["uu"] = (u * u).astype(BF)

    def down(v, r):
        v["x2"] = v.pop("x1") + _nn(v.pop("uu"), wdown_ref[...])

    def norm_out(v, r):
        y_ref[r, :] = _rms(v.pop("x2"), gfin_ref[...])

    values = [{} for _ in groups]
    for stage in (norm_in, branches, merge, out_proj, residual, up, act, down, norm_out):
        for v, r in zip(values, groups):
            stage(v, r)


def _post(x, oTd, oTm, w, tm, n_groups):
    nb, t, _ = x.shape
    tok = pl.BlockSpec((None, tm, D_MODEL), lambda b, i: (b, i, 0))
    feat = pl.BlockSpec((None, D_MODEL, tm), lambda b, i: (b, 0, i))
    weights = [w["gmix"], w["wgT"], w["wodT"], w["womT"], w["woutT"], w["gmlp"], w["wup"],
               w["wdown"], w["gfin"]]
    return pl.pallas_call(
        functools.partial(_post_kernel, n_groups=n_groups), grid=(nb, t // tm),
        in_specs=[tok, feat, feat] + [_const_spec(a.shape) for a in weights],
        out_specs=tok, out_shape=jax.ShapeDtypeStruct((nb, t, D_MODEL), F32),
        compiler_params=_params(2), name="post",
    )(x, oTd, oTm, *weights)


def _prep_weights(norm_mix, w_in, mla_q_norm, mla_w_uq, mla_kv_norm, mla_w_uk, mla_w_uv, w_o_diff,
                  w_o_mla, w_out, norm_mlp, w_up, w_down, norm_final):
    d = D_MODEL
    o_cq = 3 * d
    o_ckv = o_cq + MLA_Q_LORA
    o_kr = o_ckv + MLA_KV_LORA
    o_g = o_kr + MLA_ROPE_DIM
    half = MLA_ROPE_DIM // 2
    rot = lambda a: jnp.concatenate([-a[..., half:], a[..., :half]], axis=-1)
    zpad = lambda a, n: jnp.zeros(a.shape[:-1] + (n,), a.dtype)

    wkr = w_in[:, o_kr:o_g]
    place = lambda a: jnp.concatenate([a, zpad(a, 32), a, zpad(a, 32)], axis=-1)
    wlat = jnp.concatenate([w_in[:, o_cq:o_kr], place(wkr), place(rot(wkr))], axis=-1)

    nope = mla_w_uq[:, :, :MLA_NOPE_DIM]
    rope = mla_w_uq[:, :, MLA_NOPE_DIM:]

    def arrange(n, r):
        even = jnp.concatenate([n[:, 0::2], r[:, 0::2], zpad(r[:, 0::2], 32)], axis=-1)
        odd = jnp.concatenate([r[:, 1::2], zpad(r[:, 1::2], 32), n[:, 1::2]], axis=-1)
        both = jnp.stack([even, odd], axis=2)
        return both.reshape(MLA_Q_LORA, MLA_HEADS * LANES)

    row = lambda a: a.reshape(1, -1).astype(F32)
    return {
        "gmix": row(norm_mix), "gq": row(mla_q_norm), "gkv": row(mla_kv_norm),
        "gmlp": row(norm_mlp), "gfin": row(norm_final),
        "wqT": w_in[:, 0:d].T.astype(BF),
        "wk": w_in[:, d:2 * d].astype(BF),
        "wv": w_in[:, 2 * d:3 * d].astype(BF),
        "wlat": wlat.astype(BF),
        "wgT": w_in[:, o_g:].T.astype(BF),
        "wuqaT": arrange(nope, rope).T.astype(BF),
        "wuqbT": rot(rope).reshape(MLA_Q_LORA, MLA_HEADS * MLA_ROPE_DIM).T.astype(BF),
        "wuk": mla_w_uk.reshape(MLA_KV_LORA, d).astype(BF),
        "wuvT": mla_w_uv.reshape(MLA_KV_LORA, d).T.astype(BF),
        "wodT": w_o_diff.reshape(d, d).T.astype(BF),
        "womT": w_o_mla.reshape(d, d).T.astype(BF),
        "woutT": w_out.T.astype(BF),
        "wup": w_up.astype(BF),
        "wdown": w_down.astype(BF),
    }


def _rope_tables(pos):
    half = MLA_ROPE_DIM // 2
    inv = jnp.power(ROPE_THETA, -jnp.arange(half, dtype=F32) * 2.0 / MLA_ROPE_DIM)
    ang = pos.astype(F32)[:, None] * inv[None, :]
    cos2 = jnp.tile(jnp.cos(ang), (1, 2))
    sin2 = jnp.tile(jnp.sin(ang), (1, 2))
    t = pos.shape[0]
    z32 = jnp.zeros((t, 32), F32)
    place = lambda a: jnp.concatenate([a, z32, a, z32], axis=1)
    scale = (MLA_NOPE_DIM + MLA_ROPE_DIM) ** -0.5 * LOG2E
    ones = jnp.ones((t, MLA_NOPE_DIM), F32)
    z64 = jnp.zeros((t, MLA_NOPE_DIM), F32)
    ct = jnp.concatenate([ones, cos2, z32, cos2, z32, ones], axis=1) * scale
    st = jnp.concatenate([z64, sin2, z32, sin2, z32, z64], axis=1) * scale
    return {"cosp": place(cos2), "sinp": place(sin2), "ct": ct.T, "st": st.T}


def _score_tiles(rel_bias, k_pos_tiles, q_pos, shift_bucket):
    k_pos = jnp.stack(k_pos_tiles)
    rel = k_pos[:, :, None] - q_pos[None, None, :]
    mask = (k_pos // CHUNK)[:, :, None] <= (q_pos // CHUNK)[None, None, :]
    return _bias_tiles(rel_bias, _t5_bucket(rel).astype(jnp.int32), mask.astype(jnp.int32), shift_bucket)


def _decode_score_tiles(rel_bias, k_pos, q_pos, new, shift_bucket):
    lane = jnp.arange(2 * DIFF_HEADS * new, dtype=jnp.int32)
    qp = q_pos[lane % new]
    rel = k_pos[:, None] - qp[None, :]
    mask = (k_pos // CHUNK)[:, None] <= (qp // CHUNK)[None, :]
    tab = rel_bias[:, (lane // new) % DIFF_HEADS].astype(F32)
    return _bias_lanes(tab, _t5_bucket(rel).astype(jnp.int32), mask.astype(jnp.int32), shift_bucket)


def kernel(x_prompt, x_sample, cache_diff_k, cache_diff_v, cache_mla_ckv, cache_mla_krope, rel_bias,
           norm_mix, w_in, lam_q1, lam_k1, lam_q2, lam_k2, diff_subln, mla_q_norm, mla_w_uq, mla_kv_norm,
           mla_w_uk, mla_w_uv, w_o_diff, w_o_mla, w_out, norm_mlp, w_up, w_down, norm_final):
    assert norm_mix.shape[0] == 1, "single-layer model"
    nb, t, _ = x_prompt.shape
    ns, new, _ = x_sample.shape
    past = cache_diff_k.shape[2]
    diff_blk = min(DIFF_CACHE_BLK, past)
    assert t % TQ == 0 and past % CACHE_BLK == 0 and new <= DEC_PAD and new % 16 == 0
    assert past % diff_blk == 0 and diff_blk >= TK
    assert past % CHUNK == 0 and new <= CHUNK

    w = _prep_weights(norm_mix[0], w_in[0], mla_q_norm[0], mla_w_uq[0], mla_kv_norm[0], mla_w_uk[0],
                      mla_w_uv[0], w_o_diff[0], w_o_mla[0], w_out[0], norm_mlp[0], w_up[0], w_down[0],
                      norm_final)
    lamp = jnp.stack([lam_q1[0], lam_k1[0], lam_q2[0], lam_k2[0]]).astype(F32)
    g_sub = diff_subln[0].reshape(DIFF_V_DIM, 1).astype(F32)

    pos_p = jnp.arange(t, dtype=jnp.int32)
    p = _project(x_prompt, w, _rope_tables(pos_p), tm=512, n_groups=2)
    blk = jnp.arange(TQ, dtype=jnp.int32)
    tiles_p = _score_tiles(rel_bias, [TQ + blk, blk], TQ + blk, _far_bucket())
    mask_p = jnp.where((blk // CHUNK)[:, None] <= (blk // CHUNK)[None, :], 0.0, NEG_INF).astype(F32)
    oTd = _diff_prompt(p, tiles_p, lamp, g_sub)
    oTm = _mla_prompt(p, mask_p)
    y_prompt = _post(x_prompt, oTd, oTm, w, tm=512, n_groups=2)

    toks = ns * new
    xs = x_sample.reshape(1, toks, D_MODEL)
    pos_s = past + jnp.arange(DEC_PAD, dtype=jnp.int32)
    s = _project(xs, w, _rope_tables(jnp.tile(pos_s[:new], ns)), tm=toks, n_groups=1)
    per_stream_q = lambda a: jnp.pad(a[0].reshape(-1, ns, new).transpose(1, 0, 2),
                                     ((0, 0), (0, 0), (0, DEC_PAD - new)))
    s.update(qTd=per_stream_q(s["qTd"]), qTm=per_stream_q(s["qTm"]),
             kb=s["kb"].reshape(ns, new, D_MODEL), ckv_s=s["ckv"].reshape(ns, new, MLA_KV_LORA),
             kr_s=s["kr"].reshape(ns, new, MLA_ROPE_DIM),
             vf_s=s["vf"].reshape(ns, new * DIFF_HEADS, DIFF_V_DIM))
    bias_c = _decode_score_tiles(rel_bias, jnp.arange(past - diff_blk, past, dtype=jnp.int32), pos_s, new,
                                 _far_bucket())
    bias_n = _decode_score_tiles(rel_bias, pos_s[:new], pos_s, new, _far_bucket())
    oTd_s = _diff_decode(s, cache_diff_k[0], cache_diff_v[0], bias_c, bias_n, lamp, g_sub, new, diff_blk)
    oTm_s = _mla_decode(s, cache_mla_ckv[0], cache_mla_krope[0], w, new)
    token_major = lambda a: a[:, :, :new].transpose(1, 0, 2).reshape(1, D_MODEL, toks)
    y_sample = _post(xs, token_major(oTd_s), token_major(oTm_s), w, tm=toks, n_groups=1)

    heads = lambda a, lead: a.reshape((1,) + lead + (DIFF_HEADS, DIFF_V_DIM))
    return (y_prompt, y_sample.reshape(ns, new, D_MODEL),
            heads(p["kf"], (nb, t)), heads(p["vf"], (nb, t)), p["ckv"][None], p["kr"][None],
            heads(s["kf"], (ns, new)), heads(s["vf"], (ns, new)),
            s["ckv"].reshape(1, ns, new, MLA_KV_LORA), s["kr"].reshape(1, ns, new, MLA_ROPE_DIM))
```

```python
import functools
import math

import jax
import jax.numpy as jnp
from jax import lax
from jax.experimental import pallas as pl
from jax.experimental.pallas import tpu as pltpu

D_MODEL = 1024
CHUNK = 64
DIFF_HEADS = 8
DIFF_HEAD_DIM = 64
DIFF_V_DIM = 128
MLA_HEADS = 16
MLA_NOPE_DIM = 64
MLA_ROPE_DIM = 32
MLA_V_DIM = 64
MLA_Q_LORA = 256
MLA_KV_LORA = 256
NUM_BUCKETS = 32
MAX_DISTANCE = 128
ROPE_THETA = 10000.0
EPS = 1e-6
NEG_INF = -1e30
LAMBDA_INIT = 0.8 - 0.6 * math.exp(-0.3 * 0)

BF = jnp.bfloat16
F32 = jnp.float32

LANES = 128
TQ = 256
TK = 256
TR = 256
HEADS_PER_STEP = 4
DEC_PAD = LANES
CACHE_BLK = 512
DIFF_CACHE_BLK = 1024
DECODE_SPLIT = 2
VMEM_LIMIT = 56 * 1024 * 1024
LOG2E = math.log2(math.e)
ONES_ROWS = 16


def _nn(a, b):
    return jnp.dot(a, b, preferred_element_type=F32)


def _nt(a, b):
    return lax.dot_general(a, b, (((1,), (1,)), ((), ())), preferred_element_type=F32)


def _tn(a, b):
    return lax.dot_general(a, b, (((0,), (0,)), ((), ())), preferred_element_type=F32)


def _rms(x, g):
    return x * lax.rsqrt(jnp.mean(x * x, axis=-1, keepdims=True) + EPS) * g


def _const_spec(shape):
    zeros = (0,) * len(shape)
    return pl.BlockSpec(shape, lambda *_: zeros, pipeline_mode=pl.Buffered(1))


def _params(n_axes):
    return pltpu.CompilerParams(dimension_semantics=("arbitrary",) * n_axes,
                                vmem_limit_bytes=VMEM_LIMIT)


def _bias_kernel(tab_ref, bkt_ref, msk_ref, o_ref, *, shift_bucket):
    h = pl.program_id(0)
    bkt = bkt_ref[...]
    acc = jnp.zeros(bkt.shape, F32)
    for b in range(NUM_BUCKETS):
        acc = jnp.where(bkt == b, tab_ref[b, h], acc)
    if shift_bucket is not None:
        acc = acc - tab_ref[shift_bucket, h]
    o_ref[...] = jnp.where(msk_ref[...] != 0, acc * LOG2E, NEG_INF)


def _bias_tiles(rel_bias, bucket, mask, shift_bucket):
    n, r, c = bucket.shape
    return pl.pallas_call(
        functools.partial(_bias_kernel, shift_bucket=shift_bucket),
        grid=(DIFF_HEADS,),
        in_specs=[pl.BlockSpec(memory_space=pltpu.SMEM),
                  pl.BlockSpec((n, r, c), lambda h: (0, 0, 0)),
                  pl.BlockSpec((n, r, c), lambda h: (0, 0, 0))],
        out_specs=pl.BlockSpec((None, n, r, c), lambda h: (h, 0, 0, 0)),
        out_shape=jax.ShapeDtypeStruct((DIFF_HEADS, n, r, c), F32),
        compiler_params=_params(1),
        name="bias_tiles",
    )(rel_bias, bucket, mask)


def _bias_lanes_kernel(tab_ref, bkt_ref, msk_ref, o_ref, *, shift_bucket):
    bkt = bkt_ref[...]
    acc = jnp.zeros(bkt.shape, F32)
    for b in range(NUM_BUCKETS):
        acc = jnp.where(bkt == b, tab_ref[b:b + 1, :], acc)
    if shift_bucket is not None:
        acc = acc - tab_ref[shift_bucket:shift_bucket + 1, :]
    o_ref[...] = jnp.where(msk_ref[...] != 0, acc * LOG2E, NEG_INF)


def _bias_lanes(tab, bucket, mask, shift_bucket):
    return pl.pallas_call(
        functools.partial(_bias_lanes_kernel, shift_bucket=shift_bucket),
        out_shape=jax.ShapeDtypeStruct(bucket.shape, F32), name="bias_lanes",
    )(tab, bucket, mask)


def _t5_bucket(rel):
    half = NUM_BUCKETS // 2
    max_exact = half // 2
    n = jnp.abs(rel)
    nf = jnp.maximum(n, max_exact).astype(F32)
    large = max_exact + (jnp.log(nf / max_exact) / math.log(MAX_DISTANCE / max_exact)
                         * (half - max_exact)).astype(jnp.int32)
    large = jnp.minimum(large, half - 1)
    return jnp.where(rel > 0, half, 0) + jnp.where(n < max_exact, n, large)


def _far_bucket():
    half = NUM_BUCKETS // 2
    max_exact = half // 2
    assert max_exact + math.log(TK / max_exact) / math.log(MAX_DISTANCE / max_exact) * (half - max_exact) >= half
    return half - 1


def _proj_kernel(x_ref, gmix_ref, wqT_ref, wk_ref, wv_ref, wlat_ref, gq_ref, gkv_ref,
                 wuqaT_ref, wuqbT_ref, wuk_ref, wuvT_ref, cosp_ref, sinp_ref, ct_ref, st_ref,
                 qTd_ref, kf_ref, kb_ref, vf_ref, vb_ref, qTm_ref, ckv_ref, kn_ref, vTm_ref,
                 r_ref, kr_ref, *, n_groups):
    rows = x_ref.shape[0] // n_groups
    groups = [slice(i * rows, (i + 1) * rows) for i in range(n_groups)]
    pair = 2 * LANES
    rd = MLA_ROPE_DIM

    def norm_in(v, r):
        v["hb"] = _rms(x_ref[r, :], gmix_ref[...]).astype(BF)

    def latent(v, r):
        v["lat"] = _nn(v["hb"], wlat_ref[...])

    def qkv(v, r):
        hb = v.pop("hb")
        qTd_ref[:, r] = (_nt(wqT_ref[...], hb) * (DIFF_HEAD_DIM ** -0.5 * LOG2E)).astype(BF)
        k = _nn(hb, wk_ref[...])
        kb_ref[r, :] = k.astype(BF)
        val = _nn(hb, wv_ref[...])
        vb_ref[r, :] = val.astype(BF)
        for h in range(DIFF_HEADS):
            dst = pl.ds(r.start * DIFF_HEADS + h, rows, stride=DIFF_HEADS)
            kf_ref[dst, :] = k[:, h * DIFF_V_DIM:(h + 1) * DIFF_V_DIM]
            vf_ref[dst, :] = val[:, h * DIFF_V_DIM:(h + 1) * DIFF_V_DIM]

    def latent_norms(v, r):
        lat = v["lat"]
        v["cq"] = _rms(lat[:, 0:MLA_Q_LORA], gq_ref[...]).astype(BF)
        ckv = _rms(lat[:, MLA_Q_LORA:MLA_Q_LORA + MLA_KV_LORA], gkv_ref[...])
        ckv_ref[r, :] = ckv
        v["cb"] = ckv.astype(BF)

    def up(v, r):
        cq, cb = v.pop("cq"), v.pop("cb")
        v["qa"] = _nt(wuqaT_ref[...], cq)
        v["qb"] = _nt(wuqbT_ref[...], cq)
        kn_ref[r, :] = _nn(cb, wuk_ref[...]).astype(BF)
        vTm_ref[:, r] = _nt(wuvT_ref[...], cb).astype(BF)

    def rope(v, r):
        qa, qb, lat = v.pop("qa"), v.pop("qb"), v.pop("lat")
        ct = ct_ref[:, r]
        st = st_ref[:, r]
        z = jnp.zeros((rd, rows), F32)
        nd = MLA_NOPE_DIM
        packed = 2 * (nd + rd)
        for p in range(MLA_HEADS // 2):
            a = qa[p * packed:(p + 1) * packed, :]
            b_even = qb[2 * p * rd:(2 * p + 1) * rd, :]
            b_odd = qb[(2 * p + 1) * rd:(2 * p + 2) * rd, :]
            e_rope = slice(nd, nd + rd)
            o_rope = slice(LANES, LANES + rd)
            qTm_ref[p * pair:(p + 1) * pair, r] = jnp.concatenate([
                a[0:nd] * ct[0:nd],
                a[nd:nd + rd] * ct[e_rope] + b_even * st[e_rope], z,
                a[nd + rd:nd + 2 * rd] * ct[o_rope] + b_odd * st[o_rope], z,
                a[nd + 2 * rd:packed] * ct[LANES + nd:pair]], axis=0).astype(BF)
        o = MLA_Q_LORA + MLA_KV_LORA
        kr = lat[:, o:o + LANES] * cosp_ref[r, :] + lat[:, o + LANES:o + 2 * LANES] * sinp_ref[r, :]
        r_ref[r, :] = kr.astype(BF)
        kr_ref[r, :] = kr[:, 0:MLA_ROPE_DIM]

    values = [{} for _ in groups]
    for stage in (norm_in, latent, qkv, latent_norms, up, rope):
        for v, r in zip(values, groups):
            stage(v, r)


def _project(x, w, tabs, tm, n_groups):
    nb, t, _ = x.shape
    nt = t // tm
    tok = lambda width: pl.BlockSpec((None, tm, width), lambda b, i: (b, i, 0))
    feat = lambda rows: pl.BlockSpec((None, rows, tm), lambda b, i: (b, 0, i))
    sds = jax.ShapeDtypeStruct
    weights = [w["gmix"], w["wqT"], w["wk"], w["wv"], w["wlat"], w["gq"], w["gkv"],
               w["wuqaT"], w["wuqbT"], w["wuk"], w["wuvT"]]
    in_specs = ([tok(D_MODEL)] + [_const_spec(a.shape) for a in weights]
                + [pl.BlockSpec((tm, LANES), lambda b, i: (i, 0)),
                   pl.BlockSpec((tm, LANES), lambda b, i: (i, 0)),
                   pl.BlockSpec((2 * LANES, tm), lambda b, i: (0, i)),
                   pl.BlockSpec((2 * LANES, tm), lambda b, i: (0, i))])
    head_rows = pl.BlockSpec((None, tm * DIFF_HEADS, DIFF_V_DIM), lambda b, i: (b, i, 0))
    out_shape = [sds((nb, D_MODEL, t), BF), sds((nb, t * DIFF_HEADS, DIFF_V_DIM), F32), sds((nb, t, D_MODEL), BF),
                 sds((nb, t * DIFF_HEADS, DIFF_V_DIM), F32), sds((nb, t, D_MODEL), BF),
                 sds((nb, MLA_HEADS * LANES, t), BF), sds((nb, t, MLA_KV_LORA), F32),
                 sds((nb, t, D_MODEL), BF), sds((nb, D_MODEL, t), BF),
                 sds((nb, t, LANES), BF), sds((nb, t, MLA_ROPE_DIM), F32)]
    out_specs = [feat(D_MODEL), head_rows, tok(D_MODEL), head_rows, tok(D_MODEL),
                 feat(MLA_HEADS * LANES), tok(MLA_KV_LORA), tok(D_MODEL), feat(D_MODEL),
                 tok(LANES), tok(MLA_ROPE_DIM)]
    outs = pl.pallas_call(
        functools.partial(_proj_kernel, n_groups=n_groups), grid=(nb, nt), in_specs=in_specs,
        out_specs=out_specs, out_shape=out_shape,
        compiler_params=_params(2), name="project",
    )(x, *weights, tabs["cosp"], tabs["sinp"], tabs["ct"], tabs["st"])
    names = ["qTd", "kf", "kb", "vf", "vb", "qTm", "ckv", "kn", "vTm", "r", "kr"]
    return dict(zip(names, outs))


def _split_maps(qT):
    rows = lax.broadcasted_iota(jnp.int32, qT.shape, 0)
    qf = qT.astype(F32)
    zero = jnp.zeros_like(qf)
    return (jnp.where(rows < DIFF_HEAD_DIM, qf, zero).astype(BF),
            jnp.where(rows >= DIFF_HEAD_DIM, qf, zero).astype(BF))


def _lambda(lam_ref):
    lp = lam_ref[...]
    a = jnp.sum(lp[0:1, :] * lp[1:2, :], axis=1, keepdims=True)
    b = jnp.sum(lp[2:3, :] * lp[3:4, :], axis=1, keepdims=True)
    return jnp.exp(a) - jnp.exp(b) + LAMBDA_INIT


def _diff_finish(o1, o2, lam, g):
    o = o1 - lam * o2
    ms = jnp.mean(o * o, axis=0, keepdims=True)
    return (o * lax.rsqrt(ms + EPS) * g * (1.0 - LAMBDA_INIT)).astype(BF)


def _mla_keys(kp, r):
    lane = lax.broadcasted_iota(jnp.int32, kp.shape, 1)
    kpf = kp.astype(F32)
    rf = r.astype(F32)
    return (jnp.where(lane < MLA_NOPE_DIM, kpf, rf).astype(BF),
            jnp.where(lane >= MLA_NOPE_DIM, kpf, rf).astype(BF))


def _qslice(qb):
    return slice(qb * TQ, (qb + 1) * TQ)


def _kslice(kb):
    return slice(kb * TR, (kb + 1) * TR)


def _with_ones(va_ref, vT):
    dv, t = vT.shape
    va_ref[0:dv, :] = vT
    va_ref[dv:dv + ONES_ROWS, :] = jnp.ones((ONES_ROWS, t), BF)


def _weighted_mean(acc):
    dv = acc.shape[0] - ONES_ROWS
    return acc[0:dv, :] * (1.0 / acc[dv:dv + 1, :])


def _causal_two_pass(nq, streams, s_ref, p_ref):
    items = [(stream, qb) for stream in streams for qb in range(nq)]

    def score_pass(n):
        (prep, scores, va_refs, _), qb = items[n]
        ops = prep(qb)
        m8 = [None] * len(va_refs)
        for kb in range((qb + 1) * TK // TR):
            for i, s in enumerate(scores(ops, qb, kb)):
                s_ref[n % 2, i, _kslice(kb), :] = s
                c = jnp.max(s.reshape(TR // 8, 8, TQ), axis=0)
                m8[i] = c if m8[i] is None else jnp.maximum(m8[i], c)
        return [jnp.max(m, axis=0, keepdims=True) for m in m8]

    def exp_pass(n, ms):
        qb = items[n][1]
        for kb in range((qb + 1) * TK // TR):
            for i in range(len(ms)):
                p_ref[n % 2, i, _kslice(kb), :] = jnp.exp2(s_ref[n % 2, i, _kslice(kb), :] - ms[i]).astype(BF)

    def value_pass(n):
        (_, _, va_refs, emit), qb = items[n]
        klen = (qb + 1) * TK
        emit(qb, [_nn(va[:, 0:klen], p_ref[n % 2, i, 0:klen, :]) for i, va in enumerate(va_refs)])

    ms = score_pass(0)
    for n in range(len(items) + 1):
        ms_next = score_pass(n + 1) if n + 1 < len(items) else None
        if n < len(items):
            exp_pass(n, ms)
        if n >= 1:
            value_pass(n - 1)
        ms = ms_next


def _diff_prompt_kernel(qT_ref, k_ref, v_ref, bias_ref, lam_ref, g_ref, o_ref, va_ref, s_ref, p_ref, *, t):
    lam = _lambda(lam_ref)
    gain = g_ref[...]

    def head_stream(j):
        hs = slice(j * DIFF_V_DIM, (j + 1) * DIFF_V_DIM)
        _with_ones(va_ref.at[j], v_ref[:, hs].astype(F32).T.astype(BF))

        def prep(qb):
            return _split_maps(qT_ref[hs, _qslice(qb)])

        def scores(ops, qb, kb):
            kblk = k_ref[_kslice(kb), hs]
            out = [_nn(kblk, q) for q in ops]
            grp, r0 = divmod(kb * TR, TK)
            if grp >= qb - 1:
                out = [s + bias_ref[j, qb - grp, r0:r0 + TR, :] for s in out]
            return out

        def emit(qb, accs):
            o_ref[hs, _qslice(qb)] = _diff_finish(_weighted_mean(accs[0]), _weighted_mean(accs[1]), lam, gain)

        return prep, scores, [va_ref.at[j], va_ref.at[j]], emit

    _causal_two_pass(t // TQ, [head_stream(j) for j in range(HEADS_PER_STEP)], s_ref, p_ref)


def _score_scratch(n_maps, t):
    return [pltpu.VMEM((2, n_maps, t, TQ), F32), pltpu.VMEM((2, n_maps, t, TQ), BF)]


def _diff_prompt(p, bias, lamp, g):
    nb, _, t = p["qTd"].shape
    n = HEADS_PER_STEP
    head_feat = pl.BlockSpec((None, n * DIFF_V_DIM, t), lambda b, h: (b, h, 0))
    head_tok = pl.BlockSpec((None, t, n * DIFF_V_DIM), lambda b, h: (b, 0, h))
    return pl.pallas_call(
        functools.partial(_diff_prompt_kernel, t=t),
        grid=(nb, DIFF_HEADS // n),
        in_specs=[head_feat, head_tok, head_tok,
                  pl.BlockSpec((n, 2, TK, TQ), lambda b, h: (h, 0, 0, 0)),
                  _const_spec(lamp.shape), _const_spec(g.shape)],
        out_specs=head_feat,
        out_shape=jax.ShapeDtypeStruct((nb, D_MODEL, t), BF),
        scratch_shapes=[pltpu.VMEM((n, DIFF_V_DIM + ONES_ROWS, t), BF)] + _score_scratch(2, t),
        compiler_params=_params(2), name="diff_prompt",
    )(p["qTd"], p["kb"], p["vb"], bias, lamp, g)


def _mla_prompt_kernel(qT_ref, kp_ref, r_ref, vT_ref, mask_ref, o_ref, k_ref, va_ref, s_ref, p_ref, *, t):
    def pair_stream(j):
        for i, keys in enumerate(_mla_keys(kp_ref[:, j * LANES:(j + 1) * LANES], r_ref[...])):
            k_ref[j, i] = keys
            vs = slice((2 * j + i) * MLA_V_DIM, (2 * j + i + 1) * MLA_V_DIM)
            _with_ones(va_ref.at[j, i], vT_ref[vs, :])

        def prep(qb):
            return [qT_ref[(2 * j + i) * LANES:(2 * j + i + 1) * LANES, _qslice(qb)] for i in range(2)]

        def scores(ops, qb, kb):
            out = [_nn(k_ref[j, i, _kslice(kb), :], ops[i]) for i in range(2)]
            grp, r0 = divmod(kb * TR, TK)
            if grp == qb:
                out = [s + mask_ref[r0:r0 + TR, :] for s in out]
            return out

        def emit(qb, accs):
            for i in range(2):
                vs = slice((2 * j + i) * MLA_V_DIM, (2 * j + i + 1) * MLA_V_DIM)
                o_ref[vs, _qslice(qb)] = _weighted_mean(accs[i]).astype(BF)

        return prep, scores, [va_ref.at[j, 0], va_ref.at[j, 1]], emit

    _causal_two_pass(t // TQ, [pair_stream(j) for j in range(HEADS_PER_STEP)], s_ref, p_ref)


def _mla_prompt(p, mask):
    nb, _, t = p["qTm"].shape
    n = HEADS_PER_STEP
    return pl.pallas_call(
        functools.partial(_mla_prompt_kernel, t=t),
        grid=(nb, MLA_HEADS // (2 * n)),
        in_specs=[pl.BlockSpec((None, 2 * n * LANES, t), lambda b, h: (b, h, 0)),
                  pl.BlockSpec((None, t, n * LANES), lambda b, h: (b, 0, h)),
                  pl.BlockSpec((None, t, LANES), lambda b, h: (b, 0, 0)),
                  pl.BlockSpec((None, 2 * n * MLA_V_DIM, t), lambda b, h: (b, h, 0)),
                  _const_spec(mask.shape)],
        out_specs=pl.BlockSpec((None, 2 * n * MLA_V_DIM, t), lambda b, h: (b, h, 0)),
        out_shape=jax.ShapeDtypeStruct((nb, D_MODEL, t), BF),
        scratch_shapes=[pltpu.VMEM((n, 2, t, LANES), BF),
                        pltpu.VMEM((n, 2, MLA_V_DIM + ONES_ROWS, t), BF)] + _score_scratch(2, t),
        compiler_params=_params(2), name="mla_prompt",
    )(p["qTm"], p["kn"], p["r"], p["vTm"], mask)


def _diff_decode_kernel(qT_ref, ck_ref, cv_ref, kn_ref, vn_ref, bc_ref, bn_ref, lam_ref, g_ref, o_ref,
                        qbd_ref, m_ref, l_ref, acc_ref, *, new, blk):
    nq = 2 * DIFF_HEADS * new
    assert nq == 2 * LANES and new & (new - 1) == 0
    c = pl.program_id(1)
    last = pl.num_programs(1) - 1
    log2 = lambda n: n.bit_length() - 1

    def lane_head(shape):
        lane = lax.broadcasted_iota(jnp.int32, shape, 1)
        return lax.shift_right_logical(lane, log2(new)) & (DIFF_HEADS - 1)

    @pl.when(c == 0)
    def _():
        m_ref[...] = jnp.full(m_ref.shape, NEG_INF, F32)
        l_ref[...] = jnp.zeros(l_ref.shape, F32)
        acc_ref[...] = jnp.zeros(acc_ref.shape, F32)
        shape = (2 * DIFF_HEAD_DIM, nq)
        row = lax.broadcasted_iota(jnp.int32, shape, 0)
        lane = lax.broadcasted_iota(jnp.int32, shape, 1)
        frame = lax.broadcasted_iota(jnp.int32, (DEC_PAD, nq), 0)
        spread = jnp.where((lax.broadcasted_iota(jnp.int32, (DEC_PAD, nq), 1) & (new - 1)) == frame,
                           1.0, 0.0).astype(BF)
        same_map = lax.shift_right_logical(row, log2(DIFF_HEAD_DIM)) == lax.shift_right_logical(lane, log2(LANES))
        for h in range(DIFF_HEADS):
            rep = _nn(qT_ref[h * DIFF_V_DIM:(h + 1) * DIFF_V_DIM, :], spread)
            qbd_ref[h * DIFF_V_DIM:(h + 1) * DIFF_V_DIM, :] = jnp.where(
                same_map & (lane_head(shape) == h), rep, 0.0).astype(BF)

    heads = range(DIFF_HEADS)

    def update(parts):
        scores = [_nn(jnp.concatenate([keys(h) for h in heads], axis=1), qbd_ref[...]) for keys, _, _ in parts]
        head = lane_head((DIFF_V_DIM, nq))
        for s, (_, vals, bias) in zip(scores, parts):
            if bias is not None:
                s = s + bias
            m_new = jnp.maximum(m_ref[...], jnp.max(s, axis=0, keepdims=True))
            alpha = jnp.exp2(m_ref[...] - m_new)
            p = jnp.exp2(s - m_new)
            l_ref[...] = alpha * l_ref[...] + jnp.sum(p, axis=0, keepdims=True)
            m_ref[...] = m_new
            full = _tn(jnp.concatenate([vals(h) for h in heads], axis=1), p.astype(BF))
            pv = jnp.zeros((DIFF_V_DIM, nq), F32)
            for h in heads:
                pv = jnp.where(head == h, full[h * DIFF_V_DIM:(h + 1) * DIFF_V_DIM, :], pv)
            acc_ref[...] = alpha * acc_ref[...] + pv

    sub = blk // DECODE_SPLIT

    def cache_parts(biased):
        rows = lambda ref, i: (lambda h: ref[pl.ds(i * sub * DIFF_HEADS + h, sub, stride=DIFF_HEADS), :].astype(BF))
        return [(rows(ck_ref, i), rows(cv_ref, i), bc_ref[i * sub:(i + 1) * sub, :] if biased else None)
                for i in range(DECODE_SPLIT)]

    @pl.when(c < last)
    def _():
        update(cache_parts(False))

    @pl.when(c == last)
    def _():
        hs = lambda h: slice(h * DIFF_V_DIM, (h + 1) * DIFF_V_DIM)
        update(cache_parts(True)
               + [(lambda h: kn_ref[0:new, hs(h)],
                   lambda h: vn_ref[pl.ds(h, new, stride=DIFF_HEADS), :].astype(BF), bn_ref[...])])
        o = acc_ref[...] * (1.0 / l_ref[...])
        y = _diff_finish(o[:, 0:LANES], o[:, LANES:nq], _lambda(lam_ref), g_ref[...])
        r = lax.broadcasted_iota(jnp.int32, (LANES, DEC_PAD), 0)
        col = lax.broadcasted_iota(jnp.int32, (LANES, DEC_PAD), 1)
        for h in range(DIFF_HEADS):
            fold = jnp.where(r == col + h * new, 1.0, 0.0).astype(BF)
            o_ref[hs(h), :] = _nn(y, jnp.where(col < new, fold, jnp.zeros_like(fold))).astype(BF)


def _diff_decode(s, cache_k, cache_v, bias_c, bias_n, lamp, g, new, blk):
    nb, past = cache_k.shape[:2]
    nq = 2 * DIFF_HEADS * new
    cache_k, cache_v = (a.reshape(nb, past * DIFF_HEADS, DIFF_V_DIM) for a in (cache_k, cache_v))
    stream = lambda rows: pl.BlockSpec((None, rows, D_MODEL), lambda b, c: (b, 0, 0))
    chunk = pl.BlockSpec((None, blk * DIFF_HEADS, DIFF_V_DIM), lambda b, c: (b, c, 0))
    return pl.pallas_call(
        functools.partial(_diff_decode_kernel, new=new, blk=blk),
        grid=(nb, past // blk),
        in_specs=[pl.BlockSpec((None, D_MODEL, DEC_PAD), lambda b, c: (b, 0, 0)), chunk, chunk,
                  stream(new),
                  pl.BlockSpec((None, new * DIFF_HEADS, DIFF_V_DIM), lambda b, c: (b, 0, 0)),
                  _const_spec(bias_c.shape), _const_spec(bias_n.shape),
                  _const_spec(lamp.shape), _const_spec(g.shape)],
        out_specs=pl.BlockSpec((None, D_MODEL, DEC_PAD), lambda b, c: (b, 0, 0)),
        out_shape=jax.ShapeDtypeStruct((nb, D_MODEL, DEC_PAD), BF),
        scratch_shapes=[pltpu.VMEM((D_MODEL, nq), BF),
                        pltpu.VMEM((1, nq), F32), pltpu.VMEM((1, nq), F32),
                        pltpu.VMEM((DIFF_V_DIM, nq), F32)],
        compiler_params=_params(2), name="diff_decode",
    )(s["qTd"], cache_k, cache_v, s["kb"], s["vf_s"], bias_c, bias_n, lamp, g)


def _mla_decode_kernel(qT_ref, ckv_ref, kr_ref, ckvn_ref, krn_ref, wuk_ref, wuvT_ref, o_ref, s_ref,
                       *, past, new):
    nq = MLA_HEADS * new
    assert nq == 2 * LANES and new & (new - 1) == 0
    log2 = lambda n: n.bit_length() - 1

    def groups(shape, rows_per_group):
        r = lax.broadcasted_iota(jnp.int32, shape, 0)
        c = lax.broadcasted_iota(jnp.int32, shape, 1)
        return lax.shift_right_logical(r, log2(rows_per_group)) == lax.shift_right_logical(c, log2(new))

    def frame_match(shape, lane_axis):
        a = lax.broadcasted_iota(jnp.int32, shape, lane_axis)
        b = lax.broadcasted_iota(jnp.int32, shape, 1 - lane_axis)
        return jnp.where((a & (new - 1)) == b, 1.0, 0.0).astype(BF)

    nope, rope = [], []
    for h in range(MLA_HEADS):
        base = h * LANES
        o_n, o_r = (0, MLA_NOPE_DIM) if h % 2 == 0 else (MLA_NOPE_DIM, 0)
        nope.append(qT_ref[base + o_n:base + o_n + MLA_NOPE_DIM, :])
        rope.append(qT_ref[base + o_r:base + o_r + MLA_ROPE_DIM, :])
    spread = frame_match((DEC_PAD, nq), 1)
    qn = _nn(jnp.concatenate(nope, axis=0), spread)
    qn_bd = jnp.where(groups(qn.shape, MLA_NOPE_DIM), qn, 0.0).astype(BF)
    qr = _nn(jnp.concatenate(rope, axis=0), spread)
    qr_bd = jnp.where(groups(qr.shape, MLA_ROPE_DIM), qr, 0.0)
    qrT = jnp.sum(qr_bd.reshape(MLA_HEADS, MLA_ROPE_DIM, nq), axis=0).astype(BF)
    qlatT = _nn(wuk_ref[...], qn_bd).astype(BF)

    def keys(lo, n):
        if lo < past:
            return ckv_ref[lo:lo + n, :].astype(BF), kr_ref[lo:lo + n, :].astype(BF)
        return ckvn_ref[0:new, :].astype(BF), krn_ref[0:new, :].astype(BF)

    spans = [(c0, CACHE_BLK) for c0 in range(0, past, CACHE_BLK)] + [(past, new)]
    m = None
    for lo, n in spans:
        ck, kr = keys(lo, n)
        s = _nn(ck, qlatT) + _nn(kr, qrT)
        s_ref[lo:lo + n, :] = s
        c = jnp.max(s.reshape(n // 8, 8, nq), axis=0)
        m = c if m is None else jnp.maximum(m, c)
    m = jnp.max(m, axis=0, keepdims=True)
    l = acc = None
    for lo, n in spans:
        p = jnp.exp2(s_ref[lo:lo + n, :] - m)
        ps = jnp.sum(p.reshape(n // 8, 8, nq), axis=0)
        pv = _tn(keys(lo, n)[0], p.astype(BF))
        l, acc = (ps, pv) if l is None else (l + ps, acc + pv)
    o_latT = (acc * (1.0 / jnp.sum(l, axis=0, keepdims=True))).astype(BF)
    y = _nn(wuvT_ref[...], o_latT)
    y_bd = jnp.where(groups(y.shape, MLA_V_DIM), y, 0.0).astype(BF)
    o_ref[...] = _nn(y_bd, frame_match((nq, DEC_PAD), 0)).astype(BF)


def _mla_decode(s, cache_ckv, cache_kr, w, new):
    nb, past, _ = cache_ckv.shape
    per_stream = lambda rows, cols: pl.BlockSpec((None, rows, cols), lambda b: (b, 0, 0))
    return pl.pallas_call(
        functools.partial(_mla_decode_kernel, past=past, new=new),
        grid=(nb,),
        in_specs=[per_stream(MLA_HEADS * LANES, DEC_PAD), per_stream(past, MLA_KV_LORA),
                  per_stream(past, MLA_ROPE_DIM), per_stream(new, MLA_KV_LORA),
                  per_stream(new, MLA_ROPE_DIM), _const_spec(w["wuk"].shape),
                  _const_spec(w["wuvT"].shape)],
        out_specs=per_stream(D_MODEL, DEC_PAD),
        out_shape=jax.ShapeDtypeStruct((nb, D_MODEL, DEC_PAD), BF),
        scratch_shapes=[pltpu.VMEM((past + new, MLA_HEADS * new), F32)],
        compiler_params=_params(1), name="mla_decode",
    )(s["qTm"], cache_ckv, cache_kr, s["ckv_s"], s["kr_s"], w["wuk"], w["wuvT"])


def _post_kernel(x_ref, oTd_ref, oTm_ref, gmix_ref, wgT_ref, wodT_ref, womT_ref, woutT_ref,
                 gmlp_ref, wup_ref, wdown_ref, gfin_ref, y_ref, *, n_groups):
    rows = x_ref.shape[0] // n_groups
    groups = [slice(i * rows, (i + 1) * rows) for i in range(n_groups)]
    d = D_MODEL

    def norm_in(v, r):
        v["x"] = x_ref[r, :]
        v["hb"] = _rms(v["x"], gmix_ref[...]).astype(BF)

    def branches(v, r):
        v["gT"] = _nt(wgT_ref[...], v.pop("hb"))
        v["od"] = _nn(wodT_ref[...], oTd_ref[:, r])
        v["om"] = _nn(womT_ref[...], oTm_ref[:, r])

    def merge(v, r):
        g = jax.nn.sigmoid(v.pop("gT"))
        v["mT"] = (g[0:d, :] * v.pop("od") + g[d:2 * d, :] * v.pop("om")).astype(BF)

    def out_proj(v, r):
        v["aT"] = _nn(woutT_ref[...], v.pop("mT"))

    def residual(v, r):
        v["x1"] = v.pop("x") + v.pop("aT").T
        v["h2"] = _rms(v["x1"], gmlp_ref[...]).astype(BF)

    def up(v, r):
        v["u"] = _nn(v.pop("h2"), wup_ref[...])

    def act(v, r):
        u = jnp.maximum(v.pop("u"), 0.0)
        v["uu"] = (u * u).astype(BF)

    def down(v, r):
        v["x2"] = v.pop("x1") + _nn(v.pop("uu"), wdown_ref[...])

    def norm_out(v, r):
        y_ref[r, :] = _rms(v.pop("x2"), gfin_ref[...])

    values = [{} for _ in groups]
    for stage in (norm_in, branches, merge, out_proj, residual, up, act, down, norm_out):
        for v, r in zip(values, groups):
            stage(v, r)


def _post(x, oTd, oTm, w, tm, n_groups):
    nb, t, _ = x.shape
    tok = pl.BlockSpec((None, tm, D_MODEL), lambda b, i: (b, i, 0))
    feat = pl.BlockSpec((None, D_MODEL, tm), lambda b, i: (b, 0, i))
    weights = [w["gmix"], w["wgT"], w["wodT"], w["womT"], w["woutT"], w["gmlp"], w["wup"],
               w["wdown"], w["gfin"]]
    return pl.pallas_call(
        functools.partial(_post_kernel, n_groups=n_groups), grid=(nb, t // tm),
        in_specs=[tok, feat, feat] + [_const_spec(a.shape) for a in weights],
        out_specs=tok, out_shape=jax.ShapeDtypeStruct((nb, t, D_MODEL), F32),
        compiler_params=_params(2), name="post",
    )(x, oTd, oTm, *weights)


def _prep_weights(norm_mix, w_in, mla_q_norm, mla_w_uq, mla_kv_norm, mla_w_uk, mla_w_uv, w_o_diff,
                  w_o_mla, w_out, norm_mlp, w_up, w_down, norm_final):
    d = D_MODEL
    o_cq = 3 * d
    o_ckv = o_cq + MLA_Q_LORA
    o_kr = o_ckv + MLA_KV_LORA
    o_g = o_kr + MLA_ROPE_DIM
    half = MLA_ROPE_DIM // 2
    rot = lambda a: jnp.concatenate([-a[..., half:], a[..., :half]], axis=-1)
    zpad = lambda a, n: jnp.zeros(a.shape[:-1] + (n,), a.dtype)

    wkr = w_in[:, o_kr:o_g]
    place = lambda a: jnp.concatenate([a, zpad(a, 32), a, zpad(a, 32)], axis=-1)
    wlat = jnp.concatenate([w_in[:, o_cq:o_kr], place(wkr), place(rot(wkr))], axis=-1)

    nope = mla_w_uq[:, :, :MLA_NOPE_DIM]
    rope = mla_w_uq[:, :, MLA_NOPE_DIM:]

    def arrange(n, r):
        even = jnp.concatenate([n[:, 0::2], r[:, 0::2]], axis=-1)
        odd = jnp.concatenate([r[:, 1::2], n[:, 1::2]], axis=-1)
        both = jnp.stack([even, odd], axis=2)
        return both.reshape(MLA_Q_LORA, MLA_HEADS * (MLA_NOPE_DIM + MLA_ROPE_DIM))

    row = lambda a: a.reshape(1, -1).astype(F32)
    return {
        "gmix": row(norm_mix), "gq": row(mla_q_norm), "gkv": row(mla_kv_norm),
        "gmlp": row(norm_mlp), "gfin": row(norm_final),
        "wqT": w_in[:, 0:d].T.astype(BF),
        "wk": w_in[:, d:2 * d].astype(BF),
        "wv": w_in[:, 2 * d:3 * d].astype(BF),
        "wlat": wlat.astype(BF),
        "wgT": w_in[:, o_g:].T.astype(BF),
        "wuqaT": arrange(nope, rope).T.astype(BF),
        "wuqbT": rot(rope).reshape(MLA_Q_LORA, MLA_HEADS * MLA_ROPE_DIM).T.astype(BF),
        "wuk": mla_w_uk.reshape(MLA_KV_LORA, d).astype(BF),
        "wuvT": mla_w_uv.reshape(MLA_KV_LORA, d).T.astype(BF),
        "wodT": w_o_diff.reshape(d, d).T.astype(BF),
        "womT": w_o_mla.reshape(d, d).T.astype(BF),
        "woutT": w_out.T.astype(BF),
        "wup": w_up.astype(BF),
        "wdown": w_down.astype(BF),
    }


def _rope_tables(pos):
    half = MLA_ROPE_DIM // 2
    inv = jnp.power(ROPE_THETA, -jnp.arange(half, dtype=F32) * 2.0 / MLA_ROPE_DIM)
    ang = pos.astype(F32)[:, None] * inv[None, :]
    cos2 = jnp.tile(jnp.cos(ang), (1, 2))
    sin2 = jnp.tile(jnp.sin(ang), (1, 2))
    t = pos.shape[0]
    z32 = jnp.zeros((t, 32), F32)
    place = lambda a: jnp.concatenate([a, z32, a, z32], axis=1)
    scale = (MLA_NOPE_DIM + MLA_ROPE_DIM) ** -0.5 * LOG2E
    ones = jnp.ones((t, MLA_NOPE_DIM), F32)
    z64 = jnp.zeros((t, MLA_NOPE_DIM), F32)
    ct = jnp.concatenate([ones, cos2, z32, cos2, z32, ones], axis=1) * scale
    st = jnp.concatenate([z64, sin2, z32, sin2, z32, z64], axis=1) * scale
    return {"cosp": place(cos2), "sinp": place(sin2), "ct": ct.T, "st": st.T}


def _score_tiles(rel_bias, k_pos_tiles, q_pos, shift_bucket):
    k_pos = jnp.stack(k_pos_tiles)
    rel = k_pos[:, :, None] - q_pos[None, None, :]
    mask = (k_pos // CHUNK)[:, :, None] <= (q_pos // CHUNK)[None, None, :]
    return _bias_tiles(rel_bias, _t5_bucket(rel).astype(jnp.int32), mask.astype(jnp.int32), shift_bucket)


def _decode_score_tiles(rel_bias, k_pos, q_pos, new, shift_bucket):
    lane = jnp.arange(2 * DIFF_HEADS * new, dtype=jnp.int32)
    qp = q_pos[lane % new]
    rel = k_pos[:, None] - qp[None, :]
    mask = (k_pos // CHUNK)[:, None] <= (qp // CHUNK)[None, :]
    tab = rel_bias[:, (lane // new) % DIFF_HEADS].astype(F32)
    return _bias_lanes(tab, _t5_bucket(rel).astype(jnp.int32), mask.astype(jnp.int32), shift_bucket)


def kernel(x_prompt, x_sample, cache_diff_k, cache_diff_v, cache_mla_ckv, cache_mla_krope, rel_bias,
           norm_mix, w_in, lam_q1, lam_k1, lam_q2, lam_k2, diff_subln, mla_q_norm, mla_w_uq, mla_kv_norm,
           mla_w_uk, mla_w_uv, w_o_diff, w_o_mla, w_out, norm_mlp, w_up, w_down, norm_final):
    assert norm_mix.shape[0] == 1, "single-layer model"
    nb, t, _ = x_prompt.shape
    ns, new, _ = x_sample.shape
    past = cache_diff_k.shape[2]
    diff_blk = min(DIFF_CACHE_BLK, past)
    assert t % TQ == 0 and past % CACHE_BLK == 0 and new <= DEC_PAD and new % 16 == 0
    assert past % diff_blk == 0 and diff_blk >= TK
    assert past % CHUNK == 0 and new <= CHUNK

    w = _prep_weights(norm_mix[0], w_in[0], mla_q_norm[0], mla_w_uq[0], mla_kv_norm[0], mla_w_uk[0],
                      mla_w_uv[0], w_o_diff[0], w_o_mla[0], w_out[0], norm_mlp[0], w_up[0], w_down[0],
                      norm_final)
    lamp = jnp.stack([lam_q1[0], lam_k1[0], lam_q2[0], lam_k2[0]]).astype(F32)
    g_sub = diff_subln[0].reshape(DIFF_V_DIM, 1).astype(F32)

    pos_p = jnp.arange(t, dtype=jnp.int32)
    p = _project(x_prompt, w, _rope_tables(pos_p), tm=512, n_groups=2)
    blk = jnp.arange(TQ, dtype=jnp.int32)
    tiles_p = _score_tiles(rel_bias, [TQ + blk, blk], TQ + blk, _far_bucket())
    mask_p = jnp.where((blk // CHUNK)[:, None] <= (blk // CHUNK)[None, :], 0.0, NEG_INF).astype(F32)
    oTd = _diff_prompt(p, tiles_p, lamp, g_sub)
    oTm = _mla_prompt(p, mask_p)
    y_prompt = _post(x_prompt, oTd, oTm, w, tm=512, n_groups=2)

    toks = ns * new
    xs = x_sample.reshape(1, toks, D_MODEL)
    pos_s = past + jnp.arange(DEC_PAD, dtype=jnp.int32)
    s = _project(xs, w, _rope_tables(jnp.tile(pos_s[:new], ns)), tm=toks, n_groups=1)
    per_stream_q = lambda a: jnp.pad(a[0].reshape(-1, ns, new).transpose(1, 0, 2),
                                     ((0, 0), (0, 0), (0, DEC_PAD - new)))
    s.update(qTd=per_stream_q(s["qTd"]), qTm=per_stream_q(s["qTm"]),
             kb=s["kb"].reshape(ns, new, D_MODEL), ckv_s=s["ckv"].reshape(ns, new, MLA_KV_LORA),
             kr_s=s["kr"].reshape(ns, new, MLA_ROPE_DIM),
             vf_s=s["vf"].reshape(ns, new * DIFF_HEADS, DIFF_V_DIM))
    bias_c = _decode_score_tiles(rel_bias, jnp.arange(past - diff_blk, past, dtype=jnp.int32), pos_s, new,
                                 _far_bucket())
    bias_n = _decode_score_tiles(rel_bias, pos_s[:new], pos_s, new, _far_bucket())
    oTd_s = _diff_decode(s, cache_diff_k[0], cache_diff_v[0], bias_c, bias_n, lamp, g_sub, new, diff_blk)
    oTm_s = _mla_decode(s, cache_mla_ckv[0], cache_mla_krope[0], w, new)
    token_major = lambda a: a[:, :, :new].transpose(1, 0, 2).reshape(1, D_MODEL, toks)
    y_sample = _post(xs, token_major(oTd_s), token_major(oTm_s), w, tm=toks, n_groups=1)

    heads = lambda a, lead: a.reshape((1,) + lead + (DIFF_HEADS, DIFF_V_DIM))
    return (y_prompt, y_sample.reshape(ns, new, D_MODEL),
            heads(p["kf"], (nb, t)), heads(p["vf"], (nb, t)), p["ckv"][None], p["kr"][None],
            heads(s["kf"], (ns, new)), heads(s["vf"], (ns, new)),
            s["ckv"].reshape(1, ns, new, MLA_KV_LORA), s["kr"].reshape(1, ns, new, MLA_ROPE_DIM))
```

```python
import functools
import math

import jax
import jax.numpy as jnp
from jax import lax
from jax.experimental import pallas as pl
from jax.experimental.pallas import tpu as pltpu

D_MODEL = 1024
CHUNK = 64
DIFF_HEADS = 8
DIFF_HEAD_DIM = 64
DIFF_V_DIM = 128
MLA_HEADS = 16
MLA_NOPE_DIM = 64
MLA_ROPE_DIM = 32
MLA_V_DIM = 64
MLA_Q_LORA = 256
MLA_KV_LORA = 256
NUM_BUCKETS = 32
MAX_DISTANCE = 128
ROPE_THETA = 10000.0
EPS = 1e-6
NEG_INF = -1e30
LAMBDA_INIT = 0.8 - 0.6 * math.exp(-0.3 * 0)

BF = jnp.bfloat16
F32 = jnp.float32

LANES = 128
TQ = 256
TK = 256
TR = 256
HEADS_PER_STEP = 4
DEC_PAD = LANES
CACHE_BLK = 512
DIFF_CACHE_BLK = 1024
DECODE_SPLIT = 2
VMEM_LIMIT = 56 * 1024 * 1024
LOG2E = math.log2(math.e)
ONES_ROWS = 16


def _nn(a, b):
    return jnp.dot(a, b, preferred_element_type=F32)


def _nt(a, b):
    return lax.dot_general(a, b, (((1,), (1,)), ((), ())), preferred_element_type=F32)


def _tn(a, b):
    return lax.dot_general(a, b, (((0,), (0,)), ((), ())), preferred_element_type=F32)


def _rms(x, g):
    return x * lax.rsqrt(jnp.mean(x * x, axis=-1, keepdims=True) + EPS) * g


def _const_spec(shape):
    zeros = (0,) * len(shape)
    return pl.BlockSpec(shape, lambda *_: zeros, pipeline_mode=pl.Buffered(1))


def _params(n_axes):
    return pltpu.CompilerParams(dimension_semantics=("arbitrary",) * n_axes,
                                vmem_limit_bytes=VMEM_LIMIT)


def _bias_kernel(tab_ref, bkt_ref, msk_ref, o_ref, *, shift_bucket):
    h = pl.program_id(0)
    bkt = bkt_ref[...]
    acc = jnp.zeros(bkt.shape, F32)
    for b in range(NUM_BUCKETS):
        acc = jnp.where(bkt == b, tab_ref[b, h], acc)
    if shift_bucket is not None:
        acc = acc - tab_ref[shift_bucket, h]
    o_ref[...] = jnp.where(msk_ref[...] != 0, acc * LOG2E, NEG_INF)


def _bias_tiles(rel_bias, bucket, mask, shift_bucket):
    n, r, c = bucket.shape
    return pl.pallas_call(
        functools.partial(_bias_kernel, shift_bucket=shift_bucket),
        grid=(DIFF_HEADS,),
        in_specs=[pl.BlockSpec(memory_space=pltpu.SMEM),
                  pl.BlockSpec((n, r, c), lambda h: (0, 0, 0)),
                  pl.BlockSpec((n, r, c), lambda h: (0, 0, 0))],
        out_specs=pl.BlockSpec((None, n, r, c), lambda h: (h, 0, 0, 0)),
        out_shape=jax.ShapeDtypeStruct((DIFF_HEADS, n, r, c), F32),
        compiler_params=_params(1),
        name="bias_tiles",
    )(rel_bias, bucket, mask)


def _bias_lanes_kernel(tab_ref, bkt_ref, msk_ref, o_ref, *, shift_bucket):
    bkt = bkt_ref[...]
    acc = jnp.zeros(bkt.shape, F32)
    for b in range(NUM_BUCKETS):
        acc = jnp.where(bkt == b, tab_ref[b:b + 1, :], acc)
    if shift_bucket is not None:
        acc = acc - tab_ref[shift_bucket:shift_bucket + 1, :]
    o_ref[...] = jnp.where(msk_ref[...] != 0, acc * LOG2E, NEG_INF)


def _bias_lanes(tab, bucket, mask, shift_bucket):
    return pl.pallas_call(
        functools.partial(_bias_lanes_kernel, shift_bucket=shift_bucket),
        out_shape=jax.ShapeDtypeStruct(bucket.shape, F32), name="bias_lanes",
    )(tab, bucket, mask)


def _t5_bucket(rel):
    half = NUM_BUCKETS // 2
    max_exact = half // 2
    n = jnp.abs(rel)
    nf = jnp.maximum(n, max_exact).astype(F32)
    large = max_exact + (jnp.log(nf / max_exact) / math.log(MAX_DISTANCE / max_exact)
                         * (half - max_exact)).astype(jnp.int32)
    large = jnp.minimum(large, half - 1)
    return jnp.where(rel > 0, half, 0) + jnp.where(n < max_exact, n, large)


def _far_bucket():
    half = NUM_BUCKETS // 2
    max_exact = half // 2
    assert max_exact + math.log(TK / max_exact) / math.log(MAX_DISTANCE / max_exact) * (half - max_exact) >= half
    return half - 1


def _proj_kernel(x_ref, gmix_ref, wqT_ref, wk_ref, wv_ref, wlat_ref, gq_ref, gkv_ref,
                 wuqaT_ref, wuqbT_ref, wuk_ref, wuvT_ref, cosp_ref, sinp_ref, ct_ref, st_ref,
                 qTd_ref, kf_ref, kb_ref, vf_ref, vb_ref, qTm_ref, ckv_ref, kn_ref, vTm_ref,
                 r_ref, kr_ref, *, n_groups):
    rows = x_ref.shape[0] // n_groups
    groups = [slice(i * rows, (i + 1) * rows) for i in range(n_groups)]
    pair = 2 * LANES
    rd = MLA_ROPE_DIM

    def norm_in(v, r):
        v["hb"] = _rms(x_ref[r, :], gmix_ref[...]).astype(BF)

    def latent(v, r):
        v["lat"] = _nn(v["hb"], wlat_ref[...])

    def qkv(v, r):
        hb = v.pop("hb")
        qTd_ref[:, r] = (_nt(wqT_ref[...], hb) * (DIFF_HEAD_DIM ** -0.5 * LOG2E)).astype(BF)
        k = _nn(hb, wk_ref[...])
        kb_ref[r, :] = k.astype(BF)
        val = _nn(hb, wv_ref[...])
        vb_ref[r, :] = val.astype(BF)
        for h in range(DIFF_HEADS):
            dst = pl.ds(r.start * DIFF_HEADS + h, rows, stride=DIFF_HEADS)
            kf_ref[dst, :] = k[:, h * DIFF_V_DIM:(h + 1) * DIFF_V_DIM]
            vf_ref[dst, :] = val[:, h * DIFF_V_DIM:(h + 1) * DIFF_V_DIM]

    def latent_norms(v, r):
        lat = v["lat"]
        v["cq"] = _rms(lat[:, 0:MLA_Q_LORA], gq_ref[...]).astype(BF)
        ckv = _rms(lat[:, MLA_Q_LORA:MLA_Q_LORA + MLA_KV_LORA], gkv_ref[...])
        ckv_ref[r, :] = ckv
        v["cb"] = ckv.astype(BF)

    def up(v, r):
        cq, cb = v.pop("cq"), v.pop("cb")
        v["qa"] = _nt(wuqaT_ref[...], cq)
        v["qb"] = _nt(wuqbT_ref[...], cq)
        kn = _nn(cb, wuk_ref[...])
        o = MLA_Q_LORA + MLA_KV_LORA
        lat = v["lat"]
        placed = lat[:, o:o + LANES] * cosp_ref[r, :] + lat[:, o + LANES:o + 2 * LANES] * sinp_ref[r, :]
        lane = lax.broadcasted_iota(jnp.int32, placed.shape, 1)
        for p in range(MLA_HEADS // 2):
            kp = kn[:, p * LANES:(p + 1) * LANES]
            kn_ref[r, 2 * p * LANES:(2 * p + 1) * LANES] = jnp.where(lane < MLA_NOPE_DIM, kp, placed).astype(BF)
            kn_ref[r, (2 * p + 1) * LANES:(2 * p + 2) * LANES] = jnp.where(lane >= MLA_NOPE_DIM, kp, placed).astype(BF)
        vTm_ref[:, r] = _nt(wuvT_ref[...], cb).astype(BF)

    def rope(v, r):
        qa, qb, lat = v.pop("qa"), v.pop("qb"), v.pop("lat")
        ct = ct_ref[:, r]
        st = st_ref[:, r]
        z = jnp.zeros((MLA_NOPE_DIM, rows), F32)
        for p in range(MLA_HEADS // 2):
            sl = slice(p * pair, (p + 1) * pair)
            partner = jnp.concatenate([z, qb[2 * p * rd:(2 * p + 1) * rd, :], z[0:rd, :],
                                       qb[(2 * p + 1) * rd:(2 * p + 2) * rd, :], z[0:rd, :], z], axis=0)
            qTm_ref[sl, r] = (qa[sl, :] * ct + partner * st).astype(BF)
        o = MLA_Q_LORA + MLA_KV_LORA
        kr = lat[:, o:o + LANES] * cosp_ref[r, :] + lat[:, o + LANES:o + 2 * LANES] * sinp_ref[r, :]
        r_ref[r, :] = kr.astype(BF)
        kr_ref[r, :] = kr[:, 0:MLA_ROPE_DIM]

    values = [{} for _ in groups]
    for stage in (norm_in, latent, qkv, latent_norms, up, rope):
        for v, r in zip(values, groups):
            stage(v, r)


def _project(x, w, tabs, tm, n_groups):
    nb, t, _ = x.shape
    nt = t // tm
    tok = lambda width: pl.BlockSpec((None, tm, width), lambda b, i: (b, i, 0))
    feat = lambda rows: pl.BlockSpec((None, rows, tm), lambda b, i: (b, 0, i))
    sds = jax.ShapeDtypeStruct
    weights = [w["gmix"], w["wqT"], w["wk"], w["wv"], w["wlat"], w["gq"], w["gkv"],
               w["wuqaT"], w["wuqbT"], w["wuk"], w["wuvT"]]
    in_specs = ([tok(D_MODEL)] + [_const_spec(a.shape) for a in weights]
                + [pl.BlockSpec((tm, LANES), lambda b, i: (i, 0)),
                   pl.BlockSpec((tm, LANES), lambda b, i: (i, 0)),
                   pl.BlockSpec((2 * LANES, tm), lambda b, i: (0, i)),
                   pl.BlockSpec((2 * LANES, tm), lambda b, i: (0, i))])
    head_rows = pl.BlockSpec((None, tm * DIFF_HEADS, DIFF_V_DIM), lambda b, i: (b, i, 0))
    out_shape = [sds((nb, D_MODEL, t), BF), sds((nb, t * DIFF_HEADS, DIFF_V_DIM), F32), sds((nb, t, D_MODEL), BF),
                 sds((nb, t * DIFF_HEADS, DIFF_V_DIM), F32), sds((nb, t, D_MODEL), BF),
                 sds((nb, MLA_HEADS * LANES, t), BF), sds((nb, t, MLA_KV_LORA), F32),
                 sds((nb, t, 2 * D_MODEL), BF), sds((nb, D_MODEL, t), BF),
                 sds((nb, t, LANES), BF), sds((nb, t, MLA_ROPE_DIM), F32)]
    out_specs = [feat(D_MODEL), head_rows, tok(D_MODEL), head_rows, tok(D_MODEL),
                 feat(MLA_HEADS * LANES), tok(MLA_KV_LORA), tok(2 * D_MODEL), feat(D_MODEL),
                 tok(LANES), tok(MLA_ROPE_DIM)]
    outs = pl.pallas_call(
        functools.partial(_proj_kernel, n_groups=n_groups), grid=(nb, nt), in_specs=in_specs,
        out_specs=out_specs, out_shape=out_shape,
        compiler_params=_params(2), name="project",
    )(x, *weights, tabs["cosp"], tabs["sinp"], tabs["ct"], tabs["st"])
    names = ["qTd", "kf", "kb", "vf", "vb", "qTm", "ckv", "kn", "vTm", "r", "kr"]
    return dict(zip(names, outs))


def _split_maps(qT):
    rows = lax.broadcasted_iota(jnp.int32, qT.shape, 0)
    qf = qT.astype(F32)
    zero = jnp.zeros_like(qf)
    return (jnp.where(rows < DIFF_HEAD_DIM, qf, zero).astype(BF),
            jnp.where(rows >= DIFF_HEAD_DIM, qf, zero).astype(BF))


def _lambda(lam_ref):
    lp = lam_ref[...]
    a = jnp.sum(lp[0:1, :] * lp[1:2, :], axis=1, keepdims=True)
    b = jnp.sum(lp[2:3, :] * lp[3:4, :], axis=1, keepdims=True)
    return jnp.exp(a) - jnp.exp(b) + LAMBDA_INIT


def _diff_finish(o1, o2, lam, g):
    o = o1 - lam * o2
    ms = jnp.mean(o * o, axis=0, keepdims=True)
    return (o * lax.rsqrt(ms + EPS) * g * (1.0 - LAMBDA_INIT)).astype(BF)


def _mla_keys(kp, r):
    lane = lax.broadcasted_iota(jnp.int32, kp.shape, 1)
    kpf = kp.astype(F32)
    rf = r.astype(F32)
    return (jnp.where(lane < MLA_NOPE_DIM, kpf, rf).astype(BF),
            jnp.where(lane >= MLA_NOPE_DIM, kpf, rf).astype(BF))


def _qslice(qb):
    return slice(qb * TQ, (qb + 1) * TQ)


def _kslice(kb):
    return slice(kb * TR, (kb + 1) * TR)


def _with_ones(va_ref, vT):
    dv, t = vT.shape
    va_ref[0:dv, :] = vT
    va_ref[dv:dv + ONES_ROWS, :] = jnp.ones((ONES_ROWS, t), BF)


def _weighted_mean(acc):
    dv = acc.shape[0] - ONES_ROWS
    return acc[0:dv, :] * (1.0 / acc[dv:dv + 1, :])


def _causal_two_pass(nq, streams, s_ref, p_ref):
    items = [(stream, qb) for stream in streams for qb in range(nq)]

    def score_pass(n):
        (prep, scores, va_refs, _), qb = items[n]
        ops = prep(qb)
        m8 = [None] * len(va_refs)
        for kb in range((qb + 1) * TK // TR):
            for i, s in enumerate(scores(ops, qb, kb)):
                s_ref[n % 2, i, _kslice(kb), :] = s
                c = jnp.max(s.reshape(TR // 8, 8, TQ), axis=0)
                m8[i] = c if m8[i] is None else jnp.maximum(m8[i], c)
        return [jnp.max(m, axis=0, keepdims=True) for m in m8]

    def exp_pass(n, ms):
        qb = items[n][1]
        for kb in range((qb + 1) * TK // TR):
            for i in range(len(ms)):
                p_ref[n % 2, i, _kslice(kb), :] = jnp.exp2(s_ref[n % 2, i, _kslice(kb), :] - ms[i]).astype(BF)

    def value_pass(n):
        (_, _, va_refs, emit), qb = items[n]
        klen = (qb + 1) * TK
        emit(qb, [_nn(va[:, 0:klen], p_ref[n % 2, i, 0:klen, :]) for i, va in enumerate(va_refs)])

    ms = score_pass(0)
    for n in range(len(items) + 1):
        ms_next = score_pass(n + 1) if n + 1 < len(items) else None
        if n < len(items):
            exp_pass(n, ms)
        if n >= 1:
            value_pass(n - 1)
        ms = ms_next


def _diff_prompt_kernel(qT_ref, k_ref, v_ref, bias_ref, lam_ref, g_ref, o_ref, va_ref, s_ref, p_ref, *, t):
    lam = _lambda(lam_ref)
    gain = g_ref[...]

    def head_stream(j):
        hs = slice(j * DIFF_V_DIM, (j + 1) * DIFF_V_DIM)
        _with_ones(va_ref.at[j], v_ref[:, hs].astype(F32).T.astype(BF))

        def prep(qb):
            return _split_maps(qT_ref[hs, _qslice(qb)])

        def scores(ops, qb, kb):
            kblk = k_ref[_kslice(kb), hs]
            out = [_nn(kblk, q) for q in ops]
            grp, r0 = divmod(kb * TR, TK)
            if grp >= qb - 1:
                out = [s + bias_ref[j, qb - grp, r0:r0 + TR, :] for s in out]
            return out

        def emit(qb, accs):
            o_ref[hs, _qslice(qb)] = _diff_finish(_weighted_mean(accs[0]), _weighted_mean(accs[1]), lam, gain)

        return prep, scores, [va_ref.at[j], va_ref.at[j]], emit

    _causal_two_pass(t // TQ, [head_stream(j) for j in range(HEADS_PER_STEP)], s_ref, p_ref)


def _score_scratch(n_maps, t):
    return [pltpu.VMEM((2, n_maps, t, TQ), F32), pltpu.VMEM((2, n_maps, t, TQ), BF)]


def _diff_prompt(p, bias, lamp, g):
    nb, _, t = p["qTd"].shape
    n = HEADS_PER_STEP
    head_feat = pl.BlockSpec((None, n * DIFF_V_DIM, t), lambda b, h: (b, h, 0))
    head_tok = pl.BlockSpec((None, t, n * DIFF_V_DIM), lambda b, h: (b, 0, h))
    return pl.pallas_call(
        functools.partial(_diff_prompt_kernel, t=t),
        grid=(nb, DIFF_HEADS // n),
        in_specs=[head_feat, head_tok, head_tok,
                  pl.BlockSpec((n, 2, TK, TQ), lambda b, h: (h, 0, 0, 0)),
                  _const_spec(lamp.shape), _const_spec(g.shape)],
        out_specs=head_feat,
        out_shape=jax.ShapeDtypeStruct((nb, D_MODEL, t), BF),
        scratch_shapes=[pltpu.VMEM((n, DIFF_V_DIM + ONES_ROWS, t), BF)] + _score_scratch(2, t),
        compiler_params=_params(2), name="diff_prompt",
    )(p["qTd"], p["kb"], p["vb"], bias, lamp, g)


def _mla_prompt_kernel(qT_ref, kp_ref, r_ref, vT_ref, mask_ref, o_ref, k_ref, va_ref, s_ref, p_ref, *, t):
    def pair_stream(j):
        for i in range(2):
            vs = slice((2 * j + i) * MLA_V_DIM, (2 * j + i + 1) * MLA_V_DIM)
            _with_ones(va_ref.at[j, i], vT_ref[vs, :])

        def prep(qb):
            return [qT_ref[(2 * j + i) * LANES:(2 * j + i + 1) * LANES, _qslice(qb)] for i in range(2)]

        def scores(ops, qb, kb):
            out = [_nn(kp_ref[_kslice(kb), (2 * j + i) * LANES:(2 * j + i + 1) * LANES], ops[i]) for i in range(2)]
            grp, r0 = divmod(kb * TR, TK)
            if grp == qb:
                out = [s + mask_ref[r0:r0 + TR, :] for s in out]
            return out

        def emit(qb, accs):
            for i in range(2):
                vs = slice((2 * j + i) * MLA_V_DIM, (2 * j + i + 1) * MLA_V_DIM)
                o_ref[vs, _qslice(qb)] = _weighted_mean(accs[i]).astype(BF)

        return prep, scores, [va_ref.at[j, 0], va_ref.at[j, 1]], emit

    _causal_two_pass(t // TQ, [pair_stream(j) for j in range(HEADS_PER_STEP)], s_ref, p_ref)


def _mla_prompt(p, mask):
    nb, _, t = p["qTm"].shape
    n = HEADS_PER_STEP
    return pl.pallas_call(
        functools.partial(_mla_prompt_kernel, t=t),
        grid=(nb, MLA_HEADS // (2 * n)),
        in_specs=[pl.BlockSpec((None, 2 * n * LANES, t), lambda b, h: (b, h, 0)),
                  pl.BlockSpec((None, t, 2 * n * LANES), lambda b, h: (b, 0, h)),
                  pl.BlockSpec((None, t, LANES), lambda b, h: (b, 0, 0)),
                  pl.BlockSpec((None, 2 * n * MLA_V_DIM, t), lambda b, h: (b, h, 0)),
                  _const_spec(mask.shape)],
        out_specs=pl.BlockSpec((None, 2 * n * MLA_V_DIM, t), lambda b, h: (b, h, 0)),
        out_shape=jax.ShapeDtypeStruct((nb, D_MODEL, t), BF),
        scratch_shapes=[pltpu.VMEM((n, 2, t, LANES), BF),
                        pltpu.VMEM((n, 2, MLA_V_DIM + ONES_ROWS, t), BF)] + _score_scratch(2, t),
        compiler_params=_params(2), name="mla_prompt",
    )(p["qTm"], p["kn"], p["r"], p["vTm"], mask)


def _diff_decode_kernel(qT_ref, ck_ref, cv_ref, kn_ref, vn_ref, bc_ref, bn_ref, lam_ref, g_ref, o_ref,
                        qbd_ref, m_ref, l_ref, acc_ref, *, new, blk):
    nq = 2 * DIFF_HEADS * new
    assert nq == 2 * LANES and new & (new - 1) == 0
    c = pl.program_id(1)
    last = pl.num_programs(1) - 1
    log2 = lambda n: n.bit_length() - 1

    def lane_head(shape):
        lane = lax.broadcasted_iota(jnp.int32, shape, 1)
        return lax.shift_right_logical(lane, log2(new)) & (DIFF_HEADS - 1)

    @pl.when(c == 0)
    def _():
        m_ref[...] = jnp.full(m_ref.shape, NEG_INF, F32)
        l_ref[...] = jnp.zeros(l_ref.shape, F32)
        acc_ref[...] = jnp.zeros(acc_ref.shape, F32)
        shape = (2 * DIFF_HEAD_DIM, nq)
        row = lax.broadcasted_iota(jnp.int32, shape, 0)
        lane = lax.broadcasted_iota(jnp.int32, shape, 1)
        frame = lax.broadcasted_iota(jnp.int32, (DEC_PAD, nq), 0)
        spread = jnp.where((lax.broadcasted_iota(jnp.int32, (DEC_PAD, nq), 1) & (new - 1)) == frame,
                           1.0, 0.0).astype(BF)
        same_map = lax.shift_right_logical(row, log2(DIFF_HEAD_DIM)) == lax.shift_right_logical(lane, log2(LANES))
        for h in range(DIFF_HEADS):
            rep = _nn(qT_ref[h * DIFF_V_DIM:(h + 1) * DIFF_V_DIM, :], spread)
            qbd_ref[h * DIFF_V_DIM:(h + 1) * DIFF_V_DIM, :] = jnp.where(
                same_map & (lane_head(shape) == h), rep, 0.0).astype(BF)

    heads = range(DIFF_HEADS)

    def update(parts):
        scores = [_nn(jnp.concatenate([keys(h) for h in heads], axis=1), qbd_ref[...]) for keys, _, _ in parts]
        head = lane_head((DIFF_V_DIM, nq))
        for s, (_, vals, bias) in zip(scores, parts):
            if bias is not None:
                s = s + bias
            m_new = jnp.maximum(m_ref[...], jnp.max(s, axis=0, keepdims=True))
            alpha = jnp.exp2(m_ref[...] - m_new)
            p = jnp.exp2(s - m_new)
            l_ref[...] = alpha * l_ref[...] + jnp.sum(p, axis=0, keepdims=True)
            m_ref[...] = m_new
            full = _tn(jnp.concatenate([vals(h) for h in heads], axis=1), p.astype(BF))
            pv = jnp.zeros((DIFF_V_DIM, nq), F32)
            for h in heads:
                pv = jnp.where(head == h, full[h * DIFF_V_DIM:(h + 1) * DIFF_V_DIM, :], pv)
            acc_ref[...] = alpha * acc_ref[...] + pv

    sub = blk // DECODE_SPLIT

    def cache_parts(biased):
        rows = lambda ref, i: (lambda h: ref[pl.ds(i * sub * DIFF_HEADS + h, sub, stride=DIFF_HEADS), :].astype(BF))
        return [(rows(ck_ref, i), rows(cv_ref, i), bc_ref[i * sub:(i + 1) * sub, :] if biased else None)
                for i in range(DECODE_SPLIT)]

    @pl.when(c < last)
    def _():
        update(cache_parts(False))

    @pl.when(c == last)
    def _():
        hs = lambda h: slice(h * DIFF_V_DIM, (h + 1) * DIFF_V_DIM)
        update(cache_parts(True)
               + [(lambda h: kn_ref[0:new, hs(h)],
                   lambda h: vn_ref[pl.ds(h, new, stride=DIFF_HEADS), :].astype(BF), bn_ref[...])])
        o = acc_ref[...] * (1.0 / l_ref[...])
        y = _diff_finish(o[:, 0:LANES], o[:, LANES:nq], _lambda(lam_ref), g_ref[...])
        r = lax.broadcasted_iota(jnp.int32, (LANES, DEC_PAD), 0)
        col = lax.broadcasted_iota(jnp.int32, (LANES, DEC_PAD), 1)
        for h in range(DIFF_HEADS):
            fold = jnp.where(r == col + h * new, 1.0, 0.0).astype(BF)
            o_ref[hs(h), :] = _nn(y, jnp.where(col < new, fold, jnp.zeros_like(fold))).astype(BF)


def _diff_decode(s, cache_k, cache_v, bias_c, bias_n, lamp, g, new, blk):
    nb, past = cache_k.shape[:2]
    nq = 2 * DIFF_HEADS * new
    cache_k, cache_v = (a.reshape(nb, past * DIFF_HEADS, DIFF_V_DIM) for a in (cache_k, cache_v))
    stream = lambda rows: pl.BlockSpec((None, rows, D_MODEL), lambda b, c: (b, 0, 0))
    chunk = pl.BlockSpec((None, blk * DIFF_HEADS, DIFF_V_DIM), lambda b, c: (b, c, 0))
    return pl.pallas_call(
        functools.partial(_diff_decode_kernel, new=new, blk=blk),
        grid=(nb, past // blk),
        in_specs=[pl.BlockSpec((None, D_MODEL, DEC_PAD), lambda b, c: (b, 0, 0)), chunk, chunk,
                  stream(new),
                  pl.BlockSpec((None, new * DIFF_HEADS, DIFF_V_DIM), lambda b, c: (b, 0, 0)),
                  _const_spec(bias_c.shape), _const_spec(bias_n.shape),
                  _const_spec(lamp.shape), _const_spec(g.shape)],
        out_specs=pl.BlockSpec((None, D_MODEL, DEC_PAD), lambda b, c: (b, 0, 0)),
        out_shape=jax.ShapeDtypeStruct((nb, D_MODEL, DEC_PAD), BF),
        scratch_shapes=[pltpu.VMEM((D_MODEL, nq), BF),
                        pltpu.VMEM((1, nq), F32), pltpu.VMEM((1, nq), F32),
                        pltpu.VMEM((DIFF_V_DIM, nq), F32)],
        compiler_params=_params(2), name="diff_decode",
    )(s["qTd"], cache_k, cache_v, s["kb"], s["vf_s"], bias_c, bias_n, lamp, g)


def _mla_decode_kernel(qT_ref, ckv_ref, kr_ref, ckvn_ref, krn_ref, wuk_ref, wuvT_ref, o_ref, s_ref,
                       *, past, new):
    nq = MLA_HEADS * new
    assert nq == 2 * LANES and new & (new - 1) == 0
    log2 = lambda n: n.bit_length() - 1

    def groups(shape, rows_per_group):
        r = lax.broadcasted_iota(jnp.int32, shape, 0)
        c = lax.broadcasted_iota(jnp.int32, shape, 1)
        return lax.shift_right_logical(r, log2(rows_per_group)) == lax.shift_right_logical(c, log2(new))

    def frame_match(shape, lane_axis):
        a = lax.broadcasted_iota(jnp.int32, shape, lane_axis)
        b = lax.broadcasted_iota(jnp.int32, shape, 1 - lane_axis)
        return jnp.where((a & (new - 1)) == b, 1.0, 0.0).astype(BF)

    nope, rope = [], []
    for h in range(MLA_HEADS):
        base = h * LANES
        o_n, o_r = (0, MLA_NOPE_DIM) if h % 2 == 0 else (MLA_NOPE_DIM, 0)
        nope.append(qT_ref[base + o_n:base + o_n + MLA_NOPE_DIM, :])
        rope.append(qT_ref[base + o_r:base + o_r + MLA_ROPE_DIM, :])
    spread = frame_match((DEC_PAD, nq), 1)
    qn = _nn(jnp.concatenate(nope, axis=0), spread)
    qn_bd = jnp.where(groups(qn.shape, MLA_NOPE_DIM), qn, 0.0).astype(BF)
    qr = _nn(jnp.concatenate(rope, axis=0), spread)
    qr_bd = jnp.where(groups(qr.shape, MLA_ROPE_DIM), qr, 0.0)
    qrT = jnp.sum(qr_bd.reshape(MLA_HEADS, MLA_ROPE_DIM, nq), axis=0).astype(BF)
    qlatT = _nn(wuk_ref[...], qn_bd).astype(BF)

    def keys(lo, n):
        if lo < past:
            return ckv_ref[lo:lo + n, :].astype(BF), kr_ref[lo:lo + n, :].astype(BF)
        return ckvn_ref[0:new, :].astype(BF), krn_ref[0:new, :].astype(BF)

    spans = [(c0, CACHE_BLK) for c0 in range(0, past, CACHE_BLK)] + [(past, new)]
    m = None
    for lo, n in spans:
        ck, kr = keys(lo, n)
        s = _nn(ck, qlatT) + _nn(kr, qrT)
        s_ref[lo:lo + n, :] = s
        c = jnp.max(s.reshape(n // 8, 8, nq), axis=0)
        m = c if m is None else jnp.maximum(m, c)
    m = jnp.max(m, axis=0, keepdims=True)
    l = acc = None
    for lo, n in spans:
        p = jnp.exp2(s_ref[lo:lo + n, :] - m)
        ps = jnp.sum(p.reshape(n // 8, 8, nq), axis=0)
        pv = _tn(keys(lo, n)[0], p.astype(BF))
        l, acc = (ps, pv) if l is None else (l + ps, acc + pv)
    o_latT = (acc * (1.0 / jnp.sum(l, axis=0, keepdims=True))).astype(BF)
    y = _nn(wuvT_ref[...], o_latT)
    y_bd = jnp.where(groups(y.shape, MLA_V_DIM), y, 0.0).astype(BF)
    o_ref[...] = _nn(y_bd, frame_match((nq, DEC_PAD), 0)).astype(BF)


def _mla_decode(s, cache_ckv, cache_kr, w, new):
    nb, past, _ = cache_ckv.shape
    per_stream = lambda rows, cols: pl.BlockSpec((None, rows, cols), lambda b: (b, 0, 0))
    return pl.pallas_call(
        functools.partial(_mla_decode_kernel, past=past, new=new),
        grid=(nb,),
        in_specs=[per_stream(MLA_HEADS * LANES, DEC_PAD), per_stream(past, MLA_KV_LORA),
                  per_stream(past, MLA_ROPE_DIM), per_stream(new, MLA_KV_LORA),
                  per_stream(new, MLA_ROPE_DIM), _const_spec(w["wuk"].shape),
                  _const_spec(w["wuvT"].shape)],
        out_specs=per_stream(D_MODEL, DEC_PAD),
        out_shape=jax.ShapeDtypeStruct((nb, D_MODEL, DEC_PAD), BF),
        scratch_shapes=[pltpu.VMEM((past + new, MLA_HEADS * new), F32)],
        compiler_params=_params(1), name="mla_decode",
    )(s["qTm"], cache_ckv, cache_kr, s["ckv_s"], s["kr_s"], w["wuk"], w["wuvT"])


def _post_kernel(x_ref, oTd_ref, oTm_ref, gmix_ref, wgT_ref, wodT_ref, womT_ref, woutT_ref,
                 gmlp_ref, wup_ref, wdown_ref, gfin_ref, y_ref, *, n_groups):
    rows = x_ref.shape[0] // n_groups
    groups = [slice(i * rows, (i + 1) * rows) for i in range(n_groups)]
    d = D_MODEL

    def norm_in(v, r):
        v["x"] = x_ref[r, :]
        v["hb"] = _rms(v["x"], gmix_ref[...]).astype(BF)

    def branches(v, r):
        v["gT"] = _nt(wgT_ref[...], v.pop("hb"))
        v["od"] = _nn(wodT_ref[...], oTd_ref[:, r])
        v["om"] = _nn(womT_ref[...], oTm_ref[:, r])

    def merge(v, r):
        g = jax.nn.sigmoid(v.pop("gT"))
        v["mT"] = (g[0:d, :] * v.pop("od") + g[d:2 * d, :] * v.pop("om")).astype(BF)

    def out_proj(v, r):
        v["aT"] = _nn(woutT_ref[...], v.pop("mT"))

    def residual(v, r):
        v["x1"] = v.pop("x") + v.pop("aT").T
        v["h2"] = _rms(v["x1"], gmlp_ref[...]).astype(BF)

    def up(v, r):
        v["u"] = _nn(v.pop("h2"), wup_ref[...])

    def act(v, r):
        u = jnp.maximum(v.pop("u"), 0.0)
        v["uu"] = (u * u).astype(BF)

    def down(v, r):
        v["x2"] = v.pop("x1") + _nn(v.pop("uu"), wdown_ref[...])

    def norm_out(v, r):
        y_ref[r, :] = _rms(v.pop("x2"), gfin_ref[...])

    values = [{} for _ in groups]
    for stage in (norm_in, branches, merge, out_proj, residual, up, act, down, norm_out):
        for v, r in zip(values, groups):
            stage(v, r)


def _post(x, oTd, oTm, w, tm, n_groups):
    nb, t, _ = x.shape
    tok = pl.BlockSpec((None, tm, D_MODEL), lambda b, i: (b, i, 0))
    feat = pl.BlockSpec((None, D_MODEL, tm), lambda b, i: (b, 0, i))
    weights = [w["gmix"], w["wgT"], w["wodT"], w["womT"], w["woutT"], w["gmlp"], w["wup"],
               w["wdown"], w["gfin"]]
    return pl.pallas_call(
        functools.partial(_post_kernel, n_groups=n_groups), grid=(nb, t // tm),
        in_specs=[tok, feat, feat] + [_const_spec(a.shape) for a in weights],
        out_specs=tok, out_shape=jax.ShapeDtypeStruct((nb, t, D_MODEL), F32),
        compiler_params=_params(2), name="post",
    )(x, oTd, oTm, *weights)


def _prep_weights(norm_mix, w_in, mla_q_norm, mla_w_uq, mla_kv_norm, mla_w_uk, mla_w_uv, w_o_diff,
                  w_o_mla, w_out, norm_mlp, w_up, w_down, norm_final):
    d = D_MODEL
    o_cq = 3 * d
    o_ckv = o_cq + MLA_Q_LORA
    o_kr = o_ckv + MLA_KV_LORA
    o_g = o_kr + MLA_ROPE_DIM
    half = MLA_ROPE_DIM // 2
    rot = lambda a: jnp.concatenate([-a[..., half:], a[..., :half]], axis=-1)
    zpad = lambda a, n: jnp.zeros(a.shape[:-1] + (n,), a.dtype)

    wkr = w_in[:, o_kr:o_g]
    place = lambda a: jnp.concatenate([a, zpad(a, 32), a, zpad(a, 32)], axis=-1)
    wlat = jnp.concatenate([w_in[:, o_cq:o_kr], place(wkr), place(rot(wkr))], axis=-1)

    nope = mla_w_uq[:, :, :MLA_NOPE_DIM]
    rope = mla_w_uq[:, :, MLA_NOPE_DIM:]

    def arrange(n, r):
        even = jnp.concatenate([n[:, 0::2], r[:, 0::2], zpad(r[:, 0::2], 32)], axis=-1)
        odd = jnp.concatenate([r[:, 1::2], zpad(r[:, 1::2], 32), n[:, 1::2]], axis=-1)
        both = jnp.stack([even, odd], axis=2)
        return both.reshape(MLA_Q_LORA, MLA_HEADS * LANES)

    row = lambda a: a.reshape(1, -1).astype(F32)
    return {
        "gmix": row(norm_mix), "gq": row(mla_q_norm), "gkv": row(mla_kv_norm),
        "gmlp": row(norm_mlp), "gfin": row(norm_final),
        "wqT": w_in[:, 0:d].T.astype(BF),
        "wk": w_in[:, d:2 * d].astype(BF),
        "wv": w_in[:, 2 * d:3 * d].astype(BF),
        "wlat": wlat.astype(BF),
        "wgT": w_in[:, o_g:].T.astype(BF),
        "wuqaT": arrange(nope, rope).T.astype(BF),
        "wuqbT": rot(rope).reshape(MLA_Q_LORA, MLA_HEADS * MLA_ROPE_DIM).T.astype(BF),
        "wuk": mla_w_uk.reshape(MLA_KV_LORA, d).astype(BF),
        "wuvT": mla_w_uv.reshape(MLA_KV_LORA, d).T.astype(BF),
        "wodT": w_o_diff.reshape(d, d).T.astype(BF),
        "womT": w_o_mla.reshape(d, d).T.astype(BF),
        "woutT": w_out.T.astype(BF),
        "wup": w_up.astype(BF),
        "wdown": w_down.astype(BF),
    }


def _rope_tables(pos):
    half = MLA_ROPE_DIM // 2
    inv = jnp.power(ROPE_THETA, -jnp.arange(half, dtype=F32) * 2.0 / MLA_ROPE_DIM)
    ang = pos.astype(F32)[:, None] * inv[None, :]
    cos2 = jnp.tile(jnp.cos(ang), (1, 2))
    sin2 = jnp.tile(jnp.sin(ang), (1, 2))
    t = pos.shape[0]
    z32 = jnp.zeros((t, 32), F32)
    place = lambda a: jnp.concatenate([a, z32, a, z32], axis=1)
    scale = (MLA_NOPE_DIM + MLA_ROPE_DIM) ** -0.5 * LOG2E
    ones = jnp.ones((t, MLA_NOPE_DIM), F32)
    z64 = jnp.zeros((t, MLA_NOPE_DIM), F32)
    ct = jnp.concatenate([ones, cos2, z32, cos2, z32, ones], axis=1) * scale
    st = jnp.concatenate([z64, sin2, z32, sin2, z32, z64], axis=1) * scale
    return {"cosp": place(cos2), "sinp": place(sin2), "ct": ct.T, "st": st.T}


def _score_tiles(rel_bias, k_pos_tiles, q_pos, shift_bucket):
    k_pos = jnp.stack(k_pos_tiles)
    rel = k_pos[:, :, None] - q_pos[None, None, :]
    mask = (k_pos // CHUNK)[:, :, None] <= (q_pos // CHUNK)[None, None, :]
    return _bias_tiles(rel_bias, _t5_bucket(rel).astype(jnp.int32), mask.astype(jnp.int32), shift_bucket)


def _decode_score_tiles(rel_bias, k_pos, q_pos, new, shift_bucket):
    lane = jnp.arange(2 * DIFF_HEADS * new, dtype=jnp.int32)
    qp = q_pos[lane % new]
    rel = k_pos[:, None] - qp[None, :]
    mask = (k_pos // CHUNK)[:, None] <= (qp // CHUNK)[None, :]
    tab = rel_bias[:, (lane // new) % DIFF_HEADS].astype(F32)
    return _bias_lanes(tab, _t5_bucket(rel).astype(jnp.int32), mask.astype(jnp.int32), shift_bucket)


def kernel(x_prompt, x_sample, cache_diff_k, cache_diff_v, cache_mla_ckv, cache_mla_krope, rel_bias,
           norm_mix, w_in, lam_q1, lam_k1, lam_q2, lam_k2, diff_subln, mla_q_norm, mla_w_uq, mla_kv_norm,
           mla_w_uk, mla_w_uv, w_o_diff, w_o_mla, w_out, norm_mlp, w_up, w_down, norm_final):
    assert norm_mix.shape[0] == 1, "single-layer model"
    nb, t, _ = x_prompt.shape
    ns, new, _ = x_sample.shape
    past = cache_diff_k.shape[2]
    diff_blk = min(DIFF_CACHE_BLK, past)
    assert t % TQ == 0 and past % CACHE_BLK == 0 and new <= DEC_PAD and new % 16 == 0
    assert past % diff_blk == 0 and diff_blk >= TK
    assert past % CHUNK == 0 and new <= CHUNK

    w = _prep_weights(norm_mix[0], w_in[0], mla_q_norm[0], mla_w_uq[0], mla_kv_norm[0], mla_w_uk[0],
                      mla_w_uv[0], w_o_diff[0], w_o_mla[0], w_out[0], norm_mlp[0], w_up[0], w_down[0],
                      norm_final)
    lamp = jnp.stack([lam_q1[0], lam_k1[0], lam_q2[0], lam_k2[0]]).astype(F32)
    g_sub = diff_subln[0].reshape(DIFF_V_DIM, 1).astype(F32)

    pos_p = jnp.arange(t, dtype=jnp.int32)
    p = _project(x_prompt, w, _rope_tables(pos_p), tm=512, n_groups=2)
    blk = jnp.arange(TQ, dtype=jnp.int32)
    tiles_p = _score_tiles(rel_bias, [TQ + blk, blk], TQ + blk, _far_bucket())
    mask_p = jnp.where((blk // CHUNK)[:, None] <= (blk // CHUNK)[None, :], 0.0, NEG_INF).astype(F32)
    oTd = _diff_prompt(p, tiles_p, lamp, g_sub)
    oTm = _mla_prompt(p, mask_p)
    y_prompt = _post(x_prompt, oTd, oTm, w, tm=512, n_groups=2)

    toks = ns * new
    xs = x_sample.reshape(1, toks, D_MODEL)
    pos_s = past + jnp.arange(DEC_PAD, dtype=jnp.int32)
    s = _project(xs, w, _rope_tables(jnp.tile(pos_s[:new], ns)), tm=toks, n_groups=1)
    per_stream_q = lambda a: jnp.pad(a[0].reshape(-1, ns, new).transpose(1, 0, 2),
                                     ((0, 0), (0, 0), (0, DEC_PAD - new)))
    s.update(qTd=per_stream_q(s["qTd"]), qTm=per_stream_q(s["qTm"]),
             kb=s["kb"].reshape(ns, new, D_MODEL), ckv_s=s["ckv"].reshape(ns, new, MLA_KV_LORA),
             kr_s=s["kr"].reshape(ns, new, MLA_ROPE_DIM),
             vf_s=s["vf"].reshape(ns, new * DIFF_HEADS, DIFF_V_DIM))
    bias_c = _decode_score_tiles(rel_bias, jnp.arange(past - diff_blk, past, dtype=jnp.int32), pos_s, new,
                                 _far_bucket())
    bias_n = _decode_score_tiles(rel_bias, pos_s[:new], pos_s, new, _far_bucket())
    oTd_s = _diff_decode(s, cache_diff_k[0], cache_diff_v[0], bias_c, bias_n, lamp, g_sub, new, diff_blk)
    oTm_s = _mla_decode(s, cache_mla_ckv[0], cache_mla_krope[0], w, new)
    token_major = lambda a: a[:, :, :new].transpose(1, 0, 2).reshape(1, D_MODEL, toks)
    y_sample = _post(xs, token_major(oTd_s), token_major(oTm_s), w, tm=toks, n_groups=1)

    heads = lambda a, lead: a.reshape((1,) + lead + (DIFF_HEADS, DIFF_V_DIM))
    return (y_prompt, y_sample.reshape(ns, new, D_MODEL),
            heads(p["kf"], (nb, t)), heads(p["vf"], (nb, t)), p["ckv"][None], p["kr"][None],
            heads(s["kf"], (ns, new)), heads(s["vf"], (ns, new)),
            s["ckv"].reshape(1, ns, new, MLA_KV_LORA), s["kr"].reshape(1, ns, new, MLA_ROPE_DIM))
```
